```python
import jax, jax.numpy as jnp
from jax import lax
import numpy as np

D_MODEL = 2048
BATCH = 16
SEQ = 256
DEPTH = 2
DEC_BATCH = 2
DEC_SEQ = 1024
PAST_LEN = 256

GRID_W = 64
HEAD_DIM = 128
N_Q_HEADS = 8
N_KV_HEADS = 2
Q_PER_KV = N_Q_HEADS // N_KV_HEADS
ATT_WIDTH = N_Q_HEADS * HEAD_DIM
KV_WIDTH = N_KV_HEADS * HEAD_DIM
HG_HEADS = 8
HG_DK = 128
HG_DV = 128
HG_KW = HG_HEADS * HG_DK
HG_VW = HG_HEADS * HG_DV
IN_AB = ATT_WIDTH + 2 * KV_WIDTH + 3 * HG_KW + 2 * HG_VW
MIX_WIDTH = ATT_WIDTH + HG_VW
N_AB_LAYERS = (DEPTH + 1) // 2
N_C_LAYERS = DEPTH // 2
POOL_WINDOWS = (2, 4, 8, 16)
POOL_GROUP = D_MODEL // len(POOL_WINDOWS)
D_FF = 4 * D_MODEL
Q_BLOCK = 128
HG_CHUNK = 16
ROPE_THETA = 10000.0
ROPE_HALF = HEAD_DIM // 2
EPS = 1e-6
N_MOD = 6

kernel_name = "hybrid_diffusion_prefix_ctx_step"


def rmsnorm(x, gain):
    xf = x.astype(jnp.float32)
    y = xf * lax.rsqrt(jnp.mean(xf * xf, axis=-1, keepdims=True) + EPS)
    return (y * gain.astype(jnp.float32)).astype(x.dtype)


def adaln_params(cvec, w_ada_l, b_ada_l):
    m = jax.nn.silu(cvec) @ w_ada_l + b_ada_l
    return jnp.split(m[:, None, :], N_MOD, axis=-1)


def modulate(x, gain, shift, scale):
    return rmsnorm(x, gain) * (1 + scale) + shift


def axial_rope(x):
    T = x.shape[1]
    rows = T // GRID_W
    row = jnp.repeat(jnp.arange(rows), GRID_W).astype(jnp.float32)
    col = jnp.tile(jnp.arange(GRID_W), rows).astype(jnp.float32)
    inv = ROPE_THETA ** (-jnp.arange(0, ROPE_HALF, 2, dtype=jnp.float32) / ROPE_HALF)
    ar = row[:, None] * inv
    ac = col[:, None] * inv
    ang = jnp.concatenate([ar, ar, ac, ac], axis=-1)
    bshape = (1, T) + (1,) * (x.ndim - 3) + (HEAD_DIM,)
    cos = jnp.cos(ang).reshape(bshape)
    sin = jnp.sin(ang).reshape(bshape)
    xf = x.astype(jnp.float32)
    a, b, c, d = jnp.split(xf, 4, axis=-1)
    rot = jnp.concatenate([-b, a, -d, c], axis=-1)
    return (xf * cos + rot * sin).astype(x.dtype)


def attend(q, k, v):
    B, Tq = q.shape[0], q.shape[1]
    nb = Tq // Q_BLOCK
    qb = jnp.moveaxis(q.reshape(B, nb, Q_BLOCK, N_KV_HEADS, Q_PER_KV, HEAD_DIM), 1, 0)
    scale = HEAD_DIM ** -0.5

    def one_block(qblk):
        s = jnp.einsum('bqhgd,bkhd->bhgqk', qblk, k).astype(jnp.float32) * scale
        p = jax.nn.softmax(s, axis=-1)
        return jnp.einsum('bhgqk,bkhd->bqhgd', p.astype(v.dtype), v)

    out = lax.map(one_block, qb)
    return jnp.moveaxis(out, 0, 1).reshape(B, Tq, ATT_WIDTH)


def hgrn_scan(q, logf, k, i, s0):
    B, T, H, _ = q.shape
    N = T // HG_CHUNK
    r = lambda a: a.reshape(B, N, HG_CHUNK, H, a.shape[-1])
    q, logf, k, i = r(q), r(logf), r(k), r(i)
    b = jnp.cumsum(logf, axis=2)
    mask = jnp.tril(jnp.ones((HG_CHUNK, HG_CHUNK), dtype=bool))
    diff = b[:, :, :, None] - b[:, :, None]
    decay = jnp.exp(jnp.where(mask[None, None, :, :, None, None], diff, -jnp.inf))
    att = jnp.einsum('bnthd,bnshd,bntshd->bnhts', q, k, decay)
    o = jnp.einsum('bnhts,bnshe->bnthe', att, i)
    b_last = b[:, :, -1]
    dS = jnp.einsum('bnshd,bnshe->bnhde', k * jnp.exp(b_last[:, :, None] - b), i)
    a = jnp.exp(b_last)

    def step(S, xs):
        a_n, dS_n = xs
        return a_n[..., None] * S + dS_n, S

    s_fin, s_starts = lax.scan(step, s0, (jnp.swapaxes(a, 0, 1), jnp.swapaxes(dS, 0, 1)))
    s_starts = jnp.swapaxes(s_starts, 0, 1)
    o = o + jnp.einsum('bnthd,bnhde->bnthe', q * jnp.exp(b), s_starts)
    return o.reshape(B, T, H, i.shape[-1]), s_fin


def mixer_ab(h, w_in, w_out, q_gain, k_gain, o_gain, lb, ctx_k=None, ctx_v=None, s0_f=None, s0_b=None):
    B, T, _ = h.shape
    latent = ctx_k is not None
    sizes = [ATT_WIDTH, KV_WIDTH, KV_WIDTH, HG_KW, HG_KW, HG_KW, HG_VW, HG_VW]
    pts, acc = [], 0
    for s in sizes[:-1]:
        acc += s
        pts.append(acc)
    q, k, v, hq, zf, zb, hi, hg = jnp.split(h @ w_in, pts, axis=-1)
    q = rmsnorm(q.reshape(B, T, N_KV_HEADS, Q_PER_KV, HEAD_DIM), q_gain)
    k = rmsnorm(k.reshape(B, T, N_KV_HEADS, HEAD_DIM), k_gain)
    v = v.reshape(B, T, N_KV_HEADS, HEAD_DIM)
    new_k, new_v = k, v
    if latent:
        q = axial_rope(q)
        k = jnp.concatenate([axial_rope(k), ctx_k.astype(k.dtype)], axis=1)
        v = jnp.concatenate([v, ctx_v.astype(v.dtype)], axis=1)
    att = attend(q, k, v)
    kshape = (B, T, HG_HEADS, HG_DK)
    hq = jax.nn.silu(hq.astype(jnp.float32)).reshape(kshape)
    f_f = lb[0] + (1 - lb[0]) * jax.nn.sigmoid(zf.astype(jnp.float32).reshape(kshape))
    f_b = lb[1] + (1 - lb[1]) * jax.nn.sigmoid(zb.astype(jnp.float32).reshape(kshape))
    iv = hi.astype(jnp.float32).reshape(B, T, HG_HEADS, HG_DV)
    if s0_f is None:
        s0_f = jnp.zeros((B, HG_HEADS, HG_DK, HG_DV), jnp.float32)
        s0_b = jnp.zeros((B, HG_HEADS, HG_DK, HG_DV), jnp.float32)
    o_f, s_f = hgrn_scan(hq, jnp.log(f_f), 1 - f_f, iv, s0_f.astype(jnp.float32))
    fl = lambda a: a[:, ::-1]
    o_b, s_b = hgrn_scan(fl(hq), fl(jnp.log(f_b)), fl(1 - f_b), fl(iv), s0_b.astype(jnp.float32))
    o = o_f + fl(o_b)
    o = rmsnorm(o, o_gain).reshape(B, T, HG_VW) * jax.nn.silu(hg.astype(jnp.float32))
    out = jnp.concatenate([att, o.astype(att.dtype)], axis=-1) @ w_out
    return out, (new_k, new_v, s_f.astype(h.dtype), s_b.astype(h.dtype))


def pool_mixer(h, w_pool_l, scale_l):
    B, T, _ = h.shape
    hf = h.astype(jnp.float32)
    cs = jnp.concatenate([jnp.zeros((B, 1, D_MODEL), jnp.float32), jnp.cumsum(hf, axis=1)], axis=1)
    t = jnp.arange(T)
    outs = []
    for g, w in enumerate(POOL_WINDOWS):
        lo_c, hi_c = g * POOL_GROUP, (g + 1) * POOL_GROUP
        lo = jnp.clip(t - w // 2, 0, T)
        hi = jnp.clip(t + w - w // 2, 0, T)
        csg = cs[:, :, lo_c:hi_c]
        mean = (csg[:, hi] - csg[:, lo]) / (hi - lo).astype(jnp.float32)[None, :, None]
        pooled = (mean - hf[:, :, lo_c:hi_c]).astype(h.dtype)
        outs.append(pooled @ w_pool_l[g])
    return jnp.concatenate(outs, axis=-1) * scale_l


def mlp(h, w1, w2):
    return jnp.square(jax.nn.relu(h @ w1)) @ w2


def setup_inputs(seed: int = 0) -> dict:
    key = jax.random.key(seed)
    ks = jax.random.split(key, 24)
    f32 = jnp.float32
    nrm = lambda k, shape, s=1.0: jax.random.normal(k, shape, f32) * s
    return {
        "x_prompt": nrm(ks[0], (BATCH, SEQ, D_MODEL)),
        "x_sample": nrm(ks[1], (DEC_BATCH, DEC_SEQ, D_MODEL)),
        "cache_k": nrm(ks[2], (DEC_BATCH, N_AB_LAYERS, PAST_LEN, N_KV_HEADS, HEAD_DIM)),
        "cache_v": nrm(ks[3], (DEC_BATCH, N_AB_LAYERS, PAST_LEN, N_KV_HEADS, HEAD_DIM)),
        "state_hgrn_fwd": nrm(ks[4], (DEC_BATCH, N_AB_LAYERS, HG_HEADS, HG_DK, HG_DV), 0.5),
        "state_hgrn_bwd": nrm(ks[5], (DEC_BATCH, N_AB_LAYERS, HG_HEADS, HG_DK, HG_DV), 0.5),
        "c": nrm(ks[6], (DEC_BATCH, D_MODEL)),
        "c_ctx": nrm(ks[7], (D_MODEL,)),
        "w_ada": nrm(ks[8], (DEPTH, D_MODEL, N_MOD * D_MODEL), 0.5 * D_MODEL ** -0.5),
        "b_ada": nrm(ks[9], (DEPTH, N_MOD * D_MODEL), 0.01),
        "norm_mix": 1.0 + nrm(ks[10], (DEPTH, D_MODEL), 0.02),
        "norm_mlp": 1.0 + nrm(ks[11], (DEPTH, D_MODEL), 0.02),
        "w_in_ab": nrm(ks[12], (N_AB_LAYERS, D_MODEL, IN_AB), D_MODEL ** -0.5),
        "w_out_ab": nrm(ks[13], (N_AB_LAYERS, MIX_WIDTH, D_MODEL), MIX_WIDTH ** -0.5),
        "q_norm": 1.0 + nrm(ks[14], (N_AB_LAYERS, HEAD_DIM), 0.02),
        "k_norm": 1.0 + nrm(ks[15], (N_AB_LAYERS, HEAD_DIM), 0.02),
        "hg_norm": 1.0 + nrm(ks[16], (N_AB_LAYERS, HG_DV), 0.02),
        "lb_raw": nrm(ks[17], (2, DEPTH + 1, HG_KW), 0.5),
        "w_pool": nrm(ks[18], (N_C_LAYERS, len(POOL_WINDOWS), POOL_GROUP, POOL_GROUP), POOL_GROUP ** -0.5),
        "pool_scale": 1.0 + nrm(ks[19], (N_C_LAYERS, D_MODEL), 0.1),
        "w_mlp_in": nrm(ks[20], (DEPTH, D_MODEL, D_FF), D_MODEL ** -0.5),
        "w_mlp_out": nrm(ks[21], (DEPTH, D_FF, D_MODEL), D_FF ** -0.5),
        "final_norm": 1.0 + nrm(ks[22], (D_MODEL,), 0.02),
    }


def reference(x_prompt, x_sample, cache_k, cache_v, state_hgrn_fwd, state_hgrn_bwd, c, c_ctx,
              w_ada, b_ada, norm_mix, norm_mlp, w_in_ab, w_out_ab, q_norm, k_norm, hg_norm, lb_raw,
              w_pool, pool_scale, w_mlp_in, w_mlp_out, final_norm):
    lb_all = jnp.cumsum(jax.nn.softmax(lb_raw.astype(jnp.float32), axis=1), axis=1)

    def run_stream(x, cvec, ctx):
        states = []
        for l in range(DEPTH):
            sh1, sc1, g1, sh2, sc2, g2 = adaln_params(cvec, w_ada[l], b_ada[l])
            h = modulate(x, norm_mix[l], sh1, sc1)
            j = l // 2
            if l % 2 == 0:
                lb_l = lb_all[:, l].reshape(2, HG_HEADS, HG_DK)
                if ctx is None:
                    mix, st = mixer_ab(h, w_in_ab[j], w_out_ab[j], q_norm[j], k_norm[j], hg_norm[j], lb_l)
                else:
                    mix, st = mixer_ab(h, w_in_ab[j], w_out_ab[j], q_norm[j], k_norm[j], hg_norm[j], lb_l,
                                       ctx_k=ctx[0][:, j], ctx_v=ctx[1][:, j],
                                       s0_f=ctx[2][:, j], s0_b=ctx[3][:, j])
                states.append(st)
            else:
                mix = pool_mixer(h, w_pool[j], pool_scale[j])
            x = x + g1 * mix
            h = modulate(x, norm_mlp[l], sh2, sc2)
            x = x + g2 * mlp(h, w_mlp_in[l], w_mlp_out[l])
        return rmsnorm(x, final_norm), states

    y_prompt, st_p = run_stream(x_prompt, c_ctx[None, :], None)
    y_sample, _ = run_stream(x_sample, c, (cache_k, cache_v, state_hgrn_fwd, state_hgrn_bwd))

    new_k = jnp.stack([s[0] for s in st_p], axis=1)
    new_v = jnp.stack([s[1] for s in st_p], axis=1)
    new_s_fwd = jnp.stack([s[2] for s in st_p], axis=1)
    new_s_bwd = jnp.stack([s[3] for s in st_p], axis=1)
    return (y_prompt, y_sample, new_k, new_v, new_s_fwd, new_s_bwd)
```

```python
import functools

import numpy as np
import jax
import jax.numpy as jnp
from jax import lax
from jax.experimental import pallas as pl
from jax.experimental.pallas import tpu as pltpu

F32 = jnp.float32
BF16 = jnp.bfloat16

D_MODEL = 2048
BATCH = 16
SEQ = 256
DEPTH = 2
DEC_BATCH = 2
DEC_SEQ = 1024
PAST_LEN = 256
GRID_W = 64
HEAD_DIM = 128
N_Q_HEADS = 8
N_KV_HEADS = 2
Q_PER_KV = N_Q_HEADS // N_KV_HEADS
ATT_WIDTH = N_Q_HEADS * HEAD_DIM
KV_WIDTH = N_KV_HEADS * HEAD_DIM
HG_HEADS = 8
HG_DK = 128
HG_DV = 128
HG_KW = HG_HEADS * HG_DK
HG_VW = HG_HEADS * HG_DV
IN_AB = ATT_WIDTH + 2 * KV_WIDTH + 3 * HG_KW + 2 * HG_VW
MIX_WIDTH = ATT_WIDTH + HG_VW
POOL_WINDOWS = (2, 4, 8, 16)
POOL_GROUP = D_MODEL // len(POOL_WINDOWS)
D_FF = 4 * D_MODEL
ROPE_THETA = 10000.0
ROPE_HALF = HEAD_DIM // 2
EPS = 1e-6
N_MOD = 6

N_PROMPT = BATCH * SEQ
N_SAMPLE = DEC_BATCH * DEC_SEQ
N_TOK = N_PROMPT + N_SAMPLE
ADA_ROWS = 16
TOK_TILE = 1024
N_TILES = N_TOK // TOK_TILE
N_PROMPT_TILES = N_PROMPT // TOK_TILE
PROJ_TOK_TILE = 2048
PROJ_COL_TILE = 512
OUT_COL_TILE = 512
FF_TILE = 256
HG_CHUNK = 128
HG_LEVELS = 7
HG_HEAD_BLOCK = 4
VMEM_LIMIT = 56 * 2 ** 20

COL_Q = 0
COL_K = ATT_WIDTH
COL_V = COL_K + KV_WIDTH
COL_HQ = COL_V + KV_WIDTH
COL_ZF = COL_HQ + HG_KW
COL_HI = COL_ZF + 2 * HG_KW
COL_HG = COL_HI + HG_VW


def _params(semantics):
    return pltpu.CompilerParams(dimension_semantics=semantics, vmem_limit_bytes=VMEM_LIMIT)


def _sigmoid(x):
    return 1.0 / (1.0 + jnp.exp(-x))


def _silu(x):
    return x * _sigmoid(x)


def _rms(x, gain):
    return x * lax.rsqrt(jnp.mean(x * x, axis=-1, keepdims=True) + EPS) * gain


def _bdot(a, b):
    return jnp.dot(a, b, preferred_element_type=F32)


def _mod_row(tile, tile_rows):
    first = N_PROMPT // tile_rows
    per_seq = DEC_SEQ // tile_rows
    return jnp.where(tile < first, 0, 1 + (tile - first) // per_seq)


def _mod_spec(layer, chunk, tile_rows, width=D_MODEL, col=lambda *g: 0, tile_of=lambda *g: g[0]):
    per = D_MODEL // width
    return pl.BlockSpec((None, None, 1, width),
                        lambda *g: (layer, _mod_row(tile_of(*g), tile_rows), 0, chunk * per + col(*g)))


def _ada_kernel(cv_ref, w_ref, b_ref, o_ref):
    s = _silu(cv_ref[...]).astype(BF16)
    o_ref[...] = _bdot(s, w_ref[...].astype(BF16)) + b_ref[...]


def _ada_table(cv, w_ada, b_ada):
    tn = 1024
    n = N_MOD * D_MODEL
    return pl.pallas_call(
        _ada_kernel,
        out_shape=jax.ShapeDtypeStruct((DEPTH, ADA_ROWS, n), F32),
        grid=(DEPTH, n // tn),
        in_specs=[pl.BlockSpec((ADA_ROWS, D_MODEL), lambda l, j: (0, 0)),
                  pl.BlockSpec((None, D_MODEL, tn), lambda l, j: (l, 0, j)),
                  pl.BlockSpec((None, 1, tn), lambda l, j: (l, 0, j))],
        out_specs=pl.BlockSpec((None, ADA_ROWS, tn), lambda l, j: (l, 0, j)),
        compiler_params=_params(("arbitrary", "arbitrary")),
        name="ada_table",
    )(cv, w_ada, b_ada.reshape(DEPTH, 1, n))


def _modulate(x, gain, shift, scale):
    return _rms(x, gain) * (1.0 + scale) + shift


def _mod0_kernel(xp_ref, xs_ref, gain_ref, sh_ref, sc_ref, o_ref):
    i = pl.program_id(0)

    def run(x_ref):
        o_ref[...] = _modulate(x_ref[...], gain_ref[...], sh_ref[...], sc_ref[...]).astype(BF16)

    pl.when(i < N_PROMPT // 512)(lambda: run(xp_ref))
    pl.when(i >= N_PROMPT // 512)(lambda: run(xs_ref))


def _modulate0(xp, xs, norm_mix, mods):
    tm = 512
    first = N_PROMPT // tm
    return pl.pallas_call(
        _mod0_kernel,
        out_shape=jax.ShapeDtypeStruct((N_TOK, D_MODEL), BF16),
        grid=(N_TOK // tm,),
        in_specs=[pl.BlockSpec((tm, D_MODEL), lambda i: (jnp.minimum(i, first - 1), 0)),
                  pl.BlockSpec((tm, D_MODEL), lambda i: (jnp.maximum(i - first, 0), 0)),
                  pl.BlockSpec((1, D_MODEL), lambda i: (0, 0)),
                  _mod_spec(0, 0, tm), _mod_spec(0, 1, tm)],
        out_specs=pl.BlockSpec((tm, D_MODEL), lambda i: (i, 0)),
        compiler_params=_params(("arbitrary",)),
        name="modulate0",
    )(xp, xs, norm_mix[0:1], mods, mods)


def _rope(y, cos, sin_lo, sin_hi):
    return (y * cos + pltpu.roll(y, HEAD_DIM - ROPE_HALF // 2, 1) * sin_lo
            + pltpu.roll(y, ROPE_HALF // 2, 1) * sin_hi)


def _inproj_kernel(h_ref, w_ref, qg_ref, kg_ref, cos_ref, slo_ref, shi_ref, p_ref, nk_ref, nv_ref):
    i = pl.program_id(0)
    j = pl.program_id(1)
    latent = i >= N_PROMPT // PROJ_TOK_TILE
    acc = _bdot(h_ref[...], w_ref[...].astype(BF16))
    heads = PROJ_COL_TILE // HEAD_DIM
    seqs = PROJ_TOK_TILE // DEC_SEQ

    def head(x, hh):
        return x[:, hh * HEAD_DIM:(hh + 1) * HEAD_DIM]

    def store_rope(hh, y):
        for s in range(seqs):
            rows = slice(s * DEC_SEQ, (s + 1) * DEC_SEQ)
            p_ref[rows, hh * HEAD_DIM:(hh + 1) * HEAD_DIM] = _rope(
                y[rows], cos_ref[...], slo_ref[...], shi_ref[...])

    def store_plain(hh, y):
        p_ref[:, hh * HEAD_DIM:(hh + 1) * HEAD_DIM] = y

    @pl.when(j < COL_K // PROJ_COL_TILE)
    def _():
        for hh in range(heads):
            y = _rms(head(acc, hh), qg_ref[...])
            pl.when(latent)(functools.partial(store_rope, hh, y))
            pl.when(jnp.logical_not(latent))(functools.partial(store_plain, hh, y))

    @pl.when(j == COL_K // PROJ_COL_TILE)
    def _():
        for hh in range(N_KV_HEADS):
            y = _rms(head(acc, hh), kg_ref[...])
            pl.when(latent)(functools.partial(store_rope, hh, y))

            @pl.when(jnp.logical_not(latent))
            def _():
                store_plain(hh, y)
                nk_ref[:, hh * HEAD_DIM:(hh + 1) * HEAD_DIM] = y
        v = acc[:, KV_WIDTH:]
        p_ref[:, KV_WIDTH:] = v

        @pl.when(jnp.logical_not(latent))
        def _():
            nv_ref[...] = v

    is_silu = jnp.logical_or(
        jnp.logical_and(j >= COL_HQ // PROJ_COL_TILE, j < COL_ZF // PROJ_COL_TILE),
        j >= COL_HG // PROJ_COL_TILE)

    @pl.when(is_silu)
    def _():
        p_ref[...] = _silu(acc)

    @pl.when(jnp.logical_and(j >= COL_ZF // PROJ_COL_TILE, j < COL_HG // PROJ_COL_TILE))
    def _():
        p_ref[...] = acc


def _rope_tables():
    t = np.arange(DEC_SEQ)
    row = (t // GRID_W).astype(np.float32)
    col = (t % GRID_W).astype(np.float32)
    inv = (np.float32(ROPE_THETA) ** (-np.arange(0, ROPE_HALF, 2, dtype=np.float32) / np.float32(ROPE_HALF))).astype(np.float32)
    ar = row[:, None] * inv
    ac = col[:, None] * inv
    ang = np.concatenate([ar, ar, ac, ac], axis=-1).astype(np.float32)
    cos = np.cos(ang).astype(np.float32)
    sin = np.sin(ang).astype(np.float32)
    quarter = (np.arange(HEAD_DIM) // (ROPE_HALF // 2)) % 2
    sin_lo = np.where(quarter == 0, -sin, 0.0).astype(np.float32)
    sin_hi = np.where(quarter == 1, sin, 0.0).astype(np.float32)
    return jnp.asarray(cos), jnp.asarray(sin_lo), jnp.asarray(sin_hi)


def _in_projection(h, w_in, q_gain, k_gain):
    tm, tn = PROJ_TOK_TILE, PROJ_COL_TILE
    n_prompt_tiles = N_PROMPT // tm
    cos, sin_lo, sin_hi = _rope_tables()
    table = pl.BlockSpec((DEC_SEQ, HEAD_DIM), lambda i, j: (0, 0))
    gain = pl.BlockSpec((1, HEAD_DIM), lambda i, j: (0, 0))
    state = pl.BlockSpec((tm, KV_WIDTH), lambda i, j: (jnp.minimum(i, n_prompt_tiles - 1), 0))
    return pl.pallas_call(
        _inproj_kernel,
        out_shape=(jax.ShapeDtypeStruct((N_TOK, IN_AB), F32),
                   jax.ShapeDtypeStruct((N_PROMPT, KV_WIDTH), F32),
                   jax.ShapeDtypeStruct((N_PROMPT, KV_WIDTH), F32)),
        grid=(N_TOK // tm, IN_AB // tn),
        in_specs=[pl.BlockSpec((tm, D_MODEL), lambda i, j: (i, 0)),
                  pl.BlockSpec((D_MODEL, tn), lambda i, j: (0, j)),
                  gain, gain, table, table, table],
        out_specs=(pl.BlockSpec((tm, tn), lambda i, j: (i, j)), state, state),
        compiler_params=_params(("arbitrary", "arbitrary")),
        name="in_projection",
    )(h, w_in, q_gain, k_gain, cos, sin_lo, sin_hi)


def _attn_kernel(*refs, has_ctx):
    if has_ctx:
        q_ref, k_ref, v_ref, ck_ref, cv_ref, _, o_ref = refs
    else:
        q_ref, k_ref, v_ref, o_ref = refs
    scale = HEAD_DIM ** -0.5
    nt = (((1,), (1,)), ((), ()))
    k = k_ref[...].astype(BF16)
    v = v_ref[...].astype(BF16)
    if has_ctx:
        ck = ck_ref[...].astype(BF16)
        cv = cv_ref[...].astype(BF16)
    for g in range(Q_PER_KV):
        q = q_ref[:, g * HEAD_DIM:(g + 1) * HEAD_DIM].astype(BF16)
        s = lax.dot_general(q, k, nt, preferred_element_type=F32) * scale
        m = jnp.max(s, axis=-1, keepdims=True)
        if has_ctx:
            sc = lax.dot_general(q, ck, nt, preferred_element_type=F32) * scale
            m = jnp.maximum(m, jnp.max(sc, axis=-1, keepdims=True))
        p = jnp.exp(s - m)
        den = jnp.sum(p, axis=-1, keepdims=True)
        o = _bdot(p.astype(BF16), v)
        if has_ctx:
            pc = jnp.exp(sc - m)
            den = den + jnp.sum(pc, axis=-1, keepdims=True)
            o = o + _bdot(pc.astype(BF16), cv)
        o_ref[:, g * HEAD_DIM:(g + 1) * HEAD_DIM] = (o / den).astype(o_ref.dtype)


def _attention(p, n_batch, seq, row0, tq, ctx=None, prev=None):
    q_blocks = seq // tq
    gw = Q_PER_KV * HEAD_DIM
    in_specs = [
        pl.BlockSpec((tq, gw), lambda b, h, qi: (row0 // tq + b * q_blocks + qi, h)),
        pl.BlockSpec((seq, HEAD_DIM), lambda b, h, qi: (row0 // seq + b, COL_K // HEAD_DIM + h)),
        pl.BlockSpec((seq, HEAD_DIM), lambda b, h, qi: (row0 // seq + b, COL_V // HEAD_DIM + h)),
    ]
    args = [p, p, p]
    aliases = {}
    if ctx is not None:
        ctx_spec = pl.BlockSpec((None, PAST_LEN, HEAD_DIM), lambda b, h, qi: (b, 0, h))
        in_specs += [ctx_spec, ctx_spec, pl.BlockSpec(memory_space=pl.ANY)]
        args += [ctx[0], ctx[1], prev]
        aliases = {5: 0}
    return pl.pallas_call(
        functools.partial(_attn_kernel, has_ctx=ctx is not None),
        out_shape=jax.ShapeDtypeStruct((N_TOK, ATT_WIDTH), BF16),
        grid=(n_batch, N_KV_HEADS, q_blocks),
        in_specs=in_specs,
        out_specs=pl.BlockSpec((tq, gw), lambda b, h, qi: (row0 // tq + b * q_blocks + qi, h)),
        input_output_aliases=aliases,
        compiler_params=_params(("arbitrary", "arbitrary", "arbitrary")),
        name="attention_latent" if ctx is not None else "attention_prompt",
    )(*args)


def _hgrn_constants():
    c = HG_CHUNK
    t = np.arange(c)
    cums, sels, pairs = [], [], []
    for d in range(2):
        pos = t if d == 0 else c - 1 - t
        pu, pt = pos[None, :], pos[:, None]
        ms, ss, ws = [], [], []
        for l in range(HG_LEVELS):
            m = c >> l
            blk = pos // m
            mid = (blk * m + m // 2)[:, None]
            late = ((pos % m) >= m // 2)
            ms.append(np.where(late[:, None], (pu >= mid) & (pu <= pt), (pu > pt) & (pu < mid)))
            ss.append(np.broadcast_to(late[:, None], (c, c)))
            ws.append((blk[:, None] == blk[None, :]) & late[:, None] & ~late[None, :])
        ms.append(pu <= pt)
        ms.append(pu > pt)
        cums.append(np.concatenate(ms, axis=0))
        sels.append(np.stack(ss))
        pairs.append(np.stack(ws))
    return (jnp.asarray(np.stack(cums), BF16), jnp.asarray(np.stack(sels), F32),
            jnp.asarray(np.stack(pairs), F32))


def _hgrn_kernel(*refs, n_chunks, has_s0, has_state_out):
    hq_ref, z_ref, hi_ref, hg_ref, lb_ref, og_ref, cum_ref, sel_ref, pair_ref = refs[:9]
    refs = refs[9:]
    if has_s0:
        s0_ref, _, o_ref = refs[:3]
        refs = refs[3:]
    else:
        o_ref = refs[0]
        refs = refs[1:]
    if has_state_out:
        sf_ref, sb_ref = refs[:2]
        refs = refs[2:]
    st_ref, acc_ref = refs
    d = pl.program_id(2)
    c = HG_CHUNK
    nt = (((1,), (1,)), ((), ()))
    tn = (((0,), (0,)), ((), ()))

    raw = lb_ref[d]
    e = jnp.exp(raw - jnp.max(raw, axis=0, keepdims=True))
    lb = e[0:1] / jnp.sum(e, axis=0, keepdims=True)

    for hh in range(HG_HEAD_BLOCK):
        if has_s0:
            st_ref[hh] = s0_ref[hh].T
        else:
            st_ref[hh] = jnp.zeros((HG_DV, HG_DK), F32)

    cum = cum_ref[d]

    def chunk(ci, carry):
        cidx = jnp.where(d == 0, ci, n_chunks - 1 - ci)
        rows = pl.ds(pl.multiple_of(cidx * c, c), c)
        for hh in range(HG_HEAD_BLOCK):
            cols = slice(hh * HG_DK, (hh + 1) * HG_DK)
            lbh = lb[:, cols]
            f = lbh + (1.0 - lbh) * _sigmoid(z_ref[rows, cols])
            logf = jnp.log(f)
            kk = 1.0 - f
            q = hq_ref[rows, cols]
            iv = hi_ref[rows, cols].astype(BF16)
            hi16 = logf.astype(BF16)
            lo16 = (logf - hi16.astype(F32)).astype(BF16)
            dall = _bdot(cum, hi16) + _bdot(cum, lo16)
            att = jnp.zeros((c, c), F32)
            for l in range(HG_LEVELS):
                x = (kk + sel_ref[d, l] * (q - kk)) * jnp.exp(dall[l * c:(l + 1) * c])
                xb = x.astype(BF16)
                att = att + pair_ref[d, l] * lax.dot_general(xb, xb, nt, preferred_element_type=F32)
            b_in = dall[HG_LEVELS * c:(HG_LEVELS + 1) * c]
            b_out = dall[(HG_LEVELS + 1) * c:]
            st = st_ref[hh]
            o = (_bdot(att.astype(BF16), iv)
                 + jnp.sum(q * kk, axis=-1, keepdims=True) * hi_ref[rows, cols]
                 + lax.dot_general((q * jnp.exp(b_in)).astype(BF16), st.astype(BF16), nt,
                                   preferred_element_type=F32))
            dst = lax.dot_general(iv, (kk * jnp.exp(b_out)).astype(BF16), tn, preferred_element_type=F32)
            st_ref[hh] = jnp.exp(b_in[0:1] + b_out[0:1]) * st + dst

            @pl.when(d == 0)
            def _():
                acc_ref[rows, cols] = o

            @pl.when(d == 1)
            def _():
                tot = acc_ref[rows, cols] + o
                o_ref[rows, cols] = (_rms(tot, og_ref[...]) * hg_ref[rows, cols]).astype(o_ref.dtype)
        return carry

    lax.fori_loop(0, n_chunks, chunk, 0)

    if has_state_out:
        @pl.when(d == 0)
        def _():
            for hh in range(HG_HEAD_BLOCK):
                sf_ref[hh] = st_ref[hh].T

        @pl.when(d == 1)
        def _():
            for hh in range(HG_HEAD_BLOCK):
                sb_ref[hh] = st_ref[hh].T


def _hgrn(p, lb_raw, o_gain, consts, n_batch, seq, row0, s0=None, prev=None):
    hb = HG_HEAD_BLOCK
    w = hb * HG_DK
    n_hb = HG_HEADS // hb
    cum, sel, pair = consts

    def seg(col, per_dir=False):
        if per_dir:
            return pl.BlockSpec((seq, w), lambda b, h, d: (row0 // seq + b, col // w + d * n_hb + h))
        return pl.BlockSpec((seq, w), lambda b, h, d: (row0 // seq + b, col // w + h))

    def const(a):
        return pl.BlockSpec(a.shape, lambda b, h, d: (0,) * a.ndim)

    in_specs = [seg(COL_HQ), seg(COL_ZF, True), seg(COL_HI), seg(COL_HG),
                pl.BlockSpec((2, DEPTH + 1, w), lambda b, h, d: (0, 0, h)),
                pl.BlockSpec((1, HG_DV), lambda b, h, d: (0, 0)),
                const(cum), const(sel), const(pair)]
    args = [p, p, p, p, lb_raw, o_gain, cum, sel, pair]
    aliases = {}
    has_s0 = s0 is not None
    if has_s0:
        in_specs += [pl.BlockSpec((None, None, hb, HG_DK, HG_DV), lambda b, h, d: (d, b, h, 0, 0)),
                     pl.BlockSpec(memory_space=pl.ANY)]
        args += [s0, prev]
        aliases = {len(args) - 1: 0}
    out_shape = [jax.ShapeDtypeStruct((N_TOK, HG_VW), BF16)]
    out_specs = [pl.BlockSpec((seq, w), lambda b, h, d: (row0 // seq + b, h))]
    if not has_s0:
        st_shape = jax.ShapeDtypeStruct((n_batch, HG_HEADS, HG_DK, HG_DV), F32)
        st_spec = pl.BlockSpec((None, hb, HG_DK, HG_DV), lambda b, h, d: (b, h, 0, 0))
        out_shape += [st_shape, st_shape]
        out_specs += [st_spec, st_spec]
    return pl.pallas_call(
        functools.partial(_hgrn_kernel, n_chunks=seq // HG_CHUNK, has_s0=has_s0, has_state_out=not has_s0),
        out_shape=tuple(out_shape),
        grid=(n_batch, n_hb, 2),
        in_specs=in_specs,
        out_specs=tuple(out_specs),
        scratch_shapes=[pltpu.VMEM((hb, HG_DV, HG_DK), F32), pltpu.VMEM((seq, w), F32)],
        input_output_aliases=aliases,
        compiler_params=_params(("arbitrary", "arbitrary", "arbitrary")),
        name="hgrn_latent" if has_s0 else "hgrn_prompt",
    )(*args)


def _outproj_kernel(att_ref, o_ref, wa_ref, wb_ref, xp_ref, xs_ref, g1_ref, gain_ref, sh_ref, sc_ref,
                    x1_ref, h2_ref, full_ref):
    i = pl.program_id(0)
    n = pl.program_id(1)
    acc = _bdot(att_ref[...], wa_ref[...].astype(BF16)) + _bdot(o_ref[...], wb_ref[...].astype(BF16))

    def run(x_ref):
        x1 = x_ref[...] + g1_ref[...] * acc
        x1_ref[...] = x1
        full_ref[n] = x1

    pl.when(i < N_PROMPT_TILES)(lambda: run(xp_ref))
    pl.when(i >= N_PROMPT_TILES)(lambda: run(xs_ref))

    @pl.when(n == D_MODEL // OUT_COL_TILE - 1)
    def _():
        nt = D_MODEL // OUT_COL_TILE
        ss = full_ref[0] * full_ref[0]
        ms = jnp.sum(ss, axis=-1, keepdims=True)
        for k in range(1, nt):
            ms = ms + jnp.sum(full_ref[k] * full_ref[k], axis=-1, keepdims=True)
        rstd = lax.rsqrt(ms / D_MODEL + EPS)
        for k in range(nt):
            cols = slice(k * OUT_COL_TILE, (k + 1) * OUT_COL_TILE)
            y = full_ref[k] * rstd * gain_ref[:, cols]
            h2_ref[:, cols] = (y * (1.0 + sc_ref[:, cols]) + sh_ref[:, cols]).astype(BF16)


def _out_projection(att, o, w_out, xp, xs, mods, norm_mlp):
    tm, tn = TOK_TILE, OUT_COL_TILE
    nt = D_MODEL // tn
    return pl.pallas_call(
        _outproj_kernel,
        out_shape=(jax.ShapeDtypeStruct((N_TOK, D_MODEL), F32), jax.ShapeDtypeStruct((N_TOK, D_MODEL), BF16)),
        grid=(N_TILES, nt),
        in_specs=[pl.BlockSpec((tm, ATT_WIDTH), lambda i, n: (i, 0)),
                  pl.BlockSpec((tm, HG_VW), lambda i, n: (i, 0)),
                  pl.BlockSpec((ATT_WIDTH, tn), lambda i, n: (0, n)),
                  pl.BlockSpec((HG_VW, tn), lambda i, n: (1, n)),
                  pl.BlockSpec((tm, tn), lambda i, n: (jnp.minimum(i, N_PROMPT_TILES - 1), n)),
                  pl.BlockSpec((tm, tn), lambda i, n: (jnp.maximum(i - N_PROMPT_TILES, 0), n)),
                  _mod_spec(0, 2, tm, width=tn, col=lambda i, n: n),
                  pl.BlockSpec((1, D_MODEL), lambda i, n: (0, 0)),
                  _mod_spec(0, 3, tm), _mod_spec(0, 4, tm)],
        out_specs=(pl.BlockSpec((tm, tn), lambda i, n: (i, n)),
                   pl.BlockSpec((tm, D_MODEL), lambda i, n: (i, 0))),
        scratch_shapes=[pltpu.VMEM((nt, tm, tn), F32)],
        compiler_params=_params(("arbitrary", "arbitrary")),
        name="out_projection",
    )(att, o, w_out, w_out, xp, xs, mods, norm_mlp, mods, mods)


def _mlp_kernel(h_ref, w1_ref, w2_ref, x_ref, g2_ref, *rest, final):
    if final:
        fin_ref, o_ref = rest
    else:
        (o_ref,) = rest
    j = pl.program_id(1)

    @pl.when(j == 0)
    def _():
        o_ref[...] = x_ref[...]

    a = jnp.square(jnp.maximum(_bdot(h_ref[...], w1_ref[...].astype(BF16)), 0.0))
    o_ref[...] += g2_ref[...] * _bdot(a.astype(BF16), w2_ref[...].astype(BF16))

    if final:
        @pl.when(j == D_FF // FF_TILE - 1)
        def _():
            o_ref[...] = _rms(o_ref[...], fin_ref[...])


def _mlp(h, x, w1, w2, mods, layer, tile0, n_tiles, final_norm=None):
    tm, th = TOK_TILE, FF_TILE
    final = final_norm is not None
    in_specs = [pl.BlockSpec((tm, D_MODEL), lambda i, j: (tile0 + i, 0)),
                pl.BlockSpec((D_MODEL, th), lambda i, j: (0, j)),
                pl.BlockSpec((th, D_MODEL), lambda i, j: (j, 0)),
                pl.BlockSpec((tm, D_MODEL), lambda i, j: (tile0 + i, 0)),
                _mod_spec(layer, 5, tm, tile_of=lambda i, j: tile0 + i)]
    args = [h, w1, w2, x, mods]
    if final:
        in_specs.append(pl.BlockSpec((1, D_MODEL), lambda i, j: (0, 0)))
        args.append(final_norm)
    return pl.pallas_call(
        functools.partial(_mlp_kernel, final=final),
        out_shape=jax.ShapeDtypeStruct((n_tiles * tm, D_MODEL), F32),
        grid=(n_tiles, D_FF // th),
        in_specs=in_specs,
        out_specs=pl.BlockSpec((tm, D_MODEL), lambda i, j: (i, 0)),
        compiler_params=_params(("arbitrary", "arbitrary")),
        name="mlp_final" if final else "mlp",
    )(*args)


def _pool_kernel(x_ref, wp_ref, ps_ref, gain1_ref, sh1_ref, sc1_ref, g1_ref, gain2_ref, sh2_ref, sc2_ref,
                 x3_ref, h4_ref):
    i = pl.program_id(0)
    tm = TOK_TILE
    seq = jnp.where(i < N_PROMPT_TILES, SEQ, DEC_SEQ)
    pos = lax.broadcasted_iota(jnp.int32, (tm, POOL_GROUP), 0) & (seq - 1)

    def shifted(a, off):
        r = pltpu.roll(a, (-off) % tm, 0)
        return jnp.where(jnp.logical_and(pos + off >= 0, pos + off < seq), r, 0.0)

    for g, w in enumerate(POOL_WINDOWS):
        cols = slice(g * POOL_GROUP, (g + 1) * POOL_GROUP)
        x = x_ref[:, cols]
        ms = jnp.mean(x_ref[...] * x_ref[...], axis=-1, keepdims=True) if g == 0 else ms
        h = x * lax.rsqrt(ms + EPS) * gain1_ref[:, cols] * (1.0 + sc1_ref[:, cols]) + sh1_ref[:, cols]
        back = h
        fwd = h
        m = 1
        while m < w // 2:
            back = back + shifted(back, -m)
            fwd = fwd + shifted(fwd, m)
            m *= 2
        total = shifted(back, -1) + fwd
        count = (jnp.minimum(pos + (w - w // 2), seq) - jnp.maximum(pos - w // 2, 0)).astype(F32)
        pooled = (total / count - h).astype(BF16)
        mix = _bdot(pooled, wp_ref[g].astype(BF16)) * ps_ref[:, cols]
        x3_ref[:, cols] = x + g1_ref[:, cols] * mix
    x3 = x3_ref[...]
    h4_ref[...] = _modulate(x3, gain2_ref[...], sh2_ref[...], sc2_ref[...]).astype(BF16)


def _pool_mixer(x, w_pool, pool_scale, norm_mix, norm_mlp, mods):
    tm = TOK_TILE
    vec = pl.BlockSpec((1, D_MODEL), lambda i: (0, 0))
    tile = pl.BlockSpec((tm, D_MODEL), lambda i: (i, 0))
    return pl.pallas_call(
        _pool_kernel,
        out_shape=(jax.ShapeDtypeStruct((N_TOK, D_MODEL), F32), jax.ShapeDtypeStruct((N_TOK, D_MODEL), BF16)),
        grid=(N_TILES,),
        in_specs=[tile, pl.BlockSpec((len(POOL_WINDOWS), POOL_GROUP, POOL_GROUP), lambda i: (0, 0, 0)),
                  vec, vec, _mod_spec(1, 0, tm), _mod_spec(1, 1, tm), _mod_spec(1, 2, tm),
                  vec, _mod_spec(1, 3, tm), _mod_spec(1, 4, tm)],
        out_specs=(tile, tile),
        compiler_params=_params(("arbitrary",)),
        name="pool_mixer",
    )(x, w_pool, pool_scale, norm_mix, mods, mods, mods, norm_mlp, mods, mods)


def kernel(x_prompt, x_sample, cache_k, cache_v, state_hgrn_fwd, state_hgrn_bwd, c, c_ctx, w_ada, b_ada,
           norm_mix, norm_mlp, w_in_ab, w_out_ab, q_norm, k_norm, hg_norm, lb_raw, w_pool, pool_scale,
           w_mlp_in, w_mlp_out, final_norm):
    xp = x_prompt.reshape(N_PROMPT, D_MODEL)
    xs = x_sample.reshape(N_SAMPLE, D_MODEL)
    cv = jnp.concatenate([c_ctx[None, :], c, jnp.zeros((ADA_ROWS - 1 - DEC_BATCH, D_MODEL), F32)], axis=0)
    mods = _ada_table(cv, w_ada, b_ada).reshape(DEPTH, ADA_ROWS, 1, N_MOD * D_MODEL)

    h0 = _modulate0(xp, xs, norm_mix, mods)
    proj, new_k, new_v = _in_projection(h0, w_in_ab[0], q_norm[0:1], k_norm[0:1])
    ctx_k = cache_k[:, 0].reshape(DEC_BATCH, PAST_LEN, KV_WIDTH)
    ctx_v = cache_v[:, 0].reshape(DEC_BATCH, PAST_LEN, KV_WIDTH)
    att = _attention(proj, BATCH, SEQ, 0, SEQ)
    att = _attention(proj, DEC_BATCH, DEC_SEQ, N_PROMPT, 256, ctx=(ctx_k, ctx_v), prev=att)
    consts = _hgrn_constants()
    hg_o, s_fwd, s_bwd = _hgrn(proj, lb_raw, hg_norm[0:1], consts, BATCH, SEQ, 0)
    s0 = jnp.stack([state_hgrn_fwd[:, 0], state_hgrn_bwd[:, 0]])
    (hg_o,) = _hgrn(proj, lb_raw, hg_norm[0:1], consts, DEC_BATCH, DEC_SEQ, N_PROMPT, s0=s0, prev=hg_o)
    x1, h2 = _out_projection(att, hg_o, w_out_ab[0], xp, xs, mods, norm_mlp[0:1])
    x2 = _mlp(h2, x1, w_mlp_in[0], w_mlp_out[0], mods, 0, 0, N_TILES)

    x3, h4 = _pool_mixer(x2, w_pool[0], pool_scale[0:1], norm_mix[1:2], norm_mlp[1:2], mods)
    fin = final_norm[None, :]
    y_prompt = _mlp(h4, x3, w_mlp_in[1], w_mlp_out[1], mods, 1, 0, N_PROMPT_TILES, final_norm=fin)
    y_sample = _mlp(h4, x3, w_mlp_in[1], w_mlp_out[1], mods, 1, N_PROMPT_TILES, N_TILES - N_PROMPT_TILES,
                    final_norm=fin)

    return (y_prompt.reshape(BATCH, SEQ, D_MODEL), y_sample.reshape(DEC_BATCH, DEC_SEQ, D_MODEL),
            new_k.reshape(BATCH, 1, SEQ, N_KV_HEADS, HEAD_DIM), new_v.reshape(BATCH, 1, SEQ, N_KV_HEADS, HEAD_DIM),
            s_fwd.reshape(BATCH, 1, HG_HEADS, HG_DK, HG_DV), s_bwd.reshape(BATCH, 1, HG_HEADS, HG_DK, HG_DV))
```

```python
import functools

import numpy as np
import jax
import jax.numpy as jnp
from jax import lax
from jax.experimental import pallas as pl
from jax.experimental.pallas import tpu as pltpu

F32 = jnp.float32
BF16 = jnp.bfloat16

D_MODEL = 2048
BATCH = 16
SEQ = 256
DEPTH = 2
DEC_BATCH = 2
DEC_SEQ = 1024
PAST_LEN = 256
GRID_W = 64
HEAD_DIM = 128
N_Q_HEADS = 8
N_KV_HEADS = 2
Q_PER_KV = N_Q_HEADS // N_KV_HEADS
ATT_WIDTH = N_Q_HEADS * HEAD_DIM
KV_WIDTH = N_KV_HEADS * HEAD_DIM
HG_HEADS = 8
HG_DK = 128
HG_DV = 128
HG_KW = HG_HEADS * HG_DK
HG_VW = HG_HEADS * HG_DV
IN_AB = ATT_WIDTH + 2 * KV_WIDTH + 3 * HG_KW + 2 * HG_VW
MIX_WIDTH = ATT_WIDTH + HG_VW
POOL_WINDOWS = (2, 4, 8, 16)
POOL_GROUP = D_MODEL // len(POOL_WINDOWS)
D_FF = 4 * D_MODEL
ROPE_THETA = 10000.0
ROPE_HALF = HEAD_DIM // 2
EPS = 1e-6
N_MOD = 6

N_PROMPT = BATCH * SEQ
N_SAMPLE = DEC_BATCH * DEC_SEQ
N_TOK = N_PROMPT + N_SAMPLE
ADA_ROWS = 16
TOK_TILE = 1024
N_TILES = N_TOK // TOK_TILE
N_PROMPT_TILES = N_PROMPT // TOK_TILE
PROJ_TOK_TILE = 2048
PROJ_COL_TILE = 512
OUT_COL_TILE = 512
FF_TILE = 512
MLP_STEPS = D_FF // FF_TILE
MLP_RES_COLS = D_MODEL // MLP_STEPS
MLP_OUT_CHUNK = 512
HG_CHUNK = 128
HG_LEVELS = 7
HG_HEAD_BLOCK = 4
VMEM_LIMIT = 56 * 2 ** 20

COL_Q = 0
COL_K = ATT_WIDTH
COL_V = COL_K + KV_WIDTH
COL_HQ = COL_V + KV_WIDTH
COL_ZF = COL_HQ + HG_KW
COL_HI = COL_ZF + 2 * HG_KW
COL_HG = COL_HI + HG_VW


def _params(semantics):
    return pltpu.CompilerParams(dimension_semantics=semantics, vmem_limit_bytes=VMEM_LIMIT)


def _sigmoid(x):
    return 1.0 / (1.0 + jnp.exp(-x))


def _silu(x):
    return x * _sigmoid(x)


def _rms(x, gain):
    return x * lax.rsqrt(jnp.mean(x * x, axis=-1, keepdims=True) + EPS) * gain


def _bdot(a, b):
    return jnp.dot(a, b, preferred_element_type=F32)


def _mod_row(tile, tile_rows):
    first = N_PROMPT // tile_rows
    per_seq = DEC_SEQ // tile_rows
    return jnp.where(tile < first, 0, 1 + (tile - first) // per_seq)


def _mod_spec(layer, chunk, tile_rows, width=D_MODEL, col=lambda *g: 0, tile_of=lambda *g: g[0]):
    per = D_MODEL // width
    return pl.BlockSpec((None, None, 1, width),
                        lambda *g: (layer, _mod_row(tile_of(*g), tile_rows), 0, chunk * per + col(*g)))


def _ada_kernel(cv_ref, w_ref, b_ref, o_ref):
    s = _silu(cv_ref[...]).astype(BF16)
    o_ref[...] = _bdot(s, w_ref[...].astype(BF16)) + b_ref[...]


def _ada_table(cv, w_ada, b_ada):
    tn = 1024
    n = N_MOD * D_MODEL
    return pl.pallas_call(
        _ada_kernel,
        out_shape=jax.ShapeDtypeStruct((DEPTH, ADA_ROWS, n), F32),
        grid=(DEPTH, n // tn),
        in_specs=[pl.BlockSpec((ADA_ROWS, D_MODEL), lambda l, j: (0, 0)),
                  pl.BlockSpec((None, D_MODEL, tn), lambda l, j: (l, 0, j)),
                  pl.BlockSpec((None, 1, tn), lambda l, j: (l, 0, j))],
        out_specs=pl.BlockSpec((None, ADA_ROWS, tn), lambda l, j: (l, 0, j)),
        compiler_params=_params(("arbitrary", "arbitrary")),
        name="ada_table",
    )(cv, w_ada, b_ada.reshape(DEPTH, 1, n))


def _modulate(x, gain, shift, scale):
    return _rms(x, gain) * (1.0 + scale) + shift


def _mod0_kernel(xp_ref, xs_ref, gain_ref, sh_ref, sc_ref, o_ref):
    i = pl.program_id(0)

    def run(x_ref):
        o_ref[...] = _modulate(x_ref[...], gain_ref[...], sh_ref[...], sc_ref[...]).astype(BF16)

    pl.when(i < N_PROMPT // 512)(lambda: run(xp_ref))
    pl.when(i >= N_PROMPT // 512)(lambda: run(xs_ref))


def _modulate0(xp, xs, norm_mix, mods):
    tm = 512
    first = N_PROMPT // tm
    return pl.pallas_call(
        _mod0_kernel,
        out_shape=jax.ShapeDtypeStruct((N_TOK, D_MODEL), BF16),
        grid=(N_TOK // tm,),
        in_specs=[pl.BlockSpec((tm, D_MODEL), lambda i: (jnp.minimum(i, first - 1), 0)),
                  pl.BlockSpec((tm, D_MODEL), lambda i: (jnp.maximum(i - first, 0), 0)),
                  pl.BlockSpec((1, D_MODEL), lambda i: (0, 0)),
                  _mod_spec(0, 0, tm), _mod_spec(0, 1, tm)],
        out_specs=pl.BlockSpec((tm, D_MODEL), lambda i: (i, 0)),
        compiler_params=_params(("arbitrary",)),
        name="modulate0",
    )(xp, xs, norm_mix[0:1], mods, mods)


def _rope(y, cos, sin_lo, sin_hi):
    return (y * cos + pltpu.roll(y, HEAD_DIM - ROPE_HALF // 2, 1) * sin_lo
            + pltpu.roll(y, ROPE_HALF // 2, 1) * sin_hi)


def _inproj_kernel(h_ref, w_ref, qg_ref, kg_ref, cos_ref, slo_ref, shi_ref, p_ref, nk_ref, nv_ref):
    i = pl.program_id(0)
    j = pl.program_id(1)
    latent = i >= N_PROMPT // PROJ_TOK_TILE
    acc = _bdot(h_ref[...], w_ref[...].astype(BF16))
    heads = PROJ_COL_TILE // HEAD_DIM
    seqs = PROJ_TOK_TILE // DEC_SEQ

    def head(x, hh):
        return x[:, hh * HEAD_DIM:(hh + 1) * HEAD_DIM]

    def store_rope(hh, y):
        for s in range(seqs):
            rows = slice(s * DEC_SEQ, (s + 1) * DEC_SEQ)
            p_ref[rows, hh * HEAD_DIM:(hh + 1) * HEAD_DIM] = _rope(
                y[rows], cos_ref[...], slo_ref[...], shi_ref[...])

    def store_plain(hh, y):
        p_ref[:, hh * HEAD_DIM:(hh + 1) * HEAD_DIM] = y

    @pl.when(j < COL_K // PROJ_COL_TILE)
    def _():
        for hh in range(heads):
            y = _rms(head(acc, hh), qg_ref[...])
            pl.when(latent)(functools.partial(store_rope, hh, y))
            pl.when(jnp.logical_not(latent))(functools.partial(store_plain, hh, y))

    @pl.when(j == COL_K // PROJ_COL_TILE)
    def _():
        for hh in range(N_KV_HEADS):
            y = _rms(head(acc, hh), kg_ref[...])
            pl.when(latent)(functools.partial(store_rope, hh, y))

            @pl.when(jnp.logical_not(latent))
            def _():
                store_plain(hh, y)
                nk_ref[:, hh * HEAD_DIM:(hh + 1) * HEAD_DIM] = y
        v = acc[:, KV_WIDTH:]
        p_ref[:, KV_WIDTH:] = v

        @pl.when(jnp.logical_not(latent))
        def _():
            nv_ref[...] = v

    is_silu = jnp.logical_or(
        jnp.logical_and(j >= COL_HQ // PROJ_COL_TILE, j < COL_ZF // PROJ_COL_TILE),
        j >= COL_HG // PROJ_COL_TILE)

    @pl.when(is_silu)
    def _():
        p_ref[...] = _silu(acc)

    @pl.when(jnp.logical_and(j >= COL_ZF // PROJ_COL_TILE, j < COL_HG // PROJ_COL_TILE))
    def _():
        p_ref[...] = acc


def _rope_tables():
    t = np.arange(DEC_SEQ)
    row = (t // GRID_W).astype(np.float32)
    col = (t % GRID_W).astype(np.float32)
    inv = (np.float32(ROPE_THETA) ** (-np.arange(0, ROPE_HALF, 2, dtype=np.float32) / np.float32(ROPE_HALF))).astype(np.float32)
    ar = row[:, None] * inv
    ac = col[:, None] * inv
    ang = np.concatenate([ar, ar, ac, ac], axis=-1).astype(np.float32)
    cos = np.cos(ang).astype(np.float32)
    sin = np.sin(ang).astype(np.float32)
    quarter = (np.arange(HEAD_DIM) // (ROPE_HALF // 2)) % 2
    sin_lo = np.where(quarter == 0, -sin, 0.0).astype(np.float32)
    sin_hi = np.where(quarter == 1, sin, 0.0).astype(np.float32)
    return jnp.asarray(cos), jnp.asarray(sin_lo), jnp.asarray(sin_hi)


def _in_projection(h, w_in, q_gain, k_gain):
    tm, tn = PROJ_TOK_TILE, PROJ_COL_TILE
    n_prompt_tiles = N_PROMPT // tm
    cos, sin_lo, sin_hi = _rope_tables()
    table = pl.BlockSpec((DEC_SEQ, HEAD_DIM), lambda i, j: (0, 0))
    gain = pl.BlockSpec((1, HEAD_DIM), lambda i, j: (0, 0))
    state = pl.BlockSpec((tm, KV_WIDTH), lambda i, j: (jnp.minimum(i, n_prompt_tiles - 1), 0))
    return pl.pallas_call(
        _inproj_kernel,
        out_shape=(jax.ShapeDtypeStruct((N_TOK, IN_AB), F32),
                   jax.ShapeDtypeStruct((N_PROMPT, KV_WIDTH), F32),
                   jax.ShapeDtypeStruct((N_PROMPT, KV_WIDTH), F32)),
        grid=(N_TOK // tm, IN_AB // tn),
        in_specs=[pl.BlockSpec((tm, D_MODEL), lambda i, j: (i, 0)),
                  pl.BlockSpec((D_MODEL, tn), lambda i, j: (0, j)),
                  gain, gain, table, table, table],
        out_specs=(pl.BlockSpec((tm, tn), lambda i, j: (i, j)), state, state),
        compiler_params=_params(("arbitrary", "arbitrary")),
        name="in_projection",
    )(h, w_in, q_gain, k_gain, cos, sin_lo, sin_hi)


def _attn_kernel(*refs, has_ctx):
    if has_ctx:
        q_ref, k_ref, v_ref, ck_ref, cv_ref, _, o_ref = refs
    else:
        q_ref, k_ref, v_ref, o_ref = refs
    scale = HEAD_DIM ** -0.5
    nt = (((1,), (1,)), ((), ()))
    k = k_ref[...].astype(BF16)
    v = v_ref[...].astype(BF16)
    if has_ctx:
        ck = ck_ref[...].astype(BF16)
        cv = cv_ref[...].astype(BF16)
    for g in range(Q_PER_KV):
        q = q_ref[:, g * HEAD_DIM:(g + 1) * HEAD_DIM].astype(BF16)
        s = lax.dot_general(q, k, nt, preferred_element_type=F32) * scale
        m = jnp.max(s, axis=-1, keepdims=True)
        if has_ctx:
            sc = lax.dot_general(q, ck, nt, preferred_element_type=F32) * scale
            m = jnp.maximum(m, jnp.max(sc, axis=-1, keepdims=True))
        p = jnp.exp(s - m)
        den = jnp.sum(p, axis=-1, keepdims=True)
        o = _bdot(p.astype(BF16), v)
        if has_ctx:
            pc = jnp.exp(sc - m)
            den = den + jnp.sum(pc, axis=-1, keepdims=True)
            o = o + _bdot(pc.astype(BF16), cv)
        o_ref[:, g * HEAD_DIM:(g + 1) * HEAD_DIM] = (o / den).astype(o_ref.dtype)


def _attention(p, n_batch, seq, row0, tq, ctx=None, prev=None):
    q_blocks = seq // tq
    gw = Q_PER_KV * HEAD_DIM
    in_specs = [
        pl.BlockSpec((tq, gw), lambda b, h, qi: (row0 // tq + b * q_blocks + qi, h)),
        pl.BlockSpec((seq, HEAD_DIM), lambda b, h, qi: (row0 // seq + b, COL_K // HEAD_DIM + h)),
        pl.BlockSpec((seq, HEAD_DIM), lambda b, h, qi: (row0 // seq + b, COL_V // HEAD_DIM + h)),
    ]
    args = [p, p, p]
    aliases = {}
    if ctx is not None:
        ctx_spec = pl.BlockSpec((None, PAST_LEN, HEAD_DIM), lambda b, h, qi: (b, 0, h))
        in_specs += [ctx_spec, ctx_spec, pl.BlockSpec(memory_space=pl.ANY)]
        args += [ctx[0], ctx[1], prev]
        aliases = {5: 0}
    return pl.pallas_call(
        functools.partial(_attn_kernel, has_ctx=ctx is not None),
        out_shape=jax.ShapeDtypeStruct((N_TOK, ATT_WIDTH), BF16),
        grid=(n_batch, N_KV_HEADS, q_blocks),
        in_specs=in_specs,
        out_specs=pl.BlockSpec((tq, gw), lambda b, h, qi: (row0 // tq + b * q_blocks + qi, h)),
        input_output_aliases=aliases,
        compiler_params=_params(("arbitrary", "arbitrary", "arbitrary")),
        name="attention_latent" if ctx is not None else "attention_prompt",
    )(*args)


def _hgrn_constants():
    c = HG_CHUNK
    t = np.arange(c)
    cums, sels, pairs = [], [], []
    for d in range(2):
        pos = t if d == 0 else c - 1 - t
        pu, pt = pos[None, :], pos[:, None]
        ms, ss, ws = [], [], []
        for l in range(HG_LEVELS):
            m = c >> l
            blk = pos // m
            mid = (blk * m + m // 2)[:, None]
            late = ((pos % m) >= m // 2)
            ms.append(np.where(late[:, None], (pu >= mid) & (pu <= pt), (pu > pt) & (pu < mid)))
            ss.append(np.broadcast_to(late[:, None], (c, c)))
            ws.append((blk[:, None] == blk[None, :]) & late[:, None] & ~late[None, :])
        ms.append(pu <= pt)
        ms.append(pu > pt)
        cums.append(np.concatenate(ms, axis=0))
        sels.append(np.stack(ss))
        pairs.append(np.stack(ws))
    return (jnp.asarray(np.stack(cums), BF16), jnp.asarray(np.stack(sels), F32),
            jnp.asarray(np.stack(pairs), F32))


def _hgrn_kernel(*refs, n_chunks, has_s0):
    hq_ref, zf_ref, zb_ref, hi_ref, hg_ref, lb_ref, og_ref, cum_ref, sel_ref, pair_ref = refs[:10]
    refs = refs[10:]
    if has_s0:
        s0f_ref, s0b_ref, _, o_ref = refs[:4]
        refs = refs[4:]
    else:
        o_ref, sf_ref, sb_ref = refs[:3]
        refs = refs[3:]
    st_ref, acc_ref = refs
    c = HG_CHUNK
    nt = (((1,), (1,)), ((), ()))
    tn = (((0,), (0,)), ((), ()))

    def run_direction(d, z_ref):
        raw = lb_ref[d]
        e = jnp.exp(raw - jnp.max(raw, axis=0, keepdims=True))
        lb = e[0:1] / jnp.sum(e, axis=0, keepdims=True)
        for hh in range(HG_HEAD_BLOCK):
            if has_s0:
                st_ref[hh] = (s0f_ref, s0b_ref)[d][hh].T
            else:
                st_ref[hh] = jnp.zeros((HG_DV, HG_DK), F32)
        cum = cum_ref[d]

        def chunk(ci, carry):
            cidx = ci if d == 0 else n_chunks - 1 - ci
            rows = pl.ds(pl.multiple_of(cidx * c, c), c)
            f = lb + (1.0 - lb) * _sigmoid(z_ref[rows, :])
            logf = jnp.log(f)
            hi16 = logf.astype(BF16)
            lo16 = (logf - hi16.astype(F32)).astype(BF16)
            eall = jnp.exp(_bdot(cum, hi16) + _bdot(cum, lo16))
            for hh in range(HG_HEAD_BLOCK):
                cols = slice(hh * HG_DK, (hh + 1) * HG_DK)
                kk = 1.0 - f[:, cols]
                q = hq_ref[rows, cols]
                iv = hi_ref[rows, cols]
                iv16 = iv.astype(BF16)
                att = jnp.zeros((c, c), F32)
                for l in range(HG_LEVELS):
                    x = (kk + sel_ref[d, l] * (q - kk)) * eall[l * c:(l + 1) * c, cols]
                    xb = x.astype(BF16)
                    att = att + pair_ref[d, l] * lax.dot_general(xb, xb, nt, preferred_element_type=F32)
                e_in = eall[HG_LEVELS * c:(HG_LEVELS + 1) * c, cols]
                e_out = eall[(HG_LEVELS + 1) * c:, cols]
                st = st_ref[hh]
                o = (_bdot(att.astype(BF16), iv16)
                     + jnp.sum(q * kk, axis=-1, keepdims=True) * iv
                     + lax.dot_general((q * e_in).astype(BF16), st.astype(BF16), nt, preferred_element_type=F32))
                dst = lax.dot_general(iv16, (kk * e_out).astype(BF16), tn, preferred_element_type=F32)
                st_ref[hh] = (e_in[0:1] * e_out[0:1]) * st + dst
                if d == 0:
                    acc_ref[rows, cols] = o
                else:
                    tot = acc_ref[rows, cols] + o
                    o_ref[rows, cols] = (_rms(tot, og_ref[...]) * hg_ref[rows, cols]).astype(o_ref.dtype)
            return carry

        lax.fori_loop(0, n_chunks, chunk, 0)
        if not has_s0:
            for hh in range(HG_HEAD_BLOCK):
                (sf_ref, sb_ref)[d][hh] = st_ref[hh].T

    run_direction(0, zf_ref)
    run_direction(1, zb_ref)


def _hgrn(p, lb_raw, o_gain, consts, n_batch, seq, row0, s0=None, prev=None):
    hb = HG_HEAD_BLOCK
    w = hb * HG_DK
    cum, sel, pair = consts

    def seg(col):
        return pl.BlockSpec((seq, w), lambda b, h: (row0 // seq + b, col // w + h))

    def const(a):
        return pl.BlockSpec(a.shape, lambda b, h: (0,) * a.ndim)

    in_specs = [seg(COL_HQ), seg(COL_ZF), seg(COL_ZF + HG_KW), seg(COL_HI), seg(COL_HG),
                pl.BlockSpec((2, DEPTH + 1, w), lambda b, h: (0, 0, h)),
                pl.BlockSpec((1, HG_DV), lambda b, h: (0, 0)),
                const(cum), const(sel), const(pair)]
    args = [p, p, p, p, p, lb_raw, o_gain, cum, sel, pair]
    aliases = {}
    has_s0 = s0 is not None
    st_spec = pl.BlockSpec((None, hb, HG_DK, HG_DV), lambda b, h: (b, h, 0, 0))
    if has_s0:
        in_specs += [st_spec, st_spec, pl.BlockSpec(memory_space=pl.ANY)]
        args += [s0[0], s0[1], prev]
        aliases = {len(args) - 1: 0}
    out_shape = [jax.ShapeDtypeStruct((N_TOK, HG_VW), BF16)]
    out_specs = [pl.BlockSpec((seq, w), lambda b, h: (row0 // seq + b, h))]
    if not has_s0:
        st_shape = jax.ShapeDtypeStruct((n_batch, HG_HEADS, HG_DK, HG_DV), F32)
        out_shape += [st_shape, st_shape]
        out_specs += [st_spec, st_spec]
    return pl.pallas_call(
        functools.partial(_hgrn_kernel, n_chunks=seq // HG_CHUNK, has_s0=has_s0),
        out_shape=tuple(out_shape),
        grid=(n_batch, HG_HEADS // hb),
        in_specs=in_specs,
        out_specs=tuple(out_specs),
        scratch_shapes=[pltpu.VMEM((hb, HG_DV, HG_DK), F32), pltpu.VMEM((seq, w), F32)],
        input_output_aliases=aliases,
        compiler_params=_params(("arbitrary", "arbitrary")),
        name="hgrn_latent" if has_s0 else "hgrn_prompt",
    )(*args)


def _outproj_kernel(att_ref, o_ref, wa_ref, wb_ref, xp_ref, xs_ref, g1_ref, gain_ref, sh_ref, sc_ref,
                    x1_ref, h2_ref, full_ref):
    i = pl.program_id(0)
    n = pl.program_id(1)
    acc = _bdot(att_ref[...], wa_ref[...].astype(BF16)) + _bdot(o_ref[...], wb_ref[...].astype(BF16))

    def run(x_ref):
        x1 = x_ref[...] + g1_ref[...] * acc
        x1_ref[...] = x1
        full_ref[n] = x1

    pl.when(i < N_PROMPT_TILES)(lambda: run(xp_ref))
    pl.when(i >= N_PROMPT_TILES)(lambda: run(xs_ref))

    @pl.when(n == D_MODEL // OUT_COL_TILE - 1)
    def _():
        nt = D_MODEL // OUT_COL_TILE
        ss = full_ref[0] * full_ref[0]
        ms = jnp.sum(ss, axis=-1, keepdims=True)
        for k in range(1, nt):
            ms = ms + jnp.sum(full_ref[k] * full_ref[k], axis=-1, keepdims=True)
        rstd = lax.rsqrt(ms / D_MODEL + EPS)
        for k in range(nt):
            cols = slice(k * OUT_COL_TILE, (k + 1) * OUT_COL_TILE)
            y = full_ref[k] * rstd * gain_ref[:, cols]
            h2_ref[:, cols] = (y * (1.0 + sc_ref[:, cols]) + sh_ref[:, cols]).astype(BF16)


def _out_projection(att, o, w_out, xp, xs, mods, norm_mlp):
    tm, tn = TOK_TILE, OUT_COL_TILE
    nt = D_MODEL // tn
    return pl.pallas_call(
        _outproj_kernel,
        out_shape=(jax.ShapeDtypeStruct((N_TOK, D_MODEL), F32), jax.ShapeDtypeStruct((N_TOK, D_MODEL), BF16)),
        grid=(N_TILES, nt),
        in_specs=[pl.BlockSpec((tm, ATT_WIDTH), lambda i, n: (i, 0)),
                  pl.BlockSpec((tm, HG_VW), lambda i, n: (i, 0)),
                  pl.BlockSpec((ATT_WIDTH, tn), lambda i, n: (0, n)),
                  pl.BlockSpec((HG_VW, tn), lambda i, n: (1, n)),
                  pl.BlockSpec((tm, tn), lambda i, n: (jnp.minimum(i, N_PROMPT_TILES - 1), n)),
                  pl.BlockSpec((tm, tn), lambda i, n: (jnp.maximum(i - N_PROMPT_TILES, 0), n)),
                  _mod_spec(0, 2, tm, width=tn, col=lambda i, n: n),
                  pl.BlockSpec((1, D_MODEL), lambda i, n: (0, 0)),
                  _mod_spec(0, 3, tm), _mod_spec(0, 4, tm)],
        out_specs=(pl.BlockSpec((tm, tn), lambda i, n: (i, n)),
                   pl.BlockSpec((tm, D_MODEL), lambda i, n: (i, 0))),
        scratch_shapes=[pltpu.VMEM((nt, tm, tn), F32)],
        compiler_params=_params(("arbitrary", "arbitrary")),
        name="out_projection",
    )(att, o, w_out, w_out, xp, xs, mods, norm_mlp, mods, mods)


def _mlp_kernel(h_ref, w1_ref, w2_ref, x_ref, g2_ref, *rest, final):
    if final:
        fin_ref, o_ref, res_ref = rest
    else:
        o_ref, res_ref = rest
    j = pl.program_id(1)
    res_ref[j] = x_ref[...]

    @pl.when(j == 0)
    def _():
        o_ref[...] = jnp.zeros(o_ref.shape, F32)

    a = jnp.square(jnp.maximum(_bdot(h_ref[...], w1_ref[...].astype(BF16)), 0.0)).astype(BF16)
    for n in range(D_MODEL // MLP_OUT_CHUNK):
        cols = slice(n * MLP_OUT_CHUNK, (n + 1) * MLP_OUT_CHUNK)
        o_ref[:, cols] += _bdot(a, w2_ref[:, cols].astype(BF16))

    @pl.when(j == MLP_STEPS - 1)
    def _():
        for k in range(MLP_STEPS):
            cols = slice(k * MLP_RES_COLS, (k + 1) * MLP_RES_COLS)
            o_ref[:, cols] = res_ref[k] + g2_ref[:, cols] * o_ref[:, cols]
        if final:
            o_ref[...] = _rms(o_ref[...], fin_ref[...])


def _mlp(h, x, w1, w2, mods, layer, tile0, n_tiles, final_norm=None):
    tm, th = TOK_TILE, FF_TILE
    final = final_norm is not None
    in_specs = [pl.BlockSpec((tm, D_MODEL), lambda i, j: (tile0 + i, 0), pipeline_mode=pl.Buffered(1)),
                pl.BlockSpec((None, D_MODEL, th), lambda i, j: (layer, 0, j)),
                pl.BlockSpec((None, th, D_MODEL), lambda i, j: (layer, j, 0)),
                pl.BlockSpec((tm, MLP_RES_COLS), lambda i, j: (tile0 + i, j)),
                _mod_spec(layer, 5, tm, tile_of=lambda i, j: tile0 + i)]
    args = [h, w1, w2, x, mods]
    if final:
        in_specs.append(pl.BlockSpec((1, D_MODEL), lambda i, j: (0, 0)))
        args.append(final_norm)
    return pl.pallas_call(
        functools.partial(_mlp_kernel, final=final),
        out_shape=jax.ShapeDtypeStruct((n_tiles * tm, D_MODEL), F32),
        grid=(n_tiles, MLP_STEPS),
        in_specs=in_specs,
        out_specs=pl.BlockSpec((tm, D_MODEL), lambda i, j: (i, 0)),
        scratch_shapes=[pltpu.VMEM((MLP_STEPS, tm, MLP_RES_COLS), F32)],
        compiler_params=_params(("arbitrary", "arbitrary")),
        name="mlp_final" if final else "mlp",
    )(*args)


def _pool_kernel(x_ref, wp_ref, ps_ref, gain1_ref, sh1_ref, sc1_ref, g1_ref, gain2_ref, sh2_ref, sc2_ref,
                 x3_ref, h4_ref):
    i = pl.program_id(0)
    tm = TOK_TILE
    seq = jnp.where(i < N_PROMPT_TILES, SEQ, DEC_SEQ)
    pos = lax.broadcasted_iota(jnp.int32, (tm, POOL_GROUP), 0) & (seq - 1)

    def shifted(a, off):
        r = pltpu.roll(a, (-off) % tm, 0)
        return jnp.where(jnp.logical_and(pos + off >= 0, pos + off < seq), r, 0.0)

    for g, w in enumerate(POOL_WINDOWS):
        cols = slice(g * POOL_GROUP, (g + 1) * POOL_GROUP)
        x = x_ref[:, cols]
        ms = jnp.mean(x_ref[...] * x_ref[...], axis=-1, keepdims=True) if g == 0 else ms
        h = x * lax.rsqrt(ms + EPS) * gain1_ref[:, cols] * (1.0 + sc1_ref[:, cols]) + sh1_ref[:, cols]
        back = h
        fwd = h
        m = 1
        while m < w // 2:
            back = back + shifted(back, -m)
            fwd = fwd + shifted(fwd, m)
            m *= 2
        total = shifted(back, -1) + fwd
        count = (jnp.minimum(pos + (w - w // 2), seq) - jnp.maximum(pos - w // 2, 0)).astype(F32)
        pooled = (total / count - h).astype(BF16)
        mix = _bdot(pooled, wp_ref[g].astype(BF16)) * ps_ref[:, cols]
        x3_ref[:, cols] = x + g1_ref[:, cols] * mix
    x3 = x3_ref[...]
    h4_ref[...] = _modulate(x3, gain2_ref[...], sh2_ref[...], sc2_ref[...]).astype(BF16)


def _pool_mixer(x, w_pool, pool_scale, norm_mix, norm_mlp, mods):
    tm = TOK_TILE
    vec = pl.BlockSpec((1, D_MODEL), lambda i: (0, 0))
    tile = pl.BlockSpec((tm, D_MODEL), lambda i: (i, 0))
    return pl.pallas_call(
        _pool_kernel,
        out_shape=(jax.ShapeDtypeStruct((N_TOK, D_MODEL), F32), jax.ShapeDtypeStruct((N_TOK, D_MODEL), BF16)),
        grid=(N_TILES,),
        in_specs=[tile, pl.BlockSpec((len(POOL_WINDOWS), POOL_GROUP, POOL_GROUP), lambda i: (0, 0, 0)),
                  vec, vec, _mod_spec(1, 0, tm), _mod_spec(1, 1, tm), _mod_spec(1, 2, tm),
                  vec, _mod_spec(1, 3, tm), _mod_spec(1, 4, tm)],
        out_specs=(tile, tile),
        compiler_params=_params(("arbitrary",)),
        name="pool_mixer",
    )(x, w_pool, pool_scale, norm_mix, mods, mods, mods, norm_mlp, mods, mods)


def kernel(x_prompt, x_sample, cache_k, cache_v, state_hgrn_fwd, state_hgrn_bwd, c, c_ctx, w_ada, b_ada,
           norm_mix, norm_mlp, w_in_ab, w_out_ab, q_norm, k_norm, hg_norm, lb_raw, w_pool, pool_scale,
           w_mlp_in, w_mlp_out, final_norm):
    xp = x_prompt.reshape(N_PROMPT, D_MODEL)
    xs = x_sample.reshape(N_SAMPLE, D_MODEL)
    cv = jnp.concatenate([c_ctx[None, :], c, jnp.zeros((ADA_ROWS - 1 - DEC_BATCH, D_MODEL), F32)], axis=0)
    mods = _ada_table(cv, w_ada, b_ada).reshape(DEPTH, ADA_ROWS, 1, N_MOD * D_MODEL)

    h0 = _modulate0(xp, xs, norm_mix, mods)
    proj, new_k, new_v = _in_projection(h0, w_in_ab[0], q_norm[0:1], k_norm[0:1])
    ctx_k = cache_k[:, 0].reshape(DEC_BATCH, PAST_LEN, KV_WIDTH)
    ctx_v = cache_v[:, 0].reshape(DEC_BATCH, PAST_LEN, KV_WIDTH)
    att = _attention(proj, BATCH, SEQ, 0, SEQ)
    att = _attention(proj, DEC_BATCH, DEC_SEQ, N_PROMPT, 256, ctx=(ctx_k, ctx_v), prev=att)
    consts = _hgrn_constants()
    hg_o, s_fwd, s_bwd = _hgrn(proj, lb_raw, hg_norm[0:1], consts, BATCH, SEQ, 0)
    s0 = (state_hgrn_fwd.reshape(DEC_BATCH, HG_HEADS, HG_DK, HG_DV),
          state_hgrn_bwd.reshape(DEC_BATCH, HG_HEADS, HG_DK, HG_DV))
    (hg_o,) = _hgrn(proj, lb_raw, hg_norm[0:1], consts, DEC_BATCH, DEC_SEQ, N_PROMPT, s0=s0, prev=hg_o)
    x1, h2 = _out_projection(att, hg_o, w_out_ab[0], xp, xs, mods, norm_mlp[0:1])
    x2 = _mlp(h2, x1, w_mlp_in, w_mlp_out, mods, 0, 0, N_TILES)

    x3, h4 = _pool_mixer(x2, w_pool[0], pool_scale[0:1], norm_mix[1:2], norm_mlp[1:2], mods)
    fin = final_norm[None, :]
    y_prompt = _mlp(h4, x3, w_mlp_in, w_mlp_out, mods, 1, 0, N_PROMPT_TILES, final_norm=fin)
    y_sample = _mlp(h4, x3, w_mlp_in, w_mlp_out, mods, 1, N_PROMPT_TILES, N_TILES - N_PROMPT_TILES,
                    final_norm=fin)

    return (y_prompt.reshape(BATCH, SEQ, D_MODEL), y_sample.reshape(DEC_BATCH, DEC_SEQ, D_MODEL),
            new_k.reshape(BATCH, 1, SEQ, N_KV_HEADS, HEAD_DIM), new_v.reshape(BATCH, 1, SEQ, N_KV_HEADS, HEAD_DIM),
            s_fwd.reshape(BATCH, 1, HG_HEADS, HG_DK, HG_DV), s_bwd.reshape(BATCH, 1, HG_HEADS, HG_DK, HG_DV))
```

```python
import functools

import numpy as np
import jax
import jax.numpy as jnp
from jax import lax
from jax.experimental import pallas as pl
from jax.experimental.pallas import tpu as pltpu

F32 = jnp.float32
BF16 = jnp.bfloat16

D_MODEL = 2048
BATCH = 16
SEQ = 256
DEPTH = 2
DEC_BATCH = 2
DEC_SEQ = 1024
PAST_LEN = 256
GRID_W = 64
HEAD_DIM = 128
N_Q_HEADS = 8
N_KV_HEADS = 2
Q_PER_KV = N_Q_HEADS // N_KV_HEADS
ATT_WIDTH = N_Q_HEADS * HEAD_DIM
KV_WIDTH = N_KV_HEADS * HEAD_DIM
HG_HEADS = 8
HG_DK = 128
HG_DV = 128
HG_KW = HG_HEADS * HG_DK
HG_VW = HG_HEADS * HG_DV
IN_AB = ATT_WIDTH + 2 * KV_WIDTH + 3 * HG_KW + 2 * HG_VW
MIX_WIDTH = ATT_WIDTH + HG_VW
POOL_WINDOWS = (2, 4, 8, 16)
POOL_GROUP = D_MODEL // len(POOL_WINDOWS)
D_FF = 4 * D_MODEL
ROPE_THETA = 10000.0
ROPE_HALF = HEAD_DIM // 2
EPS = 1e-6
N_MOD = 6

N_PROMPT = BATCH * SEQ
N_SAMPLE = DEC_BATCH * DEC_SEQ
N_TOK = N_PROMPT + N_SAMPLE
ADA_ROWS = 16
TOK_TILE = 1024
N_TILES = N_TOK // TOK_TILE
N_PROMPT_TILES = N_PROMPT // TOK_TILE
PROJ_TOK_TILE = 2048
PROJ_COL_TILE = 512
OUT_COL_TILE = 512
FF_TILE = 512
MLP_STEPS = D_FF // FF_TILE
MLP_RES_COLS = D_MODEL // MLP_STEPS
MLP_OUT_CHUNK = 512
HG_CHUNK = 128
HG_LEVELS = 7
HG_HEAD_BLOCK = 4
HG_SAFE_EXPONENT = 80.0
VMEM_LIMIT = 56 * 2 ** 20

COL_Q = 0
COL_K = ATT_WIDTH
COL_V = COL_K + KV_WIDTH
COL_HQ = COL_V + KV_WIDTH
COL_ZF = COL_HQ + HG_KW
COL_HI = COL_ZF + 2 * HG_KW
COL_HG = COL_HI + HG_VW


def _params(semantics):
    return pltpu.CompilerParams(dimension_semantics=semantics, vmem_limit_bytes=VMEM_LIMIT)


def _sigmoid(x):
    return 1.0 / (1.0 + jnp.exp(-x))


def _silu(x):
    return x * _sigmoid(x)


def _rms(x, gain):
    return x * lax.rsqrt(jnp.mean(x * x, axis=-1, keepdims=True) + EPS) * gain


def _bdot(a, b):
    return jnp.dot(a, b, preferred_element_type=F32)


def _mod_row(tile, tile_rows):
    first = N_PROMPT // tile_rows
    per_seq = DEC_SEQ // tile_rows
    return jnp.where(tile < first, 0, 1 + (tile - first) // per_seq)


def _mod_spec(layer, chunk, tile_rows, width=D_MODEL, col=lambda *g: 0, tile_of=lambda *g: g[0]):
    per = D_MODEL // width
    return pl.BlockSpec((None, None, 1, width),
                        lambda *g: (layer, _mod_row(tile_of(*g), tile_rows), 0, chunk * per + col(*g)))


def _ada_kernel(cv_ref, w_ref, b_ref, o_ref):
    s = _silu(cv_ref[...]).astype(BF16)
    o_ref[...] = _bdot(s, w_ref[...].astype(BF16)) + b_ref[...]


def _ada_table(cv, w_ada, b_ada):
    tn = 1024
    n = N_MOD * D_MODEL
    return pl.pallas_call(
        _ada_kernel,
        out_shape=jax.ShapeDtypeStruct((DEPTH, ADA_ROWS, n), F32),
        grid=(DEPTH, n // tn),
        in_specs=[pl.BlockSpec((ADA_ROWS, D_MODEL), lambda l, j: (0, 0)),
                  pl.BlockSpec((None, D_MODEL, tn), lambda l, j: (l, 0, j)),
                  pl.BlockSpec((None, 1, tn), lambda l, j: (l, 0, j))],
        out_specs=pl.BlockSpec((None, ADA_ROWS, tn), lambda l, j: (l, 0, j)),
        compiler_params=_params(("arbitrary", "arbitrary")),
        name="ada_table",
    )(cv, w_ada, b_ada.reshape(DEPTH, 1, n))


def _modulate(x, gain, shift, scale):
    return _rms(x, gain) * (1.0 + scale) + shift


def _mod0_kernel(xp_ref, xs_ref, gain_ref, sh_ref, sc_ref, o_ref):
    i = pl.program_id(0)

    def run(x_ref):
        o_ref[...] = _modulate(x_ref[...], gain_ref[...], sh_ref[...], sc_ref[...]).astype(BF16)

    pl.when(i < N_PROMPT // 512)(lambda: run(xp_ref))
    pl.when(i >= N_PROMPT // 512)(lambda: run(xs_ref))


def _modulate0(xp, xs, norm_mix, mods):
    tm = 512
    first = N_PROMPT // tm
    return pl.pallas_call(
        _mod0_kernel,
        out_shape=jax.ShapeDtypeStruct((N_TOK, D_MODEL), BF16),
        grid=(N_TOK // tm,),
        in_specs=[pl.BlockSpec((tm, D_MODEL), lambda i: (jnp.minimum(i, first - 1), 0)),
                  pl.BlockSpec((tm, D_MODEL), lambda i: (jnp.maximum(i - first, 0), 0)),
                  pl.BlockSpec((1, D_MODEL), lambda i: (0, 0)),
                  _mod_spec(0, 0, tm), _mod_spec(0, 1, tm)],
        out_specs=pl.BlockSpec((tm, D_MODEL), lambda i: (i, 0)),
        compiler_params=_params(("arbitrary",)),
        name="modulate0",
    )(xp, xs, norm_mix[0:1], mods, mods)


def _rope(y, cos, sin_lo, sin_hi):
    return (y * cos + pltpu.roll(y, HEAD_DIM - ROPE_HALF // 2, 1) * sin_lo
            + pltpu.roll(y, ROPE_HALF // 2, 1) * sin_hi)


def _inproj_kernel(h_ref, w_ref, qg_ref, kg_ref, cos_ref, slo_ref, shi_ref, p_ref, nk_ref, nv_ref):
    i = pl.program_id(0)
    j = pl.program_id(1)
    latent = i >= N_PROMPT // PROJ_TOK_TILE
    acc = _bdot(h_ref[...], w_ref[...].astype(BF16))
    heads = PROJ_COL_TILE // HEAD_DIM
    seqs = PROJ_TOK_TILE // DEC_SEQ

    def head(x, hh):
        return x[:, hh * HEAD_DIM:(hh + 1) * HEAD_DIM]

    def store_rope(hh, y):
        for s in range(seqs):
            rows = slice(s * DEC_SEQ, (s + 1) * DEC_SEQ)
            p_ref[rows, hh * HEAD_DIM:(hh + 1) * HEAD_DIM] = _rope(
                y[rows], cos_ref[...], slo_ref[...], shi_ref[...])

    def store_plain(hh, y):
        p_ref[:, hh * HEAD_DIM:(hh + 1) * HEAD_DIM] = y

    @pl.when(j < COL_K // PROJ_COL_TILE)
    def _():
        for hh in range(heads):
            y = _rms(head(acc, hh), qg_ref[...])
            pl.when(latent)(functools.partial(store_rope, hh, y))
            pl.when(jnp.logical_not(latent))(functools.partial(store_plain, hh, y))

    @pl.when(j == COL_K // PROJ_COL_TILE)
    def _():
        for hh in range(N_KV_HEADS):
            y = _rms(head(acc, hh), kg_ref[...])
            pl.when(latent)(functools.partial(store_rope, hh, y))

            @pl.when(jnp.logical_not(latent))
            def _():
                store_plain(hh, y)
                nk_ref[:, hh * HEAD_DIM:(hh + 1) * HEAD_DIM] = y
        v = acc[:, KV_WIDTH:]
        p_ref[:, KV_WIDTH:] = v

        @pl.when(jnp.logical_not(latent))
        def _():
            nv_ref[...] = v

    is_silu = jnp.logical_or(
        jnp.logical_and(j >= COL_HQ // PROJ_COL_TILE, j < COL_ZF // PROJ_COL_TILE),
        j >= COL_HG // PROJ_COL_TILE)

    @pl.when(is_silu)
    def _():
        p_ref[...] = _silu(acc)

    @pl.when(jnp.logical_and(j >= COL_ZF // PROJ_COL_TILE, j < COL_HG // PROJ_COL_TILE))
    def _():
        p_ref[...] = acc


def _rope_tables():
    t = np.arange(DEC_SEQ)
    row = (t // GRID_W).astype(np.float32)
    col = (t % GRID_W).astype(np.float32)
    inv = (np.float32(ROPE_THETA) ** (-np.arange(0, ROPE_HALF, 2, dtype=np.float32) / np.float32(ROPE_HALF))).astype(np.float32)
    ar = row[:, None] * inv
    ac = col[:, None] * inv
    ang = np.concatenate([ar, ar, ac, ac], axis=-1).astype(np.float32)
    cos = np.cos(ang).astype(np.float32)
    sin = np.sin(ang).astype(np.float32)
    quarter = (np.arange(HEAD_DIM) // (ROPE_HALF // 2)) % 2
    sin_lo = np.where(quarter == 0, -sin, 0.0).astype(np.float32)
    sin_hi = np.where(quarter == 1, sin, 0.0).astype(np.float32)
    return jnp.asarray(cos), jnp.asarray(sin_lo), jnp.asarray(sin_hi)


def _in_projection(h, w_in, q_gain, k_gain):
    tm, tn = PROJ_TOK_TILE, PROJ_COL_TILE
    n_prompt_tiles = N_PROMPT // tm
    cos, sin_lo, sin_hi = _rope_tables()
    table = pl.BlockSpec((DEC_SEQ, HEAD_DIM), lambda i, j: (0, 0))
    gain = pl.BlockSpec((1, HEAD_DIM), lambda i, j: (0, 0))
    state = pl.BlockSpec((tm, KV_WIDTH), lambda i, j: (jnp.minimum(i, n_prompt_tiles - 1), 0))
    return pl.pallas_call(
        _inproj_kernel,
        out_shape=(jax.ShapeDtypeStruct((N_TOK, IN_AB), F32),
                   jax.ShapeDtypeStruct((N_PROMPT, KV_WIDTH), F32),
                   jax.ShapeDtypeStruct((N_PROMPT, KV_WIDTH), F32)),
        grid=(N_TOK // tm, IN_AB // tn),
        in_specs=[pl.BlockSpec((tm, D_MODEL), lambda i, j: (i, 0)),
                  pl.BlockSpec((D_MODEL, tn), lambda i, j: (0, j)),
                  gain, gain, table, table, table],
        out_specs=(pl.BlockSpec((tm, tn), lambda i, j: (i, j)), state, state),
        compiler_params=_params(("arbitrary", "arbitrary")),
        name="in_projection",
    )(h, w_in, q_gain, k_gain, cos, sin_lo, sin_hi)


def _attn_kernel(*refs, has_ctx):
    if has_ctx:
        q_ref, k_ref, v_ref, ck_ref, cv_ref, o_ref = refs
    else:
        q_ref, k_ref, v_ref, o_ref = refs
    scale = HEAD_DIM ** -0.5
    nt = (((1,), (1,)), ((), ()))
    k = k_ref[...].astype(BF16)
    v = v_ref[...].astype(BF16)
    if has_ctx:
        ck = ck_ref[...].astype(BF16)
        cv = cv_ref[...].astype(BF16)
    for g in range(Q_PER_KV):
        q = q_ref[:, g * HEAD_DIM:(g + 1) * HEAD_DIM].astype(BF16)
        s = lax.dot_general(q, k, nt, preferred_element_type=F32) * scale
        m = jnp.max(s, axis=-1, keepdims=True)
        if has_ctx:
            sc = lax.dot_general(q, ck, nt, preferred_element_type=F32) * scale
            m = jnp.maximum(m, jnp.max(sc, axis=-1, keepdims=True))
        p = jnp.exp(s - m)
        den = jnp.sum(p, axis=-1, keepdims=True)
        o = _bdot(p.astype(BF16), v)
        if has_ctx:
            pc = jnp.exp(sc - m)
            den = den + jnp.sum(pc, axis=-1, keepdims=True)
            o = o + _bdot(pc.astype(BF16), cv)
        o_ref[:, g * HEAD_DIM:(g + 1) * HEAD_DIM] = (o / den).astype(o_ref.dtype)


def _attention(p, n_batch, seq, row0, tq, ctx=None):
    q_blocks = seq // tq
    gw = Q_PER_KV * HEAD_DIM
    in_specs = [
        pl.BlockSpec((tq, gw), lambda b, h, qi: (row0 // tq + b * q_blocks + qi, h)),
        pl.BlockSpec((seq, HEAD_DIM), lambda b, h, qi: (row0 // seq + b, COL_K // HEAD_DIM + h)),
        pl.BlockSpec((seq, HEAD_DIM), lambda b, h, qi: (row0 // seq + b, COL_V // HEAD_DIM + h)),
    ]
    args = [p, p, p]
    if ctx is not None:
        ctx_spec = pl.BlockSpec((None, PAST_LEN, HEAD_DIM), lambda b, h, qi: (b, 0, h))
        in_specs += [ctx_spec, ctx_spec]
        args += [ctx[0], ctx[1]]
    return pl.pallas_call(
        functools.partial(_attn_kernel, has_ctx=ctx is not None),
        out_shape=jax.ShapeDtypeStruct((n_batch * seq, ATT_WIDTH), BF16),
        grid=(n_batch, N_KV_HEADS, q_blocks),
        in_specs=in_specs,
        out_specs=pl.BlockSpec((tq, gw), lambda b, h, qi: (b * q_blocks + qi, h)),
        compiler_params=_params(("arbitrary", "arbitrary", "arbitrary")),
        name="attention_latent" if ctx is not None else "attention_prompt",
    )(*args)


def _hgrn_constants():
    c = HG_CHUNK
    t = np.arange(c)
    cums, sels, pairs, scans, diags = [], [], [], [], []
    for d in range(2):
        pos = t if d == 0 else c - 1 - t
        pu, pt = pos[None, :], pos[:, None]
        ms, ss, ws = [], [], []
        for l in range(HG_LEVELS):
            m = c >> l
            blk = pos // m
            mid = (blk * m + m // 2)[:, None]
            late = ((pos % m) >= m // 2)
            ms.append(np.where(late[:, None], (pu >= mid) & (pu <= pt), (pu > pt) & (pu < mid)))
            ss.append(np.broadcast_to(late[:, None], (c, c)))
            ws.append((blk[:, None] == blk[None, :]) & late[:, None] & ~late[None, :])
        ms.append(pu <= pt)
        ms.append(pu > pt)
        cums.append(np.concatenate(ms, axis=0))
        sels.append(np.stack(ss))
        pairs.append(np.stack(ws))
        scans.append(np.concatenate([pu <= pt, pu <= pt], axis=1))
        diags.append(((pos // (c // 2))[:, None] == (pos // (c // 2))[None, :]) & (pu <= pt))
    return (jnp.asarray(np.stack(cums), BF16), jnp.asarray(np.stack(sels), F32),
            jnp.asarray(np.stack(pairs), F32), jnp.asarray(np.stack(scans), BF16),
            jnp.asarray(np.stack(diags), F32))


def _hgrn_kernel(*refs, n_chunks, has_s0):
    (hq_ref, zf_ref, zb_ref, hi_ref, hg_ref, lb_ref, og_ref, cum_ref, sel_ref, pair_ref, scan_ref,
     diag_ref) = refs[:12]
    refs = refs[12:]
    if has_s0:
        s0f_ref, s0b_ref, o_ref = refs[:3]
        refs = refs[3:]
    else:
        o_ref, sf_ref, sb_ref = refs[:3]
        refs = refs[3:]
    st_ref, acc_ref = refs
    c = HG_CHUNK
    half = c // 2
    nt = (((1,), (1,)), ((), ()))
    tn = (((0,), (0,)), ((), ()))

    def run_direction(d, z_ref):
        raw = lb_ref[d]
        e = jnp.exp(raw - jnp.max(raw, axis=0, keepdims=True))
        lb = e[0:1] / jnp.sum(e, axis=0, keepdims=True)

        def initial_state(hh):
            if has_s0:
                return (s0f_ref, s0b_ref)[d][hh].T
            return jnp.zeros((HG_DV, HG_DK), F32)

        def gates(rows):
            f = lb + (1.0 - lb) * _sigmoid(z_ref[rows, :])
            logf = jnp.log(f)
            hi16 = logf.astype(BF16)
            lo16 = (logf - hi16.astype(F32)).astype(BF16)
            return f, hi16, lo16

        def row(p):
            t = p if d == 0 else c - 1 - p
            return slice(t, t + 1)

        early, late = (slice(0, half), slice(half, c)) if d == 0 else (slice(half, c), slice(0, half))

        def in_row_order(x_early, x_late):
            return jnp.concatenate([x_early, x_late] if d == 0 else [x_late, x_early], axis=0)

        def emit(rows, cols, o):
            if d == 0:
                acc_ref[rows, cols] = o
            else:
                tot = acc_ref[rows, cols] + o
                o_ref[rows, cols] = (_rms(tot, og_ref[...]) * hg_ref[rows, cols]).astype(o_ref.dtype)

        def two_level(rows, states):
            f, hi16, lo16 = gates(rows)
            b = _bdot(scan_ref[d], jnp.concatenate([hi16, lo16], axis=0))
            kk = 1.0 - f
            q = hq_ref[rows, :]
            iv16 = hi_ref[rows, :].astype(BF16)
            r_mid = b[row(half - 1)]
            x1 = in_row_order(kk[early] * jnp.exp(r_mid - b[early]), q[late] * jnp.exp(b[late] - r_mid)).astype(BF16)
            dq = in_row_order(b[early] - b[row(half // 2 - 1)], b[late] - b[row(half + half // 2 - 1)])
            xq = (q * jnp.exp(dq)).astype(BF16)
            xk = (kk * jnp.exp(-dq)).astype(BF16)
            b_end = b[row(c - 1)]
            q_in = (q * jnp.exp(b)).astype(BF16)
            k_out = (kk * jnp.exp(b_end - b)).astype(BF16)
            a_end = jnp.exp(b_end)
            new_states = []
            for hh in range(HG_HEAD_BLOCK):
                cols = slice(hh * HG_DK, (hh + 1) * HG_DK)
                g1 = lax.dot_general(x1[:, cols], x1[:, cols], nt, preferred_element_type=F32)
                g2 = lax.dot_general(xq[:, cols], xk[:, cols], nt, preferred_element_type=F32)
                att = jnp.where(pair_ref[d, 0] > 0.5, g1, jnp.where(diag_ref[d] > 0.5, g2, 0.0))
                st = states[hh]
                o = (_bdot(att.astype(BF16), iv16[:, cols])
                     + lax.dot_general(q_in[:, cols], st.astype(BF16), nt, preferred_element_type=F32))
                dst = lax.dot_general(iv16[:, cols], k_out[:, cols], tn, preferred_element_type=F32)
                new_states.append(a_end[:, cols] * st + dst)
                emit(rows, cols, o)
            span = jnp.maximum(
                jnp.maximum(b[row(0)] - b[row(half // 2 - 1)], b[row(half // 2 - 1)] - b[row(half - 1)]),
                jnp.maximum(b[row(half)] - b[row(half + half // 2 - 1)],
                            b[row(half + half // 2 - 1)] - b[row(c - 1)]))
            return new_states, span

        def all_levels(rows):
            f, hi16, lo16 = gates(rows)
            cum = cum_ref[d]
            eall = jnp.exp(_bdot(cum, hi16) + _bdot(cum, lo16))
            for hh in range(HG_HEAD_BLOCK):
                cols = slice(hh * HG_DK, (hh + 1) * HG_DK)
                kk = 1.0 - f[:, cols]
                q = hq_ref[rows, cols]
                iv = hi_ref[rows, cols]
                iv16 = iv.astype(BF16)
                att = jnp.zeros((c, c), F32)
                for l in range(HG_LEVELS):
                    x = (kk + sel_ref[d, l] * (q - kk)) * eall[l * c:(l + 1) * c, cols]
                    xb = x.astype(BF16)
                    att = att + pair_ref[d, l] * lax.dot_general(xb, xb, nt, preferred_element_type=F32)
                e_in = eall[HG_LEVELS * c:(HG_LEVELS + 1) * c, cols]
                e_out = eall[(HG_LEVELS + 1) * c:, cols]
                st = st_ref[hh]
                o = (_bdot(att.astype(BF16), iv16)
                     + jnp.sum(q * kk, axis=-1, keepdims=True) * iv
                     + lax.dot_general((q * e_in).astype(BF16), st.astype(BF16), nt, preferred_element_type=F32))
                dst = lax.dot_general(iv16, (kk * e_out).astype(BF16), tn, preferred_element_type=F32)
                st_ref[hh] = (e_in[0:1] * e_out[0:1]) * st + dst
                emit(rows, cols, o)

        states = [initial_state(hh) for hh in range(HG_HEAD_BLOCK)]
        worst = jnp.zeros((1, HG_HEAD_BLOCK * HG_DK), F32)
        for ci in range(n_chunks):
            cidx = ci if d == 0 else n_chunks - 1 - ci
            states, span = two_level(slice(cidx * c, (cidx + 1) * c), states)
            worst = jnp.maximum(worst, span)
        for hh in range(HG_HEAD_BLOCK):
            st_ref[hh] = states[hh]

        @pl.when(jnp.logical_not(jnp.max(worst) <= HG_SAFE_EXPONENT))
        def _():
            for hh in range(HG_HEAD_BLOCK):
                st_ref[hh] = initial_state(hh)

            def chunk(ci, carry):
                cidx = ci if d == 0 else n_chunks - 1 - ci
                all_levels(pl.ds(pl.multiple_of(cidx * c, c), c))
                return carry

            lax.fori_loop(0, n_chunks, chunk, 0)

        if not has_s0:
            for hh in range(HG_HEAD_BLOCK):
                (sf_ref, sb_ref)[d][hh] = st_ref[hh].T

    run_direction(0, zf_ref)
    run_direction(1, zb_ref)


def _hgrn(p, lb_raw, o_gain, consts, n_batch, seq, row0, s0=None):
    hb = HG_HEAD_BLOCK
    w = hb * HG_DK

    def seg(col):
        return pl.BlockSpec((seq, w), lambda b, h: (row0 // seq + b, col // w + h))

    def const(a):
        return pl.BlockSpec(a.shape, lambda b, h: (0,) * a.ndim)

    in_specs = [seg(COL_HQ), seg(COL_ZF), seg(COL_ZF + HG_KW), seg(COL_HI), seg(COL_HG),
                pl.BlockSpec((2, DEPTH + 1, w), lambda b, h: (0, 0, h)),
                pl.BlockSpec((1, HG_DV), lambda b, h: (0, 0))] + [const(a) for a in consts]
    args = [p, p, p, p, p, lb_raw, o_gain, *consts]
    has_s0 = s0 is not None
    st_spec = pl.BlockSpec((None, hb, HG_DK, HG_DV), lambda b, h: (b, h, 0, 0))
    if has_s0:
        in_specs += [st_spec, st_spec]
        args += [s0[0], s0[1]]
    out_shape = [jax.ShapeDtypeStruct((n_batch * seq, HG_VW), BF16)]
    out_specs = [pl.BlockSpec((seq, w), lambda b, h: (b, h))]
    if not has_s0:
        st_shape = jax.ShapeDtypeStruct((n_batch, HG_HEADS, HG_DK, HG_DV), F32)
        out_shape += [st_shape, st_shape]
        out_specs += [st_spec, st_spec]
    return pl.pallas_call(
        functools.partial(_hgrn_kernel, n_chunks=seq // HG_CHUNK, has_s0=has_s0),
        out_shape=tuple(out_shape),
        grid=(n_batch, HG_HEADS // hb),
        in_specs=in_specs,
        out_specs=tuple(out_specs),
        scratch_shapes=[pltpu.VMEM((hb, HG_DV, HG_DK), F32), pltpu.VMEM((seq, w), F32)],
        compiler_params=_params(("arbitrary", "arbitrary")),
        name="hgrn_latent" if has_s0 else "hgrn_prompt",
    )(*args)


def _outproj_kernel(attp_ref, atts_ref, hgp_ref, hgs_ref, wa_ref, wb_ref, xp_ref, xs_ref, g1_ref, gain_ref,
                    sh_ref, sc_ref, x1_ref, h2_ref, full_ref):
    i = pl.program_id(0)
    n = pl.program_id(1)

    def run(att_ref, hg_ref, x_ref):
        acc = _bdot(att_ref[...], wa_ref[...].astype(BF16)) + _bdot(hg_ref[...], wb_ref[...].astype(BF16))
        x1 = x_ref[...] + g1_ref[...] * acc
        x1_ref[...] = x1
        full_ref[n] = x1

    pl.when(i < N_PROMPT_TILES)(lambda: run(attp_ref, hgp_ref, xp_ref))
    pl.when(i >= N_PROMPT_TILES)(lambda: run(atts_ref, hgs_ref, xs_ref))

    @pl.when(n == D_MODEL // OUT_COL_TILE - 1)
    def _():
        nt = D_MODEL // OUT_COL_TILE
        ss = full_ref[0] * full_ref[0]
        ms = jnp.sum(ss, axis=-1, keepdims=True)
        for k in range(1, nt):
            ms = ms + jnp.sum(full_ref[k] * full_ref[k], axis=-1, keepdims=True)
        rstd = lax.rsqrt(ms / D_MODEL + EPS)
        for k in range(nt):
            cols = slice(k * OUT_COL_TILE, (k + 1) * OUT_COL_TILE)
            y = full_ref[k] * rstd * gain_ref[:, cols]
            h2_ref[:, cols] = (y * (1.0 + sc_ref[:, cols]) + sh_ref[:, cols]).astype(BF16)


def _out_projection(att_p, att_s, hg_p, hg_s, w_out, xp, xs, mods, norm_mlp):
    tm, tn = TOK_TILE, OUT_COL_TILE
    nt = D_MODEL // tn

    def prompt_rows(width):
        return pl.BlockSpec((tm, width), lambda i, n: (jnp.minimum(i, N_PROMPT_TILES - 1), 0))

    def sample_rows(width):
        return pl.BlockSpec((tm, width), lambda i, n: (jnp.maximum(i - N_PROMPT_TILES, 0), 0))

    return pl.pallas_call(
        _outproj_kernel,
        out_shape=(jax.ShapeDtypeStruct((N_TOK, D_MODEL), F32), jax.ShapeDtypeStruct((N_TOK, D_MODEL), BF16)),
        grid=(N_TILES, nt),
        in_specs=[prompt_rows(ATT_WIDTH), sample_rows(ATT_WIDTH), prompt_rows(HG_VW), sample_rows(HG_VW),
                  pl.BlockSpec((ATT_WIDTH, tn), lambda i, n: (0, n)),
                  pl.BlockSpec((HG_VW, tn), lambda i, n: (1, n)),
                  pl.BlockSpec((tm, tn), lambda i, n: (jnp.minimum(i, N_PROMPT_TILES - 1), n)),
                  pl.BlockSpec((tm, tn), lambda i, n: (jnp.maximum(i - N_PROMPT_TILES, 0), n)),
                  _mod_spec(0, 2, tm, width=tn, col=lambda i, n: n),
                  pl.BlockSpec((1, D_MODEL), lambda i, n: (0, 0)),
                  _mod_spec(0, 3, tm), _mod_spec(0, 4, tm)],
        out_specs=(pl.BlockSpec((tm, tn), lambda i, n: (i, n)),
                   pl.BlockSpec((tm, D_MODEL), lambda i, n: (i, 0))),
        scratch_shapes=[pltpu.VMEM((nt, tm, tn), F32)],
        compiler_params=_params(("arbitrary", "arbitrary")),
        name="out_projection",
    )(att_p, att_s, hg_p, hg_s, w_out, w_out, xp, xs, mods, norm_mlp, mods, mods)


def _mlp_kernel(h_ref, w1_ref, w2_ref, x_ref, g2_ref, *rest, final):
    if final:
        fin_ref, o_ref, res_ref = rest
    else:
        o_ref, res_ref = rest
    j = pl.program_id(1)
    res_ref[j] = x_ref[...]

    @pl.when(j == 0)
    def _():
        o_ref[...] = jnp.zeros(o_ref.shape, F32)

    a = jnp.square(jnp.maximum(_bdot(h_ref[...], w1_ref[...].astype(BF16)), 0.0)).astype(BF16)
    for n in range(D_MODEL // MLP_OUT_CHUNK):
        cols = slice(n * MLP_OUT_CHUNK, (n + 1) * MLP_OUT_CHUNK)
        o_ref[:, cols] += _bdot(a, w2_ref[:, cols].astype(BF16))

    @pl.when(j == MLP_STEPS - 1)
    def _():
        for k in range(MLP_STEPS):
            cols = slice(k * MLP_RES_COLS, (k + 1) * MLP_RES_COLS)
            o_ref[:, cols] = res_ref[k] + g2_ref[:, cols] * o_ref[:, cols]
        if final:
            o_ref[...] = _rms(o_ref[...], fin_ref[...])


def _mlp(h, x, w1, w2, mods, layer, tile0, n_tiles, final_norm=None):
    tm, th = TOK_TILE, FF_TILE
    final = final_norm is not None
    in_specs = [pl.BlockSpec((tm, D_MODEL), lambda i, j: (tile0 + i, 0), pipeline_mode=pl.Buffered(1)),
                pl.BlockSpec((None, D_MODEL, th), lambda i, j: (layer, 0, j)),
                pl.BlockSpec((None, th, D_MODEL), lambda i, j: (layer, j, 0)),
                pl.BlockSpec((tm, MLP_RES_COLS), lambda i, j: (tile0 + i, j)),
                _mod_spec(layer, 5, tm, tile_of=lambda i, j: tile0 + i)]
    args = [h, w1, w2, x, mods]
    if final:
        in_specs.append(pl.BlockSpec((1, D_MODEL), lambda i, j: (0, 0)))
        args.append(final_norm)
    return pl.pallas_call(
        functools.partial(_mlp_kernel, final=final),
        out_shape=jax.ShapeDtypeStruct((n_tiles * tm, D_MODEL), F32),
        grid=(n_tiles, MLP_STEPS),
        in_specs=in_specs,
        out_specs=pl.BlockSpec((tm, D_MODEL), lambda i, j: (i, 0)),
        scratch_shapes=[pltpu.VMEM((MLP_STEPS, tm, MLP_RES_COLS), F32)],
        compiler_params=_params(("arbitrary", "arbitrary")),
        name="mlp_final" if final else "mlp",
    )(*args)


def _pool_kernel(x_ref, wp_ref, ps_ref, gain1_ref, sh1_ref, sc1_ref, g1_ref, gain2_ref, sh2_ref, sc2_ref,
                 x3_ref, h4_ref):
    i = pl.program_id(0)
    tm = TOK_TILE
    seq = jnp.where(i < N_PROMPT_TILES, SEQ, DEC_SEQ)
    pos = lax.broadcasted_iota(jnp.int32, (tm, POOL_GROUP), 0) & (seq - 1)

    def shifted(a, off):
        r = pltpu.roll(a, (-off) % tm, 0)
        return jnp.where(jnp.logical_and(pos + off >= 0, pos + off < seq), r, 0.0)

    for g, w in enumerate(POOL_WINDOWS):
        cols = slice(g * POOL_GROUP, (g + 1) * POOL_GROUP)
        x = x_ref[:, cols]
        ms = jnp.mean(x_ref[...] * x_ref[...], axis=-1, keepdims=True) if g == 0 else ms
        h = x * lax.rsqrt(ms + EPS) * gain1_ref[:, cols] * (1.0 + sc1_ref[:, cols]) + sh1_ref[:, cols]
        back = h
        fwd = h
        m = 1
        while m < w // 2:
            back = back + shifted(back, -m)
            fwd = fwd + shifted(fwd, m)
            m *= 2
        total = shifted(back, -1) + fwd
        count = (jnp.minimum(pos + (w - w // 2), seq) - jnp.maximum(pos - w // 2, 0)).astype(F32)
        pooled = (total / count - h).astype(BF16)
        mix = _bdot(pooled, wp_ref[g].astype(BF16)) * ps_ref[:, cols]
        x3_ref[:, cols] = x + g1_ref[:, cols] * mix
    x3 = x3_ref[...]
    h4_ref[...] = _modulate(x3, gain2_ref[...], sh2_ref[...], sc2_ref[...]).astype(BF16)


def _pool_mixer(x, w_pool, pool_scale, norm_mix, norm_mlp, mods):
    tm = TOK_TILE
    vec = pl.BlockSpec((1, D_MODEL), lambda i: (0, 0))
    tile = pl.BlockSpec((tm, D_MODEL), lambda i: (i, 0))
    return pl.pallas_call(
        _pool_kernel,
        out_shape=(jax.ShapeDtypeStruct((N_TOK, D_MODEL), F32), jax.ShapeDtypeStruct((N_TOK, D_MODEL), BF16)),
        grid=(N_TILES,),
        in_specs=[tile, pl.BlockSpec((len(POOL_WINDOWS), POOL_GROUP, POOL_GROUP), lambda i: (0, 0, 0)),
                  vec, vec, _mod_spec(1, 0, tm), _mod_spec(1, 1, tm), _mod_spec(1, 2, tm),
                  vec, _mod_spec(1, 3, tm), _mod_spec(1, 4, tm)],
        out_specs=(tile, tile),
        compiler_params=_params(("arbitrary",)),
        name="pool_mixer",
    )(x, w_pool, pool_scale, norm_mix, mods, mods, mods, norm_mlp, mods, mods)


def kernel(x_prompt, x_sample, cache_k, cache_v, state_hgrn_fwd, state_hgrn_bwd, c, c_ctx, w_ada, b_ada,
           norm_mix, norm_mlp, w_in_ab, w_out_ab, q_norm, k_norm, hg_norm, lb_raw, w_pool, pool_scale,
           w_mlp_in, w_mlp_out, final_norm):
    xp = x_prompt.reshape(N_PROMPT, D_MODEL)
    xs = x_sample.reshape(N_SAMPLE, D_MODEL)
    cv = jnp.concatenate([c_ctx[None, :], c, jnp.zeros((ADA_ROWS - 1 - DEC_BATCH, D_MODEL), F32)], axis=0)
    mods = _ada_table(cv, w_ada, b_ada).reshape(DEPTH, ADA_ROWS, 1, N_MOD * D_MODEL)

    h0 = _modulate0(xp, xs, norm_mix, mods)
    proj, new_k, new_v = _in_projection(h0, w_in_ab[0], q_norm[0:1], k_norm[0:1])
    ctx_k = cache_k[:, 0].reshape(DEC_BATCH, PAST_LEN, KV_WIDTH)
    ctx_v = cache_v[:, 0].reshape(DEC_BATCH, PAST_LEN, KV_WIDTH)
    att_p = _attention(proj, BATCH, SEQ, 0, SEQ)
    att_s = _attention(proj, DEC_BATCH, DEC_SEQ, N_PROMPT, 256, ctx=(ctx_k, ctx_v))
    consts = _hgrn_constants()
    hg_p, s_fwd, s_bwd = _hgrn(proj, lb_raw, hg_norm[0:1], consts, BATCH, SEQ, 0)
    s0 = (state_hgrn_fwd.reshape(DEC_BATCH, HG_HEADS, HG_DK, HG_DV),
          state_hgrn_bwd.reshape(DEC_BATCH, HG_HEADS, HG_DK, HG_DV))
    (hg_s,) = _hgrn(proj, lb_raw, hg_norm[0:1], consts, DEC_BATCH, DEC_SEQ, N_PROMPT, s0=s0)
    x1, h2 = _out_projection(att_p, att_s, hg_p, hg_s, w_out_ab[0], xp, xs, mods, norm_mlp[0:1])
    x2 = _mlp(h2, x1, w_mlp_in, w_mlp_out, mods, 0, 0, N_TILES)

    x3, h4 = _pool_mixer(x2, w_pool[0], pool_scale[0:1], norm_mix[1:2], norm_mlp[1:2], mods)
    fin = final_norm[None, :]
    y_prompt = _mlp(h4, x3, w_mlp_in, w_mlp_out, mods, 1, 0, N_PROMPT_TILES, final_norm=fin)
    y_sample = _mlp(h4, x3, w_mlp_in, w_mlp_out, mods, 1, N_PROMPT_TILES, N_TILES - N_PROMPT_TILES,
                    final_norm=fin)

    return (y_prompt.reshape(BATCH, SEQ, D_MODEL), y_sample.reshape(DEC_BATCH, DEC_SEQ, D_MODEL),
            new_k.reshape(BATCH, 1, SEQ, N_KV_HEADS, HEAD_DIM), new_v.reshape(BATCH, 1, SEQ, N_KV_HEADS, HEAD_DIM),
            s_fwd.reshape(BATCH, 1, HG_HEADS, HG_DK, HG_DV), s_bwd.reshape(BATCH, 1, HG_HEADS, HG_DK, HG_DV))
```

```python
import functools

import numpy as np
import jax
import jax.numpy as jnp
from jax import lax
from jax.experimental import pallas as pl
from jax.experimental.pallas import tpu as pltpu

F32 = jnp.float32
BF16 = jnp.bfloat16

D_MODEL = 2048
BATCH = 16
SEQ = 256
DEPTH = 2
DEC_BATCH = 2
DEC_SEQ = 1024
PAST_LEN = 256
GRID_W = 64
HEAD_DIM = 128
N_Q_HEADS = 8
N_KV_HEADS = 2
Q_PER_KV = N_Q_HEADS // N_KV_HEADS
ATT_WIDTH = N_Q_HEADS * HEAD_DIM
KV_WIDTH = N_KV_HEADS * HEAD_DIM
HG_HEADS = 8
HG_DK = 128
HG_DV = 128
HG_KW = HG_HEADS * HG_DK
HG_VW = HG_HEADS * HG_DV
IN_AB = ATT_WIDTH + 2 * KV_WIDTH + 3 * HG_KW + 2 * HG_VW
MIX_WIDTH = ATT_WIDTH + HG_VW
POOL_WINDOWS = (2, 4, 8, 16)
POOL_GROUP = D_MODEL // len(POOL_WINDOWS)
D_FF = 4 * D_MODEL
ROPE_THETA = 10000.0
ROPE_HALF = HEAD_DIM // 2
EPS = 1e-6
N_MOD = 6

N_PROMPT = BATCH * SEQ
N_SAMPLE = DEC_BATCH * DEC_SEQ
N_TOK = N_PROMPT + N_SAMPLE
ADA_ROWS = 16
TOK_TILE = 1024
N_TILES = N_TOK // TOK_TILE
N_PROMPT_TILES = N_PROMPT // TOK_TILE
PROJ_TOK_TILE = 2048
PROJ_COL_TILE = 512
OUT_COL_TILE = 512
FF_TILE = 512
MLP_STEPS = D_FF // FF_TILE
MLP_RES_COLS = D_MODEL // MLP_STEPS
MLP_OUT_CHUNK = 512
HG_CHUNK = 128
HG_LEVELS = 7
HG_HEAD_BLOCK = 4
HG_SAFE_EXPONENT = 80.0
VMEM_LIMIT = 56 * 2 ** 20

COL_Q = 0
COL_K = ATT_WIDTH
COL_V = COL_K + KV_WIDTH
COL_HQ = COL_V + KV_WIDTH
COL_ZF = COL_HQ + HG_KW
COL_HI = COL_ZF + 2 * HG_KW
COL_HG = COL_HI + HG_VW


def _params(semantics):
    return pltpu.CompilerParams(dimension_semantics=semantics, vmem_limit_bytes=VMEM_LIMIT)


def _sigmoid(x):
    return 1.0 / (1.0 + jnp.exp(-x))


def _silu(x):
    return x * _sigmoid(x)


def _rms(x, gain):
    return x * lax.rsqrt(jnp.mean(x * x, axis=-1, keepdims=True) + EPS) * gain


def _bdot(a, b):
    return jnp.dot(a, b, preferred_element_type=F32)


def _mod_row(tile, tile_rows):
    first = N_PROMPT // tile_rows
    per_seq = DEC_SEQ // tile_rows
    return jnp.where(tile < first, 0, 1 + (tile - first) // per_seq)


def _mod_spec(layer, chunk, tile_rows, width=D_MODEL, col=lambda *g: 0, tile_of=lambda *g: g[0]):
    per = D_MODEL // width
    return pl.BlockSpec((None, None, 1, width),
                        lambda *g: (layer, _mod_row(tile_of(*g), tile_rows), 0, chunk * per + col(*g)))


def _ada_kernel(cv_ref, w_ref, b_ref, o_ref):
    s = _silu(cv_ref[...]).astype(BF16)
    o_ref[...] = _bdot(s, w_ref[...].astype(BF16)) + b_ref[...]


def _ada_table(cv, w_ada, b_ada):
    tn = 1024
    n = N_MOD * D_MODEL
    return pl.pallas_call(
        _ada_kernel,
        out_shape=jax.ShapeDtypeStruct((DEPTH, ADA_ROWS, n), F32),
        grid=(DEPTH, n // tn),
        in_specs=[pl.BlockSpec((ADA_ROWS, D_MODEL), lambda l, j: (0, 0)),
                  pl.BlockSpec((None, D_MODEL, tn), lambda l, j: (l, 0, j)),
                  pl.BlockSpec((None, 1, tn), lambda l, j: (l, 0, j))],
        out_specs=pl.BlockSpec((None, ADA_ROWS, tn), lambda l, j: (l, 0, j)),
        compiler_params=_params(("arbitrary", "arbitrary")),
        name="ada_table",
    )(cv, w_ada, b_ada.reshape(DEPTH, 1, n))


def _modulate(x, gain, shift, scale):
    return _rms(x, gain) * (1.0 + scale) + shift


def _mod0_kernel(xp_ref, xs_ref, gain_ref, sh_ref, sc_ref, o_ref):
    i = pl.program_id(0)

    def run(x_ref):
        o_ref[...] = _modulate(x_ref[...], gain_ref[...], sh_ref[...], sc_ref[...]).astype(BF16)

    pl.when(i < N_PROMPT_TILES)(lambda: run(xp_ref))
    pl.when(i >= N_PROMPT_TILES)(lambda: run(xs_ref))


def _modulate0(xp, xs, norm_mix, mods):
    tm = TOK_TILE
    first = N_PROMPT_TILES
    return pl.pallas_call(
        _mod0_kernel,
        out_shape=jax.ShapeDtypeStruct((N_TOK, D_MODEL), BF16),
        grid=(N_TOK // tm,),
        in_specs=[pl.BlockSpec((tm, D_MODEL), lambda i: (jnp.minimum(i, first - 1), 0)),
                  pl.BlockSpec((tm, D_MODEL), lambda i: (jnp.maximum(i - first, 0), 0)),
                  pl.BlockSpec((1, D_MODEL), lambda i: (0, 0)),
                  _mod_spec(0, 0, tm), _mod_spec(0, 1, tm)],
        out_specs=pl.BlockSpec((tm, D_MODEL), lambda i: (i, 0)),
        compiler_params=_params(("arbitrary",)),
        name="modulate0",
    )(xp, xs, norm_mix[0:1], mods, mods)


def _rope(y, cos, sin, perm2):
    hi = y.astype(BF16)
    lo = (y - hi.astype(F32)).astype(BF16)
    rot = _bdot(jnp.concatenate([hi, lo], axis=1), perm2)
    return y * cos + rot * sin


def _inproj_kernel(h_ref, w_ref, qg_ref, kg_ref, cos_ref, sin_ref, perm_ref, p_ref, nk_ref, nv_ref):
    i = pl.program_id(0)
    j = pl.program_id(1)
    latent = i >= N_PROMPT // PROJ_TOK_TILE
    heads = PROJ_COL_TILE // HEAD_DIM
    kv_tile = COL_K // PROJ_COL_TILE
    is_q = j < kv_tile

    def attention_tile(rope):
        gain = jnp.where(is_q, qg_ref[...], kg_ref[...])
        w = w_ref[...].astype(BF16)
        for s in range(PROJ_TOK_TILE // DEC_SEQ):
            rows = slice(s * DEC_SEQ, (s + 1) * DEC_SEQ)
            acc = _bdot(h_ref[rows, :], w)
            for hh in range(heads):
                cols = slice(hh * HEAD_DIM, (hh + 1) * HEAD_DIM)
                x = acc[:, cols]
                y = _rms(x, gain)
                if rope:
                    y = _rope(y, cos_ref[...], sin_ref[...], perm_ref[...])
                if hh >= N_KV_HEADS:
                    y = jnp.where(is_q, y, x)
                p_ref[rows, cols] = y

    pl.when(jnp.logical_and(j <= kv_tile, latent))(lambda: attention_tile(True))
    pl.when(jnp.logical_and(j <= kv_tile, jnp.logical_not(latent)))(lambda: attention_tile(False))

    @pl.when(jnp.logical_and(j == kv_tile, jnp.logical_not(latent)))
    def _():
        nk_ref[...] = p_ref[:, :KV_WIDTH]
        nv_ref[...] = p_ref[:, KV_WIDTH:]

    @pl.when(j > kv_tile)
    def _():
        is_silu = jnp.logical_or(j < COL_ZF // PROJ_COL_TILE, j >= COL_HG // PROJ_COL_TILE)
        w = w_ref[...].astype(BF16)
        for s in range(PROJ_TOK_TILE // DEC_SEQ):
            rows = slice(s * DEC_SEQ, (s + 1) * DEC_SEQ)
            acc = _bdot(h_ref[rows, :], w)
            p_ref[rows, :] = jnp.where(is_silu, _silu(acc), acc)


def _rope_tables():
    t = np.arange(DEC_SEQ)
    row = (t // GRID_W).astype(np.float32)
    col = (t % GRID_W).astype(np.float32)
    inv = (np.float32(ROPE_THETA) ** (-np.arange(0, ROPE_HALF, 2, dtype=np.float32) / np.float32(ROPE_HALF))).astype(np.float32)
    ar = row[:, None] * inv
    ac = col[:, None] * inv
    ang = np.concatenate([ar, ar, ac, ac], axis=-1).astype(np.float32)
    cos = np.cos(ang).astype(np.float32)
    sin = np.sin(ang).astype(np.float32)
    qw = ROPE_HALF // 2
    perm = np.zeros((HEAD_DIM, HEAD_DIM), np.float32)
    for k in range(qw):
        perm[qw + k, k] = -1.0
        perm[k, qw + k] = 1.0
        perm[3 * qw + k, 2 * qw + k] = -1.0
        perm[2 * qw + k, 3 * qw + k] = 1.0
    return jnp.asarray(cos), jnp.asarray(sin), jnp.asarray(np.concatenate([perm, perm], axis=0), BF16)


def _in_projection(h, w_in, q_gain, k_gain):
    tm, tn = PROJ_TOK_TILE, PROJ_COL_TILE
    n_prompt_tiles = N_PROMPT // tm
    cos, sin, perm2 = _rope_tables()
    table = pl.BlockSpec((DEC_SEQ, HEAD_DIM), lambda i, j: (0, 0))
    gain = pl.BlockSpec((1, HEAD_DIM), lambda i, j: (0, 0))
    state = pl.BlockSpec((tm, KV_WIDTH), lambda i, j: (jnp.minimum(i, n_prompt_tiles - 1), 0))
    return pl.pallas_call(
        _inproj_kernel,
        out_shape=(jax.ShapeDtypeStruct((N_TOK, IN_AB), F32),
                   jax.ShapeDtypeStruct((N_PROMPT, KV_WIDTH), F32),
                   jax.ShapeDtypeStruct((N_PROMPT, KV_WIDTH), F32)),
        grid=(N_TOK // tm, IN_AB // tn),
        in_specs=[pl.BlockSpec((tm, D_MODEL), lambda i, j: (i, 0)),
                  pl.BlockSpec((D_MODEL, tn), lambda i, j: (0, j)),
                  gain, gain, table, table, pl.BlockSpec((2 * HEAD_DIM, HEAD_DIM), lambda i, j: (0, 0))],
        out_specs=(pl.BlockSpec((tm, tn), lambda i, j: (i, j)), state, state),
        compiler_params=_params(("arbitrary", "arbitrary")),
        name="in_projection",
    )(h, w_in, q_gain, k_gain, cos, sin, perm2)


def _attn_kernel(*refs, has_ctx):
    if has_ctx:
        q_ref, k_ref, v_ref, ck_ref, cv_ref, o_ref = refs
    else:
        q_ref, k_ref, v_ref, o_ref = refs
    scale = HEAD_DIM ** -0.5
    nt = (((1,), (1,)), ((), ()))
    k = k_ref[...].astype(BF16)
    v = v_ref[...].astype(BF16)
    if has_ctx:
        ck = ck_ref[...].astype(BF16)
        cv = cv_ref[...].astype(BF16)
    for g in range(Q_PER_KV):
        q = q_ref[:, g * HEAD_DIM:(g + 1) * HEAD_DIM].astype(BF16)
        s = lax.dot_general(q, k, nt, preferred_element_type=F32) * scale
        m = jnp.max(s, axis=-1, keepdims=True)
        if has_ctx:
            sc = lax.dot_general(q, ck, nt, preferred_element_type=F32) * scale
            m = jnp.maximum(m, jnp.max(sc, axis=-1, keepdims=True))
        p = jnp.exp(s - m)
        den = jnp.sum(p, axis=-1, keepdims=True)
        o = _bdot(p.astype(BF16), v)
        if has_ctx:
            pc = jnp.exp(sc - m)
            den = den + jnp.sum(pc, axis=-1, keepdims=True)
            o = o + _bdot(pc.astype(BF16), cv)
        o_ref[:, g * HEAD_DIM:(g + 1) * HEAD_DIM] = (o / den).astype(o_ref.dtype)


def _attention(p, n_batch, seq, row0, tq, ctx=None):
    q_blocks = seq // tq
    gw = Q_PER_KV * HEAD_DIM
    in_specs = [
        pl.BlockSpec((tq, gw), lambda b, h, qi: (row0 // tq + b * q_blocks + qi, h)),
        pl.BlockSpec((seq, HEAD_DIM), lambda b, h, qi: (row0 // seq + b, COL_K // HEAD_DIM + h)),
        pl.BlockSpec((seq, HEAD_DIM), lambda b, h, qi: (row0 // seq + b, COL_V // HEAD_DIM + h)),
    ]
    args = [p, p, p]
    if ctx is not None:
        ctx_spec = pl.BlockSpec((None, PAST_LEN, HEAD_DIM), lambda b, h, qi: (b, 0, h))
        in_specs += [ctx_spec, ctx_spec]
        args += [ctx[0], ctx[1]]
    return pl.pallas_call(
        functools.partial(_attn_kernel, has_ctx=ctx is not None),
        out_shape=jax.ShapeDtypeStruct((n_batch * seq, ATT_WIDTH), BF16),
        grid=(n_batch, N_KV_HEADS, q_blocks),
        in_specs=in_specs,
        out_specs=pl.BlockSpec((tq, gw), lambda b, h, qi: (b * q_blocks + qi, h)),
        compiler_params=_params(("arbitrary", "arbitrary", "arbitrary")),
        name="attention_latent" if ctx is not None else "attention_prompt",
    )(*args)


def _hgrn_constants():
    c = HG_CHUNK
    t = np.arange(c)
    cums, sels, pairs, scans, diags = [], [], [], [], []
    for d in range(2):
        pos = t if d == 0 else c - 1 - t
        pu, pt = pos[None, :], pos[:, None]
        ms, ss, ws = [], [], []
        for l in range(HG_LEVELS):
            m = c >> l
            blk = pos // m
            mid = (blk * m + m // 2)[:, None]
            late = ((pos % m) >= m // 2)
            ms.append(np.where(late[:, None], (pu >= mid) & (pu <= pt), (pu > pt) & (pu < mid)))
            ss.append(np.broadcast_to(late[:, None], (c, c)))
            ws.append((blk[:, None] == blk[None, :]) & late[:, None] & ~late[None, :])
        ms.append(pu <= pt)
        ms.append(pu > pt)
        cums.append(np.concatenate(ms, axis=0))
        sels.append(np.stack(ss))
        pairs.append(np.stack(ws))
        scans.append(np.concatenate([pu <= pt, pu <= pt], axis=1))
        diags.append(((pos // (c // 2))[:, None] == (pos // (c // 2))[None, :]) & (pu <= pt))
    return (jnp.asarray(np.stack(cums), BF16), jnp.asarray(np.stack(sels), F32),
            jnp.asarray(np.stack(pairs), F32), jnp.asarray(np.stack(scans), BF16),
            jnp.asarray(np.stack(diags), F32))


def _hgrn_kernel(*refs, n_chunks, has_s0):
    (hq_ref, zf_ref, zb_ref, hi_ref, hg_ref, lb_ref, og_ref, cum_ref, sel_ref, pair_ref, scan_ref,
     diag_ref) = refs[:12]
    refs = refs[12:]
    if has_s0:
        s0f_ref, s0b_ref, o_ref = refs[:3]
        refs = refs[3:]
    else:
        o_ref, sf_ref, sb_ref = refs[:3]
        refs = refs[3:]
    st_ref, acc_ref = refs
    c = HG_CHUNK
    half = c // 2
    nt = (((1,), (1,)), ((), ()))
    tn = (((0,), (0,)), ((), ()))

    def direction(d, z_ref):
        raw = lb_ref[d]
        e = jnp.exp(raw - jnp.max(raw, axis=0, keepdims=True))
        lb = e[0:1] / jnp.sum(e, axis=0, keepdims=True)

        def initial_state(hh):
            if has_s0:
                return (s0f_ref, s0b_ref)[d][hh].T
            return jnp.zeros((HG_DV, HG_DK), F32)

        def gates(rows):
            f = lb + (1.0 - lb) * _sigmoid(z_ref[rows, :])
            logf = jnp.log(f)
            hi16 = logf.astype(BF16)
            lo16 = (logf - hi16.astype(F32)).astype(BF16)
            return f, hi16, lo16

        def row(p):
            t = p if d == 0 else c - 1 - p
            return slice(t, t + 1)

        early, late = (slice(0, half), slice(half, c)) if d == 0 else (slice(half, c), slice(0, half))

        def in_row_order(x_early, x_late):
            return jnp.concatenate([x_early, x_late] if d == 0 else [x_late, x_early], axis=0)

        def emit(rows, cols, o):
            if d == 0:
                acc_ref[rows, cols] = o
            else:
                tot = acc_ref[rows, cols] + o
                o_ref[rows, cols] = (_rms(tot, og_ref[...]) * hg_ref[rows, cols]).astype(o_ref.dtype)

        def two_level_operands(rows):
            f, hi16, lo16 = gates(rows)
            b = _bdot(scan_ref[d], jnp.concatenate([hi16, lo16], axis=0))
            kk = 1.0 - f
            q = hq_ref[rows, :]
            r_mid = b[row(half - 1)]
            x1 = in_row_order(kk[early] * jnp.exp(r_mid - b[early]), q[late] * jnp.exp(b[late] - r_mid)).astype(BF16)
            dq = in_row_order(b[early] - b[row(half // 2 - 1)], b[late] - b[row(half + half // 2 - 1)])
            b_end = b[row(c - 1)]
            span = jnp.maximum(
                jnp.maximum(b[row(0)] - b[row(half // 2 - 1)], b[row(half // 2 - 1)] - b[row(half - 1)]),
                jnp.maximum(b[row(half)] - b[row(half + half // 2 - 1)],
                            b[row(half + half // 2 - 1)] - b[row(c - 1)]))
            return dict(x1=x1, xq=(q * jnp.exp(dq)).astype(BF16), xk=(kk * jnp.exp(-dq)).astype(BF16),
                        q_in=(q * jnp.exp(b)).astype(BF16), k_out=(kk * jnp.exp(b_end - b)).astype(BF16),
                        a_end=jnp.exp(b_end), iv=hi_ref[rows, :].astype(BF16), span=span)

        def two_level_chunk(rows, ops, states):
            new_states = []
            for hh in range(HG_HEAD_BLOCK):
                cols = slice(hh * HG_DK, (hh + 1) * HG_DK)
                g1 = lax.dot_general(ops["x1"][:, cols], ops["x1"][:, cols], nt, preferred_element_type=F32)
                g2 = lax.dot_general(ops["xq"][:, cols], ops["xk"][:, cols], nt, preferred_element_type=F32)
                att = jnp.where(pair_ref[d, 0] > 0.5, g1, jnp.where(diag_ref[d] > 0.5, g2, 0.0))
                st = states[hh]
                o = (_bdot(att.astype(BF16), ops["iv"][:, cols])
                     + lax.dot_general(ops["q_in"][:, cols], st.astype(BF16), nt, preferred_element_type=F32))
                dst = lax.dot_general(ops["iv"][:, cols], ops["k_out"][:, cols], tn, preferred_element_type=F32)
                new_states.append(ops["a_end"][:, cols] * st + dst)
                emit(rows, cols, o)
            return new_states

        def all_levels(rows):
            f, hi16, lo16 = gates(rows)
            cum = cum_ref[d]
            eall = jnp.exp(_bdot(cum, hi16) + _bdot(cum, lo16))
            for hh in range(HG_HEAD_BLOCK):
                cols = slice(hh * HG_DK, (hh + 1) * HG_DK)
                kk = 1.0 - f[:, cols]
                q = hq_ref[rows, cols]
                iv = hi_ref[rows, cols]
                iv16 = iv.astype(BF16)
                att = jnp.zeros((c, c), F32)
                for l in range(HG_LEVELS):
                    x = (kk + sel_ref[d, l] * (q - kk)) * eall[l * c:(l + 1) * c, cols]
                    xb = x.astype(BF16)
                    att = att + pair_ref[d, l] * lax.dot_general(xb, xb, nt, preferred_element_type=F32)
                e_in = eall[HG_LEVELS * c:(HG_LEVELS + 1) * c, cols]
                e_out = eall[(HG_LEVELS + 1) * c:, cols]
                st = st_ref[hh]
                o = (_bdot(att.astype(BF16), iv16)
                     + jnp.sum(q * kk, axis=-1, keepdims=True) * iv
                     + lax.dot_general((q * e_in).astype(BF16), st.astype(BF16), nt, preferred_element_type=F32))
                dst = lax.dot_general(iv16, (kk * e_out).astype(BF16), tn, preferred_element_type=F32)
                st_ref[hh] = (e_in[0:1] * e_out[0:1]) * st + dst
                emit(rows, cols, o)

        def chunk_rows(ci):
            cidx = ci if d == 0 else n_chunks - 1 - ci
            return slice(cidx * c, (cidx + 1) * c)

        states = [initial_state(hh) for hh in range(HG_HEAD_BLOCK)]
        worst = jnp.zeros((1, HG_HEAD_BLOCK * HG_DK), F32)
        for ci in range(n_chunks):
            ops = two_level_operands(chunk_rows(ci))
            states = two_level_chunk(chunk_rows(ci), ops, states)
            worst = jnp.maximum(worst, ops["span"])
        for hh in range(HG_HEAD_BLOCK):
            st_ref[hh] = states[hh]

        @pl.when(jnp.logical_not(jnp.max(worst) <= HG_SAFE_EXPONENT))
        def _():
            for hh in range(HG_HEAD_BLOCK):
                st_ref[hh] = initial_state(hh)

            def chunk(ci, carry):
                cidx = ci if d == 0 else n_chunks - 1 - ci
                all_levels(pl.ds(pl.multiple_of(cidx * c, c), c))
                return carry

            lax.fori_loop(0, n_chunks, chunk, 0)

        if not has_s0:
            for hh in range(HG_HEAD_BLOCK):
                (sf_ref, sb_ref)[d][hh] = st_ref[hh].T

    direction(0, zf_ref)
    direction(1, zb_ref)


def _hgrn(p, lb_raw, o_gain, consts, n_batch, seq, row0, s0=None):
    hb = HG_HEAD_BLOCK
    w = hb * HG_DK

    def seg(col):
        return pl.BlockSpec((seq, w), lambda b, h: (row0 // seq + b, col // w + h))

    def const(a):
        return pl.BlockSpec(a.shape, lambda b, h: (0,) * a.ndim)

    in_specs = [seg(COL_HQ), seg(COL_ZF), seg(COL_ZF + HG_KW), seg(COL_HI), seg(COL_HG),
                pl.BlockSpec((2, DEPTH + 1, w), lambda b, h: (0, 0, h)),
                pl.BlockSpec((1, HG_DV), lambda b, h: (0, 0))] + [const(a) for a in consts]
    args = [p, p, p, p, p, lb_raw, o_gain, *consts]
    has_s0 = s0 is not None
    st_spec = pl.BlockSpec((None, hb, HG_DK, HG_DV), lambda b, h: (b, h, 0, 0))
    if has_s0:
        in_specs += [st_spec, st_spec]
        args += [s0[0], s0[1]]
    out_shape = [jax.ShapeDtypeStruct((n_batch * seq, HG_VW), BF16)]
    out_specs = [pl.BlockSpec((seq, w), lambda b, h: (b, h))]
    if not has_s0:
        st_shape = jax.ShapeDtypeStruct((n_batch, HG_HEADS, HG_DK, HG_DV), F32)
        out_shape += [st_shape, st_shape]
        out_specs += [st_spec, st_spec]
    return pl.pallas_call(
        functools.partial(_hgrn_kernel, n_chunks=seq // HG_CHUNK, has_s0=has_s0),
        out_shape=tuple(out_shape),
        grid=(n_batch, HG_HEADS // hb),
        in_specs=in_specs,
        out_specs=tuple(out_specs),
        scratch_shapes=[pltpu.VMEM((hb, HG_DV, HG_DK), F32), pltpu.VMEM((seq, w), F32)],
        compiler_params=_params(("arbitrary", "arbitrary")),
        name="hgrn_latent" if has_s0 else "hgrn_prompt",
    )(*args)


def _outproj_kernel(attp_ref, atts_ref, hgp_ref, hgs_ref, wa_ref, wb_ref, xp_ref, xs_ref, g1_ref, gain_ref,
                    sh_ref, sc_ref, x1_ref, h2_ref, full_ref):
    i = pl.program_id(0)
    n = pl.program_id(1)

    def run(att_ref, hg_ref, x_ref):
        acc = _bdot(att_ref[...], wa_ref[...].astype(BF16)) + _bdot(hg_ref[...], wb_ref[...].astype(BF16))
        x1 = x_ref[...] + g1_ref[...] * acc
        x1_ref[...] = x1
        full_ref[n] = x1

    pl.when(i < N_PROMPT_TILES)(lambda: run(attp_ref, hgp_ref, xp_ref))
    pl.when(i >= N_PROMPT_TILES)(lambda: run(atts_ref, hgs_ref, xs_ref))

    @pl.when(n == D_MODEL // OUT_COL_TILE - 1)
    def _():
        nt = D_MODEL // OUT_COL_TILE
        ss = full_ref[0] * full_ref[0]
        ms = jnp.sum(ss, axis=-1, keepdims=True)
        for k in range(1, nt):
            ms = ms + jnp.sum(full_ref[k] * full_ref[k], axis=-1, keepdims=True)
        rstd = lax.rsqrt(ms / D_MODEL + EPS)
        for k in range(nt):
            cols = slice(k * OUT_COL_TILE, (k + 1) * OUT_COL_TILE)
            y = full_ref[k] * rstd * gain_ref[:, cols]
            h2_ref[:, cols] = (y * (1.0 + sc_ref[:, cols]) + sh_ref[:, cols]).astype(BF16)


def _out_projection(att_p, att_s, hg_p, hg_s, w_out, xp, xs, mods, norm_mlp):
    tm, tn = TOK_TILE, OUT_COL_TILE
    nt = D_MODEL // tn

    def prompt_rows(width):
        return pl.BlockSpec((tm, width), lambda i, n: (jnp.minimum(i, N_PROMPT_TILES - 1), 0))

    def sample_rows(width):
        return pl.BlockSpec((tm, width), lambda i, n: (jnp.maximum(i - N_PROMPT_TILES, 0), 0))

    return pl.pallas_call(
        _outproj_kernel,
        out_shape=(jax.ShapeDtypeStruct((N_TOK, D_MODEL), F32), jax.ShapeDtypeStruct((N_TOK, D_MODEL), BF16)),
        grid=(N_TILES, nt),
        in_specs=[prompt_rows(ATT_WIDTH), sample_rows(ATT_WIDTH), prompt_rows(HG_VW), sample_rows(HG_VW),
                  pl.BlockSpec((ATT_WIDTH, tn), lambda i, n: (0, n)),
                  pl.BlockSpec((HG_VW, tn), lambda i, n: (1, n)),
                  pl.BlockSpec((tm, tn), lambda i, n: (jnp.minimum(i, N_PROMPT_TILES - 1), n)),
                  pl.BlockSpec((tm, tn), lambda i, n: (jnp.maximum(i - N_PROMPT_TILES, 0), n)),
                  _mod_spec(0, 2, tm, width=tn, col=lambda i, n: n),
                  pl.BlockSpec((1, D_MODEL), lambda i, n: (0, 0)),
                  _mod_spec(0, 3, tm), _mod_spec(0, 4, tm)],
        out_specs=(pl.BlockSpec((tm, tn), lambda i, n: (i, n)),
                   pl.BlockSpec((tm, D_MODEL), lambda i, n: (i, 0))),
        scratch_shapes=[pltpu.VMEM((nt, tm, tn), F32)],
        compiler_params=_params(("arbitrary", "arbitrary")),
        name="out_projection",
    )(att_p, att_s, hg_p, hg_s, w_out, w_out, xp, xs, mods, norm_mlp, mods, mods)


def _mlp_kernel(h_ref, w1_ref, w2_ref, x_ref, g2_ref, *rest, final):
    if final:
        fin_ref, o_ref, res_ref = rest
    else:
        o_ref, res_ref = rest
    j = pl.program_id(1)
    res_ref[j] = x_ref[...]

    @pl.when(j == 0)
    def _():
        o_ref[...] = jnp.zeros(o_ref.shape, F32)

    a = jnp.square(jnp.maximum(_bdot(h_ref[...], w1_ref[...].astype(BF16)), 0.0)).astype(BF16)
    for n in range(D_MODEL // MLP_OUT_CHUNK):
        cols = slice(n * MLP_OUT_CHUNK, (n + 1) * MLP_OUT_CHUNK)
        o_ref[:, cols] += _bdot(a, w2_ref[:, cols].astype(BF16))

    @pl.when(j == MLP_STEPS - 1)
    def _():
        for k in range(MLP_STEPS):
            cols = slice(k * MLP_RES_COLS, (k + 1) * MLP_RES_COLS)
            o_ref[:, cols] = res_ref[k] + g2_ref[:, cols] * o_ref[:, cols]
        if final:
            o_ref[...] = _rms(o_ref[...], fin_ref[...])


def _mlp(h, x, w1, w2, mods, layer, tile0, n_tiles, final_norm=None):
    tm, th = TOK_TILE, FF_TILE
    final = final_norm is not None
    in_specs = [pl.BlockSpec((tm, D_MODEL), lambda i, j: (tile0 + i, 0), pipeline_mode=pl.Buffered(1)),
                pl.BlockSpec((None, D_MODEL, th), lambda i, j: (layer, 0, j)),
                pl.BlockSpec((None, th, D_MODEL), lambda i, j: (layer, j, 0)),
                pl.BlockSpec((tm, MLP_RES_COLS), lambda i, j: (tile0 + i, j)),
                _mod_spec(layer, 5, tm, tile_of=lambda i, j: tile0 + i)]
    args = [h, w1, w2, x, mods]
    if final:
        in_specs.append(pl.BlockSpec((1, D_MODEL), lambda i, j: (0, 0)))
        args.append(final_norm)
    return pl.pallas_call(
        functools.partial(_mlp_kernel, final=final),
        out_shape=jax.ShapeDtypeStruct((n_tiles * tm, D_MODEL), F32),
        grid=(n_tiles, MLP_STEPS),
        in_specs=in_specs,
        out_specs=pl.BlockSpec((tm, D_MODEL), lambda i, j: (i, 0)),
        scratch_shapes=[pltpu.VMEM((MLP_STEPS, tm, MLP_RES_COLS), F32)],
        compiler_params=_params(("arbitrary", "arbitrary")),
        name="mlp_final" if final else "mlp",
    )(*args)


def _pool_kernel(x_ref, wp_ref, ps_ref, gain1_ref, sh1_ref, sc1_ref, g1_ref, gain2_ref, sh2_ref, sc2_ref,
                 x3_ref, h4_ref):
    i = pl.program_id(0)
    tm = TOK_TILE
    seq = jnp.where(i < N_PROMPT_TILES, SEQ, DEC_SEQ)
    pos = lax.broadcasted_iota(jnp.int32, (tm, POOL_GROUP), 0) & (seq - 1)

    def shifted(a, off):
        r = pltpu.roll(a, (-off) % tm, 0)
        return jnp.where(jnp.logical_and(pos + off >= 0, pos + off < seq), r, 0.0)

    for g, w in enumerate(POOL_WINDOWS):
        cols = slice(g * POOL_GROUP, (g + 1) * POOL_GROUP)
        x = x_ref[:, cols]
        ms = jnp.mean(x_ref[...] * x_ref[...], axis=-1, keepdims=True) if g == 0 else ms
        h = x * lax.rsqrt(ms + EPS) * gain1_ref[:, cols] * (1.0 + sc1_ref[:, cols]) + sh1_ref[:, cols]
        back = h
        fwd = h
        m = 1
        while m < w // 2:
            back = back + shifted(back, -m)
            fwd = fwd + shifted(fwd, m)
            m *= 2
        total = shifted(back, -1) + fwd
        count = (jnp.minimum(pos + (w - w // 2), seq) - jnp.maximum(pos - w // 2, 0)).astype(F32)
        pooled = (total / count - h).astype(BF16)
        mix = _bdot(pooled, wp_ref[g].astype(BF16)) * ps_ref[:, cols]
        x3_ref[:, cols] = x + g1_ref[:, cols] * mix
    x3 = x3_ref[...]
    h4_ref[...] = _modulate(x3, gain2_ref[...], sh2_ref[...], sc2_ref[...]).astype(BF16)


def _pool_mixer(x, w_pool, pool_scale, norm_mix, norm_mlp, mods):
    tm = TOK_TILE
    vec = pl.BlockSpec((1, D_MODEL), lambda i: (0, 0))
    tile = pl.BlockSpec((tm, D_MODEL), lambda i: (i, 0))
    return pl.pallas_call(
        _pool_kernel,
        out_shape=(jax.ShapeDtypeStruct((N_TOK, D_MODEL), F32), jax.ShapeDtypeStruct((N_TOK, D_MODEL), BF16)),
        grid=(N_TILES,),
        in_specs=[tile, pl.BlockSpec((len(POOL_WINDOWS), POOL_GROUP, POOL_GROUP), lambda i: (0, 0, 0)),
                  vec, vec, _mod_spec(1, 0, tm), _mod_spec(1, 1, tm), _mod_spec(1, 2, tm),
                  vec, _mod_spec(1, 3, tm), _mod_spec(1, 4, tm)],
        out_specs=(tile, tile),
        compiler_params=_params(("arbitrary",)),
        name="pool_mixer",
    )(x, w_pool, pool_scale, norm_mix, mods, mods, mods, norm_mlp, mods, mods)


def kernel(x_prompt, x_sample, cache_k, cache_v, state_hgrn_fwd, state_hgrn_bwd, c, c_ctx, w_ada, b_ada,
           norm_mix, norm_mlp, w_in_ab, w_out_ab, q_norm, k_norm, hg_norm, lb_raw, w_pool, pool_scale,
           w_mlp_in, w_mlp_out, final_norm):
    xp = x_prompt.reshape(N_PROMPT, D_MODEL)
    xs = x_sample.reshape(N_SAMPLE, D_MODEL)
    cv = jnp.concatenate([c_ctx[None, :], c, jnp.zeros((ADA_ROWS - 1 - DEC_BATCH, D_MODEL), F32)], axis=0)
    mods = _ada_table(cv, w_ada, b_ada).reshape(DEPTH, ADA_ROWS, 1, N_MOD * D_MODEL)

    h0 = _modulate0(xp, xs, norm_mix, mods)
    proj, new_k, new_v = _in_projection(h0, w_in_ab[0], q_norm[0:1], k_norm[0:1])
    ctx_k = cache_k[:, 0].reshape(DEC_BATCH, PAST_LEN, KV_WIDTH)
    ctx_v = cache_v[:, 0].reshape(DEC_BATCH, PAST_LEN, KV_WIDTH)
    att_p = _attention(proj, BATCH, SEQ, 0, SEQ)
    att_s = _attention(proj, DEC_BATCH, DEC_SEQ, N_PROMPT, 256, ctx=(ctx_k, ctx_v))
    consts = _hgrn_constants()
    hg_p, s_fwd, s_bwd = _hgrn(proj, lb_raw, hg_norm[0:1], consts, BATCH, SEQ, 0)
    s0 = (state_hgrn_fwd.reshape(DEC_BATCH, HG_HEADS, HG_DK, HG_DV),
          state_hgrn_bwd.reshape(DEC_BATCH, HG_HEADS, HG_DK, HG_DV))
    (hg_s,) = _hgrn(proj, lb_raw, hg_norm[0:1], consts, DEC_BATCH, DEC_SEQ, N_PROMPT, s0=s0)
    x1, h2 = _out_projection(att_p, att_s, hg_p, hg_s, w_out_ab[0], xp, xs, mods, norm_mlp[0:1])
    x2 = _mlp(h2, x1, w_mlp_in, w_mlp_out, mods, 0, 0, N_TILES)

    x3, h4 = _pool_mixer(x2, w_pool[0], pool_scale[0:1], norm_mix[1:2], norm_mlp[1:2], mods)
    fin = final_norm[None, :]
    y_prompt = _mlp(h4, x3, w_mlp_in, w_mlp_out, mods, 1, 0, N_PROMPT_TILES, final_norm=fin)
    y_sample = _mlp(h4, x3, w_mlp_in, w_mlp_out, mods, 1, N_PROMPT_TILES, N_TILES - N_PROMPT_TILES,
                    final_norm=fin)

    return (y_prompt.reshape(BATCH, SEQ, D_MODEL), y_sample.reshape(DEC_BATCH, DEC_SEQ, D_MODEL),
            new_k.reshape(BATCH, 1, SEQ, N_KV_HEADS, HEAD_DIM), new_v.reshape(BATCH, 1, SEQ, N_KV_HEADS, HEAD_DIM),
            s_fwd.reshape(BATCH, 1, HG_HEADS, HG_DK, HG_DV), s_bwd.reshape(BATCH, 1, HG_HEADS, HG_DK, HG_DV))
```

```python
import functools

import numpy as np
import jax
import jax.numpy as jnp
from jax import lax
from jax.experimental import pallas as pl
from jax.experimental.pallas import tpu as pltpu

F32 = jnp.float32
BF16 = jnp.bfloat16

D_MODEL = 2048
BATCH = 16
SEQ = 256
DEPTH = 2
DEC_BATCH = 2
DEC_SEQ = 1024
PAST_LEN = 256
GRID_W = 64
HEAD_DIM = 128
N_Q_HEADS = 8
N_KV_HEADS = 2
Q_PER_KV = N_Q_HEADS // N_KV_HEADS
ATT_WIDTH = N_Q_HEADS * HEAD_DIM
KV_WIDTH = N_KV_HEADS * HEAD_DIM
HG_HEADS = 8
HG_DK = 128
HG_DV = 128
HG_KW = HG_HEADS * HG_DK
HG_VW = HG_HEADS * HG_DV
IN_AB = ATT_WIDTH + 2 * KV_WIDTH + 3 * HG_KW + 2 * HG_VW
MIX_WIDTH = ATT_WIDTH + HG_VW
POOL_WINDOWS = (2, 4, 8, 16)
POOL_GROUP = D_MODEL // len(POOL_WINDOWS)
D_FF = 4 * D_MODEL
ROPE_THETA = 10000.0
ROPE_HALF = HEAD_DIM // 2
EPS = 1e-6
N_MOD = 6

N_PROMPT = BATCH * SEQ
N_SAMPLE = DEC_BATCH * DEC_SEQ
N_TOK = N_PROMPT + N_SAMPLE
ADA_ROWS = 16
ADA_COL_TILE = 2048
TOK_TILE = 1024
N_TILES = N_TOK // TOK_TILE
N_PROMPT_TILES = N_PROMPT // TOK_TILE
PROJ_TOK_TILE = 2048
PROJ_COL_TILE = 512
OUT_COL_TILE = 512
FF_TILE = 512
MLP_STEPS = D_FF // FF_TILE
MLP_RES_COLS = D_MODEL // MLP_STEPS
MLP_OUT_CHUNK = 512
HG_CHUNK = 128
HG_LEVELS = 7
HG_HEAD_BLOCK = 4
HG_SAFE_EXPONENT = 80.0
VMEM_LIMIT = 56 * 2 ** 20

COL_Q = 0
COL_K = ATT_WIDTH
COL_V = COL_K + KV_WIDTH
COL_HQ = COL_V + KV_WIDTH
COL_ZF = COL_HQ + HG_KW
COL_HI = COL_ZF + 2 * HG_KW
COL_HG = COL_HI + HG_VW


def _params(semantics):
    return pltpu.CompilerParams(dimension_semantics=semantics, vmem_limit_bytes=VMEM_LIMIT)


def _sigmoid(x):
    return 1.0 / (1.0 + jnp.exp(-x))


def _silu(x):
    return x * _sigmoid(x)


def _rms(x, gain):
    return x * lax.rsqrt(jnp.mean(x * x, axis=-1, keepdims=True) + EPS) * gain


def _bdot(a, b):
    return jnp.dot(a, b, preferred_element_type=F32)


def _mod_row(tile, tile_rows):
    first = N_PROMPT // tile_rows
    per_seq = DEC_SEQ // tile_rows
    return jnp.where(tile < first, 0, 1 + (tile - first) // per_seq)


def _mod_spec(layer, chunk, tile_rows, width=D_MODEL, col=lambda *g: 0, tile_of=lambda *g: g[0]):
    per = D_MODEL // width
    return pl.BlockSpec((None, None, 1, width),
                        lambda *g: (layer, _mod_row(tile_of(*g), tile_rows), 0, chunk * per + col(*g)))


def _ada_kernel(cv_ref, w_ref, b_ref, o_ref):
    s = _silu(cv_ref[...]).astype(BF16)
    o_ref[...] = _bdot(s, w_ref[...].astype(BF16)) + b_ref[...]


def _ada_table(cv, w_ada, b_ada):
    tn = ADA_COL_TILE
    n = N_MOD * D_MODEL
    return pl.pallas_call(
        _ada_kernel,
        out_shape=jax.ShapeDtypeStruct((DEPTH, ADA_ROWS, n), F32),
        grid=(DEPTH, n // tn),
        in_specs=[pl.BlockSpec((ADA_ROWS, D_MODEL), lambda l, j: (0, 0)),
                  pl.BlockSpec((None, D_MODEL, tn), lambda l, j: (l, 0, j)),
                  pl.BlockSpec((None, 1, tn), lambda l, j: (l, 0, j))],
        out_specs=pl.BlockSpec((None, ADA_ROWS, tn), lambda l, j: (l, 0, j)),
        compiler_params=_params(("arbitrary", "arbitrary")),
        name="ada_table",
    )(cv, w_ada, b_ada.reshape(DEPTH, 1, n))


def _modulate(x, gain, shift, scale):
    return _rms(x, gain) * (1.0 + scale) + shift


def _mod0_kernel(xp_ref, xs_ref, gain_ref, sh_ref, sc_ref, o_ref):
    i = pl.program_id(0)

    def run(x_ref):
        o_ref[...] = _modulate(x_ref[...], gain_ref[...], sh_ref[...], sc_ref[...]).astype(BF16)

    pl.when(i < N_PROMPT_TILES)(lambda: run(xp_ref))
    pl.when(i >= N_PROMPT_TILES)(lambda: run(xs_ref))


def _modulate0(xp, xs, norm_mix, mods):
    tm = TOK_TILE
    first = N_PROMPT_TILES
    return pl.pallas_call(
        _mod0_kernel,
        out_shape=jax.ShapeDtypeStruct((N_TOK, D_MODEL), BF16),
        grid=(N_TOK // tm,),
        in_specs=[pl.BlockSpec((tm, D_MODEL), lambda i: (jnp.minimum(i, first - 1), 0)),
                  pl.BlockSpec((tm, D_MODEL), lambda i: (jnp.maximum(i - first, 0), 0)),
                  pl.BlockSpec((1, D_MODEL), lambda i: (0, 0)),
                  _mod_spec(0, 0, tm), _mod_spec(0, 1, tm)],
        out_specs=pl.BlockSpec((tm, D_MODEL), lambda i: (i, 0)),
        compiler_params=_params(("arbitrary",)),
        name="modulate0",
    )(xp, xs, norm_mix[0:1], mods, mods)


def _rope(y, cos, sin, perm2):
    hi = y.astype(BF16)
    lo = (y - hi.astype(F32)).astype(BF16)
    rot = _bdot(jnp.concatenate([hi, lo], axis=1), perm2)
    return y * cos + rot * sin


def _inproj_kernel(h_ref, w_ref, qg_ref, kg_ref, cos_ref, sin_ref, perm_ref, p_ref, nk_ref, nv_ref):
    i = pl.program_id(0)
    j = pl.program_id(1)
    latent = i >= N_PROMPT // PROJ_TOK_TILE
    heads = PROJ_COL_TILE // HEAD_DIM
    kv_tile = COL_K // PROJ_COL_TILE
    is_q = j < kv_tile

    def attention_tile(rope):
        gain = jnp.where(is_q, qg_ref[...], kg_ref[...])
        w = w_ref[...].astype(BF16)
        for s in range(PROJ_TOK_TILE // DEC_SEQ):
            rows = slice(s * DEC_SEQ, (s + 1) * DEC_SEQ)
            acc = _bdot(h_ref[rows, :], w)
            for hh in range(heads):
                cols = slice(hh * HEAD_DIM, (hh + 1) * HEAD_DIM)
                x = acc[:, cols]
                y = _rms(x, gain)
                if rope:
                    y = _rope(y, cos_ref[...], sin_ref[...], perm_ref[...])
                if hh >= N_KV_HEADS:
                    y = jnp.where(is_q, y, x)
                p_ref[rows, cols] = y

    pl.when(jnp.logical_and(j <= kv_tile, latent))(lambda: attention_tile(True))
    pl.when(jnp.logical_and(j <= kv_tile, jnp.logical_not(latent)))(lambda: attention_tile(False))

    @pl.when(jnp.logical_and(j == kv_tile, jnp.logical_not(latent)))
    def _():
        nk_ref[...] = p_ref[:, :KV_WIDTH]
        nv_ref[...] = p_ref[:, KV_WIDTH:]

    @pl.when(j > kv_tile)
    def _():
        is_silu = jnp.logical_or(j < COL_ZF // PROJ_COL_TILE, j >= COL_HG // PROJ_COL_TILE)
        w = w_ref[...].astype(BF16)
        for s in range(PROJ_TOK_TILE // DEC_SEQ):
            rows = slice(s * DEC_SEQ, (s + 1) * DEC_SEQ)
            acc = _bdot(h_ref[rows, :], w)
            p_ref[rows, :] = jnp.where(is_silu, _silu(acc), acc)


def _rope_tables():
    t = np.arange(DEC_SEQ)
    row = (t // GRID_W).astype(np.float32)
    col = (t % GRID_W).astype(np.float32)
    inv = (np.float32(ROPE_THETA) ** (-np.arange(0, ROPE_HALF, 2, dtype=np.float32) / np.float32(ROPE_HALF))).astype(np.float32)
    ar = row[:, None] * inv
    ac = col[:, None] * inv
    ang = np.concatenate([ar, ar, ac, ac], axis=-1).astype(np.float32)
    cos = np.cos(ang).astype(np.float32)
    sin = np.sin(ang).astype(np.float32)
    qw = ROPE_HALF // 2
    perm = np.zeros((HEAD_DIM, HEAD_DIM), np.float32)
    for k in range(qw):
        perm[qw + k, k] = -1.0
        perm[k, qw + k] = 1.0
        perm[3 * qw + k, 2 * qw + k] = -1.0
        perm[2 * qw + k, 3 * qw + k] = 1.0
    return jnp.asarray(cos), jnp.asarray(sin), jnp.asarray(np.concatenate([perm, perm], axis=0), BF16)


def _in_projection(h, w_in, q_gain, k_gain):
    tm, tn = PROJ_TOK_TILE, PROJ_COL_TILE
    n_prompt_tiles = N_PROMPT // tm
    cos, sin, perm2 = _rope_tables()
    table = pl.BlockSpec((DEC_SEQ, HEAD_DIM), lambda i, j: (0, 0))
    gain = pl.BlockSpec((1, HEAD_DIM), lambda i, j: (0, 0))
    state = pl.BlockSpec((tm, KV_WIDTH), lambda i, j: (jnp.minimum(i, n_prompt_tiles - 1), 0))
    return pl.pallas_call(
        _inproj_kernel,
        out_shape=(jax.ShapeDtypeStruct((N_TOK, IN_AB), F32),
                   jax.ShapeDtypeStruct((N_PROMPT, KV_WIDTH), F32),
                   jax.ShapeDtypeStruct((N_PROMPT, KV_WIDTH), F32)),
        grid=(N_TOK // tm, IN_AB // tn),
        in_specs=[pl.BlockSpec((tm, D_MODEL), lambda i, j: (i, 0)),
                  pl.BlockSpec((D_MODEL, tn), lambda i, j: (0, j)),
                  gain, gain, table, table, pl.BlockSpec((2 * HEAD_DIM, HEAD_DIM), lambda i, j: (0, 0))],
        out_specs=(pl.BlockSpec((tm, tn), lambda i, j: (i, j)), state, state),
        compiler_params=_params(("arbitrary", "arbitrary")),
        name="in_projection",
    )(h, w_in, q_gain, k_gain, cos, sin, perm2)


def _attn_kernel(*refs, has_ctx):
    if has_ctx:
        q_ref, k_ref, v_ref, ck_ref, cv_ref, o_ref = refs
    else:
        q_ref, k_ref, v_ref, o_ref = refs
    scale = HEAD_DIM ** -0.5
    nt = (((1,), (1,)), ((), ()))
    k = k_ref[...].astype(BF16)
    v = v_ref[...].astype(BF16)
    if has_ctx:
        ck = ck_ref[...].astype(BF16)
        cv = cv_ref[...].astype(BF16)
    for g in range(Q_PER_KV):
        q = q_ref[:, g * HEAD_DIM:(g + 1) * HEAD_DIM].astype(BF16)
        s = lax.dot_general(q, k, nt, preferred_element_type=F32) * scale
        m = jnp.max(s, axis=-1, keepdims=True)
        if has_ctx:
            sc = lax.dot_general(q, ck, nt, preferred_element_type=F32) * scale
            m = jnp.maximum(m, jnp.max(sc, axis=-1, keepdims=True))
        p = jnp.exp(s - m)
        den = jnp.sum(p, axis=-1, keepdims=True)
        o = _bdot(p.astype(BF16), v)
        if has_ctx:
            pc = jnp.exp(sc - m)
            den = den + jnp.sum(pc, axis=-1, keepdims=True)
            o = o + _bdot(pc.astype(BF16), cv)
        o_ref[:, g * HEAD_DIM:(g + 1) * HEAD_DIM] = (o / den).astype(o_ref.dtype)


def _attention(p, n_batch, seq, row0, tq, ctx=None):
    q_blocks = seq // tq
    gw = Q_PER_KV * HEAD_DIM
    in_specs = [
        pl.BlockSpec((tq, gw), lambda b, h, qi: (row0 // tq + b * q_blocks + qi, h)),
        pl.BlockSpec((seq, HEAD_DIM), lambda b, h, qi: (row0 // seq + b, COL_K // HEAD_DIM + h)),
        pl.BlockSpec((seq, HEAD_DIM), lambda b, h, qi: (row0 // seq + b, COL_V // HEAD_DIM + h)),
    ]
    args = [p, p, p]
    if ctx is not None:
        ctx_spec = pl.BlockSpec((None, PAST_LEN, HEAD_DIM), lambda b, h, qi: (b, 0, h))
        in_specs += [ctx_spec, ctx_spec]
        args += [ctx[0], ctx[1]]
    return pl.pallas_call(
        functools.partial(_attn_kernel, has_ctx=ctx is not None),
        out_shape=jax.ShapeDtypeStruct((n_batch * seq, ATT_WIDTH), BF16),
        grid=(n_batch, N_KV_HEADS, q_blocks),
        in_specs=in_specs,
        out_specs=pl.BlockSpec((tq, gw), lambda b, h, qi: (b * q_blocks + qi, h)),
        compiler_params=_params(("arbitrary", "arbitrary", "arbitrary")),
        name="attention_latent" if ctx is not None else "attention_prompt",
    )(*args)


def _hgrn_constants():
    c = HG_CHUNK
    t = np.arange(c)
    cums, sels, pairs, scans, diags = [], [], [], [], []
    for d in range(2):
        pos = t if d == 0 else c - 1 - t
        pu, pt = pos[None, :], pos[:, None]
        ms, ss, ws = [], [], []
        for l in range(HG_LEVELS):
            m = c >> l
            blk = pos // m
            mid = (blk * m + m // 2)[:, None]
            late = ((pos % m) >= m // 2)
            ms.append(np.where(late[:, None], (pu >= mid) & (pu <= pt), (pu > pt) & (pu < mid)))
            ss.append(np.broadcast_to(late[:, None], (c, c)))
            ws.append((blk[:, None] == blk[None, :]) & late[:, None] & ~late[None, :])
        ms.append(pu <= pt)
        ms.append(pu > pt)
        cums.append(np.concatenate(ms, axis=0))
        sels.append(np.stack(ss))
        pairs.append(np.stack(ws))
        scans.append(np.concatenate([pu <= pt, pu <= pt], axis=1))
        diags.append(((pos // (c // 2))[:, None] == (pos // (c // 2))[None, :]) & (pu <= pt))
    return (jnp.asarray(np.stack(cums), BF16), jnp.asarray(np.stack(sels), F32),
            jnp.asarray(np.stack(pairs), F32), jnp.asarray(np.stack(scans), BF16),
            jnp.asarray(np.stack(diags), F32))


def _hgrn_kernel(*refs, n_chunks, has_s0):
    (hq_ref, zf_ref, zb_ref, hi_ref, hg_ref, lb_ref, og_ref, cum_ref, sel_ref, pair_ref, scan_ref,
     diag_ref) = refs[:12]
    refs = refs[12:]
    if has_s0:
        s0f_ref, s0b_ref, o_ref = refs[:3]
        refs = refs[3:]
    else:
        o_ref, sf_ref, sb_ref = refs[:3]
        refs = refs[3:]
    st_ref, acc_ref = refs
    c = HG_CHUNK
    half = c // 2
    nt = (((1,), (1,)), ((), ()))
    tn = (((0,), (0,)), ((), ()))

    def direction(d, z_ref):
        raw = lb_ref[d]
        e = jnp.exp(raw - jnp.max(raw, axis=0, keepdims=True))
        lb = e[0:1] / jnp.sum(e, axis=0, keepdims=True)

        def initial_state(hh):
            if has_s0:
                return (s0f_ref, s0b_ref)[d][hh].T
            return jnp.zeros((HG_DV, HG_DK), F32)

        def gates(rows):
            f = lb + (1.0 - lb) * _sigmoid(z_ref[rows, :])
            logf = jnp.log(f)
            hi16 = logf.astype(BF16)
            lo16 = (logf - hi16.astype(F32)).astype(BF16)
            return f, hi16, lo16

        def row(p):
            t = p if d == 0 else c - 1 - p
            return slice(t, t + 1)

        early, late = (slice(0, half), slice(half, c)) if d == 0 else (slice(half, c), slice(0, half))

        def in_row_order(x_early, x_late):
            return jnp.concatenate([x_early, x_late] if d == 0 else [x_late, x_early], axis=0)

        def emit(rows, cols, o):
            if d == 0:
                acc_ref[rows, cols] = o
            else:
                tot = acc_ref[rows, cols] + o
                o_ref[rows, cols] = (_rms(tot, og_ref[...]) * hg_ref[rows, cols]).astype(o_ref.dtype)

        def two_level_operands(rows):
            f, hi16, lo16 = gates(rows)
            b = _bdot(scan_ref[d], jnp.concatenate([hi16, lo16], axis=0))
            kk = 1.0 - f
            q = hq_ref[rows, :]
            r_mid = b[row(half - 1)]
            x1 = in_row_order(kk[early] * jnp.exp(r_mid - b[early]), q[late] * jnp.exp(b[late] - r_mid)).astype(BF16)
            dq = in_row_order(b[early] - b[row(half // 2 - 1)], b[late] - b[row(half + half // 2 - 1)])
            b_end = b[row(c - 1)]
            span = jnp.maximum(
                jnp.maximum(b[row(0)] - b[row(half // 2 - 1)], b[row(half // 2 - 1)] - b[row(half - 1)]),
                jnp.maximum(b[row(half)] - b[row(half + half // 2 - 1)],
                            b[row(half + half // 2 - 1)] - b[row(c - 1)]))
            return dict(x1=x1, xq=(q * jnp.exp(dq)).astype(BF16), xk=(kk * jnp.exp(-dq)).astype(BF16),
                        q_in=(q * jnp.exp(b)).astype(BF16), k_out=(kk * jnp.exp(b_end - b)).astype(BF16),
                        a_end=jnp.exp(b_end), iv=hi_ref[rows, :].astype(BF16), span=span)

        def two_level_chunk(rows, ops, states):
            new_states = []
            for hh in range(HG_HEAD_BLOCK):
                cols = slice(hh * HG_DK, (hh + 1) * HG_DK)
                g1 = lax.dot_general(ops["x1"][:, cols], ops["x1"][:, cols], nt, preferred_element_type=F32)
                g2 = lax.dot_general(ops["xq"][:, cols], ops["xk"][:, cols], nt, preferred_element_type=F32)
                att = jnp.where(pair_ref[d, 0] > 0.5, g1, jnp.where(diag_ref[d] > 0.5, g2, 0.0))
                st = states[hh]
                o = (_bdot(att.astype(BF16), ops["iv"][:, cols])
                     + lax.dot_general(ops["q_in"][:, cols], st.astype(BF16), nt, preferred_element_type=F32))
                dst = lax.dot_general(ops["iv"][:, cols], ops["k_out"][:, cols], tn, preferred_element_type=F32)
                new_states.append(ops["a_end"][:, cols] * st + dst)
                emit(rows, cols, o)
            return new_states

        def all_levels(rows):
            f, hi16, lo16 = gates(rows)
            cum = cum_ref[d]
            eall = jnp.exp(_bdot(cum, hi16) + _bdot(cum, lo16))
            for hh in range(HG_HEAD_BLOCK):
                cols = slice(hh * HG_DK, (hh + 1) * HG_DK)
                kk = 1.0 - f[:, cols]
                q = hq_ref[rows, cols]
                iv = hi_ref[rows, cols]
                iv16 = iv.astype(BF16)
                att = jnp.zeros((c, c), F32)
                for l in range(HG_LEVELS):
                    x = (kk + sel_ref[d, l] * (q - kk)) * eall[l * c:(l + 1) * c, cols]
                    xb = x.astype(BF16)
                    att = att + pair_ref[d, l] * lax.dot_general(xb, xb, nt, preferred_element_type=F32)
                e_in = eall[HG_LEVELS * c:(HG_LEVELS + 1) * c, cols]
                e_out = eall[(HG_LEVELS + 1) * c:, cols]
                st = st_ref[hh]
                o = (_bdot(att.astype(BF16), iv16)
                     + jnp.sum(q * kk, axis=-1, keepdims=True) * iv
                     + lax.dot_general((q * e_in).astype(BF16), st.astype(BF16), nt, preferred_element_type=F32))
                dst = lax.dot_general(iv16, (kk * e_out).astype(BF16), tn, preferred_element_type=F32)
                st_ref[hh] = (e_in[0:1] * e_out[0:1]) * st + dst
                emit(rows, cols, o)

        def chunk_rows(ci):
            cidx = ci if d == 0 else n_chunks - 1 - ci
            return slice(cidx * c, (cidx + 1) * c)

        states = [initial_state(hh) for hh in range(HG_HEAD_BLOCK)]
        worst = jnp.zeros((1, HG_HEAD_BLOCK * HG_DK), F32)
        for ci in range(n_chunks):
            ops = two_level_operands(chunk_rows(ci))
            states = two_level_chunk(chunk_rows(ci), ops, states)
            worst = jnp.maximum(worst, ops["span"])
        for hh in range(HG_HEAD_BLOCK):
            st_ref[hh] = states[hh]

        @pl.when(jnp.logical_not(jnp.max(worst) <= HG_SAFE_EXPONENT))
        def _():
            for hh in range(HG_HEAD_BLOCK):
                st_ref[hh] = initial_state(hh)

            def chunk(ci, carry):
                cidx = ci if d == 0 else n_chunks - 1 - ci
                all_levels(pl.ds(pl.multiple_of(cidx * c, c), c))
                return carry

            lax.fori_loop(0, n_chunks, chunk, 0)

        if not has_s0:
            for hh in range(HG_HEAD_BLOCK):
                (sf_ref, sb_ref)[d][hh] = st_ref[hh].T

    direction(0, zf_ref)
    direction(1, zb_ref)


def _hgrn(p, lb_raw, o_gain, consts, n_batch, seq, row0, s0=None):
    hb = HG_HEAD_BLOCK
    w = hb * HG_DK

    def seg(col):
        return pl.BlockSpec((seq, w), lambda b, h: (row0 // seq + b, col // w + h))

    def const(a):
        return pl.BlockSpec(a.shape, lambda b, h: (0,) * a.ndim)

    in_specs = [seg(COL_HQ), seg(COL_ZF), seg(COL_ZF + HG_KW), seg(COL_HI), seg(COL_HG),
                pl.BlockSpec((2, DEPTH + 1, w), lambda b, h: (0, 0, h)),
                pl.BlockSpec((1, HG_DV), lambda b, h: (0, 0))] + [const(a) for a in consts]
    args = [p, p, p, p, p, lb_raw, o_gain, *consts]
    has_s0 = s0 is not None
    st_spec = pl.BlockSpec((None, hb, HG_DK, HG_DV), lambda b, h: (b, h, 0, 0))
    if has_s0:
        in_specs += [st_spec, st_spec]
        args += [s0[0], s0[1]]
    out_shape = [jax.ShapeDtypeStruct((n_batch * seq, HG_VW), BF16)]
    out_specs = [pl.BlockSpec((seq, w), lambda b, h: (b, h))]
    if not has_s0:
        st_shape = jax.ShapeDtypeStruct((n_batch, HG_HEADS, HG_DK, HG_DV), F32)
        out_shape += [st_shape, st_shape]
        out_specs += [st_spec, st_spec]
    return pl.pallas_call(
        functools.partial(_hgrn_kernel, n_chunks=seq // HG_CHUNK, has_s0=has_s0),
        out_shape=tuple(out_shape),
        grid=(n_batch, HG_HEADS // hb),
        in_specs=in_specs,
        out_specs=tuple(out_specs),
        scratch_shapes=[pltpu.VMEM((hb, HG_DV, HG_DK), F32), pltpu.VMEM((seq, w), F32)],
        compiler_params=_params(("arbitrary", "arbitrary")),
        name="hgrn_latent" if has_s0 else "hgrn_prompt",
    )(*args)


def _outproj_kernel(attp_ref, atts_ref, hgp_ref, hgs_ref, wa_ref, wb_ref, xp_ref, xs_ref, g1_ref, gain_ref,
                    sh_ref, sc_ref, x1_ref, h2_ref, full_ref, wcache_ref):
    i = pl.program_id(0)
    n = pl.program_id(1)

    @pl.when(i == 0)
    def _():
        wcache_ref[n, 0] = wa_ref[...].astype(BF16)
        wcache_ref[n, 1] = wb_ref[...].astype(BF16)

    def run(att_ref, hg_ref, x_ref):
        acc = _bdot(att_ref[...], wcache_ref[n, 0]) + _bdot(hg_ref[...], wcache_ref[n, 1])
        x1 = x_ref[...] + g1_ref[...] * acc
        x1_ref[...] = x1
        full_ref[n] = x1

    pl.when(i < N_PROMPT_TILES)(lambda: run(attp_ref, hgp_ref, xp_ref))
    pl.when(i >= N_PROMPT_TILES)(lambda: run(atts_ref, hgs_ref, xs_ref))

    @pl.when(n == D_MODEL // OUT_COL_TILE - 1)
    def _():
        nt = D_MODEL // OUT_COL_TILE
        ss = full_ref[0] * full_ref[0]
        ms = jnp.sum(ss, axis=-1, keepdims=True)
        for k in range(1, nt):
            ms = ms + jnp.sum(full_ref[k] * full_ref[k], axis=-1, keepdims=True)
        rstd = lax.rsqrt(ms / D_MODEL + EPS)
        for k in range(nt):
            cols = slice(k * OUT_COL_TILE, (k + 1) * OUT_COL_TILE)
            y = full_ref[k] * rstd * gain_ref[:, cols]
            h2_ref[:, cols] = (y * (1.0 + sc_ref[:, cols]) + sh_ref[:, cols]).astype(BF16)


def _out_projection(att_p, att_s, hg_p, hg_s, w_out, xp, xs, mods, norm_mlp):
    tm, tn = TOK_TILE, OUT_COL_TILE
    nt = D_MODEL // tn

    def prompt_rows(width):
        return pl.BlockSpec((tm, width), lambda i, n: (jnp.minimum(i, N_PROMPT_TILES - 1), 0),
                            pipeline_mode=pl.Buffered(1))

    def sample_rows(width):
        return pl.BlockSpec((tm, width), lambda i, n: (jnp.maximum(i - N_PROMPT_TILES, 0), 0),
                            pipeline_mode=pl.Buffered(1))

    return pl.pallas_call(
        _outproj_kernel,
        out_shape=(jax.ShapeDtypeStruct((N_TOK, D_MODEL), F32), jax.ShapeDtypeStruct((N_TOK, D_MODEL), BF16)),
        grid=(N_TILES, nt),
        in_specs=[prompt_rows(ATT_WIDTH), sample_rows(ATT_WIDTH), prompt_rows(HG_VW), sample_rows(HG_VW),
                  pl.BlockSpec((ATT_WIDTH, tn), lambda i, n: (0, jnp.where(i == 0, n, nt - 1))),
                  pl.BlockSpec((HG_VW, tn), lambda i, n: (1, jnp.where(i == 0, n, nt - 1))),
                  pl.BlockSpec((tm, tn), lambda i, n: (jnp.minimum(i, N_PROMPT_TILES - 1), n)),
                  pl.BlockSpec((tm, tn), lambda i, n: (jnp.maximum(i - N_PROMPT_TILES, 0), n)),
                  _mod_spec(0, 2, tm, width=tn, col=lambda i, n: n),
                  pl.BlockSpec((1, D_MODEL), lambda i, n: (0, 0)),
                  _mod_spec(0, 3, tm), _mod_spec(0, 4, tm)],
        out_specs=(pl.BlockSpec((tm, tn), lambda i, n: (i, n)),
                   pl.BlockSpec((tm, D_MODEL), lambda i, n: (i, 0))),
        scratch_shapes=[pltpu.VMEM((nt, tm, tn), F32), pltpu.VMEM((nt, 2, ATT_WIDTH, tn), BF16)],
        compiler_params=_params(("arbitrary", "arbitrary")),
        name="out_projection",
    )(att_p, att_s, hg_p, hg_s, w_out, w_out, xp, xs, mods, norm_mlp, mods, mods)


def _mlp_kernel(h_ref, w1_ref, w2_ref, x_ref, g2_ref, *rest, final):
    if final:
        fin_ref, o_ref, res_ref = rest
    else:
        o_ref, res_ref = rest
    j = pl.program_id(1)
    res_ref[j] = x_ref[...]

    def step(first, last):
        a = jnp.square(jnp.maximum(_bdot(h_ref[...], w1_ref[...].astype(BF16)), 0.0)).astype(BF16)
        for n in range(D_MODEL // MLP_OUT_CHUNK):
            cols = slice(n * MLP_OUT_CHUNK, (n + 1) * MLP_OUT_CHUNK)
            p = _bdot(a, w2_ref[:, cols].astype(BF16))
            if not first:
                p = o_ref[:, cols] + p
            if last:
                per = MLP_OUT_CHUNK // MLP_RES_COLS
                res = jnp.concatenate([res_ref[n * per + k] for k in range(per)], axis=1)
                p = res + g2_ref[:, cols] * p
            o_ref[:, cols] = p
        if last and final:
            o_ref[...] = _rms(o_ref[...], fin_ref[...])

    pl.when(j == 0)(lambda: step(True, False))
    pl.when(jnp.logical_and(j > 0, j < MLP_STEPS - 1))(lambda: step(False, False))
    pl.when(j == MLP_STEPS - 1)(lambda: step(False, True))


def _mlp(h, x, w1, w2, mods, layer, tile0, n_tiles, final_norm=None):
    tm, th = TOK_TILE, FF_TILE
    final = final_norm is not None
    in_specs = [pl.BlockSpec((tm, D_MODEL), lambda i, j: (tile0 + i, 0), pipeline_mode=pl.Buffered(1)),
                pl.BlockSpec((None, D_MODEL, th), lambda i, j: (layer, 0, j)),
                pl.BlockSpec((None, th, D_MODEL), lambda i, j: (layer, j, 0)),
                pl.BlockSpec((tm, MLP_RES_COLS), lambda i, j: (tile0 + i, j)),
                _mod_spec(layer, 5, tm, tile_of=lambda i, j: tile0 + i)]
    args = [h, w1, w2, x, mods]
    if final:
        in_specs.append(pl.BlockSpec((1, D_MODEL), lambda i, j: (0, 0)))
        args.append(final_norm)
    return pl.pallas_call(
        functools.partial(_mlp_kernel, final=final),
        out_shape=jax.ShapeDtypeStruct((n_tiles * tm, D_MODEL), F32),
        grid=(n_tiles, MLP_STEPS),
        in_specs=in_specs,
        out_specs=pl.BlockSpec((tm, D_MODEL), lambda i, j: (i, 0)),
        scratch_shapes=[pltpu.VMEM((MLP_STEPS, tm, MLP_RES_COLS), F32)],
        compiler_params=_params(("arbitrary", "arbitrary")),
        name="mlp_final" if final else "mlp",
    )(*args)


def _pool_kernel(x_ref, wp_ref, ps_ref, gain1_ref, sh1_ref, sc1_ref, g1_ref, gain2_ref, sh2_ref, sc2_ref,
                 x3_ref, h4_ref):
    i = pl.program_id(0)
    tm = TOK_TILE
    seq = jnp.where(i < N_PROMPT_TILES, SEQ, DEC_SEQ)
    pos = lax.broadcasted_iota(jnp.int32, (tm, HEAD_DIM), 0) & (seq - 1)

    def widen(m):
        return jnp.concatenate([m] * (POOL_GROUP // HEAD_DIM), axis=1)

    keep = {off: widen(jnp.where(jnp.logical_and(pos + off >= 0, pos + off < seq), 1.0, 0.0))
            for off in (-4, -2, -1, 1, 2, 4)}

    def shifted(a, off):
        return pltpu.roll(a, (-off) % tm, 0) * keep[off]

    for g, w in enumerate(POOL_WINDOWS):
        cols = slice(g * POOL_GROUP, (g + 1) * POOL_GROUP)
        x = x_ref[:, cols]
        ms = jnp.mean(x_ref[...] * x_ref[...], axis=-1, keepdims=True) if g == 0 else ms
        h = x * lax.rsqrt(ms + EPS) * gain1_ref[:, cols] * (1.0 + sc1_ref[:, cols]) + sh1_ref[:, cols]
        back = h
        fwd = h
        m = 1
        while m < w // 2:
            back = back + shifted(back, -m)
            fwd = fwd + shifted(fwd, m)
            m *= 2
        total = shifted(back, -1) + fwd
        count = (jnp.minimum(pos + (w - w // 2), seq) - jnp.maximum(pos - w // 2, 0)).astype(F32)
        pooled = (total * widen(1.0 / count) - h).astype(BF16)
        mix = _bdot(pooled, wp_ref[g].astype(BF16)) * ps_ref[:, cols]
        x3_ref[:, cols] = x + g1_ref[:, cols] * mix
    x3 = x3_ref[...]
    h4_ref[...] = _modulate(x3, gain2_ref[...], sh2_ref[...], sc2_ref[...]).astype(BF16)


def _pool_mixer(x, w_pool, pool_scale, norm_mix, norm_mlp, mods):
    tm = TOK_TILE
    vec = pl.BlockSpec((1, D_MODEL), lambda i: (0, 0))
    tile = pl.BlockSpec((tm, D_MODEL), lambda i: (i, 0))
    return pl.pallas_call(
        _pool_kernel,
        out_shape=(jax.ShapeDtypeStruct((N_TOK, D_MODEL), F32), jax.ShapeDtypeStruct((N_TOK, D_MODEL), BF16)),
        grid=(N_TILES,),
        in_specs=[tile, pl.BlockSpec((len(POOL_WINDOWS), POOL_GROUP, POOL_GROUP), lambda i: (0, 0, 0)),
                  vec, vec, _mod_spec(1, 0, tm), _mod_spec(1, 1, tm), _mod_spec(1, 2, tm),
                  vec, _mod_spec(1, 3, tm), _mod_spec(1, 4, tm)],
        out_specs=(tile, tile),
        compiler_params=_params(("arbitrary",)),
        name="pool_mixer",
    )(x, w_pool, pool_scale, norm_mix, mods, mods, mods, norm_mlp, mods, mods)


def kernel(x_prompt, x_sample, cache_k, cache_v, state_hgrn_fwd, state_hgrn_bwd, c, c_ctx, w_ada, b_ada,
           norm_mix, norm_mlp, w_in_ab, w_out_ab, q_norm, k_norm, hg_norm, lb_raw, w_pool, pool_scale,
           w_mlp_in, w_mlp_out, final_norm):
    xp = x_prompt.reshape(N_PROMPT, D_MODEL)
    xs = x_sample.reshape(N_SAMPLE, D_MODEL)
    cv = jnp.concatenate([c_ctx[None, :], c, jnp.zeros((ADA_ROWS - 1 - DEC_BATCH, D_MODEL), F32)], axis=0)
    mods = _ada_table(cv, w_ada, b_ada).reshape(DEPTH, ADA_ROWS, 1, N_MOD * D_MODEL)

    h0 = _modulate0(xp, xs, norm_mix, mods)
    proj, new_k, new_v = _in_projection(h0, w_in_ab[0], q_norm[0:1], k_norm[0:1])
    ctx_k = cache_k[:, 0].reshape(DEC_BATCH, PAST_LEN, KV_WIDTH)
    ctx_v = cache_v[:, 0].reshape(DEC_BATCH, PAST_LEN, KV_WIDTH)
    att_p = _attention(proj, BATCH, SEQ, 0, SEQ)
    att_s = _attention(proj, DEC_BATCH, DEC_SEQ, N_PROMPT, 256, ctx=(ctx_k, ctx_v))
    consts = _hgrn_constants()
    hg_p, s_fwd, s_bwd = _hgrn(proj, lb_raw, hg_norm[0:1], consts, BATCH, SEQ, 0)
    s0 = (state_hgrn_fwd.reshape(DEC_BATCH, HG_HEADS, HG_DK, HG_DV),
          state_hgrn_bwd.reshape(DEC_BATCH, HG_HEADS, HG_DK, HG_DV))
    (hg_s,) = _hgrn(proj, lb_raw, hg_norm[0:1], consts, DEC_BATCH, DEC_SEQ, N_PROMPT, s0=s0)
    x1, h2 = _out_projection(att_p, att_s, hg_p, hg_s, w_out_ab[0], xp, xs, mods, norm_mlp[0:1])
    x2 = _mlp(h2, x1, w_mlp_in, w_mlp_out, mods, 0, 0, N_TILES)

    x3, h4 = _pool_mixer(x2, w_pool[0], pool_scale[0:1], norm_mix[1:2], norm_mlp[1:2], mods)
    fin = final_norm[None, :]
    y_prompt = _mlp(h4, x3, w_mlp_in, w_mlp_out, mods, 1, 0, N_PROMPT_TILES, final_norm=fin)
    y_sample = _mlp(h4, x3, w_mlp_in, w_mlp_out, mods, 1, N_PROMPT_TILES, N_TILES - N_PROMPT_TILES,
                    final_norm=fin)

    return (y_prompt.reshape(BATCH, SEQ, D_MODEL), y_sample.reshape(DEC_BATCH, DEC_SEQ, D_MODEL),
            new_k.reshape(BATCH, 1, SEQ, N_KV_HEADS, HEAD_DIM), new_v.reshape(BATCH, 1, SEQ, N_KV_HEADS, HEAD_DIM),
            s_fwd.reshape(BATCH, 1, HG_HEADS, HG_DK, HG_DV), s_bwd.reshape(BATCH, 1, HG_HEADS, HG_DK, HG_DV))
```

```python
import functools

import numpy as np
import jax
import jax.numpy as jnp
from jax import lax
from jax.experimental import pallas as pl
from jax.experimental.pallas import tpu as pltpu

F32 = jnp.float32
BF16 = jnp.bfloat16

D_MODEL = 2048
BATCH = 16
SEQ = 256
DEPTH = 2
DEC_BATCH = 2
DEC_SEQ = 1024
PAST_LEN = 256
GRID_W = 64
HEAD_DIM = 128
N_Q_HEADS = 8
N_KV_HEADS = 2
Q_PER_KV = N_Q_HEADS // N_KV_HEADS
ATT_WIDTH = N_Q_HEADS * HEAD_DIM
KV_WIDTH = N_KV_HEADS * HEAD_DIM
HG_HEADS = 8
HG_DK = 128
HG_DV = 128
HG_KW = HG_HEADS * HG_DK
HG_VW = HG_HEADS * HG_DV
IN_AB = ATT_WIDTH + 2 * KV_WIDTH + 3 * HG_KW + 2 * HG_VW
MIX_WIDTH = ATT_WIDTH + HG_VW
POOL_WINDOWS = (2, 4, 8, 16)
POOL_GROUP = D_MODEL // len(POOL_WINDOWS)
POOL_HALO = 8
D_FF = 4 * D_MODEL
ROPE_THETA = 10000.0
ROPE_HALF = HEAD_DIM // 2
EPS = 1e-6
N_MOD = 6

N_PROMPT = BATCH * SEQ
N_SAMPLE = DEC_BATCH * DEC_SEQ
N_TOK = N_PROMPT + N_SAMPLE
ADA_ROWS = 16
ADA_COL_TILE = 2048
TOK_TILE = 1024
N_TILES = N_TOK // TOK_TILE
N_PROMPT_TILES = N_PROMPT // TOK_TILE
PROJ_TOK_TILE = 2048
PROJ_COL_TILE = 512
OUT_COL_TILE = 512
FF_TILE = 512
MLP_STEPS = D_FF // FF_TILE
MLP_RES_COLS = D_MODEL // MLP_STEPS
MLP_OUT_CHUNK = 512
HG_CHUNK = 128
HG_LEVELS = 7
HG_HEAD_BLOCK = 4
HG_SAFE_EXPONENT = 80.0
VMEM_LIMIT = 56 * 2 ** 20

COL_Q = 0
COL_K = ATT_WIDTH
COL_V = COL_K + KV_WIDTH
COL_HQ = COL_V + KV_WIDTH
COL_ZF = COL_HQ + HG_KW
COL_HI = COL_ZF + 2 * HG_KW
COL_HG = COL_HI + HG_VW


def _params(semantics):
    return pltpu.CompilerParams(dimension_semantics=semantics, vmem_limit_bytes=VMEM_LIMIT)


def _sigmoid(x):
    return 1.0 / (1.0 + jnp.exp(-x))


def _silu(x):
    return x * _sigmoid(x)


def _rms(x, gain):
    return x * lax.rsqrt(jnp.mean(x * x, axis=-1, keepdims=True) + EPS) * gain


def _bdot(a, b):
    return jnp.dot(a, b, preferred_element_type=F32)


def _mod_row(tile, tile_rows):
    first = N_PROMPT // tile_rows
    per_seq = DEC_SEQ // tile_rows
    return jnp.where(tile < first, 0, 1 + (tile - first) // per_seq)


def _mod_spec(layer, chunk, tile_rows, width=D_MODEL, col=lambda *g: 0, tile_of=lambda *g: g[0]):
    per = D_MODEL // width
    return pl.BlockSpec((None, None, 1, width),
                        lambda *g: (layer, _mod_row(tile_of(*g), tile_rows), 0, chunk * per + col(*g)))


def _ada_kernel(cv_ref, w_ref, b_ref, o_ref):
    s = _silu(cv_ref[...]).astype(BF16)
    o_ref[...] = _bdot(s, w_ref[...].astype(BF16)) + b_ref[...]


def _ada_table(cv, w_ada, b_ada):
    tn = ADA_COL_TILE
    n = N_MOD * D_MODEL
    return pl.pallas_call(
        _ada_kernel,
        out_shape=jax.ShapeDtypeStruct((DEPTH, ADA_ROWS, n), F32),
        grid=(DEPTH, n // tn),
        in_specs=[pl.BlockSpec((ADA_ROWS, D_MODEL), lambda l, j: (0, 0)),
                  pl.BlockSpec((None, D_MODEL, tn), lambda l, j: (l, 0, j)),
                  pl.BlockSpec((None, 1, tn), lambda l, j: (l, 0, j))],
        out_specs=pl.BlockSpec((None, ADA_ROWS, tn), lambda l, j: (l, 0, j)),
        compiler_params=_params(("arbitrary", "arbitrary")),
        name="ada_table",
    )(cv, w_ada, b_ada.reshape(DEPTH, 1, n))


def _modulate(x, gain, shift, scale):
    return _rms(x, gain) * (1.0 + scale) + shift


def _mod0_kernel(xp_ref, xs_ref, gain_ref, sh_ref, sc_ref, o_ref):
    i = pl.program_id(0)

    def run(x_ref):
        o_ref[...] = _modulate(x_ref[...], gain_ref[...], sh_ref[...], sc_ref[...]).astype(BF16)

    pl.when(i < N_PROMPT_TILES)(lambda: run(xp_ref))
    pl.when(i >= N_PROMPT_TILES)(lambda: run(xs_ref))


def _modulate0(xp, xs, norm_mix, mods):
    tm = TOK_TILE
    first = N_PROMPT_TILES
    return pl.pallas_call(
        _mod0_kernel,
        out_shape=jax.ShapeDtypeStruct((N_TOK, D_MODEL), BF16),
        grid=(N_TOK // tm,),
        in_specs=[pl.BlockSpec((tm, D_MODEL), lambda i: (jnp.minimum(i, first - 1), 0)),
                  pl.BlockSpec((tm, D_MODEL), lambda i: (jnp.maximum(i - first, 0), 0)),
                  pl.BlockSpec((1, D_MODEL), lambda i: (0, 0)),
                  _mod_spec(0, 0, tm), _mod_spec(0, 1, tm)],
        out_specs=pl.BlockSpec((tm, D_MODEL), lambda i: (i, 0)),
        compiler_params=_params(("arbitrary",)),
        name="modulate0",
    )(xp, xs, norm_mix[0:1], mods, mods)


def _rope(y, cos, sin, perm2):
    hi = y.astype(BF16)
    lo = (y - hi.astype(F32)).astype(BF16)
    rot = _bdot(jnp.concatenate([hi, lo], axis=1), perm2)
    return y * cos + rot * sin


def _inproj_kernel(h_ref, w_ref, qg_ref, kg_ref, cos_ref, sin_ref, perm_ref, p_ref, nk_ref, nv_ref):
    i = pl.program_id(0)
    j = pl.program_id(1)
    latent = i >= N_PROMPT // PROJ_TOK_TILE
    heads = PROJ_COL_TILE // HEAD_DIM
    kv_tile = COL_K // PROJ_COL_TILE
    is_q = j < kv_tile

    def attention_tile(rope):
        gain = jnp.where(is_q, qg_ref[...], kg_ref[...])
        w = w_ref[...].astype(BF16)
        for s in range(PROJ_TOK_TILE // DEC_SEQ):
            rows = slice(s * DEC_SEQ, (s + 1) * DEC_SEQ)
            acc = _bdot(h_ref[rows, :], w)
            for hh in range(heads):
                cols = slice(hh * HEAD_DIM, (hh + 1) * HEAD_DIM)
                x = acc[:, cols]
                y = _rms(x, gain)
                if rope:
                    y = _rope(y, cos_ref[...], sin_ref[...], perm_ref[...])
                if hh >= N_KV_HEADS:
                    y = jnp.where(is_q, y, x)
                p_ref[rows, cols] = y

    pl.when(jnp.logical_and(j <= kv_tile, latent))(lambda: attention_tile(True))
    pl.when(jnp.logical_and(j <= kv_tile, jnp.logical_not(latent)))(lambda: attention_tile(False))

    @pl.when(jnp.logical_and(j == kv_tile, jnp.logical_not(latent)))
    def _():
        nk_ref[...] = p_ref[:, :KV_WIDTH]
        nv_ref[...] = p_ref[:, KV_WIDTH:]

    @pl.when(j > kv_tile)
    def _():
        is_silu = jnp.logical_or(j < COL_ZF // PROJ_COL_TILE, j >= COL_HG // PROJ_COL_TILE)
        w = w_ref[...].astype(BF16)
        for s in range(PROJ_TOK_TILE // DEC_SEQ):
            rows = slice(s * DEC_SEQ, (s + 1) * DEC_SEQ)
            acc = _bdot(h_ref[rows, :], w)
            p_ref[rows, :] = jnp.where(is_silu, _silu(acc), acc)


def _rope_tables():
    t = np.arange(DEC_SEQ)
    row = (t // GRID_W).astype(np.float32)
    col = (t % GRID_W).astype(np.float32)
    inv = (np.float32(ROPE_THETA) ** (-np.arange(0, ROPE_HALF, 2, dtype=np.float32) / np.float32(ROPE_HALF))).astype(np.float32)
    ar = row[:, None] * inv
    ac = col[:, None] * inv
    ang = np.concatenate([ar, ar, ac, ac], axis=-1).astype(np.float32)
    cos = np.cos(ang).astype(np.float32)
    sin = np.sin(ang).astype(np.float32)
    qw = ROPE_HALF // 2
    perm = np.zeros((HEAD_DIM, HEAD_DIM), np.float32)
    for k in range(qw):
        perm[qw + k, k] = -1.0
        perm[k, qw + k] = 1.0
        perm[3 * qw + k, 2 * qw + k] = -1.0
        perm[2 * qw + k, 3 * qw + k] = 1.0
    return jnp.asarray(cos), jnp.asarray(sin), jnp.asarray(np.concatenate([perm, perm], axis=0), BF16)


def _in_projection(h, w_in, q_gain, k_gain):
    tm, tn = PROJ_TOK_TILE, PROJ_COL_TILE
    n_prompt_tiles = N_PROMPT // tm
    cos, sin, perm2 = _rope_tables()
    table = pl.BlockSpec((DEC_SEQ, HEAD_DIM), lambda i, j: (0, 0))
    gain = pl.BlockSpec((1, HEAD_DIM), lambda i, j: (0, 0))
    state = pl.BlockSpec((tm, KV_WIDTH), lambda i, j: (jnp.minimum(i, n_prompt_tiles - 1), 0))
    return pl.pallas_call(
        _inproj_kernel,
        out_shape=(jax.ShapeDtypeStruct((N_TOK, IN_AB), F32),
                   jax.ShapeDtypeStruct((N_PROMPT, KV_WIDTH), F32),
                   jax.ShapeDtypeStruct((N_PROMPT, KV_WIDTH), F32)),
        grid=(N_TOK // tm, IN_AB // tn),
        in_specs=[pl.BlockSpec((tm, D_MODEL), lambda i, j: (i, 0)),
                  pl.BlockSpec((D_MODEL, tn), lambda i, j: (0, j)),
                  gain, gain, table, table, pl.BlockSpec((2 * HEAD_DIM, HEAD_DIM), lambda i, j: (0, 0))],
        out_specs=(pl.BlockSpec((tm, tn), lambda i, j: (i, j)), state, state),
        compiler_params=_params(("arbitrary", "arbitrary")),
        name="in_projection",
    )(h, w_in, q_gain, k_gain, cos, sin, perm2)


def _attn_kernel(*refs, has_ctx, stack):
    if has_ctx:
        q_ref, k_ref, v_ref, ck_ref, cv_ref, o_ref = refs
    else:
        q_ref, k_ref, v_ref, o_ref = refs
    scale = HEAD_DIM ** -0.5
    nt = (((1,), (1,)), ((), ()))
    k = k_ref[...].astype(BF16)
    v = v_ref[...].astype(BF16)
    if has_ctx:
        ck = ck_ref[...].astype(BF16)
        cv = cv_ref[...].astype(BF16)
    tq = q_ref.shape[0]
    for g0 in range(0, Q_PER_KV, stack):
        q = jnp.concatenate([q_ref[:, g * HEAD_DIM:(g + 1) * HEAD_DIM] for g in range(g0, g0 + stack)], axis=0)
        q = (q * scale).astype(BF16)
        s = lax.dot_general(q, k, nt, preferred_element_type=F32)
        m = jnp.max(s, axis=-1, keepdims=True)
        if has_ctx:
            sc = lax.dot_general(q, ck, nt, preferred_element_type=F32)
            m = jnp.maximum(m, jnp.max(sc, axis=-1, keepdims=True))
        p = jnp.exp(s - m)
        den = jnp.sum(p, axis=-1, keepdims=True)
        o = _bdot(p.astype(BF16), v)
        if has_ctx:
            pc = jnp.exp(sc - m)
            den = den + jnp.sum(pc, axis=-1, keepdims=True)
            o = o + _bdot(pc.astype(BF16), cv)
        o = (o / den).astype(o_ref.dtype)
        for g in range(stack):
            o_ref[:, (g0 + g) * HEAD_DIM:(g0 + g + 1) * HEAD_DIM] = o[g * tq:(g + 1) * tq]


def _attention(p, n_batch, seq, row0, tq, ctx=None):
    q_blocks = seq // tq
    gw = Q_PER_KV * HEAD_DIM
    in_specs = [
        pl.BlockSpec((tq, gw), lambda b, h, qi: (row0 // tq + b * q_blocks + qi, h)),
        pl.BlockSpec((seq, HEAD_DIM), lambda b, h, qi: (row0 // seq + b, COL_K // HEAD_DIM + h)),
        pl.BlockSpec((seq, HEAD_DIM), lambda b, h, qi: (row0 // seq + b, COL_V // HEAD_DIM + h)),
    ]
    args = [p, p, p]
    if ctx is not None:
        ctx_spec = pl.BlockSpec((None, PAST_LEN, HEAD_DIM), lambda b, h, qi: (b, 0, h))
        in_specs += [ctx_spec, ctx_spec]
        args += [ctx[0], ctx[1]]
    return pl.pallas_call(
        functools.partial(_attn_kernel, has_ctx=ctx is not None, stack=Q_PER_KV if ctx is None else 1),
        out_shape=jax.ShapeDtypeStruct((n_batch * seq, ATT_WIDTH), BF16),
        grid=(n_batch, N_KV_HEADS, q_blocks),
        in_specs=in_specs,
        out_specs=pl.BlockSpec((tq, gw), lambda b, h, qi: (b * q_blocks + qi, h)),
        compiler_params=_params(("arbitrary", "arbitrary", "arbitrary")),
        name="attention_latent" if ctx is not None else "attention_prompt",
    )(*args)


def _hgrn_constants():
    c = HG_CHUNK
    t = np.arange(c)
    cums, sels, pairs, scans, diags = [], [], [], [], []
    for d in range(2):
        pos = t if d == 0 else c - 1 - t
        pu, pt = pos[None, :], pos[:, None]
        ms, ss, ws = [], [], []
        for l in range(HG_LEVELS):
            m = c >> l
            blk = pos // m
            mid = (blk * m + m // 2)[:, None]
            late = ((pos % m) >= m // 2)
            ms.append(np.where(late[:, None], (pu >= mid) & (pu <= pt), (pu > pt) & (pu < mid)))
            ss.append(np.broadcast_to(late[:, None], (c, c)))
            ws.append((blk[:, None] == blk[None, :]) & late[:, None] & ~late[None, :])
        ms.append(pu <= pt)
        ms.append(pu > pt)
        cums.append(np.concatenate(ms, axis=0))
        sels.append(np.stack(ss))
        pairs.append(np.stack(ws))
        scans.append(np.concatenate([pu <= pt, pu <= pt], axis=1))
        diags.append(((pos // (c // 2))[:, None] == (pos // (c // 2))[None, :]) & (pu <= pt))
    return (jnp.asarray(np.stack(cums), BF16), jnp.asarray(np.stack(sels), F32),
            jnp.asarray(np.stack(pairs), F32), jnp.asarray(np.stack(scans), BF16),
            jnp.asarray(np.stack(diags), F32))


def _hgrn_kernel(*refs, n_chunks, has_s0):
    (hq_ref, zf_ref, zb_ref, hi_ref, hg_ref, lb_ref, og_ref, cum_ref, sel_ref, pair_ref, scan_ref,
     diag_ref) = refs[:12]
    refs = refs[12:]
    if has_s0:
        s0f_ref, s0b_ref, o_ref = refs[:3]
        refs = refs[3:]
    else:
        o_ref, sf_ref, sb_ref = refs[:3]
        refs = refs[3:]
    st_ref, acc_ref = refs
    c = HG_CHUNK
    half = c // 2
    nt = (((1,), (1,)), ((), ()))
    tn = (((0,), (0,)), ((), ()))

    def direction(d, z_ref):
        raw = lb_ref[d]
        e = jnp.exp(raw - jnp.max(raw, axis=0, keepdims=True))
        lb = e[0:1] / jnp.sum(e, axis=0, keepdims=True)

        def initial_state(hh):
            if has_s0:
                return (s0f_ref, s0b_ref)[d][hh].T
            return jnp.zeros((HG_DV, HG_DK), F32)

        def gates(rows):
            f = lb + (1.0 - lb) * _sigmoid(z_ref[rows, :])
            logf = jnp.log(f)
            hi16 = logf.astype(BF16)
            lo16 = (logf - hi16.astype(F32)).astype(BF16)
            return f, hi16, lo16

        def row(p):
            t = p if d == 0 else c - 1 - p
            return slice(t, t + 1)

        early, late = (slice(0, half), slice(half, c)) if d == 0 else (slice(half, c), slice(0, half))

        def in_row_order(x_early, x_late):
            return jnp.concatenate([x_early, x_late] if d == 0 else [x_late, x_early], axis=0)

        def emit(rows, cols, o):
            if d == 0:
                acc_ref[rows, cols] = o
            else:
                tot = acc_ref[rows, cols] + o
                o_ref[rows, cols] = (_rms(tot, og_ref[...]) * hg_ref[rows, cols]).astype(o_ref.dtype)

        def two_level_operands(rows):
            f, hi16, lo16 = gates(rows)
            b = _bdot(scan_ref[d], jnp.concatenate([hi16, lo16], axis=0))
            kk = 1.0 - f
            q = hq_ref[rows, :]
            r_mid = b[row(half - 1)]
            x1 = in_row_order(kk[early] * jnp.exp(r_mid - b[early]), q[late] * jnp.exp(b[late] - r_mid)).astype(BF16)
            dq = in_row_order(b[early] - b[row(half // 2 - 1)], b[late] - b[row(half + half // 2 - 1)])
            b_end = b[row(c - 1)]
            span = jnp.maximum(
                jnp.maximum(b[row(0)] - b[row(half // 2 - 1)], b[row(half // 2 - 1)] - b[row(half - 1)]),
                jnp.maximum(b[row(half)] - b[row(half + half // 2 - 1)],
                            b[row(half + half // 2 - 1)] - b[row(c - 1)]))
            return dict(x1=x1, xq=(q * jnp.exp(dq)).astype(BF16), xk=(kk * jnp.exp(-dq)).astype(BF16),
                        q_in=(q * jnp.exp(b)).astype(BF16), k_out=(kk * jnp.exp(b_end - b)).astype(BF16),
                        a_end=jnp.exp(b_end), iv=hi_ref[rows, :].astype(BF16), span=span)

        def two_level_chunk(rows, ops, states):
            new_states = []
            for hh in range(HG_HEAD_BLOCK):
                cols = slice(hh * HG_DK, (hh + 1) * HG_DK)
                g1 = _bdot(ops["x1"][:, cols], ops["x1"][:, cols].astype(F32).T.astype(BF16))
                g2 = _bdot(ops["xq"][:, cols], ops["xk"][:, cols].astype(F32).T.astype(BF16))
                att = jnp.where(pair_ref[d, 0] > 0.5, g1, jnp.where(diag_ref[d] > 0.5, g2, 0.0))
                st = states[hh]
                o = (_bdot(att.astype(BF16), ops["iv"][:, cols])
                     + lax.dot_general(ops["q_in"][:, cols], st.astype(BF16), nt, preferred_element_type=F32))
                dst = lax.dot_general(ops["iv"][:, cols], ops["k_out"][:, cols], tn, preferred_element_type=F32)
                new_states.append(ops["a_end"][:, cols] * st + dst)
                emit(rows, cols, o)
            return new_states

        def all_levels(rows):
            f, hi16, lo16 = gates(rows)
            cum = cum_ref[d]
            eall = jnp.exp(_bdot(cum, hi16) + _bdot(cum, lo16))
            for hh in range(HG_HEAD_BLOCK):
                cols = slice(hh * HG_DK, (hh + 1) * HG_DK)
                kk = 1.0 - f[:, cols]
                q = hq_ref[rows, cols]
                iv = hi_ref[rows, cols]
                iv16 = iv.astype(BF16)
                att = jnp.zeros((c, c), F32)
                for l in range(HG_LEVELS):
                    x = (kk + sel_ref[d, l] * (q - kk)) * eall[l * c:(l + 1) * c, cols]
                    xb = x.astype(BF16)
                    att = att + pair_ref[d, l] * lax.dot_general(xb, xb, nt, preferred_element_type=F32)
                e_in = eall[HG_LEVELS * c:(HG_LEVELS + 1) * c, cols]
                e_out = eall[(HG_LEVELS + 1) * c:, cols]
                st = st_ref[hh]
                o = (_bdot(att.astype(BF16), iv16)
                     + jnp.sum(q * kk, axis=-1, keepdims=True) * iv
                     + lax.dot_general((q * e_in).astype(BF16), st.astype(BF16), nt, preferred_element_type=F32))
                dst = lax.dot_general(iv16, (kk * e_out).astype(BF16), tn, preferred_element_type=F32)
                st_ref[hh] = (e_in[0:1] * e_out[0:1]) * st + dst
                emit(rows, cols, o)

        def chunk_rows(ci):
            cidx = ci if d == 0 else n_chunks - 1 - ci
            return slice(cidx * c, (cidx + 1) * c)

        states = [initial_state(hh) for hh in range(HG_HEAD_BLOCK)]
        worst = jnp.zeros((1, HG_HEAD_BLOCK * HG_DK), F32)
        for ci in range(n_chunks):
            ops = two_level_operands(chunk_rows(ci))
            states = two_level_chunk(chunk_rows(ci), ops, states)
            worst = jnp.maximum(worst, ops["span"])
        for hh in range(HG_HEAD_BLOCK):
            st_ref[hh] = states[hh]

        @pl.when(jnp.logical_not(jnp.max(worst) <= HG_SAFE_EXPONENT))
        def _():
            for hh in range(HG_HEAD_BLOCK):
                st_ref[hh] = initial_state(hh)

            def chunk(ci, carry):
                cidx = ci if d == 0 else n_chunks - 1 - ci
                all_levels(pl.ds(pl.multiple_of(cidx * c, c), c))
                return carry

            lax.fori_loop(0, n_chunks, chunk, 0)

        if not has_s0:
            for hh in range(HG_HEAD_BLOCK):
                (sf_ref, sb_ref)[d][hh] = st_ref[hh].T

    direction(0, zf_ref)
    direction(1, zb_ref)


def _hgrn(p, lb_raw, o_gain, consts, n_batch, seq, row0, s0=None):
    hb = HG_HEAD_BLOCK
    w = hb * HG_DK

    def seg(col):
        return pl.BlockSpec((seq, w), lambda b, h: (row0 // seq + b, col // w + h))

    def const(a):
        return pl.BlockSpec(a.shape, lambda b, h: (0,) * a.ndim)

    in_specs = [seg(COL_HQ), seg(COL_ZF), seg(COL_ZF + HG_KW), seg(COL_HI), seg(COL_HG),
                pl.BlockSpec((2, DEPTH + 1, w), lambda b, h: (0, 0, h)),
                pl.BlockSpec((1, HG_DV), lambda b, h: (0, 0))] + [const(a) for a in consts]
    args = [p, p, p, p, p, lb_raw, o_gain, *consts]
    has_s0 = s0 is not None
    st_spec = pl.BlockSpec((None, hb, HG_DK, HG_DV), lambda b, h: (b, h, 0, 0))
    if has_s0:
        in_specs += [st_spec, st_spec]
        args += [s0[0], s0[1]]
    out_shape = [jax.ShapeDtypeStruct((n_batch * seq, HG_VW), BF16)]
    out_specs = [pl.BlockSpec((seq, w), lambda b, h: (b, h))]
    if not has_s0:
        st_shape = jax.ShapeDtypeStruct((n_batch, HG_HEADS, HG_DK, HG_DV), F32)
        out_shape += [st_shape, st_shape]
        out_specs += [st_spec, st_spec]
    return pl.pallas_call(
        functools.partial(_hgrn_kernel, n_chunks=seq // HG_CHUNK, has_s0=has_s0),
        out_shape=tuple(out_shape),
        grid=(n_batch, HG_HEADS // hb),
        in_specs=in_specs,
        out_specs=tuple(out_specs),
        scratch_shapes=[pltpu.VMEM((hb, HG_DV, HG_DK), F32), pltpu.VMEM((seq, w), F32)],
        compiler_params=_params(("arbitrary", "arbitrary")),
        name="hgrn_latent" if has_s0 else "hgrn_prompt",
    )(*args)


def _outproj_kernel(attp_ref, atts_ref, hgp_ref, hgs_ref, wa_ref, wb_ref, xp_ref, xs_ref, g1_ref, gain_ref,
                    sh_ref, sc_ref, x1_ref, h2_ref, full_ref, wcache_ref):
    i = pl.program_id(0)
    n = pl.program_id(1)

    @pl.when(i == 0)
    def _():
        wcache_ref[n, 0] = wa_ref[...].astype(BF16)
        wcache_ref[n, 1] = wb_ref[...].astype(BF16)

    def run(att_ref, hg_ref, x_ref):
        acc = _bdot(att_ref[...], wcache_ref[n, 0]) + _bdot(hg_ref[...], wcache_ref[n, 1])
        x1 = x_ref[...] + g1_ref[...] * acc
        x1_ref[...] = x1
        full_ref[n] = x1

    pl.when(i < N_PROMPT_TILES)(lambda: run(attp_ref, hgp_ref, xp_ref))
    pl.when(i >= N_PROMPT_TILES)(lambda: run(atts_ref, hgs_ref, xs_ref))

    @pl.when(n == D_MODEL // OUT_COL_TILE - 1)
    def _():
        nt = D_MODEL // OUT_COL_TILE
        ss = full_ref[0] * full_ref[0]
        ms = jnp.sum(ss, axis=-1, keepdims=True)
        for k in range(1, nt):
            ms = ms + jnp.sum(full_ref[k] * full_ref[k], axis=-1, keepdims=True)
        rstd = lax.rsqrt(ms / D_MODEL + EPS)
        for k in range(nt):
            cols = slice(k * OUT_COL_TILE, (k + 1) * OUT_COL_TILE)
            y = full_ref[k] * rstd * gain_ref[:, cols]
            h2_ref[:, cols] = (y * (1.0 + sc_ref[:, cols]) + sh_ref[:, cols]).astype(BF16)


def _out_projection(att_p, att_s, hg_p, hg_s, w_out, xp, xs, mods, norm_mlp):
    tm, tn = TOK_TILE, OUT_COL_TILE
    nt = D_MODEL // tn

    def prompt_rows(width):
        return pl.BlockSpec((tm, width), lambda i, n: (jnp.minimum(i, N_PROMPT_TILES - 1), 0),
                            pipeline_mode=pl.Buffered(1))

    def sample_rows(width):
        return pl.BlockSpec((tm, width), lambda i, n: (jnp.maximum(i - N_PROMPT_TILES, 0), 0),
                            pipeline_mode=pl.Buffered(1))

    return pl.pallas_call(
        _outproj_kernel,
        out_shape=(jax.ShapeDtypeStruct((N_TOK, D_MODEL), F32), jax.ShapeDtypeStruct((N_TOK, D_MODEL), BF16)),
        grid=(N_TILES, nt),
        in_specs=[prompt_rows(ATT_WIDTH), sample_rows(ATT_WIDTH), prompt_rows(HG_VW), sample_rows(HG_VW),
                  pl.BlockSpec((ATT_WIDTH, tn), lambda i, n: (0, jnp.where(i == 0, n, nt - 1))),
                  pl.BlockSpec((HG_VW, tn), lambda i, n: (1, jnp.where(i == 0, n, nt - 1))),
                  pl.BlockSpec((tm, tn), lambda i, n: (jnp.minimum(i, N_PROMPT_TILES - 1),
                                                       jnp.where(i < N_PROMPT_TILES, n, nt - 1))),
                  pl.BlockSpec((tm, tn), lambda i, n: (jnp.maximum(i - N_PROMPT_TILES, 0),
                                                       jnp.where(i < N_PROMPT_TILES, 0, n))),
                  _mod_spec(0, 2, tm, width=tn, col=lambda i, n: n),
                  pl.BlockSpec((1, D_MODEL), lambda i, n: (0, 0)),
                  _mod_spec(0, 3, tm), _mod_spec(0, 4, tm)],
        out_specs=(pl.BlockSpec((tm, tn), lambda i, n: (i, n)),
                   pl.BlockSpec((tm, D_MODEL), lambda i, n: (i, 0))),
        scratch_shapes=[pltpu.VMEM((nt, tm, tn), F32), pltpu.VMEM((nt, 2, ATT_WIDTH, tn), BF16)],
        compiler_params=_params(("arbitrary", "arbitrary")),
        name="out_projection",
    )(att_p, att_s, hg_p, hg_s, w_out, w_out, xp, xs, mods, norm_mlp, mods, mods)


def _mlp_kernel(h_ref, w1_ref, w2_ref, x_ref, g2_ref, *rest, final):
    if final:
        fin_ref, o_ref, res_ref = rest
    else:
        o_ref, res_ref = rest
    j = pl.program_id(1)
    res_ref[j] = x_ref[...]

    def step(first, last):
        a = jnp.square(jnp.maximum(_bdot(h_ref[...], w1_ref[...].astype(BF16)), 0.0)).astype(BF16)
        for n in range(D_MODEL // MLP_OUT_CHUNK):
            cols = slice(n * MLP_OUT_CHUNK, (n + 1) * MLP_OUT_CHUNK)
            p = _bdot(a, w2_ref[:, cols].astype(BF16))
            if not first:
                p = o_ref[:, cols] + p
            if last:
                per = MLP_OUT_CHUNK // MLP_RES_COLS
                res = jnp.concatenate([res_ref[n * per + k] for k in range(per)], axis=1)
                p = res + g2_ref[:, cols] * p
            o_ref[:, cols] = p
        if last and final:
            o_ref[...] = _rms(o_ref[...], fin_ref[...])

    pl.when(j == 0)(lambda: step(True, False))
    pl.when(jnp.logical_and(j > 0, j < MLP_STEPS - 1))(lambda: step(False, False))
    pl.when(j == MLP_STEPS - 1)(lambda: step(False, True))


def _mlp(h, x, w1, w2, mods, layer, tile0, n_tiles, final_norm=None):
    tm, th = TOK_TILE, FF_TILE
    final = final_norm is not None
    in_specs = [pl.BlockSpec((tm, D_MODEL), lambda i, j: (tile0 + i, 0), pipeline_mode=pl.Buffered(1)),
                pl.BlockSpec((None, D_MODEL, th), lambda i, j: (layer, 0, j)),
                pl.BlockSpec((None, th, D_MODEL), lambda i, j: (layer, j, 0)),
                pl.BlockSpec((tm, MLP_RES_COLS), lambda i, j: (tile0 + i, j)),
                _mod_spec(layer, 5, tm, tile_of=lambda i, j: tile0 + i)]
    args = [h, w1, w2, x, mods]
    if final:
        in_specs.append(pl.BlockSpec((1, D_MODEL), lambda i, j: (0, 0)))
        args.append(final_norm)
    return pl.pallas_call(
        functools.partial(_mlp_kernel, final=final),
        out_shape=jax.ShapeDtypeStruct((n_tiles * tm, D_MODEL), F32),
        grid=(n_tiles, MLP_STEPS),
        in_specs=in_specs,
        out_specs=pl.BlockSpec((tm, D_MODEL), lambda i, j: (i, 0)),
        scratch_shapes=[pltpu.VMEM((MLP_STEPS, tm, MLP_RES_COLS), F32)],
        compiler_params=_params(("arbitrary", "arbitrary")),
        name="mlp_final" if final else "mlp",
    )(*args)


def _pool_kernel(x_ref, wp_ref, ps_ref, gain1_ref, sh1_ref, sc1_ref, g1_ref, gain2_ref, sh2_ref, sc2_ref,
                 x3_ref, h4_ref):
    i = pl.program_id(0)
    tm = TOK_TILE
    rstd = lax.rsqrt(jnp.mean(x_ref[...] * x_ref[...], axis=-1, keepdims=True) + EPS)
    gs1 = gain1_ref[...] * (1.0 + sc1_ref[...])
    mix_gain = g1_ref[...] * ps_ref[...]

    def widen(m):
        return jnp.concatenate([m] * (POOL_GROUP // HEAD_DIM), axis=1)

    def mix_tile(seq):
        n_seq = tm // seq
        pitch = seq + 2 * POOL_HALO
        n_pad = n_seq * pitch
        halo = jnp.zeros((POOL_HALO, POOL_GROUP), F32)
        pos = lax.broadcasted_iota(jnp.int32, (seq, HEAD_DIM), 0)

        def down(a, k):
            return pltpu.roll(a, k % n_pad, 0)

        for g, w in enumerate(POOL_WINDOWS):
            half = w // 2
            cols = slice(g * POOL_GROUP, (g + 1) * POOL_GROUP)
            x = x_ref[:, cols]
            h = x * rstd * gs1[:, cols] + sh1_ref[:, cols]
            padded = jnp.concatenate(
                [piece for s in range(n_seq) for piece in (halo, h[s * seq:(s + 1) * seq], halo)], axis=0)
            back = padded
            m = 1
            while m < half:
                back = back + down(back, m)
                m *= 2
            ahead = back if half == 1 else down(back, -(half - 1))
            total = down(back, 1) + ahead
            total = jnp.concatenate(
                [total[s * pitch + POOL_HALO:s * pitch + POOL_HALO + seq] for s in range(n_seq)], axis=0)
            count = (jnp.minimum(pos + (w - half), seq) - jnp.maximum(pos - half, 0)).astype(F32)
            inv = jnp.concatenate([widen(1.0 / count)] * n_seq, axis=0)
            pooled = (total * inv - h).astype(BF16)
            mix = _bdot(pooled, wp_ref[g].astype(BF16))
            x3_ref[:, cols] = x + mix_gain[:, cols] * mix

    pl.when(i < N_PROMPT_TILES)(lambda: mix_tile(SEQ))
    pl.when(i >= N_PROMPT_TILES)(lambda: mix_tile(DEC_SEQ))
    x3 = x3_ref[...]
    h4_ref[...] = _modulate(x3, gain2_ref[...], sh2_ref[...], sc2_ref[...]).astype(BF16)


def _pool_mixer(x, w_pool, pool_scale, norm_mix, norm_mlp, mods):
    tm = TOK_TILE
    vec = pl.BlockSpec((1, D_MODEL), lambda i: (0, 0))
    tile = pl.BlockSpec((tm, D_MODEL), lambda i: (i, 0))
    return pl.pallas_call(
        _pool_kernel,
        out_shape=(jax.ShapeDtypeStruct((N_TOK, D_MODEL), F32), jax.ShapeDtypeStruct((N_TOK, D_MODEL), BF16)),
        grid=(N_TILES,),
        in_specs=[tile, pl.BlockSpec((len(POOL_WINDOWS), POOL_GROUP, POOL_GROUP), lambda i: (0, 0, 0)),
                  vec, vec, _mod_spec(1, 0, tm), _mod_spec(1, 1, tm), _mod_spec(1, 2, tm),
                  vec, _mod_spec(1, 3, tm), _mod_spec(1, 4, tm)],
        out_specs=(tile, tile),
        compiler_params=_params(("arbitrary",)),
        name="pool_mixer",
    )(x, w_pool, pool_scale, norm_mix, mods, mods, mods, norm_mlp, mods, mods)


def kernel(x_prompt, x_sample, cache_k, cache_v, state_hgrn_fwd, state_hgrn_bwd, c, c_ctx, w_ada, b_ada,
           norm_mix, norm_mlp, w_in_ab, w_out_ab, q_norm, k_norm, hg_norm, lb_raw, w_pool, pool_scale,
           w_mlp_in, w_mlp_out, final_norm):
    xp = x_prompt.reshape(N_PROMPT, D_MODEL)
    xs = x_sample.reshape(N_SAMPLE, D_MODEL)
    cv = jnp.concatenate([c_ctx[None, :], c, jnp.zeros((ADA_ROWS - 1 - DEC_BATCH, D_MODEL), F32)], axis=0)
    mods = _ada_table(cv, w_ada, b_ada).reshape(DEPTH, ADA_ROWS, 1, N_MOD * D_MODEL)

    h0 = _modulate0(xp, xs, norm_mix, mods)
    proj, new_k, new_v = _in_projection(h0, w_in_ab[0], q_norm[0:1], k_norm[0:1])
    ctx_k = cache_k[:, 0].reshape(DEC_BATCH, PAST_LEN, KV_WIDTH)
    ctx_v = cache_v[:, 0].reshape(DEC_BATCH, PAST_LEN, KV_WIDTH)
    att_p = _attention(proj, BATCH, SEQ, 0, SEQ)
    att_s = _attention(proj, DEC_BATCH, DEC_SEQ, N_PROMPT, 256, ctx=(ctx_k, ctx_v))
    consts = _hgrn_constants()
    hg_p, s_fwd, s_bwd = _hgrn(proj, lb_raw, hg_norm[0:1], consts, BATCH, SEQ, 0)
    s0 = (state_hgrn_fwd.reshape(DEC_BATCH, HG_HEADS, HG_DK, HG_DV),
          state_hgrn_bwd.reshape(DEC_BATCH, HG_HEADS, HG_DK, HG_DV))
    (hg_s,) = _hgrn(proj, lb_raw, hg_norm[0:1], consts, DEC_BATCH, DEC_SEQ, N_PROMPT, s0=s0)
    x1, h2 = _out_projection(att_p, att_s, hg_p, hg_s, w_out_ab[0], xp, xs, mods, norm_mlp[0:1])
    x2 = _mlp(h2, x1, w_mlp_in, w_mlp_out, mods, 0, 0, N_TILES)

    x3, h4 = _pool_mixer(x2, w_pool[0], pool_scale[0:1], norm_mix[1:2], norm_mlp[1:2], mods)
    fin = final_norm[None, :]
    y_prompt = _mlp(h4, x3, w_mlp_in, w_mlp_out, mods, 1, 0, N_PROMPT_TILES, final_norm=fin)
    y_sample = _mlp(h4, x3, w_mlp_in, w_mlp_out, mods, 1, N_PROMPT_TILES, N_TILES - N_PROMPT_TILES,
                    final_norm=fin)

    return (y_prompt.reshape(BATCH, SEQ, D_MODEL), y_sample.reshape(DEC_BATCH, DEC_SEQ, D_MODEL),
            new_k.reshape(BATCH, 1, SEQ, N_KV_HEADS, HEAD_DIM), new_v.reshape(BATCH, 1, SEQ, N_KV_HEADS, HEAD_DIM),
            s_fwd.reshape(BATCH, 1, HG_HEADS, HG_DK, HG_DV), s_bwd.reshape(BATCH, 1, HG_HEADS, HG_DK, HG_DV))
```

```python
import functools

import numpy as np
import jax
import jax.numpy as jnp
from jax import lax
from jax.experimental import pallas as pl
from jax.experimental.pallas import tpu as pltpu

F32 = jnp.float32
BF16 = jnp.bfloat16

D_MODEL = 2048
BATCH = 16
SEQ = 256
DEPTH = 2
DEC_BATCH = 2
DEC_SEQ = 1024
PAST_LEN = 256
GRID_W = 64
HEAD_DIM = 128
N_Q_HEADS = 8
N_KV_HEADS = 2
Q_PER_KV = N_Q_HEADS // N_KV_HEADS
ATT_WIDTH = N_Q_HEADS * HEAD_DIM
KV_WIDTH = N_KV_HEADS * HEAD_DIM
HG_HEADS = 8
HG_DK = 128
HG_DV = 128
HG_KW = HG_HEADS * HG_DK
HG_VW = HG_HEADS * HG_DV
IN_AB = ATT_WIDTH + 2 * KV_WIDTH + 3 * HG_KW + 2 * HG_VW
MIX_WIDTH = ATT_WIDTH + HG_VW
POOL_WINDOWS = (2, 4, 8, 16)
POOL_GROUP = D_MODEL // len(POOL_WINDOWS)
POOL_HALO = 8
D_FF = 4 * D_MODEL
ROPE_THETA = 10000.0
ROPE_HALF = HEAD_DIM // 2
EPS = 1e-6
N_MOD = 6

N_PROMPT = BATCH * SEQ
N_SAMPLE = DEC_BATCH * DEC_SEQ
N_TOK = N_PROMPT + N_SAMPLE
ADA_ROWS = 16
ADA_COL_TILE = 2048
TOK_TILE = 1024
N_TILES = N_TOK // TOK_TILE
N_PROMPT_TILES = N_PROMPT // TOK_TILE
PROJ_TOK_TILE = 2048
PROJ_COL_TILE = 512
OUT_COL_TILE = 512
FF_TILE = 512
MLP_STEPS = D_FF // FF_TILE
MLP_RES_COLS = D_MODEL // MLP_STEPS
MLP_OUT_CHUNK = 512
HG_CHUNK = 128
HG_LEVELS = 7
HG_HEAD_BLOCK = 4
HG_PROMPT_SEQS = 2
HG_SAFE_EXPONENT = 80.0
VMEM_LIMIT = 56 * 2 ** 20

COL_Q = 0
COL_K = ATT_WIDTH
COL_V = COL_K + KV_WIDTH
COL_HQ = COL_V + KV_WIDTH
COL_ZF = COL_HQ + HG_KW
COL_HI = COL_ZF + 2 * HG_KW
COL_HG = COL_HI + HG_VW


def _params(semantics):
    return pltpu.CompilerParams(dimension_semantics=semantics, vmem_limit_bytes=VMEM_LIMIT)


def _sigmoid(x):
    return 1.0 / (1.0 + jnp.exp(-x))


def _silu(x):
    return x * _sigmoid(x)


def _rms(x, gain):
    return x * lax.rsqrt(jnp.mean(x * x, axis=-1, keepdims=True) + EPS) * gain


def _bdot(a, b):
    return jnp.dot(a, b, preferred_element_type=F32)


def _mod_row(tile, tile_rows):
    first = N_PROMPT // tile_rows
    per_seq = DEC_SEQ // tile_rows
    return jnp.where(tile < first, 0, 1 + (tile - first) // per_seq)


def _mod_spec(layer, chunk, tile_rows, width=D_MODEL, col=lambda *g: 0, tile_of=lambda *g: g[0]):
    per = D_MODEL // width
    return pl.BlockSpec((None, None, 1, width),
                        lambda *g: (layer, _mod_row(tile_of(*g), tile_rows), 0, chunk * per + col(*g)))


def _ada_kernel(cv_ref, w_ref, b_ref, o_ref):
    s = _silu(cv_ref[...]).astype(BF16)
    o_ref[...] = _bdot(s, w_ref[...].astype(BF16)) + b_ref[...]


def _ada_table(cv, w_ada, b_ada):
    tn = ADA_COL_TILE
    n = N_MOD * D_MODEL
    return pl.pallas_call(
        _ada_kernel,
        out_shape=jax.ShapeDtypeStruct((DEPTH, ADA_ROWS, n), F32),
        grid=(DEPTH, n // tn),
        in_specs=[pl.BlockSpec((ADA_ROWS, D_MODEL), lambda l, j: (0, 0)),
                  pl.BlockSpec((None, D_MODEL, tn), lambda l, j: (l, 0, j)),
                  pl.BlockSpec((None, 1, tn), lambda l, j: (l, 0, j))],
        out_specs=pl.BlockSpec((None, ADA_ROWS, tn), lambda l, j: (l, 0, j)),
        compiler_params=_params(("arbitrary", "arbitrary")),
        name="ada_table",
    )(cv, w_ada, b_ada.reshape(DEPTH, 1, n))


def _modulate(x, gain, shift, scale):
    return _rms(x, gain) * (1.0 + scale) + shift


def _mod0_kernel(xp_ref, xs_ref, gain_ref, sh_ref, sc_ref, o_ref):
    i = pl.program_id(0)

    def run(x_ref):
        o_ref[...] = _modulate(x_ref[...], gain_ref[...], sh_ref[...], sc_ref[...]).astype(BF16)

    pl.when(i < N_PROMPT_TILES)(lambda: run(xp_ref))
    pl.when(i >= N_PROMPT_TILES)(lambda: run(xs_ref))


def _modulate0(xp, xs, norm_mix, mods):
    tm = TOK_TILE
    first = N_PROMPT_TILES
    return pl.pallas_call(
        _mod0_kernel,
        out_shape=jax.ShapeDtypeStruct((N_TOK, D_MODEL), BF16),
        grid=(N_TOK // tm,),
        in_specs=[pl.BlockSpec((tm, D_MODEL), lambda i: (jnp.minimum(i, first - 1), 0)),
                  pl.BlockSpec((tm, D_MODEL), lambda i: (jnp.maximum(i - first, 0), 0)),
                  pl.BlockSpec((1, D_MODEL), lambda i: (0, 0)),
                  _mod_spec(0, 0, tm), _mod_spec(0, 1, tm)],
        out_specs=pl.BlockSpec((tm, D_MODEL), lambda i: (i, 0)),
        compiler_params=_params(("arbitrary",)),
        name="modulate0",
    )(xp, xs, norm_mix[0:1], mods, mods)


def _rope(y, cos, sin, perm2):
    hi = y.astype(BF16)
    lo = (y - hi.astype(F32)).astype(BF16)
    rot = _bdot(jnp.concatenate([hi, lo], axis=1), perm2)
    return y * cos + rot * sin


def _inproj_kernel(h_ref, w_ref, qg_ref, kg_ref, cos_ref, sin_ref, perm_ref, p_ref, nk_ref, nv_ref):
    i = pl.program_id(0)
    j = pl.program_id(1)
    latent = i >= N_PROMPT // PROJ_TOK_TILE
    heads = PROJ_COL_TILE // HEAD_DIM
    kv_tile = COL_K // PROJ_COL_TILE
    is_q = j < kv_tile

    def attention_tile(rope):
        gain = jnp.where(is_q, qg_ref[...], kg_ref[...])
        w = w_ref[...].astype(BF16)
        for s in range(PROJ_TOK_TILE // DEC_SEQ):
            rows = slice(s * DEC_SEQ, (s + 1) * DEC_SEQ)
            acc = _bdot(h_ref[rows, :], w)
            for hh in range(heads):
                cols = slice(hh * HEAD_DIM, (hh + 1) * HEAD_DIM)
                x = acc[:, cols]
                y = _rms(x, gain)
                if rope:
                    y = _rope(y, cos_ref[...], sin_ref[...], perm_ref[...])
                if hh >= N_KV_HEADS:
                    y = jnp.where(is_q, y, x)
                p_ref[rows, cols] = y

    pl.when(jnp.logical_and(j <= kv_tile, latent))(lambda: attention_tile(True))
    pl.when(jnp.logical_and(j <= kv_tile, jnp.logical_not(latent)))(lambda: attention_tile(False))

    @pl.when(jnp.logical_and(j == kv_tile, jnp.logical_not(latent)))
    def _():
        nk_ref[...] = p_ref[:, :KV_WIDTH]
        nv_ref[...] = p_ref[:, KV_WIDTH:]

    @pl.when(j > kv_tile)
    def _():
        is_silu = jnp.logical_or(j < COL_ZF // PROJ_COL_TILE, j >= COL_HG // PROJ_COL_TILE)
        w = w_ref[...].astype(BF16)
        for s in range(PROJ_TOK_TILE // DEC_SEQ):
            rows = slice(s * DEC_SEQ, (s + 1) * DEC_SEQ)
            acc = _bdot(h_ref[rows, :], w)
            p_ref[rows, :] = jnp.where(is_silu, _silu(acc), acc)


def _rope_tables():
    t = np.arange(DEC_SEQ)
    row = (t // GRID_W).astype(np.float32)
    col = (t % GRID_W).astype(np.float32)
    inv = (np.float32(ROPE_THETA) ** (-np.arange(0, ROPE_HALF, 2, dtype=np.float32) / np.float32(ROPE_HALF))).astype(np.float32)
    ar = row[:, None] * inv
    ac = col[:, None] * inv
    ang = np.concatenate([ar, ar, ac, ac], axis=-1).astype(np.float32)
    cos = np.cos(ang).astype(np.float32)
    sin = np.sin(ang).astype(np.float32)
    qw = ROPE_HALF // 2
    perm = np.zeros((HEAD_DIM, HEAD_DIM), np.float32)
    for k in range(qw):
        perm[qw + k, k] = -1.0
        perm[k, qw + k] = 1.0
        perm[3 * qw + k, 2 * qw + k] = -1.0
        perm[2 * qw + k, 3 * qw + k] = 1.0
    return jnp.asarray(cos), jnp.asarray(sin), jnp.asarray(np.concatenate([perm, perm], axis=0), BF16)


def _in_projection(h, w_in, q_gain, k_gain):
    tm, tn = PROJ_TOK_TILE, PROJ_COL_TILE
    n_prompt_tiles = N_PROMPT // tm
    cos, sin, perm2 = _rope_tables()
    table = pl.BlockSpec((DEC_SEQ, HEAD_DIM), lambda i, j: (0, 0))
    gain = pl.BlockSpec((1, HEAD_DIM), lambda i, j: (0, 0))
    state = pl.BlockSpec((tm, KV_WIDTH), lambda i, j: (jnp.minimum(i, n_prompt_tiles - 1), 0))
    return pl.pallas_call(
        _inproj_kernel,
        out_shape=(jax.ShapeDtypeStruct((N_TOK, IN_AB), F32),
                   jax.ShapeDtypeStruct((N_PROMPT, KV_WIDTH), F32),
                   jax.ShapeDtypeStruct((N_PROMPT, KV_WIDTH), F32)),
        grid=(N_TOK // tm, IN_AB // tn),
        in_specs=[pl.BlockSpec((tm, D_MODEL), lambda i, j: (i, 0)),
                  pl.BlockSpec((D_MODEL, tn), lambda i, j: (0, j)),
                  gain, gain, table, table, pl.BlockSpec((2 * HEAD_DIM, HEAD_DIM), lambda i, j: (0, 0))],
        out_specs=(pl.BlockSpec((tm, tn), lambda i, j: (i, j)), state, state),
        compiler_params=_params(("arbitrary", "arbitrary")),
        name="in_projection",
    )(h, w_in, q_gain, k_gain, cos, sin, perm2)


def _attn_kernel(*refs, has_ctx, stack):
    if has_ctx:
        q_ref, k_ref, v_ref, ck_ref, cv_ref, o_ref = refs
    else:
        q_ref, k_ref, v_ref, o_ref = refs
    scale = HEAD_DIM ** -0.5
    nt = (((1,), (1,)), ((), ()))
    k = k_ref[...].astype(BF16)
    v = v_ref[...].astype(BF16)
    if has_ctx:
        ck = ck_ref[...].astype(BF16)
        cv = cv_ref[...].astype(BF16)
    tq = q_ref.shape[0]
    for g0 in range(0, Q_PER_KV, stack):
        q = jnp.concatenate([q_ref[:, g * HEAD_DIM:(g + 1) * HEAD_DIM] for g in range(g0, g0 + stack)], axis=0)
        q = (q * scale).astype(BF16)
        s = lax.dot_general(q, k, nt, preferred_element_type=F32)
        m = jnp.max(s, axis=-1, keepdims=True)
        if has_ctx:
            sc = lax.dot_general(q, ck, nt, preferred_element_type=F32)
            m = jnp.maximum(m, jnp.max(sc, axis=-1, keepdims=True))
        p = jnp.exp(s - m)
        den = jnp.sum(p, axis=-1, keepdims=True)
        o = _bdot(p.astype(BF16), v)
        if has_ctx:
            pc = jnp.exp(sc - m)
            den = den + jnp.sum(pc, axis=-1, keepdims=True)
            o = o + _bdot(pc.astype(BF16), cv)
        o = (o / den).astype(o_ref.dtype)
        for g in range(stack):
            o_ref[:, (g0 + g) * HEAD_DIM:(g0 + g + 1) * HEAD_DIM] = o[g * tq:(g + 1) * tq]


def _attention(p, n_batch, seq, row0, tq, ctx=None):
    q_blocks = seq // tq
    gw = Q_PER_KV * HEAD_DIM
    in_specs = [
        pl.BlockSpec((tq, gw), lambda b, h, qi: (row0 // tq + b * q_blocks + qi, h)),
        pl.BlockSpec((seq, HEAD_DIM), lambda b, h, qi: (row0 // seq + b, COL_K // HEAD_DIM + h)),
        pl.BlockSpec((seq, HEAD_DIM), lambda b, h, qi: (row0 // seq + b, COL_V // HEAD_DIM + h)),
    ]
    args = [p, p, p]
    if ctx is not None:
        ctx_spec = pl.BlockSpec((None, PAST_LEN, HEAD_DIM), lambda b, h, qi: (b, 0, h))
        in_specs += [ctx_spec, ctx_spec]
        args += [ctx[0], ctx[1]]
    return pl.pallas_call(
        functools.partial(_attn_kernel, has_ctx=ctx is not None, stack=Q_PER_KV if ctx is None else 1),
        out_shape=jax.ShapeDtypeStruct((n_batch * seq, ATT_WIDTH), BF16),
        grid=(n_batch, N_KV_HEADS, q_blocks),
        in_specs=in_specs,
        out_specs=pl.BlockSpec((tq, gw), lambda b, h, qi: (b * q_blocks + qi, h)),
        compiler_params=_params(("arbitrary", "arbitrary", "arbitrary")),
        name="attention_latent" if ctx is not None else "attention_prompt",
    )(*args)


def _hgrn_constants():
    c = HG_CHUNK
    t = np.arange(c)
    cums, sels, pairs, scans, diags = [], [], [], [], []
    for d in range(2):
        pos = t if d == 0 else c - 1 - t
        pu, pt = pos[None, :], pos[:, None]
        ms, ss, ws = [], [], []
        for l in range(HG_LEVELS):
            m = c >> l
            blk = pos // m
            mid = (blk * m + m // 2)[:, None]
            late = ((pos % m) >= m // 2)
            ms.append(np.where(late[:, None], (pu >= mid) & (pu <= pt), (pu > pt) & (pu < mid)))
            ss.append(np.broadcast_to(late[:, None], (c, c)))
            ws.append((blk[:, None] == blk[None, :]) & late[:, None] & ~late[None, :])
        ms.append(pu <= pt)
        ms.append(pu > pt)
        cums.append(np.concatenate(ms, axis=0))
        sels.append(np.stack(ss))
        pairs.append(np.stack(ws))
        scans.append(np.concatenate([pu <= pt, pu <= pt], axis=1))
        diags.append(((pos // (c // 2))[:, None] == (pos // (c // 2))[None, :]) & (pu <= pt))
    return (jnp.asarray(np.stack(cums), BF16), jnp.asarray(np.stack(sels), F32),
            jnp.asarray(np.stack(pairs), F32), jnp.asarray(np.stack(scans), BF16),
            jnp.asarray(np.stack(diags), F32))


def _hgrn_kernel(*refs, n_seqs, n_chunks, has_s0):
    (hq_ref, zf_ref, zb_ref, hi_ref, hg_ref, lb_ref, og_ref, cum_ref, sel_ref, pair_ref, scan_ref,
     diag_ref) = refs[:12]
    refs = refs[12:]
    if has_s0:
        s0f_ref, s0b_ref, o_ref = refs[:3]
        refs = refs[3:]
    else:
        o_ref, sf_ref, sb_ref = refs[:3]
        refs = refs[3:]
    st_ref, acc_ref = refs
    hb = st_ref.shape[0]
    c = HG_CHUNK
    half = c // 2
    nt = (((1,), (1,)), ((), ()))
    tn = (((0,), (0,)), ((), ()))

    def direction(d, z_ref):
        raw = lb_ref[d]
        e = jnp.exp(raw - jnp.max(raw, axis=0, keepdims=True))
        lb = e[0:1] / jnp.sum(e, axis=0, keepdims=True)

        def initial_state(s, hh):
            if has_s0:
                return (s0f_ref, s0b_ref)[d][s, hh].T
            return jnp.zeros((HG_DV, HG_DK), F32)

        def gates(rows):
            f = lb + (1.0 - lb) * _sigmoid(z_ref[rows, :])
            logf = jnp.log(f)
            hi16 = logf.astype(BF16)
            lo16 = (logf - hi16.astype(F32)).astype(BF16)
            return f, hi16, lo16

        def row(p):
            t = p if d == 0 else c - 1 - p
            return slice(t, t + 1)

        early, late = (slice(0, half), slice(half, c)) if d == 0 else (slice(half, c), slice(0, half))

        def in_row_order(x_early, x_late):
            return jnp.concatenate([x_early, x_late] if d == 0 else [x_late, x_early], axis=0)

        def emit(rows, cols, o):
            if d == 0:
                acc_ref[rows, cols] = o
            else:
                tot = acc_ref[rows, cols] + o
                o_ref[rows, cols] = (_rms(tot, og_ref[...]) * hg_ref[rows, cols]).astype(o_ref.dtype)

        def two_level_operands(rows):
            f, hi16, lo16 = gates(rows)
            b = _bdot(scan_ref[d], jnp.concatenate([hi16, lo16], axis=0))
            kk = 1.0 - f
            q = hq_ref[rows, :]
            r_mid = b[row(half - 1)]
            x1 = in_row_order(kk[early] * jnp.exp(r_mid - b[early]), q[late] * jnp.exp(b[late] - r_mid)).astype(BF16)
            dq = in_row_order(b[early] - b[row(half // 2 - 1)], b[late] - b[row(half + half // 2 - 1)])
            b_end = b[row(c - 1)]
            span = jnp.maximum(
                jnp.maximum(b[row(0)] - b[row(half // 2 - 1)], b[row(half // 2 - 1)] - b[row(half - 1)]),
                jnp.maximum(b[row(half)] - b[row(half + half // 2 - 1)],
                            b[row(half + half // 2 - 1)] - b[row(c - 1)]))
            return dict(x1=x1, xq=(q * jnp.exp(dq)).astype(BF16), xk=(kk * jnp.exp(-dq)).astype(BF16),
                        q_in=(q * jnp.exp(b)).astype(BF16), k_out=(kk * jnp.exp(b_end - b)).astype(BF16),
                        a_end=jnp.exp(b_end), iv=hi_ref[rows, :].astype(BF16), span=span)

        def two_level_chunk(rows, ops, states):
            new_states = []
            for hh in range(hb):
                cols = slice(hh * HG_DK, (hh + 1) * HG_DK)
                g1 = _bdot(ops["x1"][:, cols], ops["x1"][:, cols].astype(F32).T.astype(BF16))
                g2 = _bdot(ops["xq"][:, cols], ops["xk"][:, cols].astype(F32).T.astype(BF16))
                att = jnp.where(pair_ref[d, 0] > 0.5, g1, jnp.where(diag_ref[d] > 0.5, g2, 0.0))
                st = states[hh]
                o = (_bdot(att.astype(BF16), ops["iv"][:, cols])
                     + lax.dot_general(ops["q_in"][:, cols], st.astype(BF16), nt, preferred_element_type=F32))
                dst = lax.dot_general(ops["iv"][:, cols], ops["k_out"][:, cols], tn, preferred_element_type=F32)
                new_states.append(ops["a_end"][:, cols] * st + dst)
                emit(rows, cols, o)
            return new_states

        def all_levels(rows):
            f, hi16, lo16 = gates(rows)
            cum = cum_ref[d]
            eall = jnp.exp(_bdot(cum, hi16) + _bdot(cum, lo16))
            for hh in range(hb):
                cols = slice(hh * HG_DK, (hh + 1) * HG_DK)
                kk = 1.0 - f[:, cols]
                q = hq_ref[rows, cols]
                iv = hi_ref[rows, cols]
                iv16 = iv.astype(BF16)
                att = jnp.zeros((c, c), F32)
                for l in range(HG_LEVELS):
                    x = (kk + sel_ref[d, l] * (q - kk)) * eall[l * c:(l + 1) * c, cols]
                    xb = x.astype(BF16)
                    att = att + pair_ref[d, l] * lax.dot_general(xb, xb, nt, preferred_element_type=F32)
                e_in = eall[HG_LEVELS * c:(HG_LEVELS + 1) * c, cols]
                e_out = eall[(HG_LEVELS + 1) * c:, cols]
                st = st_ref[hh]
                o = (_bdot(att.astype(BF16), iv16)
                     + jnp.sum(q * kk, axis=-1, keepdims=True) * iv
                     + lax.dot_general((q * e_in).astype(BF16), st.astype(BF16), nt, preferred_element_type=F32))
                dst = lax.dot_general(iv16, (kk * e_out).astype(BF16), tn, preferred_element_type=F32)
                st_ref[hh] = (e_in[0:1] * e_out[0:1]) * st + dst
                emit(rows, cols, o)

        def chunk_start(s, ci):
            cidx = ci if d == 0 else n_chunks - 1 - ci
            return (s * n_chunks + cidx) * c

        worst = jnp.zeros((1, hb * HG_DK), F32)
        states = [[initial_state(s, hh) for hh in range(hb)] for s in range(n_seqs)]
        for ci in range(n_chunks):
            for s in range(n_seqs):
                rows = slice(chunk_start(s, ci), chunk_start(s, ci) + c)
                ops = two_level_operands(rows)
                states[s] = two_level_chunk(rows, ops, states[s])
                worst = jnp.maximum(worst, ops["span"])
        if not has_s0:
            for s in range(n_seqs):
                for hh in range(hb):
                    (sf_ref, sb_ref)[d][s, hh] = states[s][hh].T

        @pl.when(jnp.logical_not(jnp.max(worst) <= HG_SAFE_EXPONENT))
        def _():
            for s in range(n_seqs):
                for hh in range(hb):
                    st_ref[hh] = initial_state(s, hh)

                def chunk(ci, carry):
                    all_levels(pl.ds(pl.multiple_of(chunk_start(s, ci), c), c))
                    return carry

                lax.fori_loop(0, n_chunks, chunk, 0)
                if not has_s0:
                    for hh in range(hb):
                        (sf_ref, sb_ref)[d][s, hh] = st_ref[hh].T

    direction(0, zf_ref)
    direction(1, zb_ref)


def _hgrn(p, lb_raw, o_gain, consts, n_batch, seq, row0, n_seqs, s0=None):
    hb = HG_HEAD_BLOCK
    w = hb * HG_DK
    rows = n_seqs * seq

    def seg(col):
        return pl.BlockSpec((rows, w), lambda b, h: (row0 // rows + b, col // w + h))

    def const(a):
        return pl.BlockSpec(a.shape, lambda b, h: (0,) * a.ndim)

    in_specs = [seg(COL_HQ), seg(COL_ZF), seg(COL_ZF + HG_KW), seg(COL_HI), seg(COL_HG),
                pl.BlockSpec((2, DEPTH + 1, w), lambda b, h: (0, 0, h)),
                pl.BlockSpec((1, HG_DV), lambda b, h: (0, 0))] + [const(a) for a in consts]
    args = [p, p, p, p, p, lb_raw, o_gain, *consts]
    has_s0 = s0 is not None
    st_spec = pl.BlockSpec((n_seqs, hb, HG_DK, HG_DV), lambda b, h: (b, h, 0, 0))
    if has_s0:
        in_specs += [st_spec, st_spec]
        args += [s0[0], s0[1]]
    out_shape = [jax.ShapeDtypeStruct((n_batch * seq, HG_VW), BF16)]
    out_specs = [pl.BlockSpec((rows, w), lambda b, h: (b, h))]
    if not has_s0:
        st_shape = jax.ShapeDtypeStruct((n_batch, HG_HEADS, HG_DK, HG_DV), F32)
        out_shape += [st_shape, st_shape]
        out_specs += [st_spec, st_spec]
    return pl.pallas_call(
        functools.partial(_hgrn_kernel, n_seqs=n_seqs, n_chunks=seq // HG_CHUNK, has_s0=has_s0),
        out_shape=tuple(out_shape),
        grid=(n_batch // n_seqs, HG_HEADS // hb),
        in_specs=in_specs,
        out_specs=tuple(out_specs),
        scratch_shapes=[pltpu.VMEM((hb, HG_DV, HG_DK), F32), pltpu.VMEM((rows, w), F32)],
        compiler_params=_params(("arbitrary", "arbitrary")),
        name="hgrn_latent" if has_s0 else "hgrn_prompt",
    )(*args)


def _outproj_kernel(attp_ref, atts_ref, hgp_ref, hgs_ref, wa_ref, wb_ref, xp_ref, xs_ref, g1_ref, gain_ref,
                    sh_ref, sc_ref, x1_ref, h2_ref):
    i = pl.program_id(0)
    n = pl.program_id(1)
    cols = pl.ds(pl.multiple_of(n * OUT_COL_TILE, OUT_COL_TILE), OUT_COL_TILE)

    def run(att_ref, hg_ref, x_ref):
        acc = _bdot(att_ref[...], wa_ref[...].astype(BF16)) + _bdot(hg_ref[...], wb_ref[...].astype(BF16))
        x1_ref[:, cols] = x_ref[...] + g1_ref[...] * acc

    pl.when(i < N_PROMPT_TILES)(lambda: run(attp_ref, hgp_ref, xp_ref))
    pl.when(i >= N_PROMPT_TILES)(lambda: run(atts_ref, hgs_ref, xs_ref))

    @pl.when(n == D_MODEL // OUT_COL_TILE - 1)
    def _():
        h2_ref[...] = _modulate(x1_ref[...], gain_ref[...], sh_ref[...], sc_ref[...]).astype(BF16)


def _out_projection(att_p, att_s, hg_p, hg_s, w_out, xp, xs, mods, norm_mlp):
    tm, tn = TOK_TILE, OUT_COL_TILE
    nt = D_MODEL // tn

    def prompt_rows(width):
        return pl.BlockSpec((tm, width), lambda i, n: (jnp.minimum(i, N_PROMPT_TILES - 1), 0),
                            pipeline_mode=pl.Buffered(1))

    def sample_rows(width):
        return pl.BlockSpec((tm, width), lambda i, n: (jnp.maximum(i - N_PROMPT_TILES, 0), 0),
                            pipeline_mode=pl.Buffered(1))

    return pl.pallas_call(
        _outproj_kernel,
        out_shape=(jax.ShapeDtypeStruct((N_TOK, D_MODEL), F32), jax.ShapeDtypeStruct((N_TOK, D_MODEL), BF16)),
        grid=(N_TILES, nt),
        in_specs=[prompt_rows(ATT_WIDTH), sample_rows(ATT_WIDTH), prompt_rows(HG_VW), sample_rows(HG_VW),
                  pl.BlockSpec((ATT_WIDTH, tn), lambda i, n: (0, n)),
                  pl.BlockSpec((HG_VW, tn), lambda i, n: (1, n)),
                  pl.BlockSpec((tm, tn), lambda i, n: (jnp.minimum(i, N_PROMPT_TILES - 1),
                                                       jnp.where(i < N_PROMPT_TILES, n, nt - 1))),
                  pl.BlockSpec((tm, tn), lambda i, n: (jnp.maximum(i - N_PROMPT_TILES, 0),
                                                       jnp.where(i < N_PROMPT_TILES, 0, n))),
                  _mod_spec(0, 2, tm, width=tn, col=lambda i, n: n),
                  pl.BlockSpec((1, D_MODEL), lambda i, n: (0, 0)),
                  _mod_spec(0, 3, tm), _mod_spec(0, 4, tm)],
        out_specs=(pl.BlockSpec((tm, D_MODEL), lambda i, n: (i, 0)),
                   pl.BlockSpec((tm, D_MODEL), lambda i, n: (i, 0))),
        compiler_params=_params(("arbitrary", "arbitrary")),
        name="out_projection",
    )(att_p, att_s, hg_p, hg_s, w_out, w_out, xp, xs, mods, norm_mlp, mods, mods)


def _mlp_kernel(h_ref, w1_ref, w2_ref, x_ref, g2_ref, *rest, final):
    if final:
        fin_ref, o_ref, res_ref = rest
    else:
        o_ref, res_ref = rest
    j = pl.program_id(1)
    res_ref[j] = x_ref[...]

    def step(first, last):
        a = jnp.square(jnp.maximum(_bdot(h_ref[...], w1_ref[...].astype(BF16)), 0.0)).astype(BF16)
        for n in range(D_MODEL // MLP_OUT_CHUNK):
            cols = slice(n * MLP_OUT_CHUNK, (n + 1) * MLP_OUT_CHUNK)
            p = _bdot(a, w2_ref[:, cols].astype(BF16))
            if not first:
                p = o_ref[:, cols] + p
            if last:
                per = MLP_OUT_CHUNK // MLP_RES_COLS
                res = jnp.concatenate([res_ref[n * per + k] for k in range(per)], axis=1)
                p = res + g2_ref[:, cols] * p
            o_ref[:, cols] = p
        if last and final:
            o_ref[...] = _rms(o_ref[...], fin_ref[...])

    pl.when(j == 0)(lambda: step(True, False))
    pl.when(jnp.logical_and(j > 0, j < MLP_STEPS - 1))(lambda: step(False, False))
    pl.when(j == MLP_STEPS - 1)(lambda: step(False, True))


def _mlp(h, x, w1, w2, mods, layer, tile0, n_tiles, final_norm=None):
    tm, th = TOK_TILE, FF_TILE
    final = final_norm is not None
    in_specs = [pl.BlockSpec((tm, D_MODEL), lambda i, j: (tile0 + i, 0), pipeline_mode=pl.Buffered(1)),
                pl.BlockSpec((None, D_MODEL, th), lambda i, j: (layer, 0, j)),
                pl.BlockSpec((None, th, D_MODEL), lambda i, j: (layer, j, 0)),
                pl.BlockSpec((tm, MLP_RES_COLS), lambda i, j: (tile0 + i, j)),
                _mod_spec(layer, 5, tm, tile_of=lambda i, j: tile0 + i)]
    args = [h, w1, w2, x, mods]
    if final:
        in_specs.append(pl.BlockSpec((1, D_MODEL), lambda i, j: (0, 0)))
        args.append(final_norm)
    return pl.pallas_call(
        functools.partial(_mlp_kernel, final=final),
        out_shape=jax.ShapeDtypeStruct((n_tiles * tm, D_MODEL), F32),
        grid=(n_tiles, MLP_STEPS),
        in_specs=in_specs,
        out_specs=pl.BlockSpec((tm, D_MODEL), lambda i, j: (i, 0)),
        scratch_shapes=[pltpu.VMEM((MLP_STEPS, tm, MLP_RES_COLS), F32)],
        compiler_params=_params(("arbitrary", "arbitrary")),
        name="mlp_final" if final else "mlp",
    )(*args)


def _pool_kernel(x_ref, wp_ref, ps_ref, gain1_ref, sh1_ref, sc1_ref, g1_ref, gain2_ref, sh2_ref, sc2_ref,
                 x3_ref, h4_ref):
    i = pl.program_id(0)
    tm = TOK_TILE
    rstd = lax.rsqrt(jnp.mean(x_ref[...] * x_ref[...], axis=-1, keepdims=True) + EPS)
    gs1 = gain1_ref[...] * (1.0 + sc1_ref[...])
    mix_gain = g1_ref[...] * ps_ref[...]

    def widen(m):
        return jnp.concatenate([m] * (POOL_GROUP // HEAD_DIM), axis=1)

    def mix_tile(seq):
        n_seq = tm // seq
        pitch = seq + 2 * POOL_HALO
        n_pad = n_seq * pitch
        halo = jnp.zeros((POOL_HALO, POOL_GROUP), F32)
        pos = lax.broadcasted_iota(jnp.int32, (seq, HEAD_DIM), 0)

        def down(a, k):
            return pltpu.roll(a, k % n_pad, 0)

        for g, w in enumerate(POOL_WINDOWS):
            half = w // 2
            cols = slice(g * POOL_GROUP, (g + 1) * POOL_GROUP)
            x = x_ref[:, cols]
            h = x * rstd * gs1[:, cols] + sh1_ref[:, cols]
            padded = jnp.concatenate(
                [piece for s in range(n_seq) for piece in (halo, h[s * seq:(s + 1) * seq], halo)], axis=0)
            back = padded
            m = 1
            while m < half:
                back = back + down(back, m)
                m *= 2
            ahead = back if half == 1 else down(back, -(half - 1))
            total = down(back, 1) + ahead
            total = jnp.concatenate(
                [total[s * pitch + POOL_HALO:s * pitch + POOL_HALO + seq] for s in range(n_seq)], axis=0)
            count = (jnp.minimum(pos + (w - half), seq) - jnp.maximum(pos - half, 0)).astype(F32)
            inv = jnp.concatenate([widen(1.0 / count)] * n_seq, axis=0)
            pooled = (total * inv - h).astype(BF16)
            mix = _bdot(pooled, wp_ref[g].astype(BF16))
            x3_ref[:, cols] = x + mix_gain[:, cols] * mix

    pl.when(i < N_PROMPT_TILES)(lambda: mix_tile(SEQ))
    pl.when(i >= N_PROMPT_TILES)(lambda: mix_tile(DEC_SEQ))
    x3 = x3_ref[...]
    h4_ref[...] = _modulate(x3, gain2_ref[...], sh2_ref[...], sc2_ref[...]).astype(BF16)


def _pool_mixer(x, w_pool, pool_scale, norm_mix, norm_mlp, mods):
    tm = TOK_TILE
    vec = pl.BlockSpec((1, D_MODEL), lambda i: (0, 0))
    tile = pl.BlockSpec((tm, D_MODEL), lambda i: (i, 0))
    return pl.pallas_call(
        _pool_kernel,
        out_shape=(jax.ShapeDtypeStruct((N_TOK, D_MODEL), F32), jax.ShapeDtypeStruct((N_TOK, D_MODEL), BF16)),
        grid=(N_TILES,),
        in_specs=[tile, pl.BlockSpec((len(POOL_WINDOWS), POOL_GROUP, POOL_GROUP), lambda i: (0, 0, 0)),
                  vec, vec, _mod_spec(1, 0, tm), _mod_spec(1, 1, tm), _mod_spec(1, 2, tm),
                  vec, _mod_spec(1, 3, tm), _mod_spec(1, 4, tm)],
        out_specs=(tile, tile),
        compiler_params=_params(("arbitrary",)),
        name="pool_mixer",
    )(x, w_pool, pool_scale, norm_mix, mods, mods, mods, norm_mlp, mods, mods)


def kernel(x_prompt, x_sample, cache_k, cache_v, state_hgrn_fwd, state_hgrn_bwd, c, c_ctx, w_ada, b_ada,
           norm_mix, norm_mlp, w_in_ab, w_out_ab, q_norm, k_norm, hg_norm, lb_raw, w_pool, pool_scale,
           w_mlp_in, w_mlp_out, final_norm):
    xp = x_prompt.reshape(N_PROMPT, D_MODEL)
    xs = x_sample.reshape(N_SAMPLE, D_MODEL)
    cv = jnp.concatenate([c_ctx[None, :], c, jnp.zeros((ADA_ROWS - 1 - DEC_BATCH, D_MODEL), F32)], axis=0)
    mods = _ada_table(cv, w_ada, b_ada).reshape(DEPTH, ADA_ROWS, 1, N_MOD * D_MODEL)

    h0 = _modulate0(xp, xs, norm_mix, mods)
    proj, new_k, new_v = _in_projection(h0, w_in_ab[0], q_norm[0:1], k_norm[0:1])
    ctx_k = cache_k[:, 0].reshape(DEC_BATCH, PAST_LEN, KV_WIDTH)
    ctx_v = cache_v[:, 0].reshape(DEC_BATCH, PAST_LEN, KV_WIDTH)
    att_p = _attention(proj, BATCH, SEQ, 0, SEQ)
    att_s = _attention(proj, DEC_BATCH, DEC_SEQ, N_PROMPT, 256, ctx=(ctx_k, ctx_v))
    consts = _hgrn_constants()
    hg_p, s_fwd, s_bwd = _hgrn(proj, lb_raw, hg_norm[0:1], consts, BATCH, SEQ, 0, HG_PROMPT_SEQS)
    s0 = (state_hgrn_fwd.reshape(DEC_BATCH, HG_HEADS, HG_DK, HG_DV),
          state_hgrn_bwd.reshape(DEC_BATCH, HG_HEADS, HG_DK, HG_DV))
    (hg_s,) = _hgrn(proj, lb_raw, hg_norm[0:1], consts, DEC_BATCH, DEC_SEQ, N_PROMPT, 1, s0=s0)
    x1, h2 = _out_projection(att_p, att_s, hg_p, hg_s, w_out_ab[0], xp, xs, mods, norm_mlp[0:1])
    x2 = _mlp(h2, x1, w_mlp_in, w_mlp_out, mods, 0, 0, N_TILES)

    x3, h4 = _pool_mixer(x2, w_pool[0], pool_scale[0:1], norm_mix[1:2], norm_mlp[1:2], mods)
    fin = final_norm[None, :]
    y_prompt = _mlp(h4, x3, w_mlp_in, w_mlp_out, mods, 1, 0, N_PROMPT_TILES, final_norm=fin)
    y_sample = _mlp(h4, x3, w_mlp_in, w_mlp_out, mods, 1, N_PROMPT_TILES, N_TILES - N_PROMPT_TILES,
                    final_norm=fin)

    return (y_prompt.reshape(BATCH, SEQ, D_MODEL), y_sample.reshape(DEC_BATCH, DEC_SEQ, D_MODEL),
            new_k.reshape(BATCH, 1, SEQ, N_KV_HEADS, HEAD_DIM), new_v.reshape(BATCH, 1, SEQ, N_KV_HEADS, HEAD_DIM),
            s_fwd.reshape(BATCH, 1, HG_HEADS, HG_DK, HG_DV), s_bwd.reshape(BATCH, 1, HG_HEADS, HG_DK, HG_DV))
```

```python
import functools

import numpy as np
import jax
import jax.numpy as jnp
from jax import lax
from jax.experimental import pallas as pl
from jax.experimental.pallas import tpu as pltpu

F32 = jnp.float32
BF16 = jnp.bfloat16

D_MODEL = 2048
BATCH = 16
SEQ = 256
DEPTH = 2
DEC_BATCH = 2
DEC_SEQ = 1024
PAST_LEN = 256
GRID_W = 64
HEAD_DIM = 128
N_Q_HEADS = 8
N_KV_HEADS = 2
Q_PER_KV = N_Q_HEADS // N_KV_HEADS
ATT_WIDTH = N_Q_HEADS * HEAD_DIM
KV_WIDTH = N_KV_HEADS * HEAD_DIM
HG_HEADS = 8
HG_DK = 128
HG_DV = 128
HG_KW = HG_HEADS * HG_DK
HG_VW = HG_HEADS * HG_DV
IN_AB = ATT_WIDTH + 2 * KV_WIDTH + 3 * HG_KW + 2 * HG_VW
MIX_WIDTH = ATT_WIDTH + HG_VW
POOL_WINDOWS = (2, 4, 8, 16)
POOL_GROUP = D_MODEL // len(POOL_WINDOWS)
POOL_HALO = 8
D_FF = 4 * D_MODEL
ROPE_THETA = 10000.0
ROPE_HALF = HEAD_DIM // 2
EPS = 1e-6
N_MOD = 6

N_PROMPT = BATCH * SEQ
N_SAMPLE = DEC_BATCH * DEC_SEQ
N_TOK = N_PROMPT + N_SAMPLE
ADA_ROWS = 16
ADA_COL_TILE = 2048
TOK_TILE = 1024
N_TILES = N_TOK // TOK_TILE
N_PROMPT_TILES = N_PROMPT // TOK_TILE
PROJ_TOK_TILE = 2048
PROJ_COL_TILE = 512
OUT_COL_TILE = 512
FF_TILE = 512
MLP_STEPS = D_FF // FF_TILE
MLP_RES_COLS = D_MODEL // MLP_STEPS
MLP_OUT_CHUNK = 512
HG_CHUNK = 128
HG_LEVELS = 7
HG_HEAD_BLOCK = 4
HG_PROMPT_SEQS = 2
HG_SAFE_EXPONENT = 80.0
VMEM_LIMIT = 56 * 2 ** 20

COL_Q = 0
COL_K = ATT_WIDTH
COL_V = COL_K + KV_WIDTH
COL_HQ = COL_V + KV_WIDTH
COL_ZF = COL_HQ + HG_KW
COL_HI = COL_ZF + 2 * HG_KW
COL_HG = COL_HI + HG_VW


def _params(semantics):
    return pltpu.CompilerParams(dimension_semantics=semantics, vmem_limit_bytes=VMEM_LIMIT)


def _sigmoid(x):
    return 1.0 / (1.0 + jnp.exp(-x))


def _silu(x):
    return x * _sigmoid(x)


def _rms(x, gain):
    return x * lax.rsqrt(jnp.mean(x * x, axis=-1, keepdims=True) + EPS) * gain


def _bdot(a, b):
    return jnp.dot(a, b, preferred_element_type=F32)


def _mod_row(tile, tile_rows):
    first = N_PROMPT // tile_rows
    per_seq = DEC_SEQ // tile_rows
    return jnp.where(tile < first, 0, 1 + (tile - first) // per_seq)


def _mod_spec(layer, chunk, tile_rows, width=D_MODEL, col=lambda *g: 0, tile_of=lambda *g: g[0]):
    per = D_MODEL // width
    return pl.BlockSpec((None, None, 1, width),
                        lambda *g: (layer, _mod_row(tile_of(*g), tile_rows), 0, chunk * per + col(*g)))


def _ada_kernel(cv_ref, w_ref, b_ref, o_ref):
    s = _silu(cv_ref[...]).astype(BF16)
    o_ref[...] = _bdot(s, w_ref[...].astype(BF16)) + b_ref[...]


def _ada_table(cv, w_ada, b_ada):
    tn = ADA_COL_TILE
    n = N_MOD * D_MODEL
    return pl.pallas_call(
        _ada_kernel,
        out_shape=jax.ShapeDtypeStruct((DEPTH, ADA_ROWS, n), F32),
        grid=(DEPTH, n // tn),
        in_specs=[pl.BlockSpec((ADA_ROWS, D_MODEL), lambda l, j: (0, 0)),
                  pl.BlockSpec((None, D_MODEL, tn), lambda l, j: (l, 0, j)),
                  pl.BlockSpec((None, 1, tn), lambda l, j: (l, 0, j))],
        out_specs=pl.BlockSpec((None, ADA_ROWS, tn), lambda l, j: (l, 0, j)),
        compiler_params=_params(("arbitrary", "arbitrary")),
        name="ada_table",
    )(cv, w_ada, b_ada.reshape(DEPTH, 1, n))


def _modulate(x, gain, shift, scale):
    return _rms(x, gain) * (1.0 + scale) + shift


def _mod0_kernel(xp_ref, xs_ref, gain_ref, sh_ref, sc_ref, o_ref):
    i = pl.program_id(0)

    def run(x_ref):
        o_ref[...] = _modulate(x_ref[...], gain_ref[...], sh_ref[...], sc_ref[...]).astype(BF16)

    pl.when(i < N_PROMPT_TILES)(lambda: run(xp_ref))
    pl.when(i >= N_PROMPT_TILES)(lambda: run(xs_ref))


def _modulate0(xp, xs, norm_mix, mods):
    tm = TOK_TILE
    first = N_PROMPT_TILES
    return pl.pallas_call(
        _mod0_kernel,
        out_shape=jax.ShapeDtypeStruct((N_TOK, D_MODEL), BF16),
        grid=(N_TOK // tm,),
        in_specs=[pl.BlockSpec((tm, D_MODEL), lambda i: (jnp.minimum(i, first - 1), 0)),
                  pl.BlockSpec((tm, D_MODEL), lambda i: (jnp.maximum(i - first, 0), 0)),
                  pl.BlockSpec((1, D_MODEL), lambda i: (0, 0)),
                  _mod_spec(0, 0, tm), _mod_spec(0, 1, tm)],
        out_specs=pl.BlockSpec((tm, D_MODEL), lambda i: (i, 0)),
        compiler_params=_params(("arbitrary",)),
        name="modulate0",
    )(xp, xs, norm_mix[0:1], mods, mods)


def _rope(y, cos, sin, perm2):
    hi = y.astype(BF16)
    lo = (y - hi.astype(F32)).astype(BF16)
    rot = _bdot(jnp.concatenate([hi, lo], axis=1), perm2)
    return y * cos + rot * sin


def _inproj_kernel(h_ref, w_ref, qg_ref, kg_ref, cos_ref, sin_ref, perm_ref, p_ref, nk_ref, nv_ref):
    i = pl.program_id(0)
    j = pl.program_id(1)
    latent = i >= N_PROMPT // PROJ_TOK_TILE
    heads = PROJ_COL_TILE // HEAD_DIM
    kv_tile = COL_K // PROJ_COL_TILE
    is_q = j < kv_tile

    def attention_tile(rope):
        gain = jnp.where(is_q, qg_ref[...], kg_ref[...])
        w = w_ref[...].astype(BF16)
        for s in range(PROJ_TOK_TILE // DEC_SEQ):
            rows = slice(s * DEC_SEQ, (s + 1) * DEC_SEQ)
            acc = _bdot(h_ref[rows, :], w)
            for hh in range(heads):
                cols = slice(hh * HEAD_DIM, (hh + 1) * HEAD_DIM)
                x = acc[:, cols]
                y = _rms(x, gain)
                if rope:
                    y = _rope(y, cos_ref[...], sin_ref[...], perm_ref[...])
                if hh >= N_KV_HEADS:
                    y = jnp.where(is_q, y, x)
                p_ref[rows, cols] = y

    pl.when(jnp.logical_and(j <= kv_tile, latent))(lambda: attention_tile(True))
    pl.when(jnp.logical_and(j <= kv_tile, jnp.logical_not(latent)))(lambda: attention_tile(False))

    @pl.when(jnp.logical_and(j == kv_tile, jnp.logical_not(latent)))
    def _():
        nk_ref[...] = p_ref[:, :KV_WIDTH]
        nv_ref[...] = p_ref[:, KV_WIDTH:]

    def plain_tile(act):
        w = w_ref[...].astype(BF16)
        for s in range(PROJ_TOK_TILE // DEC_SEQ):
            rows = slice(s * DEC_SEQ, (s + 1) * DEC_SEQ)
            p_ref[rows, :] = act(_bdot(h_ref[rows, :], w))

    is_gate = jnp.logical_and(j >= COL_ZF // PROJ_COL_TILE, j < COL_HI // PROJ_COL_TILE)
    is_value = jnp.logical_and(j >= COL_HI // PROJ_COL_TILE, j < COL_HG // PROJ_COL_TILE)
    raw = jnp.logical_or(is_gate, is_value)
    pl.when(raw)(lambda: plain_tile(lambda a: a))
    pl.when(jnp.logical_and(j > kv_tile, jnp.logical_not(raw)))(lambda: plain_tile(_silu))


def _rope_tables():
    t = np.arange(DEC_SEQ)
    row = (t // GRID_W).astype(np.float32)
    col = (t % GRID_W).astype(np.float32)
    inv = (np.float32(ROPE_THETA) ** (-np.arange(0, ROPE_HALF, 2, dtype=np.float32) / np.float32(ROPE_HALF))).astype(np.float32)
    ar = row[:, None] * inv
    ac = col[:, None] * inv
    ang = np.concatenate([ar, ar, ac, ac], axis=-1).astype(np.float32)
    cos = np.cos(ang).astype(np.float32)
    sin = np.sin(ang).astype(np.float32)
    qw = ROPE_HALF // 2
    perm = np.zeros((HEAD_DIM, HEAD_DIM), np.float32)
    for k in range(qw):
        perm[qw + k, k] = -1.0
        perm[k, qw + k] = 1.0
        perm[3 * qw + k, 2 * qw + k] = -1.0
        perm[2 * qw + k, 3 * qw + k] = 1.0
    return jnp.asarray(cos), jnp.asarray(sin), jnp.asarray(np.concatenate([perm, perm], axis=0), BF16)


def _in_projection(h, w_in, q_gain, k_gain):
    tm, tn = PROJ_TOK_TILE, PROJ_COL_TILE
    n_prompt_tiles = N_PROMPT // tm
    cos, sin, perm2 = _rope_tables()
    table = pl.BlockSpec((DEC_SEQ, HEAD_DIM), lambda i, j: (0, 0))
    gain = pl.BlockSpec((1, HEAD_DIM), lambda i, j: (0, 0))
    state = pl.BlockSpec((tm, KV_WIDTH), lambda i, j: (jnp.minimum(i, n_prompt_tiles - 1), 0))
    return pl.pallas_call(
        _inproj_kernel,
        out_shape=(jax.ShapeDtypeStruct((N_TOK, IN_AB), F32),
                   jax.ShapeDtypeStruct((N_PROMPT, KV_WIDTH), F32),
                   jax.ShapeDtypeStruct((N_PROMPT, KV_WIDTH), F32)),
        grid=(N_TOK // tm, IN_AB // tn),
        in_specs=[pl.BlockSpec((tm, D_MODEL), lambda i, j: (i, 0)),
                  pl.BlockSpec((D_MODEL, tn), lambda i, j: (0, j)),
                  gain, gain, table, table, pl.BlockSpec((2 * HEAD_DIM, HEAD_DIM), lambda i, j: (0, 0))],
        out_specs=(pl.BlockSpec((tm, tn), lambda i, j: (i, j)), state, state),
        compiler_params=_params(("arbitrary", "arbitrary")),
        name="in_projection",
    )(h, w_in, q_gain, k_gain, cos, sin, perm2)


def _attn_kernel(*refs, has_ctx, stack):
    if has_ctx:
        q_ref, k_ref, v_ref, ck_ref, cv_ref, o_ref = refs
    else:
        q_ref, k_ref, v_ref, o_ref = refs
    scale = HEAD_DIM ** -0.5
    nt = (((1,), (1,)), ((), ()))
    k = k_ref[...].astype(BF16)
    v = v_ref[...].astype(BF16)
    if has_ctx:
        ck = ck_ref[...].astype(BF16)
        cv = cv_ref[...].astype(BF16)
    tq = q_ref.shape[0]
    for g0 in range(0, Q_PER_KV, stack):
        q = jnp.concatenate([q_ref[:, g * HEAD_DIM:(g + 1) * HEAD_DIM] for g in range(g0, g0 + stack)], axis=0)
        q = (q * scale).astype(BF16)
        s = lax.dot_general(q, k, nt, preferred_element_type=F32)
        m = jnp.max(s, axis=-1, keepdims=True)
        if has_ctx:
            sc = lax.dot_general(q, ck, nt, preferred_element_type=F32)
            m = jnp.maximum(m, jnp.max(sc, axis=-1, keepdims=True))
        p = jnp.exp(s - m)
        den = jnp.sum(p, axis=-1, keepdims=True)
        o = _bdot(p.astype(BF16), v)
        if has_ctx:
            pc = jnp.exp(sc - m)
            den = den + jnp.sum(pc, axis=-1, keepdims=True)
            o = o + _bdot(pc.astype(BF16), cv)
        o = (o / den).astype(o_ref.dtype)
        for g in range(stack):
            o_ref[:, (g0 + g) * HEAD_DIM:(g0 + g + 1) * HEAD_DIM] = o[g * tq:(g + 1) * tq]


def _attention(p, n_batch, seq, row0, tq, ctx=None):
    q_blocks = seq // tq
    gw = Q_PER_KV * HEAD_DIM
    in_specs = [
        pl.BlockSpec((tq, gw), lambda b, h, qi: (row0 // tq + b * q_blocks + qi, h)),
        pl.BlockSpec((seq, HEAD_DIM), lambda b, h, qi: (row0 // seq + b, COL_K // HEAD_DIM + h)),
        pl.BlockSpec((seq, HEAD_DIM), lambda b, h, qi: (row0 // seq + b, COL_V // HEAD_DIM + h)),
    ]
    args = [p, p, p]
    if ctx is not None:
        ctx_spec = pl.BlockSpec((None, PAST_LEN, HEAD_DIM), lambda b, h, qi: (b, 0, h))
        in_specs += [ctx_spec, ctx_spec]
        args += [ctx[0], ctx[1]]
    return pl.pallas_call(
        functools.partial(_attn_kernel, has_ctx=ctx is not None, stack=Q_PER_KV if ctx is None else 1),
        out_shape=jax.ShapeDtypeStruct((n_batch * seq, ATT_WIDTH), BF16),
        grid=(n_batch, N_KV_HEADS, q_blocks),
        in_specs=in_specs,
        out_specs=pl.BlockSpec((tq, gw), lambda b, h, qi: (b * q_blocks + qi, h)),
        compiler_params=_params(("arbitrary", "arbitrary", "arbitrary")),
        name="attention_latent" if ctx is not None else "attention_prompt",
    )(*args)


def _hgrn_constants():
    c = HG_CHUNK
    t = np.arange(c)
    cums, sels, pairs, scans, diags = [], [], [], [], []
    for d in range(2):
        pos = t if d == 0 else c - 1 - t
        pu, pt = pos[None, :], pos[:, None]
        ms, ss, ws = [], [], []
        for l in range(HG_LEVELS):
            m = c >> l
            blk = pos // m
            mid = (blk * m + m // 2)[:, None]
            late = ((pos % m) >= m // 2)
            ms.append(np.where(late[:, None], (pu >= mid) & (pu <= pt), (pu > pt) & (pu < mid)))
            ss.append(np.broadcast_to(late[:, None], (c, c)))
            ws.append((blk[:, None] == blk[None, :]) & late[:, None] & ~late[None, :])
        ms.append(pu <= pt)
        ms.append(pu > pt)
        cums.append(np.concatenate(ms, axis=0))
        sels.append(np.stack(ss))
        pairs.append(np.stack(ws))
        scans.append(np.concatenate([pu <= pt, pu <= pt], axis=1))
        diags.append(((pos // (c // 2))[:, None] == (pos // (c // 2))[None, :]) & (pu <= pt))
    return (jnp.asarray(np.stack(cums), BF16), jnp.asarray(np.stack(sels), F32),
            jnp.asarray(np.stack(pairs), F32), jnp.asarray(np.stack(scans), BF16),
            jnp.asarray(np.stack(diags), F32))


def _hgrn_kernel(*refs, n_seqs, n_chunks, has_s0):
    (hq_ref, zf_ref, zb_ref, hi_ref, hg_ref, lb_ref, og_ref, cum_ref, sel_ref, pair_ref, scan_ref,
     diag_ref) = refs[:12]
    refs = refs[12:]
    if has_s0:
        s0f_ref, s0b_ref, o_ref = refs[:3]
        refs = refs[3:]
    else:
        o_ref, sf_ref, sb_ref = refs[:3]
        refs = refs[3:]
    st_ref, acc_ref = refs
    hb = st_ref.shape[0]
    c = HG_CHUNK
    half = c // 2
    nt = (((1,), (1,)), ((), ()))
    tn = (((0,), (0,)), ((), ()))

    def direction(d, z_ref):
        raw = lb_ref[d]
        e = jnp.exp(raw - jnp.max(raw, axis=0, keepdims=True))
        lb = e[0:1] / jnp.sum(e, axis=0, keepdims=True)

        def initial_state(s, hh):
            if has_s0:
                return (s0f_ref, s0b_ref)[d][s, hh].T
            return jnp.zeros((HG_DV, HG_DK), F32)

        def gates(rows):
            f = lb + (1.0 - lb) * _sigmoid(z_ref[rows, :])
            logf = jnp.log(f)
            hi16 = logf.astype(BF16)
            lo16 = (logf - hi16.astype(F32)).astype(BF16)
            return f, hi16, lo16

        def row(p):
            t = p if d == 0 else c - 1 - p
            return slice(t, t + 1)

        early, late = (slice(0, half), slice(half, c)) if d == 0 else (slice(half, c), slice(0, half))

        def in_row_order(x_early, x_late):
            return jnp.concatenate([x_early, x_late] if d == 0 else [x_late, x_early], axis=0)

        def emit(rows, cols, o):
            if d == 0:
                acc_ref[rows, cols] = o
            else:
                tot = acc_ref[rows, cols] + o
                o_ref[rows, cols] = (_rms(tot, og_ref[...]) * hg_ref[rows, cols]).astype(o_ref.dtype)

        def two_level_operands(rows):
            f, hi16, lo16 = gates(rows)
            b = _bdot(scan_ref[d], jnp.concatenate([hi16, lo16], axis=0))
            kk = 1.0 - f
            q = hq_ref[rows, :]
            r_mid = b[row(half - 1)]
            x1 = in_row_order(kk[early] * jnp.exp(r_mid - b[early]), q[late] * jnp.exp(b[late] - r_mid)).astype(BF16)
            dq = in_row_order(b[early] - b[row(half // 2 - 1)], b[late] - b[row(half + half // 2 - 1)])
            b_end = b[row(c - 1)]
            span = jnp.maximum(
                jnp.maximum(b[row(0)] - b[row(half // 2 - 1)], b[row(half // 2 - 1)] - b[row(half - 1)]),
                jnp.maximum(b[row(half)] - b[row(half + half // 2 - 1)],
                            b[row(half + half // 2 - 1)] - b[row(c - 1)]))
            return dict(x1=x1, xq=(q * jnp.exp(dq)).astype(BF16), xk=(kk * jnp.exp(-dq)).astype(BF16),
                        q_in=(q * jnp.exp(b)).astype(BF16), k_out=(kk * jnp.exp(b_end - b)).astype(BF16),
                        a_end=jnp.exp(b_end), iv=hi_ref[rows, :].astype(BF16), span=span)

        def two_level_chunk(rows, ops, states, need_state):
            new_states = []
            for hh in range(hb):
                cols = slice(hh * HG_DK, (hh + 1) * HG_DK)
                g1 = _bdot(ops["x1"][:, cols], ops["x1"][:, cols].astype(F32).T.astype(BF16))
                g2 = _bdot(ops["xq"][:, cols], ops["xk"][:, cols].astype(F32).T.astype(BF16))
                att = jnp.where(pair_ref[d, 0] > 0.5, g1, jnp.where(diag_ref[d] > 0.5, g2, 0.0)).astype(BF16)
                st = states[hh]
                o = _bdot(att, ops["iv"][:, cols])
                if st is not None:
                    o = o + lax.dot_general(ops["q_in"][:, cols], st.astype(BF16), nt, preferred_element_type=F32)
                emit(rows, cols, o)
                if not need_state:
                    new_states.append(None)
                    continue
                dst = lax.dot_general(ops["iv"][:, cols], ops["k_out"][:, cols], tn, preferred_element_type=F32)
                new_states.append(dst if st is None else ops["a_end"][:, cols] * st + dst)
            return new_states

        def all_levels(rows):
            f, hi16, lo16 = gates(rows)
            cum = cum_ref[d]
            eall = jnp.exp(_bdot(cum, hi16) + _bdot(cum, lo16))
            for hh in range(hb):
                cols = slice(hh * HG_DK, (hh + 1) * HG_DK)
                kk = 1.0 - f[:, cols]
                q = hq_ref[rows, cols]
                iv = hi_ref[rows, cols]
                iv16 = iv.astype(BF16)
                att = jnp.zeros((c, c), F32)
                for l in range(HG_LEVELS):
                    x = (kk + sel_ref[d, l] * (q - kk)) * eall[l * c:(l + 1) * c, cols]
                    xb = x.astype(BF16)
                    att = att + pair_ref[d, l] * lax.dot_general(xb, xb, nt, preferred_element_type=F32)
                e_in = eall[HG_LEVELS * c:(HG_LEVELS + 1) * c, cols]
                e_out = eall[(HG_LEVELS + 1) * c:, cols]
                st = st_ref[hh]
                o = (_bdot(att.astype(BF16), iv16)
                     + jnp.sum(q * kk, axis=-1, keepdims=True) * iv
                     + lax.dot_general((q * e_in).astype(BF16), st.astype(BF16), nt, preferred_element_type=F32))
                dst = lax.dot_general(iv16, (kk * e_out).astype(BF16), tn, preferred_element_type=F32)
                st_ref[hh] = (e_in[0:1] * e_out[0:1]) * st + dst
                emit(rows, cols, o)

        def chunk_start(s, ci):
            cidx = ci if d == 0 else n_chunks - 1 - ci
            return (s * n_chunks + cidx) * c

        worst = jnp.zeros((1, hb * HG_DK), F32)
        states = [[initial_state(s, hh) if has_s0 else None for hh in range(hb)] for s in range(n_seqs)]
        for ci in range(n_chunks):
            for s in range(n_seqs):
                rows = slice(chunk_start(s, ci), chunk_start(s, ci) + c)
                ops = two_level_operands(rows)
                states[s] = two_level_chunk(rows, ops, states[s], need_state=ci < n_chunks - 1 or not has_s0)
                worst = jnp.maximum(worst, ops["span"])
        if not has_s0:
            for s in range(n_seqs):
                for hh in range(hb):
                    (sf_ref, sb_ref)[d][s, hh] = states[s][hh].T

        @pl.when(jnp.logical_not(jnp.max(worst) <= HG_SAFE_EXPONENT))
        def _():
            for s in range(n_seqs):
                for hh in range(hb):
                    st_ref[hh] = initial_state(s, hh)

                def chunk(ci, carry):
                    all_levels(pl.ds(pl.multiple_of(chunk_start(s, ci), c), c))
                    return carry

                lax.fori_loop(0, n_chunks, chunk, 0)
                if not has_s0:
                    for hh in range(hb):
                        (sf_ref, sb_ref)[d][s, hh] = st_ref[hh].T

    direction(0, zf_ref)
    direction(1, zb_ref)


def _hgrn(p, lb_raw, o_gain, consts, n_batch, seq, row0, n_seqs, s0=None):
    hb = HG_HEAD_BLOCK
    w = hb * HG_DK
    rows = n_seqs * seq

    def seg(col):
        return pl.BlockSpec((rows, w), lambda b, h: (row0 // rows + b, col // w + h))

    def const(a):
        return pl.BlockSpec(a.shape, lambda b, h: (0,) * a.ndim)

    in_specs = [seg(COL_HQ), seg(COL_ZF), seg(COL_ZF + HG_KW), seg(COL_HI), seg(COL_HG),
                pl.BlockSpec((2, DEPTH + 1, w), lambda b, h: (0, 0, h)),
                pl.BlockSpec((1, HG_DV), lambda b, h: (0, 0))] + [const(a) for a in consts]
    args = [p, p, p, p, p, lb_raw, o_gain, *consts]
    has_s0 = s0 is not None
    st_spec = pl.BlockSpec((n_seqs, hb, HG_DK, HG_DV), lambda b, h: (b, h, 0, 0))
    if has_s0:
        in_specs += [st_spec, st_spec]
        args += [s0[0], s0[1]]
    out_shape = [jax.ShapeDtypeStruct((n_batch * seq, HG_VW), BF16)]
    out_specs = [pl.BlockSpec((rows, w), lambda b, h: (b, h))]
    if not has_s0:
        st_shape = jax.ShapeDtypeStruct((n_batch, HG_HEADS, HG_DK, HG_DV), F32)
        out_shape += [st_shape, st_shape]
        out_specs += [st_spec, st_spec]
    return pl.pallas_call(
        functools.partial(_hgrn_kernel, n_seqs=n_seqs, n_chunks=seq // HG_CHUNK, has_s0=has_s0),
        out_shape=tuple(out_shape),
        grid=(n_batch // n_seqs, HG_HEADS // hb),
        in_specs=in_specs,
        out_specs=tuple(out_specs),
        scratch_shapes=[pltpu.VMEM((hb, HG_DV, HG_DK), F32), pltpu.VMEM((rows, w), F32)],
        compiler_params=_params(("arbitrary", "arbitrary")),
        name="hgrn_latent" if has_s0 else "hgrn_prompt",
    )(*args)


def _outproj_kernel(attp_ref, atts_ref, hgp_ref, hgs_ref, wa_ref, wb_ref, xp_ref, xs_ref, g1_ref, gain_ref,
                    sh_ref, sc_ref, x1_ref, h2_ref, full_ref):
    i = pl.program_id(0)
    n = pl.program_id(1)

    def run(att_ref, hg_ref, x_ref):
        acc = _bdot(att_ref[...], wa_ref[...].astype(BF16)) + _bdot(hg_ref[...], wb_ref[...].astype(BF16))
        x1 = x_ref[...] + g1_ref[...] * acc
        x1_ref[...] = x1
        full_ref[n] = x1

    pl.when(i < N_PROMPT_TILES)(lambda: run(attp_ref, hgp_ref, xp_ref))
    pl.when(i >= N_PROMPT_TILES)(lambda: run(atts_ref, hgs_ref, xs_ref))

    @pl.when(n == D_MODEL // OUT_COL_TILE - 1)
    def _():
        nt = D_MODEL // OUT_COL_TILE
        ms = jnp.sum(full_ref[0] * full_ref[0], axis=-1, keepdims=True)
        for k in range(1, nt):
            ms = ms + jnp.sum(full_ref[k] * full_ref[k], axis=-1, keepdims=True)
        rstd = lax.rsqrt(ms / D_MODEL + EPS)
        gs = gain_ref[...] * (1.0 + sc_ref[...])
        for k in range(nt):
            cols = slice(k * OUT_COL_TILE, (k + 1) * OUT_COL_TILE)
            h2_ref[:, cols] = (full_ref[k] * rstd * gs[:, cols] + sh_ref[:, cols]).astype(BF16)


def _out_projection(att_p, att_s, hg_p, hg_s, w_out, xp, xs, mods, norm_mlp):
    tm, tn = TOK_TILE, OUT_COL_TILE
    nt = D_MODEL // tn

    def prompt_rows(width):
        return pl.BlockSpec((tm, width), lambda i, n: (jnp.minimum(i, N_PROMPT_TILES - 1), 0),
                            pipeline_mode=pl.Buffered(1))

    def sample_rows(width):
        return pl.BlockSpec((tm, width), lambda i, n: (jnp.maximum(i - N_PROMPT_TILES, 0), 0),
                            pipeline_mode=pl.Buffered(1))

    return pl.pallas_call(
        _outproj_kernel,
        out_shape=(jax.ShapeDtypeStruct((N_TOK, D_MODEL), F32), jax.ShapeDtypeStruct((N_TOK, D_MODEL), BF16)),
        grid=(N_TILES, nt),
        in_specs=[prompt_rows(ATT_WIDTH), sample_rows(ATT_WIDTH), prompt_rows(HG_VW), sample_rows(HG_VW),
                  pl.BlockSpec((ATT_WIDTH, tn), lambda i, n: (0, n)),
                  pl.BlockSpec((HG_VW, tn), lambda i, n: (1, n)),
                  pl.BlockSpec((tm, tn), lambda i, n: (jnp.minimum(i, N_PROMPT_TILES - 1),
                                                       jnp.where(i < N_PROMPT_TILES, n, nt - 1))),
                  pl.BlockSpec((tm, tn), lambda i, n: (jnp.maximum(i - N_PROMPT_TILES, 0),
                                                       jnp.where(i < N_PROMPT_TILES, 0, n))),
                  _mod_spec(0, 2, tm, width=tn, col=lambda i, n: n),
                  pl.BlockSpec((1, D_MODEL), lambda i, n: (0, 0)),
                  _mod_spec(0, 3, tm), _mod_spec(0, 4, tm)],
        out_specs=(pl.BlockSpec((tm, tn), lambda i, n: (i, n)),
                   pl.BlockSpec((tm, D_MODEL), lambda i, n: (i, 0))),
        scratch_shapes=[pltpu.VMEM((nt, tm, tn), F32)],
        compiler_params=_params(("arbitrary", "arbitrary")),
        name="out_projection",
    )(att_p, att_s, hg_p, hg_s, w_out, w_out, xp, xs, mods, norm_mlp, mods, mods)


def _mlp_kernel(h_ref, w1_ref, w2_ref, x_ref, g2_ref, *rest, final):
    if final:
        fin_ref, o_ref, res_ref = rest
    else:
        o_ref, res_ref = rest
    j = pl.program_id(1)
    res_ref[j] = x_ref[...]

    def step(first, last):
        a = jnp.square(jnp.maximum(_bdot(h_ref[...], w1_ref[...].astype(BF16)), 0.0)).astype(BF16)
        for n in range(D_MODEL // MLP_OUT_CHUNK):
            cols = slice(n * MLP_OUT_CHUNK, (n + 1) * MLP_OUT_CHUNK)
            p = _bdot(a, w2_ref[:, cols].astype(BF16))
            if not first:
                p = o_ref[:, cols] + p
            if last:
                per = MLP_OUT_CHUNK // MLP_RES_COLS
                res = jnp.concatenate([res_ref[n * per + k] for k in range(per)], axis=1)
                p = res + g2_ref[:, cols] * p
            o_ref[:, cols] = p
        if last and final:
            o_ref[...] = _rms(o_ref[...], fin_ref[...])

    pl.when(j == 0)(lambda: step(True, False))
    pl.when(jnp.logical_and(j > 0, j < MLP_STEPS - 1))(lambda: step(False, False))
    pl.when(j == MLP_STEPS - 1)(lambda: step(False, True))


def _mlp(h, x, w1, w2, mods, layer, tile0, n_tiles, final_norm=None):
    tm, th = TOK_TILE, FF_TILE
    final = final_norm is not None
    in_specs = [pl.BlockSpec((tm, D_MODEL), lambda i, j: (tile0 + i, 0), pipeline_mode=pl.Buffered(1)),
                pl.BlockSpec((None, D_MODEL, th), lambda i, j: (layer, 0, j)),
                pl.BlockSpec((None, th, D_MODEL), lambda i, j: (layer, j, 0)),
                pl.BlockSpec((tm, MLP_RES_COLS), lambda i, j: (tile0 + i, j)),
                _mod_spec(layer, 5, tm, tile_of=lambda i, j: tile0 + i)]
    args = [h, w1, w2, x, mods]
    if final:
        in_specs.append(pl.BlockSpec((1, D_MODEL), lambda i, j: (0, 0)))
        args.append(final_norm)
    return pl.pallas_call(
        functools.partial(_mlp_kernel, final=final),
        out_shape=jax.ShapeDtypeStruct((n_tiles * tm, D_MODEL), F32),
        grid=(n_tiles, MLP_STEPS),
        in_specs=in_specs,
        out_specs=pl.BlockSpec((tm, D_MODEL), lambda i, j: (i, 0)),
        scratch_shapes=[pltpu.VMEM((MLP_STEPS, tm, MLP_RES_COLS), F32)],
        compiler_params=_params(("arbitrary", "arbitrary")),
        name="mlp_final" if final else "mlp",
    )(*args)


def _pool_kernel(x_ref, wp_ref, ps_ref, gain1_ref, sh1_ref, sc1_ref, g1_ref, gain2_ref, sh2_ref, sc2_ref,
                 x3_ref, h4_ref):
    i = pl.program_id(0)
    tm = TOK_TILE
    rstd = lax.rsqrt(jnp.mean(x_ref[...] * x_ref[...], axis=-1, keepdims=True) + EPS)
    gs1 = gain1_ref[...] * (1.0 + sc1_ref[...])
    mix_gain = g1_ref[...] * ps_ref[...]

    def widen(m):
        return jnp.concatenate([m] * (POOL_GROUP // HEAD_DIM), axis=1)

    def mix_tile(seq):
        n_seq = tm // seq
        pitch = seq + 2 * POOL_HALO
        n_pad = n_seq * pitch
        halo = jnp.zeros((POOL_HALO, POOL_GROUP), F32)
        pos = lax.broadcasted_iota(jnp.int32, (seq, HEAD_DIM), 0)

        def down(a, k):
            return pltpu.roll(a, k % n_pad, 0)

        for g, w in enumerate(POOL_WINDOWS):
            half = w // 2
            cols = slice(g * POOL_GROUP, (g + 1) * POOL_GROUP)
            x = x_ref[:, cols]
            h = x * rstd * gs1[:, cols] + sh1_ref[:, cols]
            padded = jnp.concatenate(
                [piece for s in range(n_seq) for piece in (halo, h[s * seq:(s + 1) * seq], halo)], axis=0)
            back = padded
            m = 1
            while m < half:
                back = back + down(back, m)
                m *= 2
            ahead = back if half == 1 else down(back, -(half - 1))
            total = down(back, 1) + ahead
            total = jnp.concatenate(
                [total[s * pitch + POOL_HALO:s * pitch + POOL_HALO + seq] for s in range(n_seq)], axis=0)
            count = (jnp.minimum(pos + (w - half), seq) - jnp.maximum(pos - half, 0)).astype(F32)
            inv = jnp.concatenate([widen(1.0 / count)] * n_seq, axis=0)
            pooled = (total * inv - h).astype(BF16)
            mix = _bdot(pooled, wp_ref[g].astype(BF16))
            x3_ref[:, cols] = x + mix_gain[:, cols] * mix

    pl.when(i < N_PROMPT_TILES)(lambda: mix_tile(SEQ))
    pl.when(i >= N_PROMPT_TILES)(lambda: mix_tile(DEC_SEQ))
    x3 = x3_ref[...]
    h4_ref[...] = _modulate(x3, gain2_ref[...], sh2_ref[...], sc2_ref[...]).astype(BF16)


def _pool_mixer(x, w_pool, pool_scale, norm_mix, norm_mlp, mods):
    tm = TOK_TILE
    vec = pl.BlockSpec((1, D_MODEL), lambda i: (0, 0))
    tile = pl.BlockSpec((tm, D_MODEL), lambda i: (i, 0))
    return pl.pallas_call(
        _pool_kernel,
        out_shape=(jax.ShapeDtypeStruct((N_TOK, D_MODEL), F32), jax.ShapeDtypeStruct((N_TOK, D_MODEL), BF16)),
        grid=(N_TILES,),
        in_specs=[tile, pl.BlockSpec((len(POOL_WINDOWS), POOL_GROUP, POOL_GROUP), lambda i: (0, 0, 0)),
                  vec, vec, _mod_spec(1, 0, tm), _mod_spec(1, 1, tm), _mod_spec(1, 2, tm),
                  vec, _mod_spec(1, 3, tm), _mod_spec(1, 4, tm)],
        out_specs=(tile, tile),
        compiler_params=_params(("arbitrary",)),
        name="pool_mixer",
    )(x, w_pool, pool_scale, norm_mix, mods, mods, mods, norm_mlp, mods, mods)


def kernel(x_prompt, x_sample, cache_k, cache_v, state_hgrn_fwd, state_hgrn_bwd, c, c_ctx, w_ada, b_ada,
           norm_mix, norm_mlp, w_in_ab, w_out_ab, q_norm, k_norm, hg_norm, lb_raw, w_pool, pool_scale,
           w_mlp_in, w_mlp_out, final_norm):
    xp = x_prompt.reshape(N_PROMPT, D_MODEL)
    xs = x_sample.reshape(N_SAMPLE, D_MODEL)
    cv = jnp.concatenate([c_ctx[None, :], c, jnp.zeros((ADA_ROWS - 1 - DEC_BATCH, D_MODEL), F32)], axis=0)
    mods = _ada_table(cv, w_ada, b_ada).reshape(DEPTH, ADA_ROWS, 1, N_MOD * D_MODEL)

    h0 = _modulate0(xp, xs, norm_mix, mods)
    proj, new_k, new_v = _in_projection(h0, w_in_ab[0], q_norm[0:1], k_norm[0:1])
    ctx_k = cache_k[:, 0].reshape(DEC_BATCH, PAST_LEN, KV_WIDTH)
    ctx_v = cache_v[:, 0].reshape(DEC_BATCH, PAST_LEN, KV_WIDTH)
    att_p = _attention(proj, BATCH, SEQ, 0, SEQ)
    att_s = _attention(proj, DEC_BATCH, DEC_SEQ, N_PROMPT, 256, ctx=(ctx_k, ctx_v))
    consts = _hgrn_constants()
    hg_p, s_fwd, s_bwd = _hgrn(proj, lb_raw, hg_norm[0:1], consts, BATCH, SEQ, 0, HG_PROMPT_SEQS)
    s0 = (state_hgrn_fwd.reshape(DEC_BATCH, HG_HEADS, HG_DK, HG_DV),
          state_hgrn_bwd.reshape(DEC_BATCH, HG_HEADS, HG_DK, HG_DV))
    (hg_s,) = _hgrn(proj, lb_raw, hg_norm[0:1], consts, DEC_BATCH, DEC_SEQ, N_PROMPT, 1, s0=s0)
    x1, h2 = _out_projection(att_p, att_s, hg_p, hg_s, w_out_ab[0], xp, xs, mods, norm_mlp[0:1])
    x2 = _mlp(h2, x1, w_mlp_in, w_mlp_out, mods, 0, 0, N_TILES)

    x3, h4 = _pool_mixer(x2, w_pool[0], pool_scale[0:1], norm_mix[1:2], norm_mlp[1:2], mods)
    fin = final_norm[None, :]
    y_prompt = _mlp(h4, x3, w_mlp_in, w_mlp_out, mods, 1, 0, N_PROMPT_TILES, final_norm=fin)
    y_sample = _mlp(h4, x3, w_mlp_in, w_mlp_out, mods, 1, N_PROMPT_TILES, N_TILES - N_PROMPT_TILES,
                    final_norm=fin)

    return (y_prompt.reshape(BATCH, SEQ, D_MODEL), y_sample.reshape(DEC_BATCH, DEC_SEQ, D_MODEL),
            new_k.reshape(BATCH, 1, SEQ, N_KV_HEADS, HEAD_DIM), new_v.reshape(BATCH, 1, SEQ, N_KV_HEADS, HEAD_DIM),
            s_fwd.reshape(BATCH, 1, HG_HEADS, HG_DK, HG_DV), s_bwd.reshape(BATCH, 1, HG_HEADS, HG_DK, HG_DV))
```

```python
import functools

import numpy as np
import jax
import jax.numpy as jnp
from jax import lax
from jax.experimental import pallas as pl
from jax.experimental.pallas import tpu as pltpu

F32 = jnp.float32
BF16 = jnp.bfloat16

D_MODEL = 2048
BATCH = 16
SEQ = 256
DEPTH = 2
DEC_BATCH = 2
DEC_SEQ = 1024
PAST_LEN = 256
GRID_W = 64
HEAD_DIM = 128
N_Q_HEADS = 8
N_KV_HEADS = 2
Q_PER_KV = N_Q_HEADS // N_KV_HEADS
ATT_WIDTH = N_Q_HEADS * HEAD_DIM
KV_WIDTH = N_KV_HEADS * HEAD_DIM
HG_HEADS = 8
HG_DK = 128
HG_DV = 128
HG_KW = HG_HEADS * HG_DK
HG_VW = HG_HEADS * HG_DV
IN_AB = ATT_WIDTH + 2 * KV_WIDTH + 3 * HG_KW + 2 * HG_VW
MIX_WIDTH = ATT_WIDTH + HG_VW
POOL_WINDOWS = (2, 4, 8, 16)
POOL_GROUP = D_MODEL // len(POOL_WINDOWS)
POOL_HALO = 8
D_FF = 4 * D_MODEL
ROPE_THETA = 10000.0
ROPE_HALF = HEAD_DIM // 2
EPS = 1e-6
N_MOD = 6

N_PROMPT = BATCH * SEQ
N_SAMPLE = DEC_BATCH * DEC_SEQ
N_TOK = N_PROMPT + N_SAMPLE
ADA_ROWS = 16
ADA_COL_TILE = 2048
TOK_TILE = 1024
N_TILES = N_TOK // TOK_TILE
N_PROMPT_TILES = N_PROMPT // TOK_TILE
PROJ_TOK_TILE = 2048
PROJ_COL_TILE = 512
OUT_COL_TILE = 512
FF_TILE = 512
MLP_STEPS = D_FF // FF_TILE
MLP_RES_COLS = D_MODEL // MLP_STEPS
MLP_OUT_CHUNK = 512
HG_CHUNK = 128
HG_LEVELS = 7
HG_HEAD_BLOCK = 4
HG_PROMPT_SEQS = 2
HG_SAFE_EXPONENT = 80.0
VMEM_LIMIT = 56 * 2 ** 20

COL_Q = 0
COL_K = ATT_WIDTH
COL_V = COL_K + KV_WIDTH
COL_HQ = COL_V + KV_WIDTH
COL_ZF = COL_HQ + HG_KW
COL_HI = COL_ZF + 2 * HG_KW
COL_HG = COL_HI + HG_VW


def _params(semantics):
    return pltpu.CompilerParams(dimension_semantics=semantics, vmem_limit_bytes=VMEM_LIMIT)


def _sigmoid(x):
    return 1.0 / (1.0 + jnp.exp(-x))


def _silu(x):
    return x * _sigmoid(x)


def _rms(x, gain):
    return x * lax.rsqrt(jnp.mean(x * x, axis=-1, keepdims=True) + EPS) * gain


def _bdot(a, b):
    return jnp.dot(a, b, preferred_element_type=F32)


def _mod_row(tile, tile_rows):
    first = N_PROMPT // tile_rows
    per_seq = DEC_SEQ // tile_rows
    return jnp.where(tile < first, 0, 1 + (tile - first) // per_seq)


def _mod_spec(layer, chunk, tile_rows, width=D_MODEL, col=lambda *g: 0, tile_of=lambda *g: g[0]):
    per = D_MODEL // width
    return pl.BlockSpec((None, None, 1, width),
                        lambda *g: (layer, _mod_row(tile_of(*g), tile_rows), 0, chunk * per + col(*g)))


def _ada_kernel(cv_ref, w_ref, b_ref, o_ref):
    s = _silu(cv_ref[...]).astype(BF16)
    o_ref[...] = _bdot(s, w_ref[...].astype(BF16)) + b_ref[...]


def _ada_table(cv, w_ada, b_ada):
    tn = ADA_COL_TILE
    n = N_MOD * D_MODEL
    return pl.pallas_call(
        _ada_kernel,
        out_shape=jax.ShapeDtypeStruct((DEPTH, ADA_ROWS, n), F32),
        grid=(DEPTH, n // tn),
        in_specs=[pl.BlockSpec((ADA_ROWS, D_MODEL), lambda l, j: (0, 0)),
                  pl.BlockSpec((None, D_MODEL, tn), lambda l, j: (l, 0, j)),
                  pl.BlockSpec((None, 1, tn), lambda l, j: (l, 0, j))],
        out_specs=pl.BlockSpec((None, ADA_ROWS, tn), lambda l, j: (l, 0, j)),
        compiler_params=_params(("arbitrary", "arbitrary")),
        name="ada_table",
    )(cv, w_ada, b_ada.reshape(DEPTH, 1, n))


def _modulate(x, gain, shift, scale):
    return _rms(x, gain) * (1.0 + scale) + shift


def _mod0_kernel(xp_ref, xs_ref, gain_ref, sh_ref, sc_ref, o_ref):
    i = pl.program_id(0)

    def run(x_ref):
        o_ref[...] = _modulate(x_ref[...], gain_ref[...], sh_ref[...], sc_ref[...]).astype(BF16)

    pl.when(i < N_PROMPT_TILES)(lambda: run(xp_ref))
    pl.when(i >= N_PROMPT_TILES)(lambda: run(xs_ref))


def _modulate0(xp, xs, norm_mix, mods):
    tm = TOK_TILE
    first = N_PROMPT_TILES
    return pl.pallas_call(
        _mod0_kernel,
        out_shape=jax.ShapeDtypeStruct((N_TOK, D_MODEL), BF16),
        grid=(N_TOK // tm,),
        in_specs=[pl.BlockSpec((tm, D_MODEL), lambda i: (jnp.minimum(i, first - 1), 0)),
                  pl.BlockSpec((tm, D_MODEL), lambda i: (jnp.maximum(i - first, 0), 0)),
                  pl.BlockSpec((1, D_MODEL), lambda i: (0, 0)),
                  _mod_spec(0, 0, tm), _mod_spec(0, 1, tm)],
        out_specs=pl.BlockSpec((tm, D_MODEL), lambda i: (i, 0)),
        compiler_params=_params(("arbitrary",)),
        name="modulate0",
    )(xp, xs, norm_mix[0:1], mods, mods)


def _rope(y, cos, sin, perm2):
    hi = y.astype(BF16)
    lo = (y - hi.astype(F32)).astype(BF16)
    rot = _bdot(jnp.concatenate([hi, lo], axis=1), perm2)
    return y * cos + rot * sin


def _inproj_kernel(h_ref, w_ref, qg_ref, kg_ref, cos_ref, sin_ref, perm_ref, p_ref, nk_ref, nv_ref):
    i = pl.program_id(0)
    j = pl.program_id(1)
    latent = i >= N_PROMPT // PROJ_TOK_TILE
    heads = PROJ_COL_TILE // HEAD_DIM
    kv_tile = COL_K // PROJ_COL_TILE
    is_q = j < kv_tile

    def attention_tile(rope):
        gain = jnp.where(is_q, qg_ref[...], kg_ref[...])
        w = w_ref[...].astype(BF16)
        for s in range(PROJ_TOK_TILE // DEC_SEQ):
            rows = slice(s * DEC_SEQ, (s + 1) * DEC_SEQ)
            acc = _bdot(h_ref[rows, :], w)
            for hh in range(heads):
                cols = slice(hh * HEAD_DIM, (hh + 1) * HEAD_DIM)
                x = acc[:, cols]
                y = _rms(x, gain)
                if rope:
                    y = _rope(y, cos_ref[...], sin_ref[...], perm_ref[...])
                if hh >= N_KV_HEADS:
                    y = jnp.where(is_q, y, x)
                p_ref[rows, cols] = y

    pl.when(jnp.logical_and(j <= kv_tile, latent))(lambda: attention_tile(True))
    pl.when(jnp.logical_and(j <= kv_tile, jnp.logical_not(latent)))(lambda: attention_tile(False))

    @pl.when(jnp.logical_and(j == kv_tile, jnp.logical_not(latent)))
    def _():
        nk_ref[...] = p_ref[:, :KV_WIDTH]
        nv_ref[...] = p_ref[:, KV_WIDTH:]

    def plain_tile(act):
        w = w_ref[...].astype(BF16)
        for s in range(PROJ_TOK_TILE // DEC_SEQ):
            rows = slice(s * DEC_SEQ, (s + 1) * DEC_SEQ)
            p_ref[rows, :] = act(_bdot(h_ref[rows, :], w))

    is_gate = jnp.logical_and(j >= COL_ZF // PROJ_COL_TILE, j < COL_HI // PROJ_COL_TILE)
    is_value = jnp.logical_and(j >= COL_HI // PROJ_COL_TILE, j < COL_HG // PROJ_COL_TILE)
    raw = jnp.logical_or(is_gate, is_value)
    pl.when(raw)(lambda: plain_tile(lambda a: a))
    pl.when(jnp.logical_and(j > kv_tile, jnp.logical_not(raw)))(lambda: plain_tile(_silu))


def _rope_tables():
    t = np.arange(DEC_SEQ)
    row = (t // GRID_W).astype(np.float32)
    col = (t % GRID_W).astype(np.float32)
    inv = (np.float32(ROPE_THETA) ** (-np.arange(0, ROPE_HALF, 2, dtype=np.float32) / np.float32(ROPE_HALF))).astype(np.float32)
    ar = row[:, None] * inv
    ac = col[:, None] * inv
    ang = np.concatenate([ar, ar, ac, ac], axis=-1).astype(np.float32)
    cos = np.cos(ang).astype(np.float32)
    sin = np.sin(ang).astype(np.float32)
    qw = ROPE_HALF // 2
    perm = np.zeros((HEAD_DIM, HEAD_DIM), np.float32)
    for k in range(qw):
        perm[qw + k, k] = -1.0
        perm[k, qw + k] = 1.0
        perm[3 * qw + k, 2 * qw + k] = -1.0
        perm[2 * qw + k, 3 * qw + k] = 1.0
    return jnp.asarray(cos), jnp.asarray(sin), jnp.asarray(np.concatenate([perm, perm], axis=0), BF16)


def _in_projection(h, w_in, q_gain, k_gain):
    tm, tn = PROJ_TOK_TILE, PROJ_COL_TILE
    n_prompt_tiles = N_PROMPT // tm
    cos, sin, perm2 = _rope_tables()
    table = pl.BlockSpec((DEC_SEQ, HEAD_DIM), lambda i, j: (0, 0))
    gain = pl.BlockSpec((1, HEAD_DIM), lambda i, j: (0, 0))
    state = pl.BlockSpec((tm, KV_WIDTH), lambda i, j: (jnp.minimum(i, n_prompt_tiles - 1), 0))
    return pl.pallas_call(
        _inproj_kernel,
        out_shape=(jax.ShapeDtypeStruct((N_TOK, IN_AB), F32),
                   jax.ShapeDtypeStruct((N_PROMPT, KV_WIDTH), F32),
                   jax.ShapeDtypeStruct((N_PROMPT, KV_WIDTH), F32)),
        grid=(N_TOK // tm, IN_AB // tn),
        in_specs=[pl.BlockSpec((tm, D_MODEL), lambda i, j: (i, 0)),
                  pl.BlockSpec((D_MODEL, tn), lambda i, j: (0, j)),
                  gain, gain, table, table, pl.BlockSpec((2 * HEAD_DIM, HEAD_DIM), lambda i, j: (0, 0))],
        out_specs=(pl.BlockSpec((tm, tn), lambda i, j: (i, j)), state, state),
        compiler_params=_params(("arbitrary", "arbitrary")),
        name="in_projection",
    )(h, w_in, q_gain, k_gain, cos, sin, perm2)


def _attn_kernel(*refs, has_ctx, stack):
    if has_ctx:
        q_ref, k_ref, v_ref, ck_ref, cv_ref, o_ref = refs
    else:
        q_ref, k_ref, v_ref, o_ref = refs
    scale = HEAD_DIM ** -0.5
    nt = (((1,), (1,)), ((), ()))
    tq = q_ref.shape[0]
    for hk in range(k_ref.shape[1] // HEAD_DIM):
        kcols = slice(hk * HEAD_DIM, (hk + 1) * HEAD_DIM)
        k = k_ref[:, kcols].astype(BF16)
        v = v_ref[:, kcols].astype(BF16)
        if has_ctx:
            ck = ck_ref[:, kcols].astype(BF16)
            cv = cv_ref[:, kcols].astype(BF16)
        for g0 in range(hk * Q_PER_KV, (hk + 1) * Q_PER_KV, stack):
            q = jnp.concatenate([q_ref[:, g * HEAD_DIM:(g + 1) * HEAD_DIM] for g in range(g0, g0 + stack)], axis=0)
            q = (q * scale).astype(BF16)
            s = lax.dot_general(q, k, nt, preferred_element_type=F32)
            m = jnp.max(s, axis=-1, keepdims=True)
            if has_ctx:
                sc = lax.dot_general(q, ck, nt, preferred_element_type=F32)
                m = jnp.maximum(m, jnp.max(sc, axis=-1, keepdims=True))
            p = jnp.exp(s - m)
            den = jnp.sum(p, axis=-1, keepdims=True)
            o = _bdot(p.astype(BF16), v)
            if has_ctx:
                pc = jnp.exp(sc - m)
                den = den + jnp.sum(pc, axis=-1, keepdims=True)
                o = o + _bdot(pc.astype(BF16), cv)
            o = (o / den).astype(o_ref.dtype)
            for g in range(stack):
                o_ref[:, (g0 + g) * HEAD_DIM:(g0 + g + 1) * HEAD_DIM] = o[g * tq:(g + 1) * tq]


def _attention(p, n_batch, seq, row0, tq, ctx=None):
    q_blocks = seq // tq
    kvh = N_KV_HEADS if ctx is None else 1
    gw = kvh * Q_PER_KV * HEAD_DIM
    kw = kvh * HEAD_DIM
    in_specs = [
        pl.BlockSpec((tq, gw), lambda b, h, qi: (row0 // tq + b * q_blocks + qi, h)),
        pl.BlockSpec((seq, kw), lambda b, h, qi: (row0 // seq + b, COL_K // kw + h)),
        pl.BlockSpec((seq, kw), lambda b, h, qi: (row0 // seq + b, COL_V // kw + h)),
    ]
    args = [p, p, p]
    if ctx is not None:
        ctx_spec = pl.BlockSpec((None, PAST_LEN, kw), lambda b, h, qi: (b, 0, h))
        in_specs += [ctx_spec, ctx_spec]
        args += [ctx[0], ctx[1]]
    return pl.pallas_call(
        functools.partial(_attn_kernel, has_ctx=ctx is not None, stack=Q_PER_KV if ctx is None else 1),
        out_shape=jax.ShapeDtypeStruct((n_batch * seq, ATT_WIDTH), BF16),
        grid=(n_batch, N_KV_HEADS // kvh, q_blocks),
        in_specs=in_specs,
        out_specs=pl.BlockSpec((tq, gw), lambda b, h, qi: (b * q_blocks + qi, h)),
        compiler_params=_params(("arbitrary", "arbitrary", "arbitrary")),
        name="attention_latent" if ctx is not None else "attention_prompt",
    )(*args)


def _hgrn_constants():
    c = HG_CHUNK
    t = np.arange(c)
    cums, sels, pairs, scans, diags = [], [], [], [], []
    for d in range(2):
        pos = t if d == 0 else c - 1 - t
        pu, pt = pos[None, :], pos[:, None]
        ms, ss, ws = [], [], []
        for l in range(HG_LEVELS):
            m = c >> l
            blk = pos // m
            mid = (blk * m + m // 2)[:, None]
            late = ((pos % m) >= m // 2)
            ms.append(np.where(late[:, None], (pu >= mid) & (pu <= pt), (pu > pt) & (pu < mid)))
            ss.append(np.broadcast_to(late[:, None], (c, c)))
            ws.append((blk[:, None] == blk[None, :]) & late[:, None] & ~late[None, :])
        ms.append(pu <= pt)
        ms.append(pu > pt)
        cums.append(np.concatenate(ms, axis=0))
        sels.append(np.stack(ss))
        pairs.append(np.stack(ws))
        scans.append(np.concatenate([pu <= pt, pu <= pt], axis=1))
        diags.append(((pos // (c // 2))[:, None] == (pos // (c // 2))[None, :]) & (pu <= pt))
    return (jnp.asarray(np.stack(cums), BF16), jnp.asarray(np.stack(sels), F32),
            jnp.asarray(np.stack(pairs), F32), jnp.asarray(np.stack(scans), BF16),
            jnp.asarray(np.stack(diags), F32))


def _hgrn_kernel(*refs, n_seqs, n_chunks, has_s0):
    (hq_ref, zf_ref, zb_ref, hi_ref, hg_ref, lb_ref, og_ref, cum_ref, sel_ref, pair_ref, scan_ref,
     diag_ref) = refs[:12]
    refs = refs[12:]
    if has_s0:
        s0f_ref, s0b_ref, o_ref = refs[:3]
        refs = refs[3:]
    else:
        o_ref, sf_ref, sb_ref = refs[:3]
        refs = refs[3:]
    st_ref, acc_ref = refs
    hb = st_ref.shape[0]
    c = HG_CHUNK
    half = c // 2
    nt = (((1,), (1,)), ((), ()))
    tn = (((0,), (0,)), ((), ()))

    def direction(d, z_ref):
        raw = lb_ref[d]
        e = jnp.exp(raw - jnp.max(raw, axis=0, keepdims=True))
        lb = e[0:1] / jnp.sum(e, axis=0, keepdims=True)

        def initial_state(s, hh):
            if has_s0:
                return (s0f_ref, s0b_ref)[d][s, hh].T
            return jnp.zeros((HG_DV, HG_DK), F32)

        def gates(rows):
            f = lb + (1.0 - lb) * _sigmoid(z_ref[rows, :])
            logf = jnp.log(f)
            hi16 = logf.astype(BF16)
            lo16 = (logf - hi16.astype(F32)).astype(BF16)
            return f, hi16, lo16

        def row(p):
            t = p if d == 0 else c - 1 - p
            return slice(t, t + 1)

        early, late = (slice(0, half), slice(half, c)) if d == 0 else (slice(half, c), slice(0, half))

        def in_row_order(x_early, x_late):
            return jnp.concatenate([x_early, x_late] if d == 0 else [x_late, x_early], axis=0)

        def emit(rows, cols, o):
            if d == 0:
                acc_ref[rows, cols] = o
            else:
                tot = acc_ref[rows, cols] + o
                o_ref[rows, cols] = (_rms(tot, og_ref[...]) * hg_ref[rows, cols]).astype(o_ref.dtype)

        def two_level_operands(rows):
            f, hi16, lo16 = gates(rows)
            b = _bdot(scan_ref[d], jnp.concatenate([hi16, lo16], axis=0))
            kk = 1.0 - f
            q = hq_ref[rows, :]
            r_mid = b[row(half - 1)]
            x1 = in_row_order(kk[early] * jnp.exp(r_mid - b[early]), q[late] * jnp.exp(b[late] - r_mid)).astype(BF16)
            dq = in_row_order(b[early] - b[row(half // 2 - 1)], b[late] - b[row(half + half // 2 - 1)])
            b_end = b[row(c - 1)]
            span = jnp.maximum(
                jnp.maximum(b[row(0)] - b[row(half // 2 - 1)], b[row(half // 2 - 1)] - b[row(half - 1)]),
                jnp.maximum(b[row(half)] - b[row(half + half // 2 - 1)],
                            b[row(half + half // 2 - 1)] - b[row(c - 1)]))
            return dict(x1=x1, xq=(q * jnp.exp(dq)).astype(BF16), xk=(kk * jnp.exp(-dq)).astype(BF16),
                        q_in=(q * jnp.exp(b)).astype(BF16), k_out=(kk * jnp.exp(b_end - b)).astype(BF16),
                        a_end=jnp.exp(b_end), iv=hi_ref[rows, :].astype(BF16), span=span)

        def two_level_chunk(rows, ops, states, need_state):
            new_states = []
            for hh in range(hb):
                cols = slice(hh * HG_DK, (hh + 1) * HG_DK)
                g1 = _bdot(ops["x1"][:, cols], ops["x1"][:, cols].astype(F32).T.astype(BF16))
                g2 = _bdot(ops["xq"][:, cols], ops["xk"][:, cols].astype(F32).T.astype(BF16))
                att = jnp.where(pair_ref[d, 0] > 0.5, g1, jnp.where(diag_ref[d] > 0.5, g2, 0.0)).astype(BF16)
                st = states[hh]
                o = _bdot(att, ops["iv"][:, cols])
                if st is not None:
                    o = o + lax.dot_general(ops["q_in"][:, cols], st.astype(BF16), nt, preferred_element_type=F32)
                emit(rows, cols, o)
                if not need_state:
                    new_states.append(None)
                    continue
                dst = lax.dot_general(ops["iv"][:, cols], ops["k_out"][:, cols], tn, preferred_element_type=F32)
                new_states.append(dst if st is None else ops["a_end"][:, cols] * st + dst)
            return new_states

        def all_levels(rows):
            f, hi16, lo16 = gates(rows)
            cum = cum_ref[d]
            eall = jnp.exp(_bdot(cum, hi16) + _bdot(cum, lo16))
            for hh in range(hb):
                cols = slice(hh * HG_DK, (hh + 1) * HG_DK)
                kk = 1.0 - f[:, cols]
                q = hq_ref[rows, cols]
                iv = hi_ref[rows, cols]
                iv16 = iv.astype(BF16)
                att = jnp.zeros((c, c), F32)
                for l in range(HG_LEVELS):
                    x = (kk + sel_ref[d, l] * (q - kk)) * eall[l * c:(l + 1) * c, cols]
                    xb = x.astype(BF16)
                    att = att + pair_ref[d, l] * lax.dot_general(xb, xb, nt, preferred_element_type=F32)
                e_in = eall[HG_LEVELS * c:(HG_LEVELS + 1) * c, cols]
                e_out = eall[(HG_LEVELS + 1) * c:, cols]
                st = st_ref[hh]
                o = (_bdot(att.astype(BF16), iv16)
                     + jnp.sum(q * kk, axis=-1, keepdims=True) * iv
                     + lax.dot_general((q * e_in).astype(BF16), st.astype(BF16), nt, preferred_element_type=F32))
                dst = lax.dot_general(iv16, (kk * e_out).astype(BF16), tn, preferred_element_type=F32)
                st_ref[hh] = (e_in[0:1] * e_out[0:1]) * st + dst
                emit(rows, cols, o)

        def chunk_start(s, ci):
            cidx = ci if d == 0 else n_chunks - 1 - ci
            return (s * n_chunks + cidx) * c

        worst = jnp.zeros((1, hb * HG_DK), F32)
        states = [[initial_state(s, hh) if has_s0 else None for hh in range(hb)] for s in range(n_seqs)]
        for ci in range(n_chunks):
            for s in range(n_seqs):
                rows = slice(chunk_start(s, ci), chunk_start(s, ci) + c)
                ops = two_level_operands(rows)
                states[s] = two_level_chunk(rows, ops, states[s], need_state=ci < n_chunks - 1 or not has_s0)
                worst = jnp.maximum(worst, ops["span"])
        if not has_s0:
            for s in range(n_seqs):
                for hh in range(hb):
                    (sf_ref, sb_ref)[d][s, hh] = states[s][hh].T

        @pl.when(jnp.logical_not(jnp.max(worst) <= HG_SAFE_EXPONENT))
        def _():
            for s in range(n_seqs):
                for hh in range(hb):
                    st_ref[hh] = initial_state(s, hh)

                def chunk(ci, carry):
                    all_levels(pl.ds(pl.multiple_of(chunk_start(s, ci), c), c))
                    return carry

                lax.fori_loop(0, n_chunks, chunk, 0)
                if not has_s0:
                    for hh in range(hb):
                        (sf_ref, sb_ref)[d][s, hh] = st_ref[hh].T

    direction(0, zf_ref)
    direction(1, zb_ref)


def _hgrn(p, lb_raw, o_gain, consts, n_batch, seq, row0, n_seqs, s0=None):
    hb = HG_HEAD_BLOCK
    w = hb * HG_DK
    rows = n_seqs * seq

    def seg(col):
        return pl.BlockSpec((rows, w), lambda b, h: (row0 // rows + b, col // w + h))

    def const(a):
        return pl.BlockSpec(a.shape, lambda b, h: (0,) * a.ndim)

    in_specs = [seg(COL_HQ), seg(COL_ZF), seg(COL_ZF + HG_KW), seg(COL_HI), seg(COL_HG),
                pl.BlockSpec((2, DEPTH + 1, w), lambda b, h: (0, 0, h)),
                pl.BlockSpec((1, HG_DV), lambda b, h: (0, 0))] + [const(a) for a in consts]
    args = [p, p, p, p, p, lb_raw, o_gain, *consts]
    has_s0 = s0 is not None
    st_spec = pl.BlockSpec((n_seqs, hb, HG_DK, HG_DV), lambda b, h: (b, h, 0, 0))
    if has_s0:
        in_specs += [st_spec, st_spec]
        args += [s0[0], s0[1]]
    out_shape = [jax.ShapeDtypeStruct((n_batch * seq, HG_VW), BF16)]
    out_specs = [pl.BlockSpec((rows, w), lambda b, h: (b, h))]
    if not has_s0:
        st_shape = jax.ShapeDtypeStruct((n_batch, HG_HEADS, HG_DK, HG_DV), F32)
        out_shape += [st_shape, st_shape]
        out_specs += [st_spec, st_spec]
    return pl.pallas_call(
        functools.partial(_hgrn_kernel, n_seqs=n_seqs, n_chunks=seq // HG_CHUNK, has_s0=has_s0),
        out_shape=tuple(out_shape),
        grid=(n_batch // n_seqs, HG_HEADS // hb),
        in_specs=in_specs,
        out_specs=tuple(out_specs),
        scratch_shapes=[pltpu.VMEM((hb, HG_DV, HG_DK), F32), pltpu.VMEM((rows, w), F32)],
        compiler_params=_params(("arbitrary", "arbitrary")),
        name="hgrn_latent" if has_s0 else "hgrn_prompt",
    )(*args)


def _outproj_kernel(attp_ref, atts_ref, hgp_ref, hgs_ref, wa_ref, wb_ref, xp_ref, xs_ref, g1_ref, gain_ref,
                    sh_ref, sc_ref, x1_ref, h2_ref, full_ref, wcache_ref):
    i = pl.program_id(0)
    n = pl.program_id(1)

    @pl.when(i == 0)
    def _():
        wcache_ref[n, 0] = wa_ref[...].astype(BF16)
        wcache_ref[n, 1] = wb_ref[...].astype(BF16)

    def run(att_ref, hg_ref, x_ref):
        acc = _bdot(att_ref[...], wcache_ref[n, 0]) + _bdot(hg_ref[...], wcache_ref[n, 1])
        x1 = x_ref[...] + g1_ref[...] * acc
        x1_ref[...] = x1
        full_ref[n] = x1

    pl.when(i < N_PROMPT_TILES)(lambda: run(attp_ref, hgp_ref, xp_ref))
    pl.when(i >= N_PROMPT_TILES)(lambda: run(atts_ref, hgs_ref, xs_ref))

    @pl.when(n == D_MODEL // OUT_COL_TILE - 1)
    def _():
        nt = D_MODEL // OUT_COL_TILE
        ms = jnp.sum(full_ref[0] * full_ref[0], axis=-1, keepdims=True)
        for k in range(1, nt):
            ms = ms + jnp.sum(full_ref[k] * full_ref[k], axis=-1, keepdims=True)
        rstd = lax.rsqrt(ms / D_MODEL + EPS)
        gs = gain_ref[...] * (1.0 + sc_ref[...])
        for k in range(nt):
            cols = slice(k * OUT_COL_TILE, (k + 1) * OUT_COL_TILE)
            h2_ref[:, cols] = (full_ref[k] * rstd * gs[:, cols] + sh_ref[:, cols]).astype(BF16)


def _out_projection(att_p, att_s, hg_p, hg_s, w_out, xp, xs, mods, norm_mlp):
    tm, tn = TOK_TILE, OUT_COL_TILE
    nt = D_MODEL // tn

    def prompt_rows(width):
        return pl.BlockSpec((tm, width), lambda i, n: (jnp.minimum(i, N_PROMPT_TILES - 1), 0),
                            pipeline_mode=pl.Buffered(1))

    def sample_rows(width):
        return pl.BlockSpec((tm, width), lambda i, n: (jnp.maximum(i - N_PROMPT_TILES, 0), 0),
                            pipeline_mode=pl.Buffered(1))

    return pl.pallas_call(
        _outproj_kernel,
        out_shape=(jax.ShapeDtypeStruct((N_TOK, D_MODEL), F32), jax.ShapeDtypeStruct((N_TOK, D_MODEL), BF16)),
        grid=(N_TILES, nt),
        in_specs=[prompt_rows(ATT_WIDTH), sample_rows(ATT_WIDTH), prompt_rows(HG_VW), sample_rows(HG_VW),
                  pl.BlockSpec((ATT_WIDTH, tn), lambda i, n: (0, jnp.where(i == 0, n, nt - 1))),
                  pl.BlockSpec((HG_VW, tn), lambda i, n: (1, jnp.where(i == 0, n, nt - 1))),
                  pl.BlockSpec((tm, tn), lambda i, n: (jnp.minimum(i, N_PROMPT_TILES - 1),
                                                       jnp.where(i < N_PROMPT_TILES, n, nt - 1))),
                  pl.BlockSpec((tm, tn), lambda i, n: (jnp.maximum(i - N_PROMPT_TILES, 0),
                                                       jnp.where(i < N_PROMPT_TILES, 0, n))),
                  _mod_spec(0, 2, tm, width=tn, col=lambda i, n: n),
                  pl.BlockSpec((1, D_MODEL), lambda i, n: (0, 0)),
                  _mod_spec(0, 3, tm), _mod_spec(0, 4, tm)],
        out_specs=(pl.BlockSpec((tm, tn), lambda i, n: (i, n)),
                   pl.BlockSpec((tm, D_MODEL), lambda i, n: (i, 0))),
        scratch_shapes=[pltpu.VMEM((nt, tm, tn), F32), pltpu.VMEM((nt, 2, ATT_WIDTH, tn), BF16)],
        compiler_params=_params(("arbitrary", "arbitrary")),
        name="out_projection",
    )(att_p, att_s, hg_p, hg_s, w_out, w_out, xp, xs, mods, norm_mlp, mods, mods)


def _mlp_kernel(h_ref, w1_ref, w2_ref, x_ref, g2_ref, *rest, final):
    if final:
        fin_ref, o_ref, res_ref = rest
    else:
        o_ref, res_ref = rest
    j = pl.program_id(1)
    res_ref[j] = x_ref[...]

    def step(first, last):
        a = jnp.square(jnp.maximum(_bdot(h_ref[...], w1_ref[...].astype(BF16)), 0.0)).astype(BF16)
        for n in range(D_MODEL // MLP_OUT_CHUNK):
            cols = slice(n * MLP_OUT_CHUNK, (n + 1) * MLP_OUT_CHUNK)
            p = _bdot(a, w2_ref[:, cols].astype(BF16))
            if not first:
                p = o_ref[:, cols] + p
            if last:
                per = MLP_OUT_CHUNK // MLP_RES_COLS
                res = jnp.concatenate([res_ref[n * per + k] for k in range(per)], axis=1)
                p = res + g2_ref[:, cols] * p
            o_ref[:, cols] = p
        if last and final:
            o_ref[...] = _rms(o_ref[...], fin_ref[...])

    pl.when(j == 0)(lambda: step(True, False))
    pl.when(jnp.logical_and(j > 0, j < MLP_STEPS - 1))(lambda: step(False, False))
    pl.when(j == MLP_STEPS - 1)(lambda: step(False, True))


def _mlp(h, x, w1, w2, mods, layer, tile0, n_tiles, final_norm=None):
    tm, th = TOK_TILE, FF_TILE
    final = final_norm is not None
    in_specs = [pl.BlockSpec((tm, D_MODEL), lambda i, j: (tile0 + i, 0), pipeline_mode=pl.Buffered(1)),
                pl.BlockSpec((None, D_MODEL, th), lambda i, j: (layer, 0, j)),
                pl.BlockSpec((None, th, D_MODEL), lambda i, j: (layer, j, 0)),
                pl.BlockSpec((tm, MLP_RES_COLS), lambda i, j: (tile0 + i, j)),
                _mod_spec(layer, 5, tm, tile_of=lambda i, j: tile0 + i)]
    args = [h, w1, w2, x, mods]
    if final:
        in_specs.append(pl.BlockSpec((1, D_MODEL), lambda i, j: (0, 0)))
        args.append(final_norm)
    return pl.pallas_call(
        functools.partial(_mlp_kernel, final=final),
        out_shape=jax.ShapeDtypeStruct((n_tiles * tm, D_MODEL), F32),
        grid=(n_tiles, MLP_STEPS),
        in_specs=in_specs,
        out_specs=pl.BlockSpec((tm, D_MODEL), lambda i, j: (i, 0)),
        scratch_shapes=[pltpu.VMEM((MLP_STEPS, tm, MLP_RES_COLS), F32)],
        compiler_params=_params(("arbitrary", "arbitrary")),
        name="mlp_final" if final else "mlp",
    )(*args)


def _pool_kernel(x_ref, wp_ref, ps_ref, gain1_ref, sh1_ref, sc1_ref, g1_ref, gain2_ref, sh2_ref, sc2_ref,
                 x3_ref, h4_ref):
    i = pl.program_id(0)
    tm = TOK_TILE
    rstd = lax.rsqrt(jnp.mean(x_ref[...] * x_ref[...], axis=-1, keepdims=True) + EPS)
    gs1 = gain1_ref[...] * (1.0 + sc1_ref[...])
    mix_gain = g1_ref[...] * ps_ref[...]

    def widen(m):
        return jnp.concatenate([m] * (POOL_GROUP // HEAD_DIM), axis=1)

    def mix_tile(seq):
        n_seq = tm // seq
        pitch = seq + 2 * POOL_HALO
        n_pad = n_seq * pitch
        halo = jnp.zeros((POOL_HALO, POOL_GROUP), F32)
        pos = lax.broadcasted_iota(jnp.int32, (seq, HEAD_DIM), 0)

        def down(a, k):
            return pltpu.roll(a, k % n_pad, 0)

        for g, w in enumerate(POOL_WINDOWS):
            half = w // 2
            cols = slice(g * POOL_GROUP, (g + 1) * POOL_GROUP)
            x = x_ref[:, cols]
            h = x * rstd * gs1[:, cols] + sh1_ref[:, cols]
            padded = jnp.concatenate(
                [piece for s in range(n_seq) for piece in (halo, h[s * seq:(s + 1) * seq], halo)], axis=0)
            back = padded
            m = 1
            while m < half:
                back = back + down(back, m)
                m *= 2
            ahead = back if half == 1 else down(back, -(half - 1))
            total = down(back, 1) + ahead
            total = jnp.concatenate(
                [total[s * pitch + POOL_HALO:s * pitch + POOL_HALO + seq] for s in range(n_seq)], axis=0)
            count = (jnp.minimum(pos + (w - half), seq) - jnp.maximum(pos - half, 0)).astype(F32)
            inv = jnp.concatenate([widen(1.0 / count)] * n_seq, axis=0)
            pooled = (total * inv - h).astype(BF16)
            mix = _bdot(pooled, wp_ref[g].astype(BF16))
            x3_ref[:, cols] = x + mix_gain[:, cols] * mix

    pl.when(i < N_PROMPT_TILES)(lambda: mix_tile(SEQ))
    pl.when(i >= N_PROMPT_TILES)(lambda: mix_tile(DEC_SEQ))
    x3 = x3_ref[...]
    h4_ref[...] = _modulate(x3, gain2_ref[...], sh2_ref[...], sc2_ref[...]).astype(BF16)


def _pool_mixer(x, w_pool, pool_scale, norm_mix, norm_mlp, mods):
    tm = TOK_TILE
    vec = pl.BlockSpec((1, D_MODEL), lambda i: (0, 0))
    tile = pl.BlockSpec((tm, D_MODEL), lambda i: (i, 0))
    return pl.pallas_call(
        _pool_kernel,
        out_shape=(jax.ShapeDtypeStruct((N_TOK, D_MODEL), F32), jax.ShapeDtypeStruct((N_TOK, D_MODEL), BF16)),
        grid=(N_TILES,),
        in_specs=[tile, pl.BlockSpec((len(POOL_WINDOWS), POOL_GROUP, POOL_GROUP), lambda i: (0, 0, 0)),
                  vec, vec, _mod_spec(1, 0, tm), _mod_spec(1, 1, tm), _mod_spec(1, 2, tm),
                  vec, _mod_spec(1, 3, tm), _mod_spec(1, 4, tm)],
        out_specs=(tile, tile),
        compiler_params=_params(("arbitrary",)),
        name="pool_mixer",
    )(x, w_pool, pool_scale, norm_mix, mods, mods, mods, norm_mlp, mods, mods)


def kernel(x_prompt, x_sample, cache_k, cache_v, state_hgrn_fwd, state_hgrn_bwd, c, c_ctx, w_ada, b_ada,
           norm_mix, norm_mlp, w_in_ab, w_out_ab, q_norm, k_norm, hg_norm, lb_raw, w_pool, pool_scale,
           w_mlp_in, w_mlp_out, final_norm):
    xp = x_prompt.reshape(N_PROMPT, D_MODEL)
    xs = x_sample.reshape(N_SAMPLE, D_MODEL)
    cv = jnp.concatenate([c_ctx[None, :], c, jnp.zeros((ADA_ROWS - 1 - DEC_BATCH, D_MODEL), F32)], axis=0)
    mods = _ada_table(cv, w_ada, b_ada).reshape(DEPTH, ADA_ROWS, 1, N_MOD * D_MODEL)

    h0 = _modulate0(xp, xs, norm_mix, mods)
    proj, new_k, new_v = _in_projection(h0, w_in_ab[0], q_norm[0:1], k_norm[0:1])
    ctx_k = cache_k[:, 0].reshape(DEC_BATCH, PAST_LEN, KV_WIDTH)
    ctx_v = cache_v[:, 0].reshape(DEC_BATCH, PAST_LEN, KV_WIDTH)
    att_p = _attention(proj, BATCH, SEQ, 0, SEQ)
    att_s = _attention(proj, DEC_BATCH, DEC_SEQ, N_PROMPT, 256, ctx=(ctx_k, ctx_v))
    consts = _hgrn_constants()
    hg_p, s_fwd, s_bwd = _hgrn(proj, lb_raw, hg_norm[0:1], consts, BATCH, SEQ, 0, HG_PROMPT_SEQS)
    s0 = (state_hgrn_fwd.reshape(DEC_BATCH, HG_HEADS, HG_DK, HG_DV),
          state_hgrn_bwd.reshape(DEC_BATCH, HG_HEADS, HG_DK, HG_DV))
    (hg_s,) = _hgrn(proj, lb_raw, hg_norm[0:1], consts, DEC_BATCH, DEC_SEQ, N_PROMPT, 1, s0=s0)
    x1, h2 = _out_projection(att_p, att_s, hg_p, hg_s, w_out_ab[0], xp, xs, mods, norm_mlp[0:1])
    x2 = _mlp(h2, x1, w_mlp_in, w_mlp_out, mods, 0, 0, N_TILES)

    x3, h4 = _pool_mixer(x2, w_pool[0], pool_scale[0:1], norm_mix[1:2], norm_mlp[1:2], mods)
    fin = final_norm[None, :]
    y_prompt = _mlp(h4, x3, w_mlp_in, w_mlp_out, mods, 1, 0, N_PROMPT_TILES, final_norm=fin)
    y_sample = _mlp(h4, x3, w_mlp_in, w_mlp_out, mods, 1, N_PROMPT_TILES, N_TILES - N_PROMPT_TILES,
                    final_norm=fin)

    return (y_prompt.reshape(BATCH, SEQ, D_MODEL), y_sample.reshape(DEC_BATCH, DEC_SEQ, D_MODEL),
            new_k.reshape(BATCH, 1, SEQ, N_KV_HEADS, HEAD_DIM), new_v.reshape(BATCH, 1, SEQ, N_KV_HEADS, HEAD_DIM),
            s_fwd.reshape(BATCH, 1, HG_HEADS, HG_DK, HG_DV), s_bwd.reshape(BATCH, 1, HG_HEADS, HG_DK, HG_DV))
```

```python
import functools

import numpy as np
import jax
import jax.numpy as jnp
from jax import lax
from jax.experimental import pallas as pl
from jax.experimental.pallas import tpu as pltpu

F32 = jnp.float32
BF16 = jnp.bfloat16

D_MODEL = 2048
BATCH = 16
SEQ = 256
DEPTH = 2
DEC_BATCH = 2
DEC_SEQ = 1024
PAST_LEN = 256
GRID_W = 64
HEAD_DIM = 128
N_Q_HEADS = 8
N_KV_HEADS = 2
Q_PER_KV = N_Q_HEADS // N_KV_HEADS
ATT_WIDTH = N_Q_HEADS * HEAD_DIM
KV_WIDTH = N_KV_HEADS * HEAD_DIM
HG_HEADS = 8
HG_DK = 128
HG_DV = 128
HG_KW = HG_HEADS * HG_DK
HG_VW = HG_HEADS * HG_DV
IN_AB = ATT_WIDTH + 2 * KV_WIDTH + 3 * HG_KW + 2 * HG_VW
MIX_WIDTH = ATT_WIDTH + HG_VW
POOL_WINDOWS = (2, 4, 8, 16)
POOL_GROUP = D_MODEL // len(POOL_WINDOWS)
POOL_HALO = 8
D_FF = 4 * D_MODEL
ROPE_THETA = 10000.0
ROPE_HALF = HEAD_DIM // 2
EPS = 1e-6
N_MOD = 6

N_PROMPT = BATCH * SEQ
N_SAMPLE = DEC_BATCH * DEC_SEQ
N_TOK = N_PROMPT + N_SAMPLE
ADA_ROWS = 16
ADA_COL_TILE = 1024
ADA_KEEP = 2 * D_MODEL // ADA_COL_TILE
MOD0_TOK_TILE = 512
TOK_TILE = 1024
N_TILES = N_TOK // TOK_TILE
N_PROMPT_TILES = N_PROMPT // TOK_TILE
PROJ_TOK_TILE = 2048
PROJ_COL_TILE = 512
OUT_COL_TILE = 512
FF_TILE = 512
MLP_STEPS = D_FF // FF_TILE
MLP_RES_COLS = D_MODEL // MLP_STEPS
MLP_OUT_CHUNK = 512
HG_CHUNK = 128
HG_LEVELS = 7
HG_HEAD_BLOCK = 4
HG_PROMPT_SEQS = 2
HG_SAFE_EXPONENT = 80.0
VMEM_LIMIT = 56 * 2 ** 20

COL_Q = 0
COL_K = ATT_WIDTH
COL_V = COL_K + KV_WIDTH
COL_HQ = COL_V + KV_WIDTH
COL_ZF = COL_HQ + HG_KW
COL_HI = COL_ZF + 2 * HG_KW
COL_HG = COL_HI + HG_VW


def _params(semantics):
    return pltpu.CompilerParams(dimension_semantics=semantics, vmem_limit_bytes=VMEM_LIMIT)


def _sigmoid(x):
    return 1.0 / (1.0 + jnp.exp(-x))


def _silu(x):
    return x * _sigmoid(x)


def _rms(x, gain):
    return x * lax.rsqrt(jnp.mean(x * x, axis=-1, keepdims=True) + EPS) * gain


def _bdot(a, b):
    return jnp.dot(a, b, preferred_element_type=F32)


def _mod_row(tile, tile_rows):
    first = N_PROMPT // tile_rows
    per_seq = DEC_SEQ // tile_rows
    return jnp.where(tile < first, 0, 1 + (tile - first) // per_seq)


def _mod_spec(layer, chunk, tile_rows, width=D_MODEL, col=lambda *g: 0, tile_of=lambda *g: g[0]):
    per = D_MODEL // width
    return pl.BlockSpec((None, None, 1, width),
                        lambda *g: (layer, _mod_row(tile_of(*g), tile_rows), 0, chunk * per + col(*g)))


def _modulate(x, gain, shift, scale):
    return x * lax.rsqrt(jnp.mean(x * x, axis=-1, keepdims=True) + EPS) * (gain * (1.0 + scale)) + shift


def _ada_kernel(cv_ref, w_ref, b_ref, xp_ref, xs_ref, gain_ref, o_ref, h_ref, keep_ref):
    s = pl.program_id(0)
    tile = _bdot(_silu(cv_ref[...]).astype(BF16), w_ref[...].astype(BF16)) + b_ref[...]
    o_ref[...] = tile

    @pl.when(s < ADA_KEEP)
    def _():
        keep_ref[s] = tile

    t = s - ADA_KEEP
    n_prompt = N_PROMPT // MOD0_TOK_TILE

    def run(x_ref):
        r = _mod_row(t, MOD0_TOK_TILE)
        per = D_MODEL // ADA_COL_TILE
        shift = jnp.concatenate([keep_ref[k, pl.ds(r, 1), :] for k in range(per)], axis=1)
        scale = jnp.concatenate([keep_ref[per + k, pl.ds(r, 1), :] for k in range(per)], axis=1)
        h_ref[...] = _modulate(x_ref[...], gain_ref[...], shift, scale).astype(BF16)

    pl.when(jnp.logical_and(t >= 0, t < n_prompt))(lambda: run(xp_ref))
    pl.when(jnp.logical_and(t >= n_prompt, t < N_TOK // MOD0_TOK_TILE))(lambda: run(xs_ref))


def _ada_table_and_modulate(cv, w_ada, b_ada, xp, xs, norm_mix):
    tn, tm = ADA_COL_TILE, MOD0_TOK_TILE
    n = N_MOD * D_MODEL
    per_layer = n // tn
    n_prompt = N_PROMPT // tm
    n_tiles = N_TOK // tm
    assert DEPTH * per_layer >= ADA_KEEP + n_tiles

    def tok(s):
        return jnp.clip(s - ADA_KEEP, 0, n_tiles - 1)

    return pl.pallas_call(
        _ada_kernel,
        out_shape=(jax.ShapeDtypeStruct((DEPTH, ADA_ROWS, n), F32), jax.ShapeDtypeStruct((N_TOK, D_MODEL), BF16)),
        grid=(DEPTH * per_layer,),
        in_specs=[pl.BlockSpec((ADA_ROWS, D_MODEL), lambda s: (0, 0)),
                  pl.BlockSpec((None, D_MODEL, tn), lambda s: (s // per_layer, 0, s % per_layer)),
                  pl.BlockSpec((None, 1, tn), lambda s: (s // per_layer, 0, s % per_layer)),
                  pl.BlockSpec((tm, D_MODEL), lambda s: (jnp.minimum(tok(s), n_prompt - 1), 0)),
                  pl.BlockSpec((tm, D_MODEL), lambda s: (jnp.maximum(tok(s) - n_prompt, 0), 0)),
                  pl.BlockSpec((1, D_MODEL), lambda s: (0, 0))],
        out_specs=(pl.BlockSpec((None, ADA_ROWS, tn), lambda s: (s // per_layer, 0, s % per_layer)),
                   pl.BlockSpec((tm, D_MODEL), lambda s: (tok(s), 0))),
        scratch_shapes=[pltpu.VMEM((ADA_KEEP, ADA_ROWS, tn), F32)],
        compiler_params=_params(("arbitrary",)),
        name="ada_table_modulate0",
    )(cv, w_ada, b_ada.reshape(DEPTH, 1, n), xp, xs, norm_mix[0:1])


def _rope(y, cos, sin, perm2):
    hi = y.astype(BF16)
    lo = (y - hi.astype(F32)).astype(BF16)
    rot = _bdot(jnp.concatenate([hi, lo], axis=1), perm2)
    return y * cos + rot * sin


def _inproj_kernel(h_ref, w_ref, qg_ref, kg_ref, cos_ref, sin_ref, perm_ref, p_ref, nk_ref, nv_ref):
    i = pl.program_id(0)
    j = pl.program_id(1)
    latent = i >= N_PROMPT // PROJ_TOK_TILE
    heads = PROJ_COL_TILE // HEAD_DIM
    kv_tile = COL_K // PROJ_COL_TILE
    is_q = j < kv_tile

    def attention_tile(rope):
        gain = jnp.where(is_q, qg_ref[...], kg_ref[...])
        w = w_ref[...].astype(BF16)
        for s in range(PROJ_TOK_TILE // DEC_SEQ):
            rows = slice(s * DEC_SEQ, (s + 1) * DEC_SEQ)
            acc = _bdot(h_ref[rows, :], w)
            for hh in range(heads):
                cols = slice(hh * HEAD_DIM, (hh + 1) * HEAD_DIM)
                x = acc[:, cols]
                y = _rms(x, gain)
                if rope:
                    y = _rope(y, cos_ref[...], sin_ref[...], perm_ref[...])
                if hh >= N_KV_HEADS:
                    y = jnp.where(is_q, y, x)
                p_ref[rows, cols] = y

    pl.when(jnp.logical_and(j <= kv_tile, latent))(lambda: attention_tile(True))
    pl.when(jnp.logical_and(j <= kv_tile, jnp.logical_not(latent)))(lambda: attention_tile(False))

    @pl.when(jnp.logical_and(j == kv_tile, jnp.logical_not(latent)))
    def _():
        nk_ref[...] = p_ref[:, :KV_WIDTH]
        nv_ref[...] = p_ref[:, KV_WIDTH:]

    def plain_tile(act):
        w = w_ref[...].astype(BF16)
        for s in range(PROJ_TOK_TILE // DEC_SEQ):
            rows = slice(s * DEC_SEQ, (s + 1) * DEC_SEQ)
            p_ref[rows, :] = act(_bdot(h_ref[rows, :], w))

    is_gate = jnp.logical_and(j >= COL_ZF // PROJ_COL_TILE, j < COL_HI // PROJ_COL_TILE)
    is_value = jnp.logical_and(j >= COL_HI // PROJ_COL_TILE, j < COL_HG // PROJ_COL_TILE)
    raw = jnp.logical_or(is_gate, is_value)
    pl.when(raw)(lambda: plain_tile(lambda a: a))
    pl.when(jnp.logical_and(j > kv_tile, jnp.logical_not(raw)))(lambda: plain_tile(_silu))


def _rope_tables():
    t = np.arange(DEC_SEQ)
    row = (t // GRID_W).astype(np.float32)
    col = (t % GRID_W).astype(np.float32)
    inv = (np.float32(ROPE_THETA) ** (-np.arange(0, ROPE_HALF, 2, dtype=np.float32) / np.float32(ROPE_HALF))).astype(np.float32)
    ar = row[:, None] * inv
    ac = col[:, None] * inv
    ang = np.concatenate([ar, ar, ac, ac], axis=-1).astype(np.float32)
    cos = np.cos(ang).astype(np.float32)
    sin = np.sin(ang).astype(np.float32)
    qw = ROPE_HALF // 2
    perm = np.zeros((HEAD_DIM, HEAD_DIM), np.float32)
    for k in range(qw):
        perm[qw + k, k] = -1.0
        perm[k, qw + k] = 1.0
        perm[3 * qw + k, 2 * qw + k] = -1.0
        perm[2 * qw + k, 3 * qw + k] = 1.0
    return jnp.asarray(cos), jnp.asarray(sin), jnp.asarray(np.concatenate([perm, perm], axis=0), BF16)


def _in_projection(h, w_in, q_gain, k_gain):
    tm, tn = PROJ_TOK_TILE, PROJ_COL_TILE
    n_prompt_tiles = N_PROMPT // tm
    cos, sin, perm2 = _rope_tables()
    table = pl.BlockSpec((DEC_SEQ, HEAD_DIM), lambda i, j: (0, 0))
    gain = pl.BlockSpec((1, HEAD_DIM), lambda i, j: (0, 0))
    state = pl.BlockSpec((tm, KV_WIDTH), lambda i, j: (jnp.minimum(i, n_prompt_tiles - 1), 0))
    return pl.pallas_call(
        _inproj_kernel,
        out_shape=(jax.ShapeDtypeStruct((N_TOK, IN_AB), F32),
                   jax.ShapeDtypeStruct((N_PROMPT, KV_WIDTH), F32),
                   jax.ShapeDtypeStruct((N_PROMPT, KV_WIDTH), F32)),
        grid=(N_TOK // tm, IN_AB // tn),
        in_specs=[pl.BlockSpec((tm, D_MODEL), lambda i, j: (i, 0)),
                  pl.BlockSpec((D_MODEL, tn), lambda i, j: (0, j)),
                  gain, gain, table, table, pl.BlockSpec((2 * HEAD_DIM, HEAD_DIM), lambda i, j: (0, 0))],
        out_specs=(pl.BlockSpec((tm, tn), lambda i, j: (i, j)), state, state),
        compiler_params=_params(("arbitrary", "arbitrary")),
        name="in_projection",
    )(h, w_in, q_gain, k_gain, cos, sin, perm2)


def _attn_kernel(*refs, has_ctx, stack):
    if has_ctx:
        q_ref, k_ref, v_ref, ck_ref, cv_ref, o_ref = refs
    else:
        q_ref, k_ref, v_ref, o_ref = refs
    scale = HEAD_DIM ** -0.5
    nt = (((1,), (1,)), ((), ()))
    tq = q_ref.shape[0]
    for hk in range(k_ref.shape[1] // HEAD_DIM):
        kcols = slice(hk * HEAD_DIM, (hk + 1) * HEAD_DIM)
        k = k_ref[:, kcols].astype(BF16)
        v = v_ref[:, kcols].astype(BF16)
        if has_ctx:
            ck = ck_ref[:, kcols].astype(BF16)
            cv = cv_ref[:, kcols].astype(BF16)
        for g0 in range(hk * Q_PER_KV, (hk + 1) * Q_PER_KV, stack):
            q = jnp.concatenate([q_ref[:, g * HEAD_DIM:(g + 1) * HEAD_DIM] for g in range(g0, g0 + stack)], axis=0)
            q = (q * scale).astype(BF16)
            s = lax.dot_general(q, k, nt, preferred_element_type=F32)
            m = jnp.max(s, axis=-1, keepdims=True)
            if has_ctx:
                sc = lax.dot_general(q, ck, nt, preferred_element_type=F32)
                m = jnp.maximum(m, jnp.max(sc, axis=-1, keepdims=True))
            p = jnp.exp(s - m)
            den = jnp.sum(p, axis=-1, keepdims=True)
            o = _bdot(p.astype(BF16), v)
            if has_ctx:
                pc = jnp.exp(sc - m)
                den = den + jnp.sum(pc, axis=-1, keepdims=True)
                o = o + _bdot(pc.astype(BF16), cv)
            o = (o / den).astype(o_ref.dtype)
            for g in range(stack):
                o_ref[:, (g0 + g) * HEAD_DIM:(g0 + g + 1) * HEAD_DIM] = o[g * tq:(g + 1) * tq]


def _attention(p, n_batch, seq, row0, tq, ctx=None):
    q_blocks = seq // tq
    kvh = N_KV_HEADS
    gw = kvh * Q_PER_KV * HEAD_DIM
    kw = kvh * HEAD_DIM
    in_specs = [
        pl.BlockSpec((tq, gw), lambda b, h, qi: (row0 // tq + b * q_blocks + qi, h)),
        pl.BlockSpec((seq, kw), lambda b, h, qi: (row0 // seq + b, COL_K // kw + h)),
        pl.BlockSpec((seq, kw), lambda b, h, qi: (row0 // seq + b, COL_V // kw + h)),
    ]
    args = [p, p, p]
    if ctx is not None:
        ctx_spec = pl.BlockSpec((None, PAST_LEN, kw), lambda b, h, qi: (b, 0, h))
        in_specs += [ctx_spec, ctx_spec]
        args += [ctx[0], ctx[1]]
    return pl.pallas_call(
        functools.partial(_attn_kernel, has_ctx=ctx is not None, stack=Q_PER_KV if ctx is None else 1),
        out_shape=jax.ShapeDtypeStruct((n_batch * seq, ATT_WIDTH), BF16),
        grid=(n_batch, N_KV_HEADS // kvh, q_blocks),
        in_specs=in_specs,
        out_specs=pl.BlockSpec((tq, gw), lambda b, h, qi: (b * q_blocks + qi, h)),
        compiler_params=_params(("arbitrary", "arbitrary", "arbitrary")),
        name="attention_latent" if ctx is not None else "attention_prompt",
    )(*args)


def _hgrn_constants():
    c = HG_CHUNK
    t = np.arange(c)
    cums, sels, pairs, scans, diags = [], [], [], [], []
    for d in range(2):
        pos = t if d == 0 else c - 1 - t
        pu, pt = pos[None, :], pos[:, None]
        ms, ss, ws = [], [], []
        for l in range(HG_LEVELS):
            m = c >> l
            blk = pos // m
            mid = (blk * m + m // 2)[:, None]
            late = ((pos % m) >= m // 2)
            ms.append(np.where(late[:, None], (pu >= mid) & (pu <= pt), (pu > pt) & (pu < mid)))
            ss.append(np.broadcast_to(late[:, None], (c, c)))
            ws.append((blk[:, None] == blk[None, :]) & late[:, None] & ~late[None, :])
        ms.append(pu <= pt)
        ms.append(pu > pt)
        cums.append(np.concatenate(ms, axis=0))
        sels.append(np.stack(ss))
        pairs.append(np.stack(ws))
        scans.append(np.concatenate([pu <= pt, pu <= pt], axis=1))
        diags.append(((pos // (c // 2))[:, None] == (pos // (c // 2))[None, :]) & (pu <= pt))
    return (jnp.asarray(np.stack(cums), BF16), jnp.asarray(np.stack(sels), F32),
            jnp.asarray(np.stack(pairs), F32), jnp.asarray(np.stack(scans), BF16),
            jnp.asarray(np.stack(diags), F32))


def _hgrn_kernel(*refs, n_seqs, n_chunks, has_s0):
    (hq_ref, zf_ref, zb_ref, hi_ref, hg_ref, lb_ref, og_ref, cum_ref, sel_ref, pair_ref, scan_ref,
     diag_ref) = refs[:12]
    refs = refs[12:]
    if has_s0:
        s0f_ref, s0b_ref, o_ref = refs[:3]
        refs = refs[3:]
    else:
        o_ref, sf_ref, sb_ref = refs[:3]
        refs = refs[3:]
    st_ref, acc_ref = refs
    hb = st_ref.shape[0]
    c = HG_CHUNK
    half = c // 2
    nt = (((1,), (1,)), ((), ()))
    tn = (((0,), (0,)), ((), ()))

    def direction(d, z_ref):
        raw = lb_ref[d]
        e = jnp.exp(raw - jnp.max(raw, axis=0, keepdims=True))
        lb = e[0:1] / jnp.sum(e, axis=0, keepdims=True)

        def initial_state(s, hh):
            if has_s0:
                return (s0f_ref, s0b_ref)[d][s, hh].T
            return jnp.zeros((HG_DV, HG_DK), F32)

        def gates(rows):
            f = lb + (1.0 - lb) * _sigmoid(z_ref[rows, :])
            logf = jnp.log(f)
            hi16 = logf.astype(BF16)
            lo16 = (logf - hi16.astype(F32)).astype(BF16)
            return f, hi16, lo16

        def row(p):
            t = p if d == 0 else c - 1 - p
            return slice(t, t + 1)

        early, late = (slice(0, half), slice(half, c)) if d == 0 else (slice(half, c), slice(0, half))

        def in_row_order(x_early, x_late):
            return jnp.concatenate([x_early, x_late] if d == 0 else [x_late, x_early], axis=0)

        def emit(rows, cols, o):
            if d == 0:
                acc_ref[rows, cols] = o
            else:
                tot = acc_ref[rows, cols] + o
                o_ref[rows, cols] = (_rms(tot, og_ref[...]) * hg_ref[rows, cols]).astype(o_ref.dtype)

        def two_level_operands(rows):
            f, hi16, lo16 = gates(rows)
            b = _bdot(scan_ref[d], jnp.concatenate([hi16, lo16], axis=0))
            kk = 1.0 - f
            q = hq_ref[rows, :]
            r_mid = b[row(half - 1)]
            x1 = in_row_order(kk[early] * jnp.exp(r_mid - b[early]), q[late] * jnp.exp(b[late] - r_mid)).astype(BF16)
            dq = in_row_order(b[early] - b[row(half // 2 - 1)], b[late] - b[row(half + half // 2 - 1)])
            b_end = b[row(c - 1)]
            span = jnp.maximum(
                jnp.maximum(b[row(0)] - b[row(half // 2 - 1)], b[row(half // 2 - 1)] - b[row(half - 1)]),
                jnp.maximum(b[row(half)] - b[row(half + half // 2 - 1)],
                            b[row(half + half // 2 - 1)] - b[row(c - 1)]))
            return dict(x1=x1, xq=(q * jnp.exp(dq)).astype(BF16), xk=(kk * jnp.exp(-dq)).astype(BF16),
                        q_in=(q * jnp.exp(b)).astype(BF16), k_out=(kk * jnp.exp(b_end - b)).astype(BF16),
                        a_end=jnp.exp(b_end), iv=hi_ref[rows, :].astype(BF16), span=span)

        def two_level_chunk(rows, ops, states, need_state):
            new_states = []
            for hh in range(hb):
                cols = slice(hh * HG_DK, (hh + 1) * HG_DK)
                g1 = _bdot(ops["x1"][:, cols], ops["x1"][:, cols].astype(F32).T.astype(BF16))
                g2 = _bdot(ops["xq"][:, cols], ops["xk"][:, cols].astype(F32).T.astype(BF16))
                att = jnp.where(pair_ref[d, 0] > 0.5, g1, jnp.where(diag_ref[d] > 0.5, g2, 0.0)).astype(BF16)
                st = states[hh]
                o = _bdot(att, ops["iv"][:, cols])
                if st is not None:
                    o = o + lax.dot_general(ops["q_in"][:, cols], st.astype(BF16), nt, preferred_element_type=F32)
                emit(rows, cols, o)
                if not need_state:
                    new_states.append(None)
                    continue
                dst = lax.dot_general(ops["iv"][:, cols], ops["k_out"][:, cols], tn, preferred_element_type=F32)
                new_states.append(dst if st is None else ops["a_end"][:, cols] * st + dst)
            return new_states

        def all_levels(rows):
            f, hi16, lo16 = gates(rows)
            cum = cum_ref[d]
            eall = jnp.exp(_bdot(cum, hi16) + _bdot(cum, lo16))
            for hh in range(hb):
                cols = slice(hh * HG_DK, (hh + 1) * HG_DK)
                kk = 1.0 - f[:, cols]
                q = hq_ref[rows, cols]
                iv = hi_ref[rows, cols]
                iv16 = iv.astype(BF16)
                att = jnp.zeros((c, c), F32)
                for l in range(HG_LEVELS):
                    x = (kk + sel_ref[d, l] * (q - kk)) * eall[l * c:(l + 1) * c, cols]
                    xb = x.astype(BF16)
                    att = att + pair_ref[d, l] * lax.dot_general(xb, xb, nt, preferred_element_type=F32)
                e_in = eall[HG_LEVELS * c:(HG_LEVELS + 1) * c, cols]
                e_out = eall[(HG_LEVELS + 1) * c:, cols]
                st = st_ref[hh]
                o = (_bdot(att.astype(BF16), iv16)
                     + jnp.sum(q * kk, axis=-1, keepdims=True) * iv
                     + lax.dot_general((q * e_in).astype(BF16), st.astype(BF16), nt, preferred_element_type=F32))
                dst = lax.dot_general(iv16, (kk * e_out).astype(BF16), tn, preferred_element_type=F32)
                st_ref[hh] = (e_in[0:1] * e_out[0:1]) * st + dst
                emit(rows, cols, o)

        def chunk_start(s, ci):
            cidx = ci if d == 0 else n_chunks - 1 - ci
            return (s * n_chunks + cidx) * c

        worst = jnp.zeros((1, hb * HG_DK), F32)
        states = [[initial_state(s, hh) if has_s0 else None for hh in range(hb)] for s in range(n_seqs)]
        for ci in range(n_chunks):
            for s in range(n_seqs):
                rows = slice(chunk_start(s, ci), chunk_start(s, ci) + c)
                ops = two_level_operands(rows)
                states[s] = two_level_chunk(rows, ops, states[s], need_state=ci < n_chunks - 1 or not has_s0)
                worst = jnp.maximum(worst, ops["span"])
        if not has_s0:
            for s in range(n_seqs):
                for hh in range(hb):
                    (sf_ref, sb_ref)[d][s, hh] = states[s][hh].T

        @pl.when(jnp.logical_not(jnp.max(worst) <= HG_SAFE_EXPONENT))
        def _():
            for s in range(n_seqs):
                for hh in range(hb):
                    st_ref[hh] = initial_state(s, hh)

                def chunk(ci, carry):
                    all_levels(pl.ds(pl.multiple_of(chunk_start(s, ci), c), c))
                    return carry

                lax.fori_loop(0, n_chunks, chunk, 0)
                if not has_s0:
                    for hh in range(hb):
                        (sf_ref, sb_ref)[d][s, hh] = st_ref[hh].T

    direction(0, zf_ref)
    direction(1, zb_ref)


def _hgrn(p, lb_raw, o_gain, consts, n_batch, seq, row0, n_seqs, s0=None):
    hb = HG_HEAD_BLOCK
    w = hb * HG_DK
    rows = n_seqs * seq

    def seg(col):
        return pl.BlockSpec((rows, w), lambda b, h: (row0 // rows + b, col // w + h))

    def const(a):
        return pl.BlockSpec(a.shape, lambda b, h: (0,) * a.ndim)

    in_specs = [seg(COL_HQ), seg(COL_ZF), seg(COL_ZF + HG_KW), seg(COL_HI), seg(COL_HG),
                pl.BlockSpec((2, DEPTH + 1, w), lambda b, h: (0, 0, h)),
                pl.BlockSpec((1, HG_DV), lambda b, h: (0, 0))] + [const(a) for a in consts]
    args = [p, p, p, p, p, lb_raw, o_gain, *consts]
    has_s0 = s0 is not None
    st_spec = pl.BlockSpec((n_seqs, hb, HG_DK, HG_DV), lambda b, h: (b, h, 0, 0))
    if has_s0:
        in_specs += [st_spec, st_spec]
        args += [s0[0], s0[1]]
    out_shape = [jax.ShapeDtypeStruct((n_batch * seq, HG_VW), BF16)]
    out_specs = [pl.BlockSpec((rows, w), lambda b, h: (b, h))]
    if not has_s0:
        st_shape = jax.ShapeDtypeStruct((n_batch, HG_HEADS, HG_DK, HG_DV), F32)
        out_shape += [st_shape, st_shape]
        out_specs += [st_spec, st_spec]
    return pl.pallas_call(
        functools.partial(_hgrn_kernel, n_seqs=n_seqs, n_chunks=seq // HG_CHUNK, has_s0=has_s0),
        out_shape=tuple(out_shape),
        grid=(n_batch // n_seqs, HG_HEADS // hb),
        in_specs=in_specs,
        out_specs=tuple(out_specs),
        scratch_shapes=[pltpu.VMEM((hb, HG_DV, HG_DK), F32), pltpu.VMEM((rows, w), F32)],
        compiler_params=_params(("arbitrary", "arbitrary")),
        name="hgrn_latent" if has_s0 else "hgrn_prompt",
    )(*args)


def _outproj_kernel(attp_ref, atts_ref, hgp_ref, hgs_ref, wa_ref, wb_ref, xp_ref, xs_ref, g1_ref, gain_ref,
                    sh_ref, sc_ref, x1_ref, h2_ref, full_ref, wcache_ref):
    i = pl.program_id(0)
    n = pl.program_id(1)

    @pl.when(i == 0)
    def _():
        wcache_ref[n, 0] = wa_ref[...].astype(BF16)
        wcache_ref[n, 1] = wb_ref[...].astype(BF16)

    def run(att_ref, hg_ref, x_ref):
        acc = _bdot(att_ref[...], wcache_ref[n, 0]) + _bdot(hg_ref[...], wcache_ref[n, 1])
        x1 = x_ref[...] + g1_ref[...] * acc
        x1_ref[...] = x1
        full_ref[n] = x1

    pl.when(i < N_PROMPT_TILES)(lambda: run(attp_ref, hgp_ref, xp_ref))
    pl.when(i >= N_PROMPT_TILES)(lambda: run(atts_ref, hgs_ref, xs_ref))

    @pl.when(n == D_MODEL // OUT_COL_TILE - 1)
    def _():
        nt = D_MODEL // OUT_COL_TILE
        ms = jnp.sum(full_ref[0] * full_ref[0], axis=-1, keepdims=True)
        for k in range(1, nt):
            ms = ms + jnp.sum(full_ref[k] * full_ref[k], axis=-1, keepdims=True)
        rstd = lax.rsqrt(ms / D_MODEL + EPS)
        gs = gain_ref[...] * (1.0 + sc_ref[...])
        for k in range(nt):
            cols = slice(k * OUT_COL_TILE, (k + 1) * OUT_COL_TILE)
            h2_ref[:, cols] = (full_ref[k] * rstd * gs[:, cols] + sh_ref[:, cols]).astype(BF16)


def _out_projection(att_p, att_s, hg_p, hg_s, w_out, xp, xs, mods, norm_mlp):
    tm, tn = TOK_TILE, OUT_COL_TILE
    nt = D_MODEL // tn

    def prompt_rows(width):
        return pl.BlockSpec((tm, width), lambda i, n: (jnp.minimum(i, N_PROMPT_TILES - 1), 0),
                            pipeline_mode=pl.Buffered(1))

    def sample_rows(width):
        return pl.BlockSpec((tm, width), lambda i, n: (jnp.maximum(i - N_PROMPT_TILES, 0), 0),
                            pipeline_mode=pl.Buffered(1))

    return pl.pallas_call(
        _outproj_kernel,
        out_shape=(jax.ShapeDtypeStruct((N_TOK, D_MODEL), F32), jax.ShapeDtypeStruct((N_TOK, D_MODEL), BF16)),
        grid=(N_TILES, nt),
        in_specs=[prompt_rows(ATT_WIDTH), sample_rows(ATT_WIDTH), prompt_rows(HG_VW), sample_rows(HG_VW),
                  pl.BlockSpec((ATT_WIDTH, tn), lambda i, n: (0, jnp.where(i == 0, n, nt - 1))),
                  pl.BlockSpec((HG_VW, tn), lambda i, n: (1, jnp.where(i == 0, n, nt - 1))),
                  pl.BlockSpec((tm, tn), lambda i, n: (jnp.minimum(i, N_PROMPT_TILES - 1),
                                                       jnp.where(i < N_PROMPT_TILES, n, nt - 1))),
                  pl.BlockSpec((tm, tn), lambda i, n: (jnp.maximum(i - N_PROMPT_TILES, 0),
                                                       jnp.where(i < N_PROMPT_TILES, 0, n))),
                  _mod_spec(0, 2, tm, width=tn, col=lambda i, n: n),
                  pl.BlockSpec((1, D_MODEL), lambda i, n: (0, 0)),
                  _mod_spec(0, 3, tm), _mod_spec(0, 4, tm)],
        out_specs=(pl.BlockSpec((tm, tn), lambda i, n: (i, n)),
                   pl.BlockSpec((tm, D_MODEL), lambda i, n: (i, 0))),
        scratch_shapes=[pltpu.VMEM((nt, tm, tn), F32), pltpu.VMEM((nt, 2, ATT_WIDTH, tn), BF16)],
        compiler_params=_params(("arbitrary", "arbitrary")),
        name="out_projection",
    )(att_p, att_s, hg_p, hg_s, w_out, w_out, xp, xs, mods, norm_mlp, mods, mods)


def _mlp_kernel(h_ref, w1_ref, w2_ref, x_ref, g2_ref, *rest, final):
    if final:
        fin_ref, o_ref, res_ref = rest
    else:
        o_ref, res_ref = rest
    j = pl.program_id(1)
    res_ref[j] = x_ref[...]

    def step(first, last):
        a = jnp.square(jnp.maximum(_bdot(h_ref[...], w1_ref[...].astype(BF16)), 0.0)).astype(BF16)
        for n in range(D_MODEL // MLP_OUT_CHUNK):
            cols = slice(n * MLP_OUT_CHUNK, (n + 1) * MLP_OUT_CHUNK)
            p = _bdot(a, w2_ref[:, cols].astype(BF16))
            if not first:
                p = o_ref[:, cols] + p
            if last:
                per = MLP_OUT_CHUNK // MLP_RES_COLS
                res = jnp.concatenate([res_ref[n * per + k] for k in range(per)], axis=1)
                p = res + g2_ref[:, cols] * p
            o_ref[:, cols] = p
        if last and final:
            o_ref[...] = _rms(o_ref[...], fin_ref[...])

    pl.when(j == 0)(lambda: step(True, False))
    pl.when(jnp.logical_and(j > 0, j < MLP_STEPS - 1))(lambda: step(False, False))
    pl.when(j == MLP_STEPS - 1)(lambda: step(False, True))


def _mlp(h, x, w1, w2, mods, layer, tile0, n_tiles, final_norm=None):
    tm, th = TOK_TILE, FF_TILE
    final = final_norm is not None
    in_specs = [pl.BlockSpec((tm, D_MODEL), lambda i, j: (tile0 + i, 0), pipeline_mode=pl.Buffered(1)),
                pl.BlockSpec((None, D_MODEL, th), lambda i, j: (layer, 0, j)),
                pl.BlockSpec((None, th, D_MODEL), lambda i, j: (layer, j, 0)),
                pl.BlockSpec((tm, MLP_RES_COLS), lambda i, j: (tile0 + i, j)),
                _mod_spec(layer, 5, tm, tile_of=lambda i, j: tile0 + i)]
    args = [h, w1, w2, x, mods]
    if final:
        in_specs.append(pl.BlockSpec((1, D_MODEL), lambda i, j: (0, 0)))
        args.append(final_norm)
    return pl.pallas_call(
        functools.partial(_mlp_kernel, final=final),
        out_shape=jax.ShapeDtypeStruct((n_tiles * tm, D_MODEL), F32),
        grid=(n_tiles, MLP_STEPS),
        in_specs=in_specs,
        out_specs=pl.BlockSpec((tm, D_MODEL), lambda i, j: (i, 0)),
        scratch_shapes=[pltpu.VMEM((MLP_STEPS, tm, MLP_RES_COLS), F32)],
        compiler_params=_params(("arbitrary", "arbitrary")),
        name="mlp_final" if final else "mlp",
    )(*args)


def _pool_kernel(x_ref, wp_ref, ps_ref, gain1_ref, sh1_ref, sc1_ref, g1_ref, gain2_ref, sh2_ref, sc2_ref,
                 x3_ref, h4_ref):
    i = pl.program_id(0)
    tm = TOK_TILE
    rstd = lax.rsqrt(jnp.mean(x_ref[...] * x_ref[...], axis=-1, keepdims=True) + EPS)
    gs1 = gain1_ref[...] * (1.0 + sc1_ref[...])
    mix_gain = g1_ref[...] * ps_ref[...]

    def widen(m):
        return jnp.concatenate([m] * (POOL_GROUP // HEAD_DIM), axis=1)

    def mix_tile(seq):
        n_seq = tm // seq
        pitch = seq + 2 * POOL_HALO
        n_pad = n_seq * pitch
        halo = jnp.zeros((POOL_HALO, POOL_GROUP), F32)
        pos = lax.broadcasted_iota(jnp.int32, (seq, HEAD_DIM), 0)

        def down(a, k):
            return pltpu.roll(a, k % n_pad, 0)

        for g, w in enumerate(POOL_WINDOWS):
            half = w // 2
            cols = slice(g * POOL_GROUP, (g + 1) * POOL_GROUP)
            x = x_ref[:, cols]
            h = x * rstd * gs1[:, cols] + sh1_ref[:, cols]
            padded = jnp.concatenate(
                [piece for s in range(n_seq) for piece in (halo, h[s * seq:(s + 1) * seq], halo)], axis=0)
            back = padded
            m = 1
            while m < half:
                back = back + down(back, m)
                m *= 2
            ahead = back if half == 1 else down(back, -(half - 1))
            total = down(back, 1) + ahead
            total = jnp.concatenate(
                [total[s * pitch + POOL_HALO:s * pitch + POOL_HALO + seq] for s in range(n_seq)], axis=0)
            count = (jnp.minimum(pos + (w - half), seq) - jnp.maximum(pos - half, 0)).astype(F32)
            inv = jnp.concatenate([widen(1.0 / count)] * n_seq, axis=0)
            pooled = (total * inv - h).astype(BF16)
            mix = _bdot(pooled, wp_ref[g].astype(BF16))
            x3_ref[:, cols] = x + mix_gain[:, cols] * mix

    pl.when(i < N_PROMPT_TILES)(lambda: mix_tile(SEQ))
    pl.when(i >= N_PROMPT_TILES)(lambda: mix_tile(DEC_SEQ))
    x3 = x3_ref[...]
    rstd3 = lax.rsqrt(jnp.mean(x3 * x3, axis=-1, keepdims=True) + EPS)
    h4_ref[...] = (x3 * rstd3 * (gain2_ref[...] * (1.0 + sc2_ref[...])) + sh2_ref[...]).astype(BF16)


def _pool_mixer(x, w_pool, pool_scale, norm_mix, norm_mlp, mods):
    tm = TOK_TILE
    vec = pl.BlockSpec((1, D_MODEL), lambda i: (0, 0))
    tile = pl.BlockSpec((tm, D_MODEL), lambda i: (i, 0))
    return pl.pallas_call(
        _pool_kernel,
        out_shape=(jax.ShapeDtypeStruct((N_TOK, D_MODEL), F32), jax.ShapeDtypeStruct((N_TOK, D_MODEL), BF16)),
        grid=(N_TILES,),
        in_specs=[tile, pl.BlockSpec((len(POOL_WINDOWS), POOL_GROUP, POOL_GROUP), lambda i: (0, 0, 0)),
                  vec, vec, _mod_spec(1, 0, tm), _mod_spec(1, 1, tm), _mod_spec(1, 2, tm),
                  vec, _mod_spec(1, 3, tm), _mod_spec(1, 4, tm)],
        out_specs=(tile, tile),
        compiler_params=_params(("arbitrary",)),
        name="pool_mixer",
    )(x, w_pool, pool_scale, norm_mix, mods, mods, mods, norm_mlp, mods, mods)


def kernel(x_prompt, x_sample, cache_k, cache_v, state_hgrn_fwd, state_hgrn_bwd, c, c_ctx, w_ada, b_ada,
           norm_mix, norm_mlp, w_in_ab, w_out_ab, q_norm, k_norm, hg_norm, lb_raw, w_pool, pool_scale,
           w_mlp_in, w_mlp_out, final_norm):
    xp = x_prompt.reshape(N_PROMPT, D_MODEL)
    xs = x_sample.reshape(N_SAMPLE, D_MODEL)
    cv = jnp.concatenate([c_ctx[None, :], c, jnp.zeros((ADA_ROWS - 1 - DEC_BATCH, D_MODEL), F32)], axis=0)
    mods, h0 = _ada_table_and_modulate(cv, w_ada, b_ada, xp, xs, norm_mix)
    mods = mods.reshape(DEPTH, ADA_ROWS, 1, N_MOD * D_MODEL)

    proj, new_k, new_v = _in_projection(h0, w_in_ab[0], q_norm[0:1], k_norm[0:1])
    ctx_k = cache_k[:, 0].reshape(DEC_BATCH, PAST_LEN, KV_WIDTH)
    ctx_v = cache_v[:, 0].reshape(DEC_BATCH, PAST_LEN, KV_WIDTH)
    att_p = _attention(proj, BATCH, SEQ, 0, SEQ)
    att_s = _attention(proj, DEC_BATCH, DEC_SEQ, N_PROMPT, 256, ctx=(ctx_k, ctx_v))
    consts = _hgrn_constants()
    hg_p, s_fwd, s_bwd = _hgrn(proj, lb_raw, hg_norm[0:1], consts, BATCH, SEQ, 0, HG_PROMPT_SEQS)
    s0 = (state_hgrn_fwd.reshape(DEC_BATCH, HG_HEADS, HG_DK, HG_DV),
          state_hgrn_bwd.reshape(DEC_BATCH, HG_HEADS, HG_DK, HG_DV))
    (hg_s,) = _hgrn(proj, lb_raw, hg_norm[0:1], consts, DEC_BATCH, DEC_SEQ, N_PROMPT, 1, s0=s0)
    x1, h2 = _out_projection(att_p, att_s, hg_p, hg_s, w_out_ab[0], xp, xs, mods, norm_mlp[0:1])
    x2 = _mlp(h2, x1, w_mlp_in, w_mlp_out, mods, 0, 0, N_TILES)

    x3, h4 = _pool_mixer(x2, w_pool[0], pool_scale[0:1], norm_mix[1:2], norm_mlp[1:2], mods)
    fin = final_norm[None, :]
    y_prompt = _mlp(h4, x3, w_mlp_in, w_mlp_out, mods, 1, 0, N_PROMPT_TILES, final_norm=fin)
    y_sample = _mlp(h4, x3, w_mlp_in, w_mlp_out, mods, 1, N_PROMPT_TILES, N_TILES - N_PROMPT_TILES,
                    final_norm=fin)

    return (y_prompt.reshape(BATCH, SEQ, D_MODEL), y_sample.reshape(DEC_BATCH, DEC_SEQ, D_MODEL),
            new_k.reshape(BATCH, 1, SEQ, N_KV_HEADS, HEAD_DIM), new_v.reshape(BATCH, 1, SEQ, N_KV_HEADS, HEAD_DIM),
            s_fwd.reshape(BATCH, 1, HG_HEADS, HG_DK, HG_DV), s_bwd.reshape(BATCH, 1, HG_HEADS, HG_DK, HG_DV))
```

```python
import functools

import numpy as np
import jax
import jax.numpy as jnp
from jax import lax
from jax.experimental import pallas as pl
from jax.experimental.pallas import tpu as pltpu

F32 = jnp.float32
BF16 = jnp.bfloat16

D_MODEL = 2048
BATCH = 16
SEQ = 256
DEPTH = 2
DEC_BATCH = 2
DEC_SEQ = 1024
PAST_LEN = 256
GRID_W = 64
HEAD_DIM = 128
N_Q_HEADS = 8
N_KV_HEADS = 2
Q_PER_KV = N_Q_HEADS // N_KV_HEADS
ATT_WIDTH = N_Q_HEADS * HEAD_DIM
KV_WIDTH = N_KV_HEADS * HEAD_DIM
HG_HEADS = 8
HG_DK = 128
HG_DV = 128
HG_KW = HG_HEADS * HG_DK
HG_VW = HG_HEADS * HG_DV
IN_AB = ATT_WIDTH + 2 * KV_WIDTH + 3 * HG_KW + 2 * HG_VW
MIX_WIDTH = ATT_WIDTH + HG_VW
POOL_WINDOWS = (2, 4, 8, 16)
POOL_GROUP = D_MODEL // len(POOL_WINDOWS)
POOL_HALO = 8
D_FF = 4 * D_MODEL
ROPE_THETA = 10000.0
ROPE_HALF = HEAD_DIM // 2
EPS = 1e-6
N_MOD = 6

N_PROMPT = BATCH * SEQ
N_SAMPLE = DEC_BATCH * DEC_SEQ
N_TOK = N_PROMPT + N_SAMPLE
ADA_ROWS = 16
ADA_COL_TILE = 1024
ADA_KEEP = 2 * D_MODEL // ADA_COL_TILE
MOD0_TOK_TILE = 512
TOK_TILE = 1024
N_TILES = N_TOK // TOK_TILE
N_PROMPT_TILES = N_PROMPT // TOK_TILE
PROJ_TOK_TILE = 2048
PROJ_COL_TILE = 512
OUT_COL_TILE = 512
FF_TILE = 512
MLP_STEPS = D_FF // FF_TILE
MLP_RES_COLS = D_MODEL // MLP_STEPS
MLP_OUT_CHUNK = 512
HG_CHUNK = 128
HG_LEVELS = 7
HG_HEAD_BLOCK = 4
HG_PROMPT_SEQS = 2
HG_SAFE_EXPONENT = 80.0
V7X_VMEM_BYTES = 64 * 2 ** 20
VMEM_LIMIT = V7X_VMEM_BYTES - 8 * 2 ** 20

COL_Q = 0
COL_K = ATT_WIDTH
COL_V = COL_K + KV_WIDTH
COL_HQ = COL_V + KV_WIDTH
COL_ZF = COL_HQ + HG_KW
COL_HI = COL_ZF + 2 * HG_KW
COL_HG = COL_HI + HG_VW


def _params(semantics):
    return pltpu.CompilerParams(dimension_semantics=semantics, vmem_limit_bytes=VMEM_LIMIT)


def _sigmoid(x):
    return 1.0 / (1.0 + jnp.exp(-x))


def _silu(x):
    return x * _sigmoid(x)


def _rms(x, gain):
    return x * lax.rsqrt(jnp.mean(x * x, axis=-1, keepdims=True) + EPS) * gain


def _bdot(a, b):
    return jnp.dot(a, b, preferred_element_type=F32)


def _mod_row(tile, tile_rows):
    first = N_PROMPT // tile_rows
    per_seq = DEC_SEQ // tile_rows
    return jnp.where(tile < first, 0, 1 + (tile - first) // per_seq)


def _mod_spec(layer, chunk, tile_rows, width=D_MODEL, col=lambda *g: 0, tile_of=lambda *g: g[0]):
    per = D_MODEL // width
    return pl.BlockSpec((None, None, 1, width),
                        lambda *g: (layer, _mod_row(tile_of(*g), tile_rows), 0, chunk * per + col(*g)))


def _modulate(x, gain, shift, scale):
    return x * lax.rsqrt(jnp.mean(x * x, axis=-1, keepdims=True) + EPS) * (gain * (1.0 + scale)) + shift


def _ada_kernel(cv_ref, w_ref, b_ref, xp_ref, xs_ref, gain_ref, o_ref, h_ref, keep_ref):
    s = pl.program_id(0)
    tile = _bdot(_silu(cv_ref[...]).astype(BF16), w_ref[...].astype(BF16)) + b_ref[...]
    o_ref[...] = tile

    @pl.when(s < ADA_KEEP)
    def _():
        keep_ref[s] = tile

    t = s - ADA_KEEP
    n_prompt = N_PROMPT // MOD0_TOK_TILE

    def run(x_ref):
        r = _mod_row(t, MOD0_TOK_TILE)
        per = D_MODEL // ADA_COL_TILE
        shift = jnp.concatenate([keep_ref[k, pl.ds(r, 1), :] for k in range(per)], axis=1)
        scale = jnp.concatenate([keep_ref[per + k, pl.ds(r, 1), :] for k in range(per)], axis=1)
        h_ref[...] = _modulate(x_ref[...], gain_ref[...], shift, scale).astype(BF16)

    pl.when(jnp.logical_and(t >= 0, t < n_prompt))(lambda: run(xp_ref))
    pl.when(jnp.logical_and(t >= n_prompt, t < N_TOK // MOD0_TOK_TILE))(lambda: run(xs_ref))


def _ada_table_and_modulate(cv, w_ada, b_ada, xp, xs, norm_mix):
    tn, tm = ADA_COL_TILE, MOD0_TOK_TILE
    n = N_MOD * D_MODEL
    per_layer = n // tn
    n_prompt = N_PROMPT // tm
    n_tiles = N_TOK // tm
    assert DEPTH * per_layer >= ADA_KEEP + n_tiles

    def tok(s):
        return jnp.clip(s - ADA_KEEP, 0, n_tiles - 1)

    return pl.pallas_call(
        _ada_kernel,
        out_shape=(jax.ShapeDtypeStruct((DEPTH, ADA_ROWS, n), F32), jax.ShapeDtypeStruct((N_TOK, D_MODEL), BF16)),
        grid=(DEPTH * per_layer,),
        in_specs=[pl.BlockSpec((ADA_ROWS, D_MODEL), lambda s: (0, 0)),
                  pl.BlockSpec((None, D_MODEL, tn), lambda s: (s // per_layer, 0, s % per_layer)),
                  pl.BlockSpec((None, 1, tn), lambda s: (s // per_layer, 0, s % per_layer)),
                  pl.BlockSpec((tm, D_MODEL), lambda s: (jnp.minimum(tok(s), n_prompt - 1), 0)),
                  pl.BlockSpec((tm, D_MODEL), lambda s: (jnp.maximum(tok(s) - n_prompt, 0), 0)),
                  pl.BlockSpec((1, D_MODEL), lambda s: (0, 0))],
        out_specs=(pl.BlockSpec((None, ADA_ROWS, tn), lambda s: (s // per_layer, 0, s % per_layer)),
                   pl.BlockSpec((tm, D_MODEL), lambda s: (tok(s), 0))),
        scratch_shapes=[pltpu.VMEM((ADA_KEEP, ADA_ROWS, tn), F32)],
        compiler_params=_params(("arbitrary",)),
        name="ada_table_modulate0",
    )(cv, w_ada, b_ada.reshape(DEPTH, 1, n), xp, xs, norm_mix[0:1])


def _rope(y, cos, sin, perm2):
    hi = y.astype(BF16)
    lo = (y - hi.astype(F32)).astype(BF16)
    rot = _bdot(jnp.concatenate([hi, lo], axis=1), perm2)
    return y * cos + rot * sin


def _inproj_kernel(h_ref, w_ref, qg_ref, kg_ref, cos_ref, sin_ref, perm_ref, p_ref, nk_ref, nv_ref):
    i = pl.program_id(0)
    j = pl.program_id(1)
    latent = i >= N_PROMPT // PROJ_TOK_TILE
    heads = PROJ_COL_TILE // HEAD_DIM
    kv_tile = COL_K // PROJ_COL_TILE
    is_q = j < kv_tile

    def attention_tile(rope):
        gain = jnp.where(is_q, qg_ref[...], kg_ref[...])
        w = w_ref[...].astype(BF16)
        for s in range(PROJ_TOK_TILE // DEC_SEQ):
            rows = slice(s * DEC_SEQ, (s + 1) * DEC_SEQ)
            acc = _bdot(h_ref[rows, :], w)
            for hh in range(heads):
                cols = slice(hh * HEAD_DIM, (hh + 1) * HEAD_DIM)
                x = acc[:, cols]
                y = _rms(x, gain)
                if rope:
                    y = _rope(y, cos_ref[...], sin_ref[...], perm_ref[...])
                if hh >= N_KV_HEADS:
                    y = jnp.where(is_q, y, x)
                p_ref[rows, cols] = y

    pl.when(jnp.logical_and(j <= kv_tile, latent))(lambda: attention_tile(True))
    pl.when(jnp.logical_and(j <= kv_tile, jnp.logical_not(latent)))(lambda: attention_tile(False))

    @pl.when(jnp.logical_and(j == kv_tile, jnp.logical_not(latent)))
    def _():
        nk_ref[...] = p_ref[:, :KV_WIDTH]
        nv_ref[...] = p_ref[:, KV_WIDTH:]

    def plain_tile(act):
        w = w_ref[...].astype(BF16)
        for s in range(PROJ_TOK_TILE // DEC_SEQ):
            rows = slice(s * DEC_SEQ, (s + 1) * DEC_SEQ)
            p_ref[rows, :] = act(_bdot(h_ref[rows, :], w))

    is_gate = jnp.logical_and(j >= COL_ZF // PROJ_COL_TILE, j < COL_HI // PROJ_COL_TILE)
    is_value = jnp.logical_and(j >= COL_HI // PROJ_COL_TILE, j < COL_HG // PROJ_COL_TILE)
    raw = jnp.logical_or(is_gate, is_value)
    pl.when(raw)(lambda: plain_tile(lambda a: a))
    pl.when(jnp.logical_and(j > kv_tile, jnp.logical_not(raw)))(lambda: plain_tile(_silu))


def _rope_tables():
    t = np.arange(DEC_SEQ)
    row = (t // GRID_W).astype(np.float32)
    col = (t % GRID_W).astype(np.float32)
    inv = (np.float32(ROPE_THETA) ** (-np.arange(0, ROPE_HALF, 2, dtype=np.float32) / np.float32(ROPE_HALF))).astype(np.float32)
    ar = row[:, None] * inv
    ac = col[:, None] * inv
    ang = np.concatenate([ar, ar, ac, ac], axis=-1).astype(np.float32)
    cos = np.cos(ang).astype(np.float32)
    sin = np.sin(ang).astype(np.float32)
    qw = ROPE_HALF // 2
    perm = np.zeros((HEAD_DIM, HEAD_DIM), np.float32)
    for k in range(qw):
        perm[qw + k, k] = -1.0
        perm[k, qw + k] = 1.0
        perm[3 * qw + k, 2 * qw + k] = -1.0
        perm[2 * qw + k, 3 * qw + k] = 1.0
    return jnp.asarray(cos), jnp.asarray(sin), jnp.asarray(np.concatenate([perm, perm], axis=0), BF16)


def _in_projection(h, w_in, q_gain, k_gain):
    tm, tn = PROJ_TOK_TILE, PROJ_COL_TILE
    n_prompt_tiles = N_PROMPT // tm
    cos, sin, perm2 = _rope_tables()
    table = pl.BlockSpec((DEC_SEQ, HEAD_DIM), lambda i, j: (0, 0))
    gain = pl.BlockSpec((1, HEAD_DIM), lambda i, j: (0, 0))
    state = pl.BlockSpec((tm, KV_WIDTH), lambda i, j: (jnp.minimum(i, n_prompt_tiles - 1), 0))
    return pl.pallas_call(
        _inproj_kernel,
        out_shape=(jax.ShapeDtypeStruct((N_TOK, IN_AB), F32),
                   jax.ShapeDtypeStruct((N_PROMPT, KV_WIDTH), F32),
                   jax.ShapeDtypeStruct((N_PROMPT, KV_WIDTH), F32)),
        grid=(N_TOK // tm, IN_AB // tn),
        in_specs=[pl.BlockSpec((tm, D_MODEL), lambda i, j: (i, 0)),
                  pl.BlockSpec((D_MODEL, tn), lambda i, j: (0, j)),
                  gain, gain, table, table, pl.BlockSpec((2 * HEAD_DIM, HEAD_DIM), lambda i, j: (0, 0))],
        out_specs=(pl.BlockSpec((tm, tn), lambda i, j: (i, j)), state, state),
        compiler_params=_params(("arbitrary", "arbitrary")),
        name="in_projection",
    )(h, w_in, q_gain, k_gain, cos, sin, perm2)


def _attn_kernel(*refs, has_ctx, stack):
    if has_ctx:
        q_ref, k_ref, v_ref, ck_ref, cv_ref, o_ref = refs
    else:
        q_ref, k_ref, v_ref, o_ref = refs
    scale = HEAD_DIM ** -0.5
    nt = (((1,), (1,)), ((), ()))
    tq = q_ref.shape[0]
    for hk in range(k_ref.shape[1] // HEAD_DIM):
        kcols = slice(hk * HEAD_DIM, (hk + 1) * HEAD_DIM)
        k = k_ref[:, kcols].astype(BF16)
        v = v_ref[:, kcols].astype(BF16)
        if has_ctx:
            ck = ck_ref[:, kcols].astype(BF16)
            cv = cv_ref[:, kcols].astype(BF16)
        for g0 in range(hk * Q_PER_KV, (hk + 1) * Q_PER_KV, stack):
            q = jnp.concatenate([q_ref[:, g * HEAD_DIM:(g + 1) * HEAD_DIM] for g in range(g0, g0 + stack)], axis=0)
            q = (q * scale).astype(BF16)
            s = lax.dot_general(q, k, nt, preferred_element_type=F32)
            m = jnp.max(s, axis=-1, keepdims=True)
            if has_ctx:
                sc = lax.dot_general(q, ck, nt, preferred_element_type=F32)
                m = jnp.maximum(m, jnp.max(sc, axis=-1, keepdims=True))
            p = jnp.exp(s - m)
            den = jnp.sum(p, axis=-1, keepdims=True)
            o = _bdot(p.astype(BF16), v)
            if has_ctx:
                pc = jnp.exp(sc - m)
                den = den + jnp.sum(pc, axis=-1, keepdims=True)
                o = o + _bdot(pc.astype(BF16), cv)
            o = (o / den).astype(o_ref.dtype)
            for g in range(stack):
                o_ref[:, (g0 + g) * HEAD_DIM:(g0 + g + 1) * HEAD_DIM] = o[g * tq:(g + 1) * tq]


def _attention(p, n_batch, seq, row0, tq, ctx=None):
    q_blocks = seq // tq
    kvh = N_KV_HEADS
    gw = kvh * Q_PER_KV * HEAD_DIM
    kw = kvh * HEAD_DIM
    in_specs = [
        pl.BlockSpec((tq, gw), lambda b, h, qi: (row0 // tq + b * q_blocks + qi, h)),
        pl.BlockSpec((seq, kw), lambda b, h, qi: (row0 // seq + b, COL_K // kw + h)),
        pl.BlockSpec((seq, kw), lambda b, h, qi: (row0 // seq + b, COL_V // kw + h)),
    ]
    args = [p, p, p]
    if ctx is not None:
        ctx_spec = pl.BlockSpec((None, PAST_LEN, kw), lambda b, h, qi: (b, 0, h))
        in_specs += [ctx_spec, ctx_spec]
        args += [ctx[0], ctx[1]]
    return pl.pallas_call(
        functools.partial(_attn_kernel, has_ctx=ctx is not None, stack=Q_PER_KV if ctx is None else 1),
        out_shape=jax.ShapeDtypeStruct((n_batch * seq, ATT_WIDTH), BF16),
        grid=(n_batch, N_KV_HEADS // kvh, q_blocks),
        in_specs=in_specs,
        out_specs=pl.BlockSpec((tq, gw), lambda b, h, qi: (b * q_blocks + qi, h)),
        compiler_params=_params(("arbitrary", "arbitrary", "arbitrary")),
        name="attention_latent" if ctx is not None else "attention_prompt",
    )(*args)


def _hgrn_constants():
    c = HG_CHUNK
    t = np.arange(c)
    cums, sels, pairs, scans, diags = [], [], [], [], []
    for d in range(2):
        pos = t if d == 0 else c - 1 - t
        pu, pt = pos[None, :], pos[:, None]
        ms, ss, ws = [], [], []
        for l in range(HG_LEVELS):
            m = c >> l
            blk = pos // m
            mid = (blk * m + m // 2)[:, None]
            late = ((pos % m) >= m // 2)
            ms.append(np.where(late[:, None], (pu >= mid) & (pu <= pt), (pu > pt) & (pu < mid)))
            ss.append(np.broadcast_to(late[:, None], (c, c)))
            ws.append((blk[:, None] == blk[None, :]) & late[:, None] & ~late[None, :])
        ms.append(pu <= pt)
        ms.append(pu > pt)
        cums.append(np.concatenate(ms, axis=0))
        sels.append(np.stack(ss))
        pairs.append(np.stack(ws))
        scans.append(np.concatenate([pu <= pt, pu <= pt], axis=1))
        diags.append(((pos // (c // 2))[:, None] == (pos // (c // 2))[None, :]) & (pu <= pt))
    return (jnp.asarray(np.stack(cums), BF16), jnp.asarray(np.stack(sels), F32),
            jnp.asarray(np.stack(pairs), F32), jnp.asarray(np.stack(scans), BF16),
            jnp.asarray(np.stack(diags), F32))


def _hgrn_kernel(*refs, n_seqs, n_chunks, has_s0):
    (hq_ref, zf_ref, zb_ref, hi_ref, hg_ref, lb_ref, og_ref, cum_ref, sel_ref, pair_ref, scan_ref,
     diag_ref) = refs[:12]
    refs = refs[12:]
    if has_s0:
        s0f_ref, s0b_ref, o_ref = refs[:3]
        refs = refs[3:]
    else:
        o_ref, sf_ref, sb_ref = refs[:3]
        refs = refs[3:]
    st_ref, acc_ref = refs
    hb = st_ref.shape[0]
    c = HG_CHUNK
    half = c // 2
    nt = (((1,), (1,)), ((), ()))
    tn = (((0,), (0,)), ((), ()))

    def direction(d, z_ref):
        raw = lb_ref[d]
        e = jnp.exp(raw - jnp.max(raw, axis=0, keepdims=True))
        lb = e[0:1] / jnp.sum(e, axis=0, keepdims=True)

        def initial_state(s, hh):
            if has_s0:
                return (s0f_ref, s0b_ref)[d][s, hh].T
            return jnp.zeros((HG_DV, HG_DK), F32)

        def gates(rows):
            f = lb + (1.0 - lb) * _sigmoid(z_ref[rows, :])
            logf = jnp.log(f)
            hi16 = logf.astype(BF16)
            lo16 = (logf - hi16.astype(F32)).astype(BF16)
            return f, hi16, lo16

        def row(p):
            t = p if d == 0 else c - 1 - p
            return slice(t, t + 1)

        early, late = (slice(0, half), slice(half, c)) if d == 0 else (slice(half, c), slice(0, half))

        def in_row_order(x_early, x_late):
            return jnp.concatenate([x_early, x_late] if d == 0 else [x_late, x_early], axis=0)

        def emit(rows, cols, o):
            if d == 0:
                acc_ref[rows, cols] = o
            else:
                tot = acc_ref[rows, cols] + o
                o_ref[rows, cols] = (_rms(tot, og_ref[...]) * hg_ref[rows, cols]).astype(o_ref.dtype)

        def two_level_operands(rows):
            f, hi16, lo16 = gates(rows)
            b = _bdot(scan_ref[d], jnp.concatenate([hi16, lo16], axis=0))
            kk = 1.0 - f
            q = hq_ref[rows, :]
            r_mid = b[row(half - 1)]
            x1 = in_row_order(kk[early] * jnp.exp(r_mid - b[early]), q[late] * jnp.exp(b[late] - r_mid)).astype(BF16)
            dq = in_row_order(b[early] - b[row(half // 2 - 1)], b[late] - b[row(half + half // 2 - 1)])
            b_end = b[row(c - 1)]
            span = jnp.maximum(
                jnp.maximum(b[row(0)] - b[row(half // 2 - 1)], b[row(half // 2 - 1)] - b[row(half - 1)]),
                jnp.maximum(b[row(half)] - b[row(half + half // 2 - 1)],
                            b[row(half + half // 2 - 1)] - b[row(c - 1)]))
            return dict(x1=x1, xq=(q * jnp.exp(dq)).astype(BF16), xk=(kk * jnp.exp(-dq)).astype(BF16),
                        q_in=(q * jnp.exp(b)).astype(BF16), k_out=(kk * jnp.exp(b_end - b)).astype(BF16),
                        a_end=jnp.exp(b_end), iv=hi_ref[rows, :].astype(BF16), span=span)

        def two_level_chunk(rows, ops, states, need_state):
            new_states = []
            for hh in range(hb):
                cols = slice(hh * HG_DK, (hh + 1) * HG_DK)
                g1 = _bdot(ops["x1"][:, cols], ops["x1"][:, cols].astype(F32).T.astype(BF16))
                g2 = _bdot(ops["xq"][:, cols], ops["xk"][:, cols].astype(F32).T.astype(BF16))
                att = jnp.where(pair_ref[d, 0] > 0.5, g1, jnp.where(diag_ref[d] > 0.5, g2, 0.0)).astype(BF16)
                st = states[hh]
                o = _bdot(att, ops["iv"][:, cols])
                if st is not None:
                    o = o + lax.dot_general(ops["q_in"][:, cols], st.astype(BF16), nt, preferred_element_type=F32)
                pending.append((rows, cols, o))
                if not need_state:
                    new_states.append(None)
                    continue
                dst = lax.dot_general(ops["iv"][:, cols], ops["k_out"][:, cols], tn, preferred_element_type=F32)
                new_states.append(dst if st is None else ops["a_end"][:, cols] * st + dst)
            return new_states

        def all_levels(rows):
            f, hi16, lo16 = gates(rows)
            cum = cum_ref[d]
            eall = jnp.exp(_bdot(cum, hi16) + _bdot(cum, lo16))
            for hh in range(hb):
                cols = slice(hh * HG_DK, (hh + 1) * HG_DK)
                kk = 1.0 - f[:, cols]
                q = hq_ref[rows, cols]
                iv = hi_ref[rows, cols]
                iv16 = iv.astype(BF16)
                att = jnp.zeros((c, c), F32)
                for l in range(HG_LEVELS):
                    x = (kk + sel_ref[d, l] * (q - kk)) * eall[l * c:(l + 1) * c, cols]
                    xb = x.astype(BF16)
                    att = att + pair_ref[d, l] * lax.dot_general(xb, xb, nt, preferred_element_type=F32)
                e_in = eall[HG_LEVELS * c:(HG_LEVELS + 1) * c, cols]
                e_out = eall[(HG_LEVELS + 1) * c:, cols]
                st = st_ref[hh]
                o = (_bdot(att.astype(BF16), iv16)
                     + jnp.sum(q * kk, axis=-1, keepdims=True) * iv
                     + lax.dot_general((q * e_in).astype(BF16), st.astype(BF16), nt, preferred_element_type=F32))
                dst = lax.dot_general(iv16, (kk * e_out).astype(BF16), tn, preferred_element_type=F32)
                st_ref[hh] = (e_in[0:1] * e_out[0:1]) * st + dst
                emit(rows, cols, o)

        def chunk_start(s, ci):
            cidx = ci if d == 0 else n_chunks - 1 - ci
            return (s * n_chunks + cidx) * c

        worst = jnp.zeros((1, hb * HG_DK), F32)
        pending = []
        states = [[initial_state(s, hh) if has_s0 else None for hh in range(hb)] for s in range(n_seqs)]
        for ci in range(n_chunks):
            for s in range(n_seqs):
                rows = slice(chunk_start(s, ci), chunk_start(s, ci) + c)
                ops = two_level_operands(rows)
                states[s] = two_level_chunk(rows, ops, states[s], need_state=ci < n_chunks - 1 or not has_s0)
                worst = jnp.maximum(worst, ops["span"])
        for item in pending:
            emit(*item)
        if not has_s0:
            for s in range(n_seqs):
                for hh in range(hb):
                    (sf_ref, sb_ref)[d][s, hh] = states[s][hh].T

        @pl.when(jnp.logical_not(jnp.max(worst) <= HG_SAFE_EXPONENT))
        def _():
            for s in range(n_seqs):
                for hh in range(hb):
                    st_ref[hh] = initial_state(s, hh)

                def chunk(ci, carry):
                    all_levels(pl.ds(pl.multiple_of(chunk_start(s, ci), c), c))
                    return carry

                lax.fori_loop(0, n_chunks, chunk, 0)
                if not has_s0:
                    for hh in range(hb):
                        (sf_ref, sb_ref)[d][s, hh] = st_ref[hh].T

    direction(0, zf_ref)
    direction(1, zb_ref)


def _hgrn(p, lb_raw, o_gain, consts, n_batch, seq, row0, n_seqs, s0=None):
    hb = HG_HEAD_BLOCK
    w = hb * HG_DK
    rows = n_seqs * seq

    def seg(col):
        return pl.BlockSpec((rows, w), lambda b, h: (row0 // rows + b, col // w + h))

    def const(a):
        return pl.BlockSpec(a.shape, lambda b, h: (0,) * a.ndim)

    in_specs = [seg(COL_HQ), seg(COL_ZF), seg(COL_ZF + HG_KW), seg(COL_HI), seg(COL_HG),
                pl.BlockSpec((2, DEPTH + 1, w), lambda b, h: (0, 0, h)),
                pl.BlockSpec((1, HG_DV), lambda b, h: (0, 0))] + [const(a) for a in consts]
    args = [p, p, p, p, p, lb_raw, o_gain, *consts]
    has_s0 = s0 is not None
    st_spec = pl.BlockSpec((n_seqs, hb, HG_DK, HG_DV), lambda b, h: (b, h, 0, 0))
    if has_s0:
        in_specs += [st_spec, st_spec]
        args += [s0[0], s0[1]]
    out_shape = [jax.ShapeDtypeStruct((n_batch * seq, HG_VW), BF16)]
    out_specs = [pl.BlockSpec((rows, w), lambda b, h: (b, h))]
    if not has_s0:
        st_shape = jax.ShapeDtypeStruct((n_batch, HG_HEADS, HG_DK, HG_DV), F32)
        out_shape += [st_shape, st_shape]
        out_specs += [st_spec, st_spec]
    return pl.pallas_call(
        functools.partial(_hgrn_kernel, n_seqs=n_seqs, n_chunks=seq // HG_CHUNK, has_s0=has_s0),
        out_shape=tuple(out_shape),
        grid=(n_batch // n_seqs, HG_HEADS // hb),
        in_specs=in_specs,
        out_specs=tuple(out_specs),
        scratch_shapes=[pltpu.VMEM((hb, HG_DV, HG_DK), F32), pltpu.VMEM((rows, w), F32)],
        compiler_params=_params(("arbitrary", "arbitrary")),
        name="hgrn_latent" if has_s0 else "hgrn_prompt",
    )(*args)


def _outproj_kernel(attp_ref, atts_ref, hgp_ref, hgs_ref, wa_ref, wb_ref, xp_ref, xs_ref, g1_ref, gain_ref,
                    sh_ref, sc_ref, x1_ref, h2_ref, full_ref, wcache_ref):
    i = pl.program_id(0)
    n = pl.program_id(1)

    @pl.when(i == 0)
    def _():
        wcache_ref[n, 0] = wa_ref[...].astype(BF16)
        wcache_ref[n, 1] = wb_ref[...].astype(BF16)

    def run(att_ref, hg_ref, x_ref):
        acc = _bdot(att_ref[...], wcache_ref[n, 0]) + _bdot(hg_ref[...], wcache_ref[n, 1])
        x1 = x_ref[...] + g1_ref[...] * acc
        x1_ref[...] = x1
        full_ref[n] = x1

    pl.when(i < N_PROMPT_TILES)(lambda: run(attp_ref, hgp_ref, xp_ref))
    pl.when(i >= N_PROMPT_TILES)(lambda: run(atts_ref, hgs_ref, xs_ref))

    @pl.when(n == D_MODEL // OUT_COL_TILE - 1)
    def _():
        nt = D_MODEL // OUT_COL_TILE
        ms = jnp.sum(full_ref[0] * full_ref[0], axis=-1, keepdims=True)
        for k in range(1, nt):
            ms = ms + jnp.sum(full_ref[k] * full_ref[k], axis=-1, keepdims=True)
        rstd = lax.rsqrt(ms / D_MODEL + EPS)
        gs = gain_ref[...] * (1.0 + sc_ref[...])
        for k in range(nt):
            cols = slice(k * OUT_COL_TILE, (k + 1) * OUT_COL_TILE)
            h2_ref[:, cols] = (full_ref[k] * rstd * gs[:, cols] + sh_ref[:, cols]).astype(BF16)


def _out_projection(att_p, att_s, hg_p, hg_s, w_out, xp, xs, mods, norm_mlp):
    tm, tn = TOK_TILE, OUT_COL_TILE
    nt = D_MODEL // tn

    def prompt_rows(width):
        return pl.BlockSpec((tm, width), lambda i, n: (jnp.minimum(i, N_PROMPT_TILES - 1), 0),
                            pipeline_mode=pl.Buffered(1))

    def sample_rows(width):
        return pl.BlockSpec((tm, width), lambda i, n: (jnp.maximum(i - N_PROMPT_TILES, 0), 0),
                            pipeline_mode=pl.Buffered(1))

    return pl.pallas_call(
        _outproj_kernel,
        out_shape=(jax.ShapeDtypeStruct((N_TOK, D_MODEL), F32), jax.ShapeDtypeStruct((N_TOK, D_MODEL), BF16)),
        grid=(N_TILES, nt),
        in_specs=[prompt_rows(ATT_WIDTH), sample_rows(ATT_WIDTH), prompt_rows(HG_VW), sample_rows(HG_VW),
                  pl.BlockSpec((ATT_WIDTH, tn), lambda i, n: (0, jnp.where(i == 0, n, nt - 1))),
                  pl.BlockSpec((HG_VW, tn), lambda i, n: (1, jnp.where(i == 0, n, nt - 1))),
                  pl.BlockSpec((tm, tn), lambda i, n: (jnp.minimum(i, N_PROMPT_TILES - 1),
                                                       jnp.where(i < N_PROMPT_TILES, n, nt - 1))),
                  pl.BlockSpec((tm, tn), lambda i, n: (jnp.maximum(i - N_PROMPT_TILES, 0),
                                                       jnp.where(i < N_PROMPT_TILES, 0, n))),
                  _mod_spec(0, 2, tm, width=tn, col=lambda i, n: n),
                  pl.BlockSpec((1, D_MODEL), lambda i, n: (0, 0)),
                  _mod_spec(0, 3, tm), _mod_spec(0, 4, tm)],
        out_specs=(pl.BlockSpec((tm, tn), lambda i, n: (i, n)),
                   pl.BlockSpec((tm, D_MODEL), lambda i, n: (i, 0))),
        scratch_shapes=[pltpu.VMEM((nt, tm, tn), F32), pltpu.VMEM((nt, 2, ATT_WIDTH, tn), BF16)],
        compiler_params=_params(("arbitrary", "arbitrary")),
        name="out_projection",
    )(att_p, att_s, hg_p, hg_s, w_out, w_out, xp, xs, mods, norm_mlp, mods, mods)


def _mlp_kernel(h_ref, w1_ref, w2_ref, x_ref, g2_ref, *rest, final):
    if final:
        fin_ref, o_ref, res_ref = rest
    else:
        o_ref, res_ref = rest
    j = pl.program_id(1)
    res_ref[j] = x_ref[...]

    def step(first, last):
        a = jnp.square(jnp.maximum(_bdot(h_ref[...], w1_ref[...].astype(BF16)), 0.0)).astype(BF16)
        for n in range(D_MODEL // MLP_OUT_CHUNK):
            cols = slice(n * MLP_OUT_CHUNK, (n + 1) * MLP_OUT_CHUNK)
            p = _bdot(a, w2_ref[:, cols].astype(BF16))
            if not first:
                p = o_ref[:, cols] + p
            if last:
                per = MLP_OUT_CHUNK // MLP_RES_COLS
                res = jnp.concatenate([res_ref[n * per + k] for k in range(per)], axis=1)
                p = res + g2_ref[:, cols] * p
            o_ref[:, cols] = p
        if last and final:
            o_ref[...] = _rms(o_ref[...], fin_ref[...])

    pl.when(j == 0)(lambda: step(True, False))
    pl.when(jnp.logical_and(j > 0, j < MLP_STEPS - 1))(lambda: step(False, False))
    pl.when(j == MLP_STEPS - 1)(lambda: step(False, True))


def _mlp(h, x, w1, w2, mods, layer, tile0, n_tiles, final_norm=None):
    tm, th = TOK_TILE, FF_TILE
    final = final_norm is not None
    in_specs = [pl.BlockSpec((tm, D_MODEL), lambda i, j: (tile0 + i, 0), pipeline_mode=pl.Buffered(1)),
                pl.BlockSpec((None, D_MODEL, th), lambda i, j: (layer, 0, j)),
                pl.BlockSpec((None, th, D_MODEL), lambda i, j: (layer, j, 0)),
                pl.BlockSpec((tm, MLP_RES_COLS), lambda i, j: (tile0 + i, j)),
                _mod_spec(layer, 5, tm, tile_of=lambda i, j: tile0 + i)]
    args = [h, w1, w2, x, mods]
    if final:
        in_specs.append(pl.BlockSpec((1, D_MODEL), lambda i, j: (0, 0)))
        args.append(final_norm)
    return pl.pallas_call(
        functools.partial(_mlp_kernel, final=final),
        out_shape=jax.ShapeDtypeStruct((n_tiles * tm, D_MODEL), F32),
        grid=(n_tiles, MLP_STEPS),
        in_specs=in_specs,
        out_specs=pl.BlockSpec((tm, D_MODEL), lambda i, j: (i, 0)),
        scratch_shapes=[pltpu.VMEM((MLP_STEPS, tm, MLP_RES_COLS), F32)],
        compiler_params=_params(("arbitrary", "arbitrary")),
        name="mlp_final" if final else "mlp",
    )(*args)


def _pool_kernel(x_ref, wp_ref, ps_ref, gain1_ref, sh1_ref, sc1_ref, g1_ref, gain2_ref, sh2_ref, sc2_ref,
                 x3_ref, h4_ref):
    i = pl.program_id(0)
    tm = TOK_TILE
    rstd = lax.rsqrt(jnp.mean(x_ref[...] * x_ref[...], axis=-1, keepdims=True) + EPS)
    gs1 = gain1_ref[...] * (1.0 + sc1_ref[...])
    mix_gain = g1_ref[...] * ps_ref[...]

    def widen(m):
        return jnp.concatenate([m] * (POOL_GROUP // HEAD_DIM), axis=1)

    def mix_tile(seq):
        n_seq = tm // seq
        pitch = seq + 2 * POOL_HALO
        n_pad = n_seq * pitch
        halo = jnp.zeros((POOL_HALO, POOL_GROUP), F32)
        pos = lax.broadcasted_iota(jnp.int32, (seq, HEAD_DIM), 0)

        def down(a, k):
            return pltpu.roll(a, k % n_pad, 0)

        for g, w in enumerate(POOL_WINDOWS):
            half = w // 2
            cols = slice(g * POOL_GROUP, (g + 1) * POOL_GROUP)
            x = x_ref[:, cols]
            h = x * rstd * gs1[:, cols] + sh1_ref[:, cols]
            padded = jnp.concatenate(
                [piece for s in range(n_seq) for piece in (halo, h[s * seq:(s + 1) * seq], halo)], axis=0)
            back = padded
            m = 1
            while m < half:
                back = back + down(back, m)
                m *= 2
            ahead = back if half == 1 else down(back, -(half - 1))
            total = down(back, 1) + ahead
            total = jnp.concatenate(
                [total[s * pitch + POOL_HALO:s * pitch + POOL_HALO + seq] for s in range(n_seq)], axis=0)
            count = (jnp.minimum(pos + (w - half), seq) - jnp.maximum(pos - half, 0)).astype(F32)
            inv = jnp.concatenate([widen(1.0 / count)] * n_seq, axis=0)
            pooled = (total * inv - h).astype(BF16)
            mix = _bdot(pooled, wp_ref[g].astype(BF16))
            x3_ref[:, cols] = x + mix_gain[:, cols] * mix

    pl.when(i < N_PROMPT_TILES)(lambda: mix_tile(SEQ))
    pl.when(i >= N_PROMPT_TILES)(lambda: mix_tile(DEC_SEQ))
    x3 = x3_ref[...]
    rstd3 = lax.rsqrt(jnp.mean(x3 * x3, axis=-1, keepdims=True) + EPS)
    h4_ref[...] = (x3 * rstd3 * (gain2_ref[...] * (1.0 + sc2_ref[...])) + sh2_ref[...]).astype(BF16)


def _pool_mixer(x, w_pool, pool_scale, norm_mix, norm_mlp, mods):
    tm = TOK_TILE
    vec = pl.BlockSpec((1, D_MODEL), lambda i: (0, 0))
    tile = pl.BlockSpec((tm, D_MODEL), lambda i: (i, 0))
    return pl.pallas_call(
        _pool_kernel,
        out_shape=(jax.ShapeDtypeStruct((N_TOK, D_MODEL), F32), jax.ShapeDtypeStruct((N_TOK, D_MODEL), BF16)),
        grid=(N_TILES,),
        in_specs=[tile, pl.BlockSpec((len(POOL_WINDOWS), POOL_GROUP, POOL_GROUP), lambda i: (0, 0, 0)),
                  vec, vec, _mod_spec(1, 0, tm), _mod_spec(1, 1, tm), _mod_spec(1, 2, tm),
                  vec, _mod_spec(1, 3, tm), _mod_spec(1, 4, tm)],
        out_specs=(tile, tile),
        compiler_params=_params(("arbitrary",)),
        name="pool_mixer",
    )(x, w_pool, pool_scale, norm_mix, mods, mods, mods, norm_mlp, mods, mods)


def kernel(x_prompt, x_sample, cache_k, cache_v, state_hgrn_fwd, state_hgrn_bwd, c, c_ctx, w_ada, b_ada,
           norm_mix, norm_mlp, w_in_ab, w_out_ab, q_norm, k_norm, hg_norm, lb_raw, w_pool, pool_scale,
           w_mlp_in, w_mlp_out, final_norm):
    xp = x_prompt.reshape(N_PROMPT, D_MODEL)
    xs = x_sample.reshape(N_SAMPLE, D_MODEL)
    cv = jnp.concatenate([c_ctx[None, :], c, jnp.zeros((ADA_ROWS - 1 - DEC_BATCH, D_MODEL), F32)], axis=0)
    mods, h0 = _ada_table_and_modulate(cv, w_ada, b_ada, xp, xs, norm_mix)
    mods = mods.reshape(DEPTH, ADA_ROWS, 1, N_MOD * D_MODEL)

    proj, new_k, new_v = _in_projection(h0, w_in_ab[0], q_norm[0:1], k_norm[0:1])
    ctx_k = cache_k[:, 0].reshape(DEC_BATCH, PAST_LEN, KV_WIDTH)
    ctx_v = cache_v[:, 0].reshape(DEC_BATCH, PAST_LEN, KV_WIDTH)
    att_p = _attention(proj, BATCH, SEQ, 0, SEQ)
    att_s = _attention(proj, DEC_BATCH, DEC_SEQ, N_PROMPT, 512, ctx=(ctx_k, ctx_v))
    consts = _hgrn_constants()
    hg_p, s_fwd, s_bwd = _hgrn(proj, lb_raw, hg_norm[0:1], consts, BATCH, SEQ, 0, HG_PROMPT_SEQS)
    s0 = (state_hgrn_fwd.reshape(DEC_BATCH, HG_HEADS, HG_DK, HG_DV),
          state_hgrn_bwd.reshape(DEC_BATCH, HG_HEADS, HG_DK, HG_DV))
    (hg_s,) = _hgrn(proj, lb_raw, hg_norm[0:1], consts, DEC_BATCH, DEC_SEQ, N_PROMPT, 1, s0=s0)
    x1, h2 = _out_projection(att_p, att_s, hg_p, hg_s, w_out_ab[0], xp, xs, mods, norm_mlp[0:1])
    x2 = _mlp(h2, x1, w_mlp_in, w_mlp_out, mods, 0, 0, N_TILES)

    x3, h4 = _pool_mixer(x2, w_pool[0], pool_scale[0:1], norm_mix[1:2], norm_mlp[1:2], mods)
    fin = final_norm[None, :]
    y_prompt = _mlp(h4, x3, w_mlp_in, w_mlp_out, mods, 1, 0, N_PROMPT_TILES, final_norm=fin)
    y_sample = _mlp(h4, x3, w_mlp_in, w_mlp_out, mods, 1, N_PROMPT_TILES, N_TILES - N_PROMPT_TILES,
                    final_norm=fin)

    return (y_prompt.reshape(BATCH, SEQ, D_MODEL), y_sample.reshape(DEC_BATCH, DEC_SEQ, D_MODEL),
            new_k.reshape(BATCH, 1, SEQ, N_KV_HEADS, HEAD_DIM), new_v.reshape(BATCH, 1, SEQ, N_KV_HEADS, HEAD_DIM),
            s_fwd.reshape(BATCH, 1, HG_HEADS, HG_DK, HG_DV), s_bwd.reshape(BATCH, 1, HG_HEADS, HG_DK, HG_DV))
```

```python
import functools

import numpy as np
import jax
import jax.numpy as jnp
from jax import lax
from jax.experimental import pallas as pl
from jax.experimental.pallas import tpu as pltpu

F32 = jnp.float32
BF16 = jnp.bfloat16

D_MODEL = 2048
BATCH = 16
SEQ = 256
DEPTH = 2
DEC_BATCH = 2
DEC_SEQ = 1024
PAST_LEN = 256
GRID_W = 64
HEAD_DIM = 128
N_Q_HEADS = 8
N_KV_HEADS = 2
Q_PER_KV = N_Q_HEADS // N_KV_HEADS
ATT_WIDTH = N_Q_HEADS * HEAD_DIM
KV_WIDTH = N_KV_HEADS * HEAD_DIM
HG_HEADS = 8
HG_DK = 128
HG_DV = 128
HG_KW = HG_HEADS * HG_DK
HG_VW = HG_HEADS * HG_DV
IN_AB = ATT_WIDTH + 2 * KV_WIDTH + 3 * HG_KW + 2 * HG_VW
MIX_WIDTH = ATT_WIDTH + HG_VW
POOL_WINDOWS = (2, 4, 8, 16)
POOL_GROUP = D_MODEL // len(POOL_WINDOWS)
POOL_HALO = 8
D_FF = 4 * D_MODEL
ROPE_THETA = 10000.0
ROPE_HALF = HEAD_DIM // 2
EPS = 1e-6
N_MOD = 6

N_PROMPT = BATCH * SEQ
N_SAMPLE = DEC_BATCH * DEC_SEQ
N_TOK = N_PROMPT + N_SAMPLE
ADA_ROWS = 16
ADA_COL_TILE = 1024
ADA_KEEP = 2 * D_MODEL // ADA_COL_TILE
MOD0_TOK_TILE = 512
TOK_TILE = 1024
N_TILES = N_TOK // TOK_TILE
N_PROMPT_TILES = N_PROMPT // TOK_TILE
PROJ_TOK_TILE = 2048
PROJ_COL_TILE = 512
OUT_COL_TILE = 512
FF_TILE = 512
MLP_STEPS = D_FF // FF_TILE
MLP_RES_COLS = D_MODEL // MLP_STEPS
MLP_OUT_CHUNK = 512
HG_CHUNK = 128
HG_LEVELS = 7
HG_HEAD_BLOCK = 4
HG_PROMPT_SEQS = 2
HG_SAFE_EXPONENT = 80.0
V7X_VMEM_BYTES = 64 * 2 ** 20
VMEM_LIMIT = V7X_VMEM_BYTES - 8 * 2 ** 20

COL_Q = 0
COL_K = ATT_WIDTH
COL_V = COL_K + KV_WIDTH
COL_HQ = COL_V + KV_WIDTH
COL_ZF = COL_HQ + HG_KW
COL_HI = COL_ZF + 2 * HG_KW
COL_HG = COL_HI + HG_VW


def _params(semantics):
    return pltpu.CompilerParams(dimension_semantics=semantics, vmem_limit_bytes=VMEM_LIMIT)


def _sigmoid(x):
    return 1.0 / (1.0 + jnp.exp(-x))


def _silu(x):
    return x * _sigmoid(x)


def _rms(x, gain):
    return x * lax.rsqrt(jnp.mean(x * x, axis=-1, keepdims=True) + EPS) * gain


def _bdot(a, b):
    return jnp.dot(a, b, preferred_element_type=F32)


def _mod_row(tile, tile_rows):
    first = N_PROMPT // tile_rows
    per_seq = DEC_SEQ // tile_rows
    return jnp.where(tile < first, 0, 1 + (tile - first) // per_seq)


def _mod_spec(layer, chunk, tile_rows, width=D_MODEL, col=lambda *g: 0, tile_of=lambda *g: g[0]):
    per = D_MODEL // width
    return pl.BlockSpec((None, None, 1, width),
                        lambda *g: (layer, _mod_row(tile_of(*g), tile_rows), 0, chunk * per + col(*g)))


def _modulate(x, gain, shift, scale):
    return x * lax.rsqrt(jnp.mean(x * x, axis=-1, keepdims=True) + EPS) * (gain * (1.0 + scale)) + shift


def _ada_kernel(cv_ref, w_ref, b_ref, xp_ref, xs_ref, gain_ref, o_ref, h_ref, keep_ref):
    s = pl.program_id(0)
    tile = _bdot(_silu(cv_ref[...]).astype(BF16), w_ref[...].astype(BF16)) + b_ref[...]
    o_ref[:, 0, :] = tile

    @pl.when(s < ADA_KEEP)
    def _():
        keep_ref[s] = tile

    t = s - ADA_KEEP
    n_prompt = N_PROMPT // MOD0_TOK_TILE

    def run(x_ref):
        r = _mod_row(t, MOD0_TOK_TILE)
        per = D_MODEL // ADA_COL_TILE
        shift = jnp.concatenate([keep_ref[k, pl.ds(r, 1), :] for k in range(per)], axis=1)
        scale = jnp.concatenate([keep_ref[per + k, pl.ds(r, 1), :] for k in range(per)], axis=1)
        h_ref[...] = _modulate(x_ref[...], gain_ref[...], shift, scale).astype(BF16)

    pl.when(jnp.logical_and(t >= 0, t < n_prompt))(lambda: run(xp_ref))
    pl.when(jnp.logical_and(t >= n_prompt, t < N_TOK // MOD0_TOK_TILE))(lambda: run(xs_ref))


def _ada_table_and_modulate(cv, w_ada, b_ada, xp, xs, norm_mix):
    tn, tm = ADA_COL_TILE, MOD0_TOK_TILE
    n = N_MOD * D_MODEL
    per_layer = n // tn
    n_prompt = N_PROMPT // tm
    n_tiles = N_TOK // tm
    assert DEPTH * per_layer >= ADA_KEEP + n_tiles

    def tok(s):
        return jnp.clip(s - ADA_KEEP, 0, n_tiles - 1)

    return pl.pallas_call(
        _ada_kernel,
        out_shape=(jax.ShapeDtypeStruct((DEPTH, ADA_ROWS, 1, n), F32), jax.ShapeDtypeStruct((N_TOK, D_MODEL), BF16)),
        grid=(DEPTH * per_layer,),
        in_specs=[pl.BlockSpec((ADA_ROWS, D_MODEL), lambda s: (0, 0)),
                  pl.BlockSpec((None, D_MODEL, tn), lambda s: (s // per_layer, 0, s % per_layer)),
                  pl.BlockSpec((None, 1, tn), lambda s: (s // per_layer, 0, s % per_layer)),
                  pl.BlockSpec((tm, D_MODEL), lambda s: (jnp.minimum(tok(s), n_prompt - 1), 0)),
                  pl.BlockSpec((tm, D_MODEL), lambda s: (jnp.maximum(tok(s) - n_prompt, 0), 0)),
                  pl.BlockSpec((1, D_MODEL), lambda s: (0, 0))],
        out_specs=(pl.BlockSpec((None, ADA_ROWS, 1, tn), lambda s: (s // per_layer, 0, 0, s % per_layer)),
                   pl.BlockSpec((tm, D_MODEL), lambda s: (tok(s), 0))),
        scratch_shapes=[pltpu.VMEM((ADA_KEEP, ADA_ROWS, tn), F32)],
        compiler_params=_params(("arbitrary",)),
        name="ada_table_modulate0",
    )(cv, w_ada, b_ada.reshape(DEPTH, 1, n), xp, xs, norm_mix[0:1])


def _rope(y, cos, sin, perm2):
    hi = y.astype(BF16)
    lo = (y - hi.astype(F32)).astype(BF16)
    rot = _bdot(jnp.concatenate([hi, lo], axis=1), perm2)
    return y * cos + rot * sin


def _inproj_kernel(h_ref, w_ref, qg_ref, kg_ref, cos_ref, sin_ref, perm_ref, p_ref, nk_ref, nv_ref):
    i = pl.program_id(0)
    j = pl.program_id(1)
    latent = i >= N_PROMPT // PROJ_TOK_TILE
    heads = PROJ_COL_TILE // HEAD_DIM
    kv_tile = COL_K // PROJ_COL_TILE
    is_q = j < kv_tile

    def attention_tile(rope):
        gain = jnp.where(is_q, qg_ref[...], kg_ref[...])
        w = w_ref[...].astype(BF16)
        for s in range(PROJ_TOK_TILE // DEC_SEQ):
            rows = slice(s * DEC_SEQ, (s + 1) * DEC_SEQ)
            acc = _bdot(h_ref[rows, :], w)
            for hh in range(heads):
                cols = slice(hh * HEAD_DIM, (hh + 1) * HEAD_DIM)
                x = acc[:, cols]
                y = _rms(x, gain)
                if rope:
                    y = _rope(y, cos_ref[...], sin_ref[...], perm_ref[...])
                if hh >= N_KV_HEADS:
                    y = jnp.where(is_q, y, x)
                p_ref[rows, cols] = y

    pl.when(jnp.logical_and(j <= kv_tile, latent))(lambda: attention_tile(True))
    pl.when(jnp.logical_and(j <= kv_tile, jnp.logical_not(latent)))(lambda: attention_tile(False))

    @pl.when(jnp.logical_and(j == kv_tile, jnp.logical_not(latent)))
    def _():
        for s in range(PROJ_TOK_TILE // SEQ):
            rows = slice(s * SEQ, (s + 1) * SEQ)
            for hh in range(N_KV_HEADS):
                nk_ref[s, :, hh, :] = p_ref[rows, hh * HEAD_DIM:(hh + 1) * HEAD_DIM]
                nv_ref[s, :, hh, :] = p_ref[rows, KV_WIDTH + hh * HEAD_DIM:KV_WIDTH + (hh + 1) * HEAD_DIM]

    def plain_tile(act):
        w = w_ref[...].astype(BF16)
        for s in range(PROJ_TOK_TILE // DEC_SEQ):
            rows = slice(s * DEC_SEQ, (s + 1) * DEC_SEQ)
            p_ref[rows, :] = act(_bdot(h_ref[rows, :], w))

    is_gate = jnp.logical_and(j >= COL_ZF // PROJ_COL_TILE, j < COL_HI // PROJ_COL_TILE)
    is_value = jnp.logical_and(j >= COL_HI // PROJ_COL_TILE, j < COL_HG // PROJ_COL_TILE)
    raw = jnp.logical_or(is_gate, is_value)
    pl.when(raw)(lambda: plain_tile(lambda a: a))
    pl.when(jnp.logical_and(j > kv_tile, jnp.logical_not(raw)))(lambda: plain_tile(_silu))


def _rope_tables():
    t = np.arange(DEC_SEQ)
    row = (t // GRID_W).astype(np.float32)
    col = (t % GRID_W).astype(np.float32)
    inv = (np.float32(ROPE_THETA) ** (-np.arange(0, ROPE_HALF, 2, dtype=np.float32) / np.float32(ROPE_HALF))).astype(np.float32)
    ar = row[:, None] * inv
    ac = col[:, None] * inv
    ang = np.concatenate([ar, ar, ac, ac], axis=-1).astype(np.float32)
    cos = np.cos(ang).astype(np.float32)
    sin = np.sin(ang).astype(np.float32)
    qw = ROPE_HALF // 2
    perm = np.zeros((HEAD_DIM, HEAD_DIM), np.float32)
    for k in range(qw):
        perm[qw + k, k] = -1.0
        perm[k, qw + k] = 1.0
        perm[3 * qw + k, 2 * qw + k] = -1.0
        perm[2 * qw + k, 3 * qw + k] = 1.0
    return jnp.asarray(cos), jnp.asarray(sin), jnp.asarray(np.concatenate([perm, perm], axis=0), BF16)


def _in_projection(h, w_in, q_gain, k_gain):
    tm, tn = PROJ_TOK_TILE, PROJ_COL_TILE
    n_prompt_tiles = N_PROMPT // tm
    cos, sin, perm2 = _rope_tables()
    table = pl.BlockSpec((DEC_SEQ, HEAD_DIM), lambda i, j: (0, 0))
    gain = pl.BlockSpec((1, HEAD_DIM), lambda i, j: (0, 0))
    state = pl.BlockSpec((tm // SEQ, None, SEQ, N_KV_HEADS, HEAD_DIM),
                         lambda i, j: (jnp.minimum(i, n_prompt_tiles - 1), 0, 0, 0, 0),
                         pipeline_mode=pl.Buffered(1))
    state_shape = jax.ShapeDtypeStruct((BATCH, 1, SEQ, N_KV_HEADS, HEAD_DIM), F32)
    return pl.pallas_call(
        _inproj_kernel,
        out_shape=(jax.ShapeDtypeStruct((N_TOK, IN_AB), F32), state_shape, state_shape),
        grid=(N_TOK // tm, IN_AB // tn),
        in_specs=[pl.BlockSpec((tm, D_MODEL), lambda i, j: (i, 0)),
                  pl.BlockSpec((D_MODEL, tn), lambda i, j: (0, j)),
                  gain, gain, table, table, pl.BlockSpec((2 * HEAD_DIM, HEAD_DIM), lambda i, j: (0, 0))],
        out_specs=(pl.BlockSpec((tm, tn), lambda i, j: (i, j)), state, state),
        compiler_params=_params(("arbitrary", "arbitrary")),
        name="in_projection",
    )(h, w_in, q_gain, k_gain, cos, sin, perm2)


def _attn_kernel(*refs, has_ctx, stack):
    if has_ctx:
        q_ref, k_ref, v_ref, ck_ref, cv_ref, o_ref = refs
    else:
        q_ref, k_ref, v_ref, o_ref = refs
    scale = HEAD_DIM ** -0.5
    nt = (((1,), (1,)), ((), ()))
    tq = q_ref.shape[0]
    for hk in range(k_ref.shape[1] // HEAD_DIM):
        kcols = slice(hk * HEAD_DIM, (hk + 1) * HEAD_DIM)
        k = k_ref[:, kcols].astype(BF16)
        v = v_ref[:, kcols].astype(BF16)
        if has_ctx:
            ck = ck_ref[:, hk, :].astype(BF16)
            cv = cv_ref[:, hk, :].astype(BF16)
        for g0 in range(hk * Q_PER_KV, (hk + 1) * Q_PER_KV, stack):
            q = jnp.concatenate([q_ref[:, g * HEAD_DIM:(g + 1) * HEAD_DIM] for g in range(g0, g0 + stack)], axis=0)
            q = (q * scale).astype(BF16)
            s = lax.dot_general(q, k, nt, preferred_element_type=F32)
            m = jnp.max(s, axis=-1, keepdims=True)
            if has_ctx:
                sc = lax.dot_general(q, ck, nt, preferred_element_type=F32)
                m = jnp.maximum(m, jnp.max(sc, axis=-1, keepdims=True))
            p = jnp.exp(s - m)
            den = jnp.sum(p, axis=-1, keepdims=True)
            o = _bdot(p.astype(BF16), v)
            if has_ctx:
                pc = jnp.exp(sc - m)
                den = den + jnp.sum(pc, axis=-1, keepdims=True)
                o = o + _bdot(pc.astype(BF16), cv)
            o = (o / den).astype(o_ref.dtype)
            for g in range(stack):
                o_ref[:, (g0 + g) * HEAD_DIM:(g0 + g + 1) * HEAD_DIM] = o[g * tq:(g + 1) * tq]


def _attention(p, n_batch, seq, row0, tq, ctx=None):
    q_blocks = seq // tq
    kvh = N_KV_HEADS
    gw = kvh * Q_PER_KV * HEAD_DIM
    kw = kvh * HEAD_DIM
    in_specs = [
        pl.BlockSpec((tq, gw), lambda b, h, qi: (row0 // tq + b * q_blocks + qi, h)),
        pl.BlockSpec((seq, kw), lambda b, h, qi: (row0 // seq + b, COL_K // kw + h)),
        pl.BlockSpec((seq, kw), lambda b, h, qi: (row0 // seq + b, COL_V // kw + h)),
    ]
    args = [p, p, p]
    if ctx is not None:
        assert kvh == N_KV_HEADS
        ctx_spec = pl.BlockSpec((None, None, PAST_LEN, N_KV_HEADS, HEAD_DIM), lambda b, h, qi: (b, 0, 0, 0, 0))
        in_specs += [ctx_spec, ctx_spec]
        args += [ctx[0], ctx[1]]
    return pl.pallas_call(
        functools.partial(_attn_kernel, has_ctx=ctx is not None, stack=Q_PER_KV if ctx is None else 1),
        out_shape=jax.ShapeDtypeStruct((n_batch * seq, ATT_WIDTH), BF16),
        grid=(n_batch, N_KV_HEADS // kvh, q_blocks),
        in_specs=in_specs,
        out_specs=pl.BlockSpec((tq, gw), lambda b, h, qi: (b * q_blocks + qi, h)),
        compiler_params=_params(("arbitrary", "arbitrary", "arbitrary")),
        name="attention_latent" if ctx is not None else "attention_prompt",
    )(*args)


def _hgrn_constants():
    c = HG_CHUNK
    t = np.arange(c)
    cums, sels, pairs, scans, diags = [], [], [], [], []
    for d in range(2):
        pos = t if d == 0 else c - 1 - t
        pu, pt = pos[None, :], pos[:, None]
        ms, ss, ws = [], [], []
        for l in range(HG_LEVELS):
            m = c >> l
            blk = pos // m
            mid = (blk * m + m // 2)[:, None]
            late = ((pos % m) >= m // 2)
            ms.append(np.where(late[:, None], (pu >= mid) & (pu <= pt), (pu > pt) & (pu < mid)))
            ss.append(np.broadcast_to(late[:, None], (c, c)))
            ws.append((blk[:, None] == blk[None, :]) & late[:, None] & ~late[None, :])
        ms.append(pu <= pt)
        ms.append(pu > pt)
        cums.append(np.concatenate(ms, axis=0))
        sels.append(np.stack(ss))
        pairs.append(np.stack(ws))
        scans.append(np.concatenate([pu <= pt, pu <= pt], axis=1))
        diags.append(((pos // (c // 2))[:, None] == (pos // (c // 2))[None, :]) & (pu <= pt))
    return (jnp.asarray(np.stack(cums), BF16), jnp.asarray(np.stack(sels), F32),
            jnp.asarray(np.stack(pairs), F32), jnp.asarray(np.stack(scans), BF16),
            jnp.asarray(np.stack(diags), F32))


def _hgrn_kernel(*refs, n_seqs, n_chunks, has_s0):
    (hq_ref, zf_ref, zb_ref, hi_ref, hg_ref, lb_ref, og_ref, cum_ref, sel_ref, pair_ref, scan_ref,
     diag_ref) = refs[:12]
    refs = refs[12:]
    if has_s0:
        s0f_ref, s0b_ref, o_ref = refs[:3]
        refs = refs[3:]
    else:
        o_ref, sf_ref, sb_ref = refs[:3]
        refs = refs[3:]
    st_ref, acc_ref = refs
    hb = st_ref.shape[0]
    c = HG_CHUNK
    half = c // 2
    nt = (((1,), (1,)), ((), ()))
    tn = (((0,), (0,)), ((), ()))

    def direction(d, z_ref):
        raw = lb_ref[d]
        e = jnp.exp(raw - jnp.max(raw, axis=0, keepdims=True))
        lb = e[0:1] / jnp.sum(e, axis=0, keepdims=True)

        def initial_state(s, hh):
            if has_s0:
                return (s0f_ref, s0b_ref)[d][s, hh].T
            return jnp.zeros((HG_DV, HG_DK), F32)

        def gates(rows):
            f = lb + (1.0 - lb) * _sigmoid(z_ref[rows, :])
            logf = jnp.log(f)
            hi16 = logf.astype(BF16)
            lo16 = (logf - hi16.astype(F32)).astype(BF16)
            return f, hi16, lo16

        def row(p):
            t = p if d == 0 else c - 1 - p
            return slice(t, t + 1)

        early, late = (slice(0, half), slice(half, c)) if d == 0 else (slice(half, c), slice(0, half))

        def in_row_order(x_early, x_late):
            return jnp.concatenate([x_early, x_late] if d == 0 else [x_late, x_early], axis=0)

        def emit(rows, cols, o):
            if d == 0:
                acc_ref[rows, cols] = o
            else:
                tot = acc_ref[rows, cols] + o
                o_ref[rows, cols] = (_rms(tot, og_ref[...]) * hg_ref[rows, cols]).astype(o_ref.dtype)

        def two_level_operands(rows):
            f, hi16, lo16 = gates(rows)
            b = _bdot(scan_ref[d], jnp.concatenate([hi16, lo16], axis=0))
            kk = 1.0 - f
            q = hq_ref[rows, :]
            r_mid = b[row(half - 1)]
            x1 = in_row_order(kk[early] * jnp.exp(r_mid - b[early]), q[late] * jnp.exp(b[late] - r_mid)).astype(BF16)
            dq = in_row_order(b[early] - b[row(half // 2 - 1)], b[late] - b[row(half + half // 2 - 1)])
            b_end = b[row(c - 1)]
            span = jnp.maximum(
                jnp.maximum(b[row(0)] - b[row(half // 2 - 1)], b[row(half // 2 - 1)] - b[row(half - 1)]),
                jnp.maximum(b[row(half)] - b[row(half + half // 2 - 1)],
                            b[row(half + half // 2 - 1)] - b[row(c - 1)]))
            return dict(x1=x1, xq=(q * jnp.exp(dq)).astype(BF16), xk=(kk * jnp.exp(-dq)).astype(BF16),
                        q_in=(q * jnp.exp(b)).astype(BF16), k_out=(kk * jnp.exp(b_end - b)).astype(BF16),
                        a_end=jnp.exp(b_end), iv=hi_ref[rows, :].astype(BF16), span=span)

        def two_level_chunk(rows, ops, states, need_state):
            new_states = []
            for hh in range(hb):
                cols = slice(hh * HG_DK, (hh + 1) * HG_DK)
                g1 = _bdot(ops["x1"][:, cols], ops["x1"][:, cols].astype(F32).T.astype(BF16))
                g2 = _bdot(ops["xq"][:, cols], ops["xk"][:, cols].astype(F32).T.astype(BF16))
                att = jnp.where(pair_ref[d, 0] > 0.5, g1, jnp.where(diag_ref[d] > 0.5, g2, 0.0)).astype(BF16)
                st = states[hh]
                o = _bdot(att, ops["iv"][:, cols])
                if st is not None:
                    o = o + lax.dot_general(ops["q_in"][:, cols], st.astype(BF16), nt, preferred_element_type=F32)
                pending.append((rows, cols, o))
                if not need_state:
                    new_states.append(None)
                    continue
                dst = lax.dot_general(ops["iv"][:, cols], ops["k_out"][:, cols], tn, preferred_element_type=F32)
                new_states.append(dst if st is None else ops["a_end"][:, cols] * st + dst)
            return new_states

        def all_levels(rows):
            f, hi16, lo16 = gates(rows)
            cum = cum_ref[d]
            eall = jnp.exp(_bdot(cum, hi16) + _bdot(cum, lo16))
            for hh in range(hb):
                cols = slice(hh * HG_DK, (hh + 1) * HG_DK)
                kk = 1.0 - f[:, cols]
                q = hq_ref[rows, cols]
                iv = hi_ref[rows, cols]
                iv16 = iv.astype(BF16)
                att = jnp.zeros((c, c), F32)
                for l in range(HG_LEVELS):
                    x = (kk + sel_ref[d, l] * (q - kk)) * eall[l * c:(l + 1) * c, cols]
                    xb = x.astype(BF16)
                    att = att + pair_ref[d, l] * lax.dot_general(xb, xb, nt, preferred_element_type=F32)
                e_in = eall[HG_LEVELS * c:(HG_LEVELS + 1) * c, cols]
                e_out = eall[(HG_LEVELS + 1) * c:, cols]
                st = st_ref[hh]
                o = (_bdot(att.astype(BF16), iv16)
                     + jnp.sum(q * kk, axis=-1, keepdims=True) * iv
                     + lax.dot_general((q * e_in).astype(BF16), st.astype(BF16), nt, preferred_element_type=F32))
                dst = lax.dot_general(iv16, (kk * e_out).astype(BF16), tn, preferred_element_type=F32)
                st_ref[hh] = (e_in[0:1] * e_out[0:1]) * st + dst
                emit(rows, cols, o)

        def chunk_start(s, ci):
            cidx = ci if d == 0 else n_chunks - 1 - ci
            return (s * n_chunks + cidx) * c

        worst = jnp.zeros((1, hb * HG_DK), F32)
        pending = []
        states = [[initial_state(s, hh) if has_s0 else None for hh in range(hb)] for s in range(n_seqs)]
        for ci in range(n_chunks):
            for s in range(n_seqs):
                rows = slice(chunk_start(s, ci), chunk_start(s, ci) + c)
                ops = two_level_operands(rows)
                states[s] = two_level_chunk(rows, ops, states[s], need_state=ci < n_chunks - 1 or not has_s0)
                worst = jnp.maximum(worst, ops["span"])
        for item in pending:
            emit(*item)
        if not has_s0:
            for s in range(n_seqs):
                for hh in range(hb):
                    (sf_ref, sb_ref)[d][s, hh] = states[s][hh].T

        @pl.when(jnp.logical_not(jnp.max(worst) <= HG_SAFE_EXPONENT))
        def _():
            for s in range(n_seqs):
                for hh in range(hb):
                    st_ref[hh] = initial_state(s, hh)

                def chunk(ci, carry):
                    all_levels(pl.ds(pl.multiple_of(chunk_start(s, ci), c), c))
                    return carry

                lax.fori_loop(0, n_chunks, chunk, 0)
                if not has_s0:
                    for hh in range(hb):
                        (sf_ref, sb_ref)[d][s, hh] = st_ref[hh].T

    direction(0, zf_ref)
    direction(1, zb_ref)


def _hgrn(p, lb_raw, o_gain, consts, n_batch, seq, row0, n_seqs, s0=None):
    hb = HG_HEAD_BLOCK
    w = hb * HG_DK
    rows = n_seqs * seq

    def seg(col):
        return pl.BlockSpec((rows, w), lambda b, h: (row0 // rows + b, col // w + h))

    def const(a):
        return pl.BlockSpec(a.shape, lambda b, h: (0,) * a.ndim)

    in_specs = [seg(COL_HQ), seg(COL_ZF), seg(COL_ZF + HG_KW), seg(COL_HI), seg(COL_HG),
                pl.BlockSpec((2, DEPTH + 1, w), lambda b, h: (0, 0, h)),
                pl.BlockSpec((1, HG_DV), lambda b, h: (0, 0))] + [const(a) for a in consts]
    args = [p, p, p, p, p, lb_raw, o_gain, *consts]
    has_s0 = s0 is not None
    st_spec = pl.BlockSpec((n_seqs, hb, HG_DK, HG_DV), lambda b, h: (b, h, 0, 0))
    if has_s0:
        in_specs += [st_spec, st_spec]
        args += [s0[0], s0[1]]
    out_shape = [jax.ShapeDtypeStruct((n_batch * seq, HG_VW), BF16)]
    out_specs = [pl.BlockSpec((rows, w), lambda b, h: (b, h))]
    if not has_s0:
        st_shape = jax.ShapeDtypeStruct((n_batch, HG_HEADS, HG_DK, HG_DV), F32)
        out_shape += [st_shape, st_shape]
        out_specs += [st_spec, st_spec]
    return pl.pallas_call(
        functools.partial(_hgrn_kernel, n_seqs=n_seqs, n_chunks=seq // HG_CHUNK, has_s0=has_s0),
        out_shape=tuple(out_shape),
        grid=(n_batch // n_seqs, HG_HEADS // hb),
        in_specs=in_specs,
        out_specs=tuple(out_specs),
        scratch_shapes=[pltpu.VMEM((hb, HG_DV, HG_DK), F32), pltpu.VMEM((rows, w), F32)],
        compiler_params=_params(("arbitrary", "arbitrary")),
        name="hgrn_latent" if has_s0 else "hgrn_prompt",
    )(*args)


def _outproj_kernel(attp_ref, atts_ref, hgp_ref, hgs_ref, wa_ref, wb_ref, xp_ref, xs_ref, g1_ref, gain_ref,
                    sh_ref, sc_ref, x1_ref, h2_ref, full_ref, wcache_ref):
    i = pl.program_id(0)
    n = pl.program_id(1)

    @pl.when(i == 0)
    def _():
        wcache_ref[n, 0] = wa_ref[...].astype(BF16)
        wcache_ref[n, 1] = wb_ref[...].astype(BF16)

    def run(att_ref, hg_ref, x_ref):
        acc = _bdot(att_ref[...], wcache_ref[n, 0]) + _bdot(hg_ref[...], wcache_ref[n, 1])
        x1 = x_ref[...] + g1_ref[...] * acc
        x1_ref[...] = x1
        full_ref[n] = x1

    pl.when(i < N_PROMPT_TILES)(lambda: run(attp_ref, hgp_ref, xp_ref))
    pl.when(i >= N_PROMPT_TILES)(lambda: run(atts_ref, hgs_ref, xs_ref))

    @pl.when(n == D_MODEL // OUT_COL_TILE - 1)
    def _():
        nt = D_MODEL // OUT_COL_TILE
        ms = jnp.sum(full_ref[0] * full_ref[0], axis=-1, keepdims=True)
        for k in range(1, nt):
            ms = ms + jnp.sum(full_ref[k] * full_ref[k], axis=-1, keepdims=True)
        rstd = lax.rsqrt(ms / D_MODEL + EPS)
        gs = gain_ref[...] * (1.0 + sc_ref[...])
        for k in range(nt):
            cols = slice(k * OUT_COL_TILE, (k + 1) * OUT_COL_TILE)
            h2_ref[:, cols] = (full_ref[k] * rstd * gs[:, cols] + sh_ref[:, cols]).astype(BF16)


def _out_projection(att_p, att_s, hg_p, hg_s, w_out, xp, xs, mods, norm_mlp):
    tm, tn = TOK_TILE, OUT_COL_TILE
    nt = D_MODEL // tn

    def prompt_rows(width):
        return pl.BlockSpec((tm, width), lambda i, n: (jnp.minimum(i, N_PROMPT_TILES - 1), 0),
                            pipeline_mode=pl.Buffered(1))

    def sample_rows(width):
        return pl.BlockSpec((tm, width), lambda i, n: (jnp.maximum(i - N_PROMPT_TILES, 0), 0),
                            pipeline_mode=pl.Buffered(1))

    return pl.pallas_call(
        _outproj_kernel,
        out_shape=(jax.ShapeDtypeStruct((N_TOK, D_MODEL), F32), jax.ShapeDtypeStruct((N_TOK, D_MODEL), BF16)),
        grid=(N_TILES, nt),
        in_specs=[prompt_rows(ATT_WIDTH), sample_rows(ATT_WIDTH), prompt_rows(HG_VW), sample_rows(HG_VW),
                  pl.BlockSpec((ATT_WIDTH, tn), lambda i, n: (0, jnp.where(i == 0, n, nt - 1))),
                  pl.BlockSpec((HG_VW, tn), lambda i, n: (1, jnp.where(i == 0, n, nt - 1))),
                  pl.BlockSpec((tm, tn), lambda i, n: (jnp.minimum(i, N_PROMPT_TILES - 1),
                                                       jnp.where(i < N_PROMPT_TILES, n, nt - 1))),
                  pl.BlockSpec((tm, tn), lambda i, n: (jnp.maximum(i - N_PROMPT_TILES, 0),
                                                       jnp.where(i < N_PROMPT_TILES, 0, n))),
                  _mod_spec(0, 2, tm, width=tn, col=lambda i, n: n),
                  pl.BlockSpec((1, D_MODEL), lambda i, n: (0, 0)),
                  _mod_spec(0, 3, tm), _mod_spec(0, 4, tm)],
        out_specs=(pl.BlockSpec((tm, tn), lambda i, n: (i, n)),
                   pl.BlockSpec((tm, D_MODEL), lambda i, n: (i, 0))),
        scratch_shapes=[pltpu.VMEM((nt, tm, tn), F32), pltpu.VMEM((nt, 2, ATT_WIDTH, tn), BF16)],
        compiler_params=_params(("arbitrary", "arbitrary")),
        name="out_projection",
    )(att_p, att_s, hg_p, hg_s, w_out, w_out, xp, xs, mods, norm_mlp, mods, mods)


def _mlp_kernel(h_ref, w1_ref, w2_ref, x_ref, g2_ref, *rest, final):
    if final:
        fin_ref, o_ref, res_ref = rest
    else:
        o_ref, res_ref = rest
    j = pl.program_id(1)
    res_ref[j] = x_ref[...]

    def step(first, last):
        a = jnp.square(jnp.maximum(_bdot(h_ref[...], w1_ref[...].astype(BF16)), 0.0)).astype(BF16)
        for n in range(D_MODEL // MLP_OUT_CHUNK):
            cols = slice(n * MLP_OUT_CHUNK, (n + 1) * MLP_OUT_CHUNK)
            p = _bdot(a, w2_ref[:, cols].astype(BF16))
            if not first:
                p = o_ref[:, cols] + p
            if last:
                per = MLP_OUT_CHUNK // MLP_RES_COLS
                res = jnp.concatenate([res_ref[n * per + k] for k in range(per)], axis=1)
                p = res + g2_ref[:, cols] * p
            o_ref[:, cols] = p
        if last and final:
            o_ref[...] = _rms(o_ref[...], fin_ref[...])

    pl.when(j == 0)(lambda: step(True, False))
    pl.when(jnp.logical_and(j > 0, j < MLP_STEPS - 1))(lambda: step(False, False))
    pl.when(j == MLP_STEPS - 1)(lambda: step(False, True))


def _mlp(h, x, w1, w2, mods, layer, tile0, n_tiles, final_norm=None):
    tm, th = TOK_TILE, FF_TILE
    final = final_norm is not None
    in_specs = [pl.BlockSpec((tm, D_MODEL), lambda i, j: (tile0 + i, 0), pipeline_mode=pl.Buffered(1)),
                pl.BlockSpec((None, D_MODEL, th), lambda i, j: (layer, 0, j)),
                pl.BlockSpec((None, th, D_MODEL), lambda i, j: (layer, j, 0)),
                pl.BlockSpec((tm, MLP_RES_COLS), lambda i, j: (tile0 + i, j)),
                _mod_spec(layer, 5, tm, tile_of=lambda i, j: tile0 + i)]
    args = [h, w1, w2, x, mods]
    if final:
        in_specs.append(pl.BlockSpec((1, D_MODEL), lambda i, j: (0, 0)))
        args.append(final_norm)
    return pl.pallas_call(
        functools.partial(_mlp_kernel, final=final),
        out_shape=jax.ShapeDtypeStruct((n_tiles * tm, D_MODEL), F32),
        grid=(n_tiles, MLP_STEPS),
        in_specs=in_specs,
        out_specs=pl.BlockSpec((tm, D_MODEL), lambda i, j: (i, 0)),
        scratch_shapes=[pltpu.VMEM((MLP_STEPS, tm, MLP_RES_COLS), F32)],
        compiler_params=_params(("arbitrary", "arbitrary")),
        name="mlp_final" if final else "mlp",
    )(*args)


def _pool_kernel(x_ref, wp_ref, ps_ref, gain1_ref, sh1_ref, sc1_ref, g1_ref, gain2_ref, sh2_ref, sc2_ref,
                 x3_ref, h4_ref):
    i = pl.program_id(0)
    tm = TOK_TILE
    rstd = lax.rsqrt(jnp.mean(x_ref[...] * x_ref[...], axis=-1, keepdims=True) + EPS)
    gs1 = gain1_ref[...] * (1.0 + sc1_ref[...])
    mix_gain = g1_ref[...] * ps_ref[...]

    def widen(m):
        return jnp.concatenate([m] * (POOL_GROUP // HEAD_DIM), axis=1)

    def mix_tile(seq):
        n_seq = tm // seq
        pitch = seq + 2 * POOL_HALO
        n_pad = n_seq * pitch
        halo = jnp.zeros((POOL_HALO, POOL_GROUP), F32)
        pos = lax.broadcasted_iota(jnp.int32, (seq, HEAD_DIM), 0)

        def down(a, k):
            return pltpu.roll(a, k % n_pad, 0)

        for g, w in enumerate(POOL_WINDOWS):
            half = w // 2
            cols = slice(g * POOL_GROUP, (g + 1) * POOL_GROUP)
            x = x_ref[:, cols]
            h = x * rstd * gs1[:, cols] + sh1_ref[:, cols]
            padded = jnp.concatenate(
                [piece for s in range(n_seq) for piece in (halo, h[s * seq:(s + 1) * seq], halo)], axis=0)
            back = padded
            m = 1
            while m < half:
                back = back + down(back, m)
                m *= 2
            ahead = back if half == 1 else down(back, -(half - 1))
            total = down(back, 1) + ahead
            total = jnp.concatenate(
                [total[s * pitch + POOL_HALO:s * pitch + POOL_HALO + seq] for s in range(n_seq)], axis=0)
            count = (jnp.minimum(pos + (w - half), seq) - jnp.maximum(pos - half, 0)).astype(F32)
            inv = jnp.concatenate([widen(1.0 / count)] * n_seq, axis=0)
            pooled = (total * inv - h).astype(BF16)
            mix = _bdot(pooled, wp_ref[g].astype(BF16))
            x3_ref[:, cols] = x + mix_gain[:, cols] * mix

    pl.when(i < N_PROMPT_TILES)(lambda: mix_tile(SEQ))
    pl.when(i >= N_PROMPT_TILES)(lambda: mix_tile(DEC_SEQ))
    x3 = x3_ref[...]
    rstd3 = lax.rsqrt(jnp.mean(x3 * x3, axis=-1, keepdims=True) + EPS)
    h4_ref[...] = (x3 * rstd3 * (gain2_ref[...] * (1.0 + sc2_ref[...])) + sh2_ref[...]).astype(BF16)


def _pool_mixer(x, w_pool, pool_scale, norm_mix, norm_mlp, mods):
    tm = TOK_TILE
    vec = pl.BlockSpec((1, D_MODEL), lambda i: (0, 0))
    tile = pl.BlockSpec((tm, D_MODEL), lambda i: (i, 0))
    return pl.pallas_call(
        _pool_kernel,
        out_shape=(jax.ShapeDtypeStruct((N_TOK, D_MODEL), F32), jax.ShapeDtypeStruct((N_TOK, D_MODEL), BF16)),
        grid=(N_TILES,),
        in_specs=[tile, pl.BlockSpec((len(POOL_WINDOWS), POOL_GROUP, POOL_GROUP), lambda i: (0, 0, 0)),
                  vec, vec, _mod_spec(1, 0, tm), _mod_spec(1, 1, tm), _mod_spec(1, 2, tm),
                  vec, _mod_spec(1, 3, tm), _mod_spec(1, 4, tm)],
        out_specs=(tile, tile),
        compiler_params=_params(("arbitrary",)),
        name="pool_mixer",
    )(x, w_pool, pool_scale, norm_mix, mods, mods, mods, norm_mlp, mods, mods)


def kernel(x_prompt, x_sample, cache_k, cache_v, state_hgrn_fwd, state_hgrn_bwd, c, c_ctx, w_ada, b_ada,
           norm_mix, norm_mlp, w_in_ab, w_out_ab, q_norm, k_norm, hg_norm, lb_raw, w_pool, pool_scale,
           w_mlp_in, w_mlp_out, final_norm):
    xp = x_prompt.reshape(N_PROMPT, D_MODEL)
    xs = x_sample.reshape(N_SAMPLE, D_MODEL)
    cv = jnp.concatenate([c_ctx[None, :], c, jnp.zeros((ADA_ROWS - 1 - DEC_BATCH, D_MODEL), F32)], axis=0)
    mods, h0 = _ada_table_and_modulate(cv, w_ada, b_ada, xp, xs, norm_mix)

    proj, new_k, new_v = _in_projection(h0, w_in_ab[0], q_norm[0:1], k_norm[0:1])
    att_p = _attention(proj, BATCH, SEQ, 0, SEQ)
    att_s = _attention(proj, DEC_BATCH, DEC_SEQ, N_PROMPT, 512, ctx=(cache_k, cache_v))
    consts = _hgrn_constants()
    hg_p, s_fwd, s_bwd = _hgrn(proj, lb_raw, hg_norm[0:1], consts, BATCH, SEQ, 0, HG_PROMPT_SEQS)
    s0 = (state_hgrn_fwd.reshape(DEC_BATCH, HG_HEADS, HG_DK, HG_DV),
          state_hgrn_bwd.reshape(DEC_BATCH, HG_HEADS, HG_DK, HG_DV))
    (hg_s,) = _hgrn(proj, lb_raw, hg_norm[0:1], consts, DEC_BATCH, DEC_SEQ, N_PROMPT, 1, s0=s0)
    x1, h2 = _out_projection(att_p, att_s, hg_p, hg_s, w_out_ab[0], xp, xs, mods, norm_mlp[0:1])
    x2 = _mlp(h2, x1, w_mlp_in, w_mlp_out, mods, 0, 0, N_TILES)

    x3, h4 = _pool_mixer(x2, w_pool[0], pool_scale[0:1], norm_mix[1:2], norm_mlp[1:2], mods)
    fin = final_norm[None, :]
    y_prompt = _mlp(h4, x3, w_mlp_in, w_mlp_out, mods, 1, 0, N_PROMPT_TILES, final_norm=fin)
    y_sample = _mlp(h4, x3, w_mlp_in, w_mlp_out, mods, 1, N_PROMPT_TILES, N_TILES - N_PROMPT_TILES,
                    final_norm=fin)

    return (y_prompt.reshape(BATCH, SEQ, D_MODEL), y_sample.reshape(DEC_BATCH, DEC_SEQ, D_MODEL),
            new_k, new_v,
            s_fwd.reshape(BATCH, 1, HG_HEADS, HG_DK, HG_DV), s_bwd.reshape(BATCH, 1, HG_HEADS, HG_DK, HG_DV))
```

```python
import functools

import numpy as np
import jax
import jax.numpy as jnp
from jax import lax
from jax.experimental import pallas as pl
from jax.experimental.pallas import tpu as pltpu

F32 = jnp.float32
BF16 = jnp.bfloat16

D_MODEL = 2048
BATCH = 16
SEQ = 256
DEPTH = 2
DEC_BATCH = 2
DEC_SEQ = 1024
PAST_LEN = 256
GRID_W = 64
HEAD_DIM = 128
N_Q_HEADS = 8
N_KV_HEADS = 2
Q_PER_KV = N_Q_HEADS // N_KV_HEADS
ATT_WIDTH = N_Q_HEADS * HEAD_DIM
KV_WIDTH = N_KV_HEADS * HEAD_DIM
HG_HEADS = 8
HG_DK = 128
HG_DV = 128
HG_KW = HG_HEADS * HG_DK
HG_VW = HG_HEADS * HG_DV
IN_AB = ATT_WIDTH + 2 * KV_WIDTH + 3 * HG_KW + 2 * HG_VW
MIX_WIDTH = ATT_WIDTH + HG_VW
POOL_WINDOWS = (2, 4, 8, 16)
POOL_GROUP = D_MODEL // len(POOL_WINDOWS)
POOL_HALO = 8
D_FF = 4 * D_MODEL
ROPE_THETA = 10000.0
ROPE_HALF = HEAD_DIM // 2
EPS = 1e-6
N_MOD = 6

N_PROMPT = BATCH * SEQ
N_SAMPLE = DEC_BATCH * DEC_SEQ
N_TOK = N_PROMPT + N_SAMPLE
ADA_ROWS = 16
ADA_COL_TILE = 1024
ADA_KEEP = 2 * D_MODEL // ADA_COL_TILE
MOD0_TOK_TILE = 512
TOK_TILE = 1024
N_TILES = N_TOK // TOK_TILE
N_PROMPT_TILES = N_PROMPT // TOK_TILE
PROJ_TOK_TILE = 2048
PROJ_COL_TILE = 512
OUT_COL_TILE = 512
FF_TILE = 512
MLP_STEPS = D_FF // FF_TILE
MLP_RES_COLS = D_MODEL // MLP_STEPS
MLP_OUT_CHUNK = 512
HG_CHUNK = 128
HG_LEVELS = 7
HG_PROMPT_HEADS = 8
HG_PROMPT_SEQS = 2
HG_LATENT_HEADS = 4
HG_SAFE_EXPONENT = 80.0
V7X_VMEM_BYTES = 64 * 2 ** 20
VMEM_LIMIT = V7X_VMEM_BYTES - 8 * 2 ** 20

COL_Q = 0
COL_K = ATT_WIDTH
COL_V = COL_K + KV_WIDTH
COL_HQ = COL_V + KV_WIDTH
COL_ZF = COL_HQ + HG_KW
COL_HI = COL_ZF + 2 * HG_KW
COL_HG = COL_HI + HG_VW
OUT_HQ = ATT_WIDTH
OUT_ZF = OUT_HQ + HG_KW
OUT_HI = OUT_ZF + 2 * HG_KW
OUT_HG = OUT_HI + HG_VW
OUT_K = OUT_HG + HG_VW
OUT_V = OUT_K + KV_WIDTH


def _params(semantics):
    return pltpu.CompilerParams(dimension_semantics=semantics, vmem_limit_bytes=VMEM_LIMIT)


def _sigmoid(x):
    return 1.0 / (1.0 + jnp.exp(-x))


def _silu(x):
    return x * _sigmoid(x)


def _rms(x, gain):
    return x * lax.rsqrt(jnp.mean(x * x, axis=-1, keepdims=True) + EPS) * gain


def _bdot(a, b):
    return jnp.dot(a, b, preferred_element_type=F32)


def _mod_row(tile, tile_rows):
    first = N_PROMPT // tile_rows
    per_seq = DEC_SEQ // tile_rows
    return jnp.where(tile < first, 0, 1 + (tile - first) // per_seq)


def _mod_spec(layer, chunk, tile_rows, width=D_MODEL, col=lambda *g: 0, tile_of=lambda *g: g[0]):
    per = D_MODEL // width
    return pl.BlockSpec((None, None, 1, width),
                        lambda *g: (layer, _mod_row(tile_of(*g), tile_rows), 0, chunk * per + col(*g)))


def _modulate(x, gain, shift, scale):
    return x * lax.rsqrt(jnp.mean(x * x, axis=-1, keepdims=True) + EPS) * (gain * (1.0 + scale)) + shift


def _ada_kernel(cv_ref, w_ref, b_ref, xp_ref, xs_ref, gain_ref, o_ref, h_ref, keep_ref):
    s = pl.program_id(0)
    tile = _bdot(_silu(cv_ref[...]).astype(BF16), w_ref[...].astype(BF16)) + b_ref[...]
    o_ref[:, 0, :] = tile

    @pl.when(s < ADA_KEEP)
    def _():
        keep_ref[s] = tile

    t = s - ADA_KEEP
    n_prompt = N_PROMPT // MOD0_TOK_TILE

    def run(x_ref):
        r = _mod_row(t, MOD0_TOK_TILE)
        per = D_MODEL // ADA_COL_TILE
        shift = jnp.concatenate([keep_ref[k, pl.ds(r, 1), :] for k in range(per)], axis=1)
        scale = jnp.concatenate([keep_ref[per + k, pl.ds(r, 1), :] for k in range(per)], axis=1)
        h_ref[...] = _modulate(x_ref[...], gain_ref[...], shift, scale).astype(BF16)

    pl.when(jnp.logical_and(t >= 0, t < n_prompt))(lambda: run(xp_ref))
    pl.when(jnp.logical_and(t >= n_prompt, t < N_TOK // MOD0_TOK_TILE))(lambda: run(xs_ref))


def _ada_table_and_modulate(cv, w_ada, b_ada, xp, xs, norm_mix):
    tn, tm = ADA_COL_TILE, MOD0_TOK_TILE
    n = N_MOD * D_MODEL
    per_layer = n // tn
    n_prompt = N_PROMPT // tm
    n_tiles = N_TOK // tm
    assert DEPTH * per_layer >= ADA_KEEP + n_tiles

    def tok(s):
        return jnp.clip(s - ADA_KEEP, 0, n_tiles - 1)

    return pl.pallas_call(
        _ada_kernel,
        out_shape=(jax.ShapeDtypeStruct((DEPTH, ADA_ROWS, 1, n), F32), jax.ShapeDtypeStruct((N_TOK, D_MODEL), BF16)),
        grid=(DEPTH * per_layer,),
        in_specs=[pl.BlockSpec((ADA_ROWS, D_MODEL), lambda s: (0, 0)),
                  pl.BlockSpec((None, D_MODEL, tn), lambda s: (s // per_layer, 0, s % per_layer)),
                  pl.BlockSpec((None, 1, tn), lambda s: (s // per_layer, 0, s % per_layer)),
                  pl.BlockSpec((tm, D_MODEL), lambda s: (jnp.minimum(tok(s), n_prompt - 1), 0)),
                  pl.BlockSpec((tm, D_MODEL), lambda s: (jnp.maximum(tok(s) - n_prompt, 0), 0)),
                  pl.BlockSpec((1, D_MODEL), lambda s: (0, 0))],
        out_specs=(pl.BlockSpec((None, ADA_ROWS, 1, tn), lambda s: (s // per_layer, 0, 0, s % per_layer)),
                   pl.BlockSpec((tm, D_MODEL), lambda s: (tok(s), 0))),
        scratch_shapes=[pltpu.VMEM((ADA_KEEP, ADA_ROWS, tn), F32)],
        compiler_params=_params(("arbitrary",)),
        name="ada_table_modulate0",
    )(cv, w_ada, b_ada.reshape(DEPTH, 1, n), xp, xs, norm_mix[0:1])


def _rope(y, cos, sin, perm2):
    hi = y.astype(BF16)
    lo = (y - hi.astype(F32)).astype(BF16)
    rot = _bdot(jnp.concatenate([hi, lo], axis=1), perm2)
    return y * cos + rot * sin


def _inproj_kernel(h_ref, w_ref, qg_ref, kg_ref, cos_ref, sin_ref, perm_ref, p_ref, nk_ref, nv_ref):
    i = pl.program_id(0)
    j = pl.program_id(1)
    latent = i >= N_PROMPT // PROJ_TOK_TILE
    heads = PROJ_COL_TILE // HEAD_DIM
    kv_tile = COL_K // PROJ_COL_TILE
    is_q = j < kv_tile

    def attention_tile(rope):
        gain = jnp.where(is_q, qg_ref[...], kg_ref[...])
        w = w_ref[...].astype(BF16)
        for s in range(PROJ_TOK_TILE // DEC_SEQ):
            rows = slice(s * DEC_SEQ, (s + 1) * DEC_SEQ)
            acc = _bdot(h_ref[rows, :], w)
            for hh in range(heads):
                cols = slice(hh * HEAD_DIM, (hh + 1) * HEAD_DIM)
                x = acc[:, cols]
                y = _rms(x, gain)
                if rope:
                    y = _rope(y, cos_ref[...], sin_ref[...], perm_ref[...])
                if hh >= N_KV_HEADS:
                    y = jnp.where(is_q, y, x)
                p_ref[rows, cols] = y

    pl.when(jnp.logical_and(j <= kv_tile, latent))(lambda: attention_tile(True))
    pl.when(jnp.logical_and(j <= kv_tile, jnp.logical_not(latent)))(lambda: attention_tile(False))

    @pl.when(jnp.logical_and(j == kv_tile, jnp.logical_not(latent)))
    def _():
        for s in range(PROJ_TOK_TILE // SEQ):
            rows = slice(s * SEQ, (s + 1) * SEQ)
            for hh in range(N_KV_HEADS):
                nk_ref[s, :, hh, :] = p_ref[rows, hh * HEAD_DIM:(hh + 1) * HEAD_DIM]
                nv_ref[s, :, hh, :] = p_ref[rows, KV_WIDTH + hh * HEAD_DIM:KV_WIDTH + (hh + 1) * HEAD_DIM]

    def plain_tile(act):
        w = w_ref[...].astype(BF16)
        for s in range(PROJ_TOK_TILE // DEC_SEQ):
            rows = slice(s * DEC_SEQ, (s + 1) * DEC_SEQ)
            p_ref[rows, :] = act(_bdot(h_ref[rows, :], w))

    is_gate = jnp.logical_and(j >= COL_ZF // PROJ_COL_TILE, j < COL_HI // PROJ_COL_TILE)
    is_value = jnp.logical_and(j >= COL_HI // PROJ_COL_TILE, j < COL_HG // PROJ_COL_TILE)
    raw = jnp.logical_or(is_gate, is_value)
    pl.when(raw)(lambda: plain_tile(lambda a: a))
    pl.when(jnp.logical_and(j > kv_tile, jnp.logical_not(raw)))(lambda: plain_tile(_silu))


def _rope_tables():
    t = np.arange(DEC_SEQ)
    row = (t // GRID_W).astype(np.float32)
    col = (t % GRID_W).astype(np.float32)
    inv = (np.float32(ROPE_THETA) ** (-np.arange(0, ROPE_HALF, 2, dtype=np.float32) / np.float32(ROPE_HALF))).astype(np.float32)
    ar = row[:, None] * inv
    ac = col[:, None] * inv
    ang = np.concatenate([ar, ar, ac, ac], axis=-1).astype(np.float32)
    cos = np.cos(ang).astype(np.float32)
    sin = np.sin(ang).astype(np.float32)
    qw = ROPE_HALF // 2
    perm = np.zeros((HEAD_DIM, HEAD_DIM), np.float32)
    for k in range(qw):
        perm[qw + k, k] = -1.0
        perm[k, qw + k] = 1.0
        perm[3 * qw + k, 2 * qw + k] = -1.0
        perm[2 * qw + k, 3 * qw + k] = 1.0
    return jnp.asarray(cos), jnp.asarray(sin), jnp.asarray(np.concatenate([perm, perm], axis=0), BF16)


def _in_projection(h, w_in, q_gain, k_gain):
    tm, tn = PROJ_TOK_TILE, PROJ_COL_TILE
    n_prompt_tiles = N_PROMPT // tm
    cos, sin, perm2 = _rope_tables()
    table = pl.BlockSpec((DEC_SEQ, HEAD_DIM), lambda i, j: (0, 0))
    gain = pl.BlockSpec((1, HEAD_DIM), lambda i, j: (0, 0))
    state = pl.BlockSpec((tm // SEQ, None, SEQ, N_KV_HEADS, HEAD_DIM),
                         lambda i, j: (jnp.minimum(i, n_prompt_tiles - 1), 0, 0, 0, 0),
                         pipeline_mode=pl.Buffered(1))
    state_shape = jax.ShapeDtypeStruct((BATCH, 1, SEQ, N_KV_HEADS, HEAD_DIM), F32)
    kv_tile = COL_K // tn

    def out_tile(j):
        return jnp.where(j < kv_tile, j, jnp.where(j == kv_tile, OUT_K // tn, j - 1))

    return pl.pallas_call(
        _inproj_kernel,
        out_shape=(jax.ShapeDtypeStruct((N_TOK, IN_AB), F32), state_shape, state_shape),
        grid=(N_TOK // tm, IN_AB // tn),
        in_specs=[pl.BlockSpec((tm, D_MODEL), lambda i, j: (i, 0)),
                  pl.BlockSpec((D_MODEL, tn), lambda i, j: (0, j)),
                  gain, gain, table, table, pl.BlockSpec((2 * HEAD_DIM, HEAD_DIM), lambda i, j: (0, 0))],
        out_specs=(pl.BlockSpec((tm, tn), lambda i, j: (i, out_tile(j))), state, state),
        compiler_params=_params(("arbitrary", "arbitrary")),
        name="in_projection",
    )(h, w_in, q_gain, k_gain, cos, sin, perm2)


def _attn_kernel(*refs, has_ctx, stack):
    if has_ctx:
        q_ref, k_ref, v_ref, ck_ref, cv_ref, o_ref = refs
    else:
        q_ref, k_ref, v_ref, o_ref = refs
    scale = HEAD_DIM ** -0.5
    nt = (((1,), (1,)), ((), ()))
    tq = q_ref.shape[0]
    for hk in range(k_ref.shape[1] // HEAD_DIM):
        kcols = slice(hk * HEAD_DIM, (hk + 1) * HEAD_DIM)
        k = k_ref[:, kcols].astype(BF16)
        v = v_ref[:, kcols].astype(BF16)
        if has_ctx:
            ck = ck_ref[:, hk, :].astype(BF16)
            cv = cv_ref[:, hk, :].astype(BF16)
        for g0 in range(hk * Q_PER_KV, (hk + 1) * Q_PER_KV, stack):
            q = jnp.concatenate([q_ref[:, g * HEAD_DIM:(g + 1) * HEAD_DIM] for g in range(g0, g0 + stack)], axis=0)
            q = (q * scale).astype(BF16)
            s = lax.dot_general(q, k, nt, preferred_element_type=F32)
            m = jnp.max(s, axis=-1, keepdims=True)
            if has_ctx:
                sc = lax.dot_general(q, ck, nt, preferred_element_type=F32)
                m = jnp.maximum(m, jnp.max(sc, axis=-1, keepdims=True))
            p = jnp.exp(s - m)
            den = jnp.sum(p, axis=-1, keepdims=True)
            o = _bdot(p.astype(BF16), v)
            if has_ctx:
                pc = jnp.exp(sc - m)
                den = den + jnp.sum(pc, axis=-1, keepdims=True)
                o = o + _bdot(pc.astype(BF16), cv)
            o = (o / den).astype(o_ref.dtype)
            for g in range(stack):
                o_ref[:, (g0 + g) * HEAD_DIM:(g0 + g + 1) * HEAD_DIM] = o[g * tq:(g + 1) * tq]


def _attention(p, n_batch, seq, row0, tq, ctx=None):
    q_blocks = seq // tq
    kvh = N_KV_HEADS
    gw = kvh * Q_PER_KV * HEAD_DIM
    kw = kvh * HEAD_DIM
    in_specs = [
        pl.BlockSpec((tq, gw), lambda b, h, qi: (row0 // tq + b * q_blocks + qi, h)),
        pl.BlockSpec((seq, kw), lambda b, h, qi: (row0 // seq + b, OUT_K // kw + h)),
        pl.BlockSpec((seq, kw), lambda b, h, qi: (row0 // seq + b, OUT_V // kw + h)),
    ]
    args = [p, p, p]
    if ctx is not None:
        assert kvh == N_KV_HEADS
        ctx_spec = pl.BlockSpec((None, None, PAST_LEN, N_KV_HEADS, HEAD_DIM), lambda b, h, qi: (b, 0, 0, 0, 0))
        in_specs += [ctx_spec, ctx_spec]
        args += [ctx[0], ctx[1]]
    return pl.pallas_call(
        functools.partial(_attn_kernel, has_ctx=ctx is not None, stack=Q_PER_KV if ctx is None else 1),
        out_shape=jax.ShapeDtypeStruct((n_batch * seq, ATT_WIDTH), BF16),
        grid=(n_batch, N_KV_HEADS // kvh, q_blocks),
        in_specs=in_specs,
        out_specs=pl.BlockSpec((tq, gw), lambda b, h, qi: (b * q_blocks + qi, h)),
        compiler_params=_params(("arbitrary", "arbitrary", "arbitrary")),
        name="attention_latent" if ctx is not None else "attention_prompt",
    )(*args)


def _hgrn_constants():
    c = HG_CHUNK
    t = np.arange(c)
    cums, sels, pairs, scans, diags = [], [], [], [], []
    for d in range(2):
        pos = t if d == 0 else c - 1 - t
        pu, pt = pos[None, :], pos[:, None]
        ms, ss, ws = [], [], []
        for l in range(HG_LEVELS):
            m = c >> l
            blk = pos // m
            mid = (blk * m + m // 2)[:, None]
            late = ((pos % m) >= m // 2)
            ms.append(np.where(late[:, None], (pu >= mid) & (pu <= pt), (pu > pt) & (pu < mid)))
            ss.append(np.broadcast_to(late[:, None], (c, c)))
            ws.append((blk[:, None] == blk[None, :]) & late[:, None] & ~late[None, :])
        ms.append(pu <= pt)
        ms.append(pu > pt)
        cums.append(np.concatenate(ms, axis=0))
        sels.append(np.stack(ss))
        pairs.append(np.stack(ws))
        scans.append(np.concatenate([pu <= pt, pu <= pt], axis=1))
        diags.append(((pos // (c // 2))[:, None] == (pos // (c // 2))[None, :]) & (pu <= pt))
    return (jnp.asarray(np.stack(cums), BF16), jnp.asarray(np.stack(sels), F32),
            jnp.asarray(np.stack(pairs), F32), jnp.asarray(np.stack(scans), BF16),
            jnp.asarray(np.stack(diags), F32))


def _hgrn_kernel(*refs, n_seqs, n_chunks, has_s0):
    (hq_ref, zf_ref, zb_ref, hi_ref, hg_ref, lb_ref, og_ref, cum_ref, sel_ref, pair_ref, scan_ref,
     diag_ref) = refs[:12]
    refs = refs[12:]
    if has_s0:
        s0f_ref, s0b_ref, o_ref = refs[:3]
        refs = refs[3:]
    else:
        o_ref, sf_ref, sb_ref = refs[:3]
        refs = refs[3:]
    st_ref, acc_ref = refs
    hb = st_ref.shape[0]
    c = HG_CHUNK
    half = c // 2
    nt = (((1,), (1,)), ((), ()))
    tn = (((0,), (0,)), ((), ()))

    def direction(d, z_ref):
        raw = lb_ref[d]
        e = jnp.exp(raw - jnp.max(raw, axis=0, keepdims=True))
        lb = e[0:1] / jnp.sum(e, axis=0, keepdims=True)

        def initial_state(s, hh):
            if has_s0:
                return (s0f_ref, s0b_ref)[d][s, hh].T
            return jnp.zeros((HG_DV, HG_DK), F32)

        def gates(rows):
            f = lb + (1.0 - lb) * _sigmoid(z_ref[rows, :])
            logf = jnp.log(f)
            hi16 = logf.astype(BF16)
            lo16 = (logf - hi16.astype(F32)).astype(BF16)
            return f, hi16, lo16

        def row(p):
            t = p if d == 0 else c - 1 - p
            return slice(t, t + 1)

        early, late = (slice(0, half), slice(half, c)) if d == 0 else (slice(half, c), slice(0, half))

        def in_row_order(x_early, x_late):
            return jnp.concatenate([x_early, x_late] if d == 0 else [x_late, x_early], axis=0)

        def emit(rows, cols, o):
            if d == 0:
                acc_ref[rows, cols] = o
            else:
                tot = acc_ref[rows, cols] + o
                o_ref[rows, cols] = (_rms(tot, og_ref[...]) * hg_ref[rows, cols]).astype(o_ref.dtype)

        def two_level_operands(rows):
            f, hi16, lo16 = gates(rows)
            b = _bdot(scan_ref[d], jnp.concatenate([hi16, lo16], axis=0))
            kk = 1.0 - f
            q = hq_ref[rows, :]
            r_mid = b[row(half - 1)]
            x1 = in_row_order(kk[early] * jnp.exp(r_mid - b[early]), q[late] * jnp.exp(b[late] - r_mid)).astype(BF16)
            dq = in_row_order(b[early] - b[row(half // 2 - 1)], b[late] - b[row(half + half // 2 - 1)])
            b_end = b[row(c - 1)]
            span = jnp.maximum(
                jnp.maximum(b[row(0)] - b[row(half // 2 - 1)], b[row(half // 2 - 1)] - b[row(half - 1)]),
                jnp.maximum(b[row(half)] - b[row(half + half // 2 - 1)],
                            b[row(half + half // 2 - 1)] - b[row(c - 1)]))
            return dict(x1=x1, xq=(q * jnp.exp(dq)).astype(BF16), xk=(kk * jnp.exp(-dq)).astype(BF16),
                        q_in=(q * jnp.exp(b)).astype(BF16), k_out=(kk * jnp.exp(b_end - b)).astype(BF16),
                        a_end=jnp.exp(b_end), iv=hi_ref[rows, :].astype(BF16), span=span)

        def two_level_chunk(rows, ops, states, need_state):
            new_states = []
            for hh in range(hb):
                cols = slice(hh * HG_DK, (hh + 1) * HG_DK)
                g1 = _bdot(ops["x1"][:, cols], ops["x1"][:, cols].astype(F32).T.astype(BF16))
                g2 = _bdot(ops["xq"][:, cols], ops["xk"][:, cols].astype(F32).T.astype(BF16))
                att = jnp.where(pair_ref[d, 0] > 0.5, g1, jnp.where(diag_ref[d] > 0.5, g2, 0.0)).astype(BF16)
                st = states[hh]
                o = _bdot(att, ops["iv"][:, cols])
                if st is not None:
                    o = o + lax.dot_general(ops["q_in"][:, cols], st.astype(BF16), nt, preferred_element_type=F32)
                pending.append((rows, cols, o))
                if not need_state:
                    new_states.append(None)
                    continue
                dst = lax.dot_general(ops["iv"][:, cols], ops["k_out"][:, cols], tn, preferred_element_type=F32)
                new_states.append(dst if st is None else ops["a_end"][:, cols] * st + dst)
            return new_states

        def all_levels(rows):
            f, hi16, lo16 = gates(rows)
            cum = cum_ref[d]
            eall = jnp.exp(_bdot(cum, hi16) + _bdot(cum, lo16))
            for hh in range(hb):
                cols = slice(hh * HG_DK, (hh + 1) * HG_DK)
                kk = 1.0 - f[:, cols]
                q = hq_ref[rows, cols]
                iv = hi_ref[rows, cols]
                iv16 = iv.astype(BF16)
                att = jnp.zeros((c, c), F32)
                for l in range(HG_LEVELS):
                    x = (kk + sel_ref[d, l] * (q - kk)) * eall[l * c:(l + 1) * c, cols]
                    xb = x.astype(BF16)
                    att = att + pair_ref[d, l] * lax.dot_general(xb, xb, nt, preferred_element_type=F32)
                e_in = eall[HG_LEVELS * c:(HG_LEVELS + 1) * c, cols]
                e_out = eall[(HG_LEVELS + 1) * c:, cols]
                st = st_ref[hh]
                o = (_bdot(att.astype(BF16), iv16)
                     + jnp.sum(q * kk, axis=-1, keepdims=True) * iv
                     + lax.dot_general((q * e_in).astype(BF16), st.astype(BF16), nt, preferred_element_type=F32))
                dst = lax.dot_general(iv16, (kk * e_out).astype(BF16), tn, preferred_element_type=F32)
                st_ref[hh] = (e_in[0:1] * e_out[0:1]) * st + dst
                emit(rows, cols, o)

        def chunk_start(s, ci):
            cidx = ci if d == 0 else n_chunks - 1 - ci
            return (s * n_chunks + cidx) * c

        worst = jnp.zeros((1, hb * HG_DK), F32)
        pending = []
        states = [[initial_state(s, hh) if has_s0 else None for hh in range(hb)] for s in range(n_seqs)]
        for ci in range(n_chunks):
            for s in range(n_seqs):
                rows = slice(chunk_start(s, ci), chunk_start(s, ci) + c)
                ops = two_level_operands(rows)
                states[s] = two_level_chunk(rows, ops, states[s], need_state=ci < n_chunks - 1 or not has_s0)
                worst = jnp.maximum(worst, ops["span"])
        for item in pending:
            emit(*item)
        if not has_s0:
            for s in range(n_seqs):
                for hh in range(hb):
                    (sf_ref, sb_ref)[d][s, hh] = states[s][hh].T

        @pl.when(jnp.logical_not(jnp.max(worst) <= HG_SAFE_EXPONENT))
        def _():
            for s in range(n_seqs):
                for hh in range(hb):
                    st_ref[hh] = initial_state(s, hh)

                def chunk(ci, carry):
                    all_levels(pl.ds(pl.multiple_of(chunk_start(s, ci), c), c))
                    return carry

                lax.fori_loop(0, n_chunks, chunk, 0)
                if not has_s0:
                    for hh in range(hb):
                        (sf_ref, sb_ref)[d][s, hh] = st_ref[hh].T

    direction(0, zf_ref)
    direction(1, zb_ref)


def _hgrn(p, lb_raw, o_gain, consts, n_batch, seq, row0, n_seqs, hb, s0=None):
    w = hb * HG_DK
    rows = n_seqs * seq

    def seg(col):
        return pl.BlockSpec((rows, w), lambda b, h: (row0 // rows + b, col // w + h))

    def const(a):
        return pl.BlockSpec(a.shape, lambda b, h: (0,) * a.ndim)

    in_specs = [seg(OUT_HQ), seg(OUT_ZF), seg(OUT_ZF + HG_KW), seg(OUT_HI), seg(OUT_HG),
                pl.BlockSpec((2, DEPTH + 1, w), lambda b, h: (0, 0, h)),
                pl.BlockSpec((1, HG_DV), lambda b, h: (0, 0))] + [const(a) for a in consts]
    args = [p, p, p, p, p, lb_raw, o_gain, *consts]
    has_s0 = s0 is not None
    st_spec = pl.BlockSpec((n_seqs, hb, HG_DK, HG_DV), lambda b, h: (b, h, 0, 0))
    if has_s0:
        in_specs += [st_spec, st_spec]
        args += [s0[0], s0[1]]
    out_shape = [jax.ShapeDtypeStruct((n_batch * seq, HG_VW), BF16)]
    out_specs = [pl.BlockSpec((rows, w), lambda b, h: (b, h))]
    if not has_s0:
        st_shape = jax.ShapeDtypeStruct((n_batch, HG_HEADS, HG_DK, HG_DV), F32)
        out_shape += [st_shape, st_shape]
        out_specs += [st_spec, st_spec]
    return pl.pallas_call(
        functools.partial(_hgrn_kernel, n_seqs=n_seqs, n_chunks=seq // HG_CHUNK, has_s0=has_s0),
        out_shape=tuple(out_shape),
        grid=(n_batch // n_seqs, HG_HEADS // hb),
        in_specs=in_specs,
        out_specs=tuple(out_specs),
        scratch_shapes=[pltpu.VMEM((hb, HG_DV, HG_DK), F32), pltpu.VMEM((rows, w), F32)],
        compiler_params=_params(("arbitrary", "arbitrary")),
        name="hgrn_latent" if has_s0 else "hgrn_prompt",
    )(*args)


def _outproj_kernel(attp_ref, atts_ref, hgp_ref, hgs_ref, wa_ref, wb_ref, xp_ref, xs_ref, g1_ref, gain_ref,
                    sh_ref, sc_ref, x1_ref, h2_ref, full_ref, wcache_ref):
    i = pl.program_id(0)
    n = pl.program_id(1)

    @pl.when(i == 0)
    def _():
        wcache_ref[n, 0] = wa_ref[...].astype(BF16)
        wcache_ref[n, 1] = wb_ref[...].astype(BF16)

    def run(att_ref, hg_ref, x_ref):
        acc = _bdot(att_ref[...], wcache_ref[n, 0]) + _bdot(hg_ref[...], wcache_ref[n, 1])
        x1 = x_ref[...] + g1_ref[...] * acc
        x1_ref[...] = x1
        full_ref[n] = x1

    pl.when(i < N_PROMPT_TILES)(lambda: run(attp_ref, hgp_ref, xp_ref))
    pl.when(i >= N_PROMPT_TILES)(lambda: run(atts_ref, hgs_ref, xs_ref))

    @pl.when(n == D_MODEL // OUT_COL_TILE - 1)
    def _():
        nt = D_MODEL // OUT_COL_TILE
        ms = jnp.sum(full_ref[0] * full_ref[0], axis=-1, keepdims=True)
        for k in range(1, nt):
            ms = ms + jnp.sum(full_ref[k] * full_ref[k], axis=-1, keepdims=True)
        rstd = lax.rsqrt(ms / D_MODEL + EPS)
        gs = gain_ref[...] * (1.0 + sc_ref[...])
        for k in range(nt):
            cols = slice(k * OUT_COL_TILE, (k + 1) * OUT_COL_TILE)
            h2_ref[:, cols] = (full_ref[k] * rstd * gs[:, cols] + sh_ref[:, cols]).astype(BF16)


def _out_projection(att_p, att_s, hg_p, hg_s, w_out, xp, xs, mods, norm_mlp):
    tm, tn = TOK_TILE, OUT_COL_TILE
    nt = D_MODEL // tn

    def prompt_rows(width):
        return pl.BlockSpec((tm, width), lambda i, n: (jnp.minimum(i, N_PROMPT_TILES - 1), 0),
                            pipeline_mode=pl.Buffered(1))

    def sample_rows(width):
        return pl.BlockSpec((tm, width), lambda i, n: (jnp.maximum(i - N_PROMPT_TILES, 0), 0),
                            pipeline_mode=pl.Buffered(1))

    return pl.pallas_call(
        _outproj_kernel,
        out_shape=(jax.ShapeDtypeStruct((N_TOK, D_MODEL), F32), jax.ShapeDtypeStruct((N_TOK, D_MODEL), BF16)),
        grid=(N_TILES, nt),
        in_specs=[prompt_rows(ATT_WIDTH), sample_rows(ATT_WIDTH), prompt_rows(HG_VW), sample_rows(HG_VW),
                  pl.BlockSpec((ATT_WIDTH, tn), lambda i, n: (0, jnp.where(i == 0, n, nt - 1))),
                  pl.BlockSpec((HG_VW, tn), lambda i, n: (1, jnp.where(i == 0, n, nt - 1))),
                  pl.BlockSpec((tm, tn), lambda i, n: (jnp.minimum(i, N_PROMPT_TILES - 1),
                                                       jnp.where(i < N_PROMPT_TILES, n, nt - 1))),
                  pl.BlockSpec((tm, tn), lambda i, n: (jnp.maximum(i - N_PROMPT_TILES, 0),
                                                       jnp.where(i < N_PROMPT_TILES, 0, n))),
                  _mod_spec(0, 2, tm, width=tn, col=lambda i, n: n),
                  pl.BlockSpec((1, D_MODEL), lambda i, n: (0, 0)),
                  _mod_spec(0, 3, tm), _mod_spec(0, 4, tm)],
        out_specs=(pl.BlockSpec((tm, tn), lambda i, n: (i, n)),
                   pl.BlockSpec((tm, D_MODEL), lambda i, n: (i, 0))),
        scratch_shapes=[pltpu.VMEM((nt, tm, tn), F32), pltpu.VMEM((nt, 2, ATT_WIDTH, tn), BF16)],
        compiler_params=_params(("arbitrary", "arbitrary")),
        name="out_projection",
    )(att_p, att_s, hg_p, hg_s, w_out, w_out, xp, xs, mods, norm_mlp, mods, mods)


def _mlp_kernel(h_ref, w1_ref, w2_ref, x_ref, g2_ref, *rest, final):
    if final:
        fin_ref, o_ref, res_ref = rest
    else:
        o_ref, res_ref = rest
    j = pl.program_id(1)
    res_ref[j] = x_ref[...]

    def step(first, last):
        a = jnp.square(jnp.maximum(_bdot(h_ref[...], w1_ref[...].astype(BF16)), 0.0)).astype(BF16)
        for n in range(D_MODEL // MLP_OUT_CHUNK):
            cols = slice(n * MLP_OUT_CHUNK, (n + 1) * MLP_OUT_CHUNK)
            p = _bdot(a, w2_ref[:, cols].astype(BF16))
            if not first:
                p = o_ref[:, cols] + p
            if last:
                per = MLP_OUT_CHUNK // MLP_RES_COLS
                res = jnp.concatenate([res_ref[n * per + k] for k in range(per)], axis=1)
                p = res + g2_ref[:, cols] * p
            o_ref[:, cols] = p
        if last and final:
            o_ref[...] = _rms(o_ref[...], fin_ref[...])

    pl.when(j == 0)(lambda: step(True, False))
    pl.when(jnp.logical_and(j > 0, j < MLP_STEPS - 1))(lambda: step(False, False))
    pl.when(j == MLP_STEPS - 1)(lambda: step(False, True))


def _mlp(h, x, w1, w2, mods, layer, tile0, n_tiles, final_norm=None):
    tm, th = TOK_TILE, FF_TILE
    final = final_norm is not None
    in_specs = [pl.BlockSpec((tm, D_MODEL), lambda i, j: (tile0 + i, 0), pipeline_mode=pl.Buffered(1)),
                pl.BlockSpec((None, D_MODEL, th), lambda i, j: (layer, 0, j)),
                pl.BlockSpec((None, th, D_MODEL), lambda i, j: (layer, j, 0)),
                pl.BlockSpec((tm, MLP_RES_COLS), lambda i, j: (tile0 + i, j)),
                _mod_spec(layer, 5, tm, tile_of=lambda i, j: tile0 + i)]
    args = [h, w1, w2, x, mods]
    if final:
        in_specs.append(pl.BlockSpec((1, D_MODEL), lambda i, j: (0, 0)))
        args.append(final_norm)
    return pl.pallas_call(
        functools.partial(_mlp_kernel, final=final),
        out_shape=jax.ShapeDtypeStruct((n_tiles * tm, D_MODEL), F32),
        grid=(n_tiles, MLP_STEPS),
        in_specs=in_specs,
        out_specs=pl.BlockSpec((tm, D_MODEL), lambda i, j: (i, 0)),
        scratch_shapes=[pltpu.VMEM((MLP_STEPS, tm, MLP_RES_COLS), F32)],
        compiler_params=_params(("arbitrary", "arbitrary")),
        name="mlp_final" if final else "mlp",
    )(*args)


def _pool_kernel(x_ref, wp_ref, ps_ref, gain1_ref, sh1_ref, sc1_ref, g1_ref, gain2_ref, sh2_ref, sc2_ref,
                 x3_ref, h4_ref):
    i = pl.program_id(0)
    tm = TOK_TILE
    rstd = lax.rsqrt(jnp.mean(x_ref[...] * x_ref[...], axis=-1, keepdims=True) + EPS)
    gs1 = gain1_ref[...] * (1.0 + sc1_ref[...])
    mix_gain = g1_ref[...] * ps_ref[...]

    def widen(m):
        return jnp.concatenate([m] * (POOL_GROUP // HEAD_DIM), axis=1)

    def mix_tile(seq):
        n_seq = tm // seq
        pitch = seq + 2 * POOL_HALO
        n_pad = n_seq * pitch
        halo = jnp.zeros((POOL_HALO, POOL_GROUP), F32)
        pos = lax.broadcasted_iota(jnp.int32, (seq, HEAD_DIM), 0)

        def down(a, k):
            return pltpu.roll(a, k % n_pad, 0)

        for g, w in enumerate(POOL_WINDOWS):
            half = w // 2
            cols = slice(g * POOL_GROUP, (g + 1) * POOL_GROUP)
            x = x_ref[:, cols]
            h = x * rstd * gs1[:, cols] + sh1_ref[:, cols]
            padded = jnp.concatenate(
                [piece for s in range(n_seq) for piece in (halo, h[s * seq:(s + 1) * seq], halo)], axis=0)
            back = padded
            m = 1
            while m < half:
                back = back + down(back, m)
                m *= 2
            ahead = back if half == 1 else down(back, -(half - 1))
            total = down(back, 1) + ahead
            total = jnp.concatenate(
                [total[s * pitch + POOL_HALO:s * pitch + POOL_HALO + seq] for s in range(n_seq)], axis=0)
            count = (jnp.minimum(pos + (w - half), seq) - jnp.maximum(pos - half, 0)).astype(F32)
            inv = jnp.concatenate([widen(1.0 / count)] * n_seq, axis=0)
            pooled = (total * inv - h).astype(BF16)
            mix = _bdot(pooled, wp_ref[g].astype(BF16))
            x3_ref[:, cols] = x + mix_gain[:, cols] * mix

    pl.when(i < N_PROMPT_TILES)(lambda: mix_tile(SEQ))
    pl.when(i >= N_PROMPT_TILES)(lambda: mix_tile(DEC_SEQ))
    x3 = x3_ref[...]
    rstd3 = lax.rsqrt(jnp.mean(x3 * x3, axis=-1, keepdims=True) + EPS)
    h4_ref[...] = (x3 * rstd3 * (gain2_ref[...] * (1.0 + sc2_ref[...])) + sh2_ref[...]).astype(BF16)


def _pool_mixer(x, w_pool, pool_scale, norm_mix, norm_mlp, mods):
    tm = TOK_TILE
    vec = pl.BlockSpec((1, D_MODEL), lambda i: (0, 0))
    tile = pl.BlockSpec((tm, D_MODEL), lambda i: (i, 0))
    return pl.pallas_call(
        _pool_kernel,
        out_shape=(jax.ShapeDtypeStruct((N_TOK, D_MODEL), F32), jax.ShapeDtypeStruct((N_TOK, D_MODEL), BF16)),
        grid=(N_TILES,),
        in_specs=[tile, pl.BlockSpec((len(POOL_WINDOWS), POOL_GROUP, POOL_GROUP), lambda i: (0, 0, 0)),
                  vec, vec, _mod_spec(1, 0, tm), _mod_spec(1, 1, tm), _mod_spec(1, 2, tm),
                  vec, _mod_spec(1, 3, tm), _mod_spec(1, 4, tm)],
        out_specs=(tile, tile),
        compiler_params=_params(("arbitrary",)),
        name="pool_mixer",
    )(x, w_pool, pool_scale, norm_mix, mods, mods, mods, norm_mlp, mods, mods)


def kernel(x_prompt, x_sample, cache_k, cache_v, state_hgrn_fwd, state_hgrn_bwd, c, c_ctx, w_ada, b_ada,
           norm_mix, norm_mlp, w_in_ab, w_out_ab, q_norm, k_norm, hg_norm, lb_raw, w_pool, pool_scale,
           w_mlp_in, w_mlp_out, final_norm):
    xp = x_prompt.reshape(N_PROMPT, D_MODEL)
    xs = x_sample.reshape(N_SAMPLE, D_MODEL)
    cv = jnp.concatenate([c_ctx[None, :], c, jnp.zeros((ADA_ROWS - 1 - DEC_BATCH, D_MODEL), F32)], axis=0)
    mods, h0 = _ada_table_and_modulate(cv, w_ada, b_ada, xp, xs, norm_mix)

    proj, new_k, new_v = _in_projection(h0, w_in_ab[0], q_norm[0:1], k_norm[0:1])
    att_p = _attention(proj, BATCH, SEQ, 0, SEQ)
    att_s = _attention(proj, DEC_BATCH, DEC_SEQ, N_PROMPT, 512, ctx=(cache_k, cache_v))
    consts = _hgrn_constants()
    hg_p, s_fwd, s_bwd = _hgrn(proj, lb_raw, hg_norm[0:1], consts, BATCH, SEQ, 0, HG_PROMPT_SEQS, HG_PROMPT_HEADS)
    s0 = (state_hgrn_fwd.reshape(DEC_BATCH, HG_HEADS, HG_DK, HG_DV),
          state_hgrn_bwd.reshape(DEC_BATCH, HG_HEADS, HG_DK, HG_DV))
    (hg_s,) = _hgrn(proj, lb_raw, hg_norm[0:1], consts, DEC_BATCH, DEC_SEQ, N_PROMPT, 1, HG_LATENT_HEADS, s0=s0)
    x1, h2 = _out_projection(att_p, att_s, hg_p, hg_s, w_out_ab[0], xp, xs, mods, norm_mlp[0:1])
    x2 = _mlp(h2, x1, w_mlp_in, w_mlp_out, mods, 0, 0, N_TILES)

    x3, h4 = _pool_mixer(x2, w_pool[0], pool_scale[0:1], norm_mix[1:2], norm_mlp[1:2], mods)
    fin = final_norm[None, :]
    y_prompt = _mlp(h4, x3, w_mlp_in, w_mlp_out, mods, 1, 0, N_PROMPT_TILES, final_norm=fin)
    y_sample = _mlp(h4, x3, w_mlp_in, w_mlp_out, mods, 1, N_PROMPT_TILES, N_TILES - N_PROMPT_TILES,
                    final_norm=fin)

    return (y_prompt.reshape(BATCH, SEQ, D_MODEL), y_sample.reshape(DEC_BATCH, DEC_SEQ, D_MODEL),
            new_k, new_v,
            s_fwd.reshape(BATCH, 1, HG_HEADS, HG_DK, HG_DV), s_bwd.reshape(BATCH, 1, HG_HEADS, HG_DK, HG_DV))
```

```python
import functools

import numpy as np
import jax
import jax.numpy as jnp
from jax import lax
from jax.experimental import pallas as pl
from jax.experimental.pallas import tpu as pltpu

F32 = jnp.float32
BF16 = jnp.bfloat16

D_MODEL = 2048
BATCH = 16
SEQ = 256
DEPTH = 2
DEC_BATCH = 2
DEC_SEQ = 1024
PAST_LEN = 256
GRID_W = 64
HEAD_DIM = 128
N_Q_HEADS = 8
N_KV_HEADS = 2
Q_PER_KV = N_Q_HEADS // N_KV_HEADS
ATT_WIDTH = N_Q_HEADS * HEAD_DIM
KV_WIDTH = N_KV_HEADS * HEAD_DIM
HG_HEADS = 8
HG_DK = 128
HG_DV = 128
HG_KW = HG_HEADS * HG_DK
HG_VW = HG_HEADS * HG_DV
IN_AB = ATT_WIDTH + 2 * KV_WIDTH + 3 * HG_KW + 2 * HG_VW
MIX_WIDTH = ATT_WIDTH + HG_VW
POOL_WINDOWS = (2, 4, 8, 16)
POOL_GROUP = D_MODEL // len(POOL_WINDOWS)
POOL_HALO = 8
D_FF = 4 * D_MODEL
ROPE_THETA = 10000.0
ROPE_HALF = HEAD_DIM // 2
EPS = 1e-6
N_MOD = 6

N_PROMPT = BATCH * SEQ
N_SAMPLE = DEC_BATCH * DEC_SEQ
N_TOK = N_PROMPT + N_SAMPLE
ADA_ROWS = 16
ADA_COL_TILE = 1024
ADA_KEEP = 2 * D_MODEL // ADA_COL_TILE
MOD0_TOK_TILE = 512
TOK_TILE = 1024
N_TILES = N_TOK // TOK_TILE
N_PROMPT_TILES = N_PROMPT // TOK_TILE
PROJ_TOK_TILE = 2048
PROJ_COL_TILE = 512
OUT_COL_TILE = 512
FF_TILE = 512
MLP_STEPS = D_FF // FF_TILE
MLP_RES_COLS = D_MODEL // MLP_STEPS
MLP_OUT_CHUNK = 512
HG_CHUNK = 128
HG_LEVELS = 7
HG_PROMPT_HEADS = 8
HG_PROMPT_SEQS = 1
HG_LATENT_HEADS = 4
HG_SAFE_EXPONENT = 80.0
V7X_VMEM_BYTES = 64 * 2 ** 20
VMEM_LIMIT = V7X_VMEM_BYTES - 8 * 2 ** 20

COL_Q = 0
COL_K = ATT_WIDTH
COL_V = COL_K + KV_WIDTH
COL_HQ = COL_V + KV_WIDTH
COL_ZF = COL_HQ + HG_KW
COL_HI = COL_ZF + 2 * HG_KW
COL_HG = COL_HI + HG_VW
OUT_HQ = ATT_WIDTH
OUT_ZF = OUT_HQ + HG_KW
OUT_HI = OUT_ZF + 2 * HG_KW
OUT_HG = OUT_HI + HG_VW
OUT_K = OUT_HG + HG_VW
OUT_V = OUT_K + KV_WIDTH


def _params(semantics):
    return pltpu.CompilerParams(dimension_semantics=semantics, vmem_limit_bytes=VMEM_LIMIT)


def _sigmoid(x):
    return 1.0 / (1.0 + jnp.exp(-x))


def _silu(x):
    return x * _sigmoid(x)


def _rms(x, gain):
    return x * lax.rsqrt(jnp.mean(x * x, axis=-1, keepdims=True) + EPS) * gain


def _bdot(a, b):
    return jnp.dot(a, b, preferred_element_type=F32)


def _mod_row(tile, tile_rows):
    first = N_PROMPT // tile_rows
    per_seq = DEC_SEQ // tile_rows
    return jnp.where(tile < first, 0, 1 + (tile - first) // per_seq)


def _mod_spec(layer, chunk, tile_rows, width=D_MODEL, col=lambda *g: 0, tile_of=lambda *g: g[0]):
    per = D_MODEL // width
    return pl.BlockSpec((None, None, 1, width),
                        lambda *g: (layer, _mod_row(tile_of(*g), tile_rows), 0, chunk * per + col(*g)))


def _modulate(x, gain, shift, scale):
    return x * lax.rsqrt(jnp.mean(x * x, axis=-1, keepdims=True) + EPS) * (gain * (1.0 + scale)) + shift


def _ada_kernel(cv_ref, w_ref, b_ref, xp_ref, xs_ref, gain_ref, o_ref, h_ref, keep_ref):
    s = pl.program_id(0)
    tile = _bdot(_silu(cv_ref[...]).astype(BF16), w_ref[...].astype(BF16)) + b_ref[...]
    o_ref[:, 0, :] = tile

    @pl.when(s < ADA_KEEP)
    def _():
        keep_ref[s] = tile

    t = s - ADA_KEEP
    n_prompt = N_PROMPT // MOD0_TOK_TILE

    def run(x_ref):
        r = _mod_row(t, MOD0_TOK_TILE)
        per = D_MODEL // ADA_COL_TILE
        shift = jnp.concatenate([keep_ref[k, pl.ds(r, 1), :] for k in range(per)], axis=1)
        scale = jnp.concatenate([keep_ref[per + k, pl.ds(r, 1), :] for k in range(per)], axis=1)
        h_ref[...] = _modulate(x_ref[...], gain_ref[...], shift, scale).astype(BF16)

    pl.when(jnp.logical_and(t >= 0, t < n_prompt))(lambda: run(xp_ref))
    pl.when(jnp.logical_and(t >= n_prompt, t < N_TOK // MOD0_TOK_TILE))(lambda: run(xs_ref))


def _ada_table_and_modulate(cv, w_ada, b_ada, xp, xs, norm_mix):
    tn, tm = ADA_COL_TILE, MOD0_TOK_TILE
    n = N_MOD * D_MODEL
    per_layer = n // tn
    n_prompt = N_PROMPT // tm
    n_tiles = N_TOK // tm
    assert DEPTH * per_layer >= ADA_KEEP + n_tiles

    def tok(s):
        return jnp.clip(s - ADA_KEEP, 0, n_tiles - 1)

    return pl.pallas_call(
        _ada_kernel,
        out_shape=(jax.ShapeDtypeStruct((DEPTH, ADA_ROWS, 1, n), F32), jax.ShapeDtypeStruct((N_TOK, D_MODEL), BF16)),
        grid=(DEPTH * per_layer,),
        in_specs=[pl.BlockSpec((ADA_ROWS, D_MODEL), lambda s: (0, 0)),
                  pl.BlockSpec((None, D_MODEL, tn), lambda s: (s // per_layer, 0, s % per_layer)),
                  pl.BlockSpec((None, 1, tn), lambda s: (s // per_layer, 0, s % per_layer)),
                  pl.BlockSpec((tm, D_MODEL), lambda s: (jnp.minimum(tok(s), n_prompt - 1), 0)),
                  pl.BlockSpec((tm, D_MODEL), lambda s: (jnp.maximum(tok(s) - n_prompt, 0), 0)),
                  pl.BlockSpec((1, D_MODEL), lambda s: (0, 0))],
        out_specs=(pl.BlockSpec((None, ADA_ROWS, 1, tn), lambda s: (s // per_layer, 0, 0, s % per_layer)),
                   pl.BlockSpec((tm, D_MODEL), lambda s: (tok(s), 0))),
        scratch_shapes=[pltpu.VMEM((ADA_KEEP, ADA_ROWS, tn), F32)],
        compiler_params=_params(("arbitrary",)),
        name="ada_table_modulate0",
    )(cv, w_ada, b_ada.reshape(DEPTH, 1, n), xp, xs, norm_mix[0:1])


def _rope(y, cos, sin, perm2):
    hi = y.astype(BF16)
    lo = (y - hi.astype(F32)).astype(BF16)
    rot = _bdot(jnp.concatenate([hi, lo], axis=1), perm2)
    return y * cos + rot * sin


def _inproj_kernel(h_ref, w_ref, qg_ref, kg_ref, cos_ref, sin_ref, perm_ref, p_ref, nk_ref, nv_ref):
    i = pl.program_id(0)
    j = pl.program_id(1)
    latent = i >= N_PROMPT // PROJ_TOK_TILE
    heads = PROJ_COL_TILE // HEAD_DIM
    kv_tile = COL_K // PROJ_COL_TILE
    is_q = j < kv_tile

    def attention_tile(rope):
        gain = jnp.where(is_q, qg_ref[...], kg_ref[...])
        w = w_ref[...].astype(BF16)
        for s in range(PROJ_TOK_TILE // DEC_SEQ):
            rows = slice(s * DEC_SEQ, (s + 1) * DEC_SEQ)
            acc = _bdot(h_ref[rows, :], w)
            for hh in range(heads):
                cols = slice(hh * HEAD_DIM, (hh + 1) * HEAD_DIM)
                x = acc[:, cols]
                y = _rms(x, gain)
                if rope:
                    y = _rope(y, cos_ref[...], sin_ref[...], perm_ref[...])
                if hh >= N_KV_HEADS:
                    y = jnp.where(is_q, y, x)
                p_ref[rows, cols] = y

    pl.when(jnp.logical_and(j <= kv_tile, latent))(lambda: attention_tile(True))
    pl.when(jnp.logical_and(j <= kv_tile, jnp.logical_not(latent)))(lambda: attention_tile(False))

    @pl.when(jnp.logical_and(j == kv_tile, jnp.logical_not(latent)))
    def _():
        for s in range(PROJ_TOK_TILE // SEQ):
            rows = slice(s * SEQ, (s + 1) * SEQ)
            for hh in range(N_KV_HEADS):
                nk_ref[s, :, hh, :] = p_ref[rows, hh * HEAD_DIM:(hh + 1) * HEAD_DIM]
                nv_ref[s, :, hh, :] = p_ref[rows, KV_WIDTH + hh * HEAD_DIM:KV_WIDTH + (hh + 1) * HEAD_DIM]

    def plain_tile(act):
        w = w_ref[...].astype(BF16)
        for s in range(PROJ_TOK_TILE // DEC_SEQ):
            rows = slice(s * DEC_SEQ, (s + 1) * DEC_SEQ)
            p_ref[rows, :] = act(_bdot(h_ref[rows, :], w))

    is_gate = jnp.logical_and(j >= COL_ZF // PROJ_COL_TILE, j < COL_HI // PROJ_COL_TILE)
    is_value = jnp.logical_and(j >= COL_HI // PROJ_COL_TILE, j < COL_HG // PROJ_COL_TILE)
    raw = jnp.logical_or(is_gate, is_value)
    pl.when(raw)(lambda: plain_tile(lambda a: a))
    pl.when(jnp.logical_and(j > kv_tile, jnp.logical_not(raw)))(lambda: plain_tile(_silu))


def _rope_tables():
    t = np.arange(DEC_SEQ)
    row = (t // GRID_W).astype(np.float32)
    col = (t % GRID_W).astype(np.float32)
    inv = (np.float32(ROPE_THETA) ** (-np.arange(0, ROPE_HALF, 2, dtype=np.float32) / np.float32(ROPE_HALF))).astype(np.float32)
    ar = row[:, None] * inv
    ac = col[:, None] * inv
    ang = np.concatenate([ar, ar, ac, ac], axis=-1).astype(np.float32)
    cos = np.cos(ang).astype(np.float32)
    sin = np.sin(ang).astype(np.float32)
    qw = ROPE_HALF // 2
    perm = np.zeros((HEAD_DIM, HEAD_DIM), np.float32)
    for k in range(qw):
        perm[qw + k, k] = -1.0
        perm[k, qw + k] = 1.0
        perm[3 * qw + k, 2 * qw + k] = -1.0
        perm[2 * qw + k, 3 * qw + k] = 1.0
    return jnp.asarray(cos), jnp.asarray(sin), jnp.asarray(np.concatenate([perm, perm], axis=0), BF16)


def _in_projection(h, w_in, q_gain, k_gain):
    tm, tn = PROJ_TOK_TILE, PROJ_COL_TILE
    n_prompt_tiles = N_PROMPT // tm
    cos, sin, perm2 = _rope_tables()
    table = pl.BlockSpec((DEC_SEQ, HEAD_DIM), lambda i, j: (0, 0))
    gain = pl.BlockSpec((1, HEAD_DIM), lambda i, j: (0, 0))
    state = pl.BlockSpec((tm // SEQ, None, SEQ, N_KV_HEADS, HEAD_DIM),
                         lambda i, j: (jnp.minimum(i, n_prompt_tiles - 1), 0, 0, 0, 0),
                         pipeline_mode=pl.Buffered(1))
    state_shape = jax.ShapeDtypeStruct((BATCH, 1, SEQ, N_KV_HEADS, HEAD_DIM), F32)
    kv_tile = COL_K // tn

    def out_tile(j):
        return jnp.where(j < kv_tile, j, jnp.where(j == kv_tile, OUT_K // tn, j - 1))

    return pl.pallas_call(
        _inproj_kernel,
        out_shape=(jax.ShapeDtypeStruct((N_TOK, IN_AB), F32), state_shape, state_shape),
        grid=(N_TOK // tm, IN_AB // tn),
        in_specs=[pl.BlockSpec((tm, D_MODEL), lambda i, j: (i, 0)),
                  pl.BlockSpec((D_MODEL, tn), lambda i, j: (0, j)),
                  gain, gain, table, table, pl.BlockSpec((2 * HEAD_DIM, HEAD_DIM), lambda i, j: (0, 0))],
        out_specs=(pl.BlockSpec((tm, tn), lambda i, j: (i, out_tile(j))), state, state),
        compiler_params=_params(("arbitrary", "arbitrary")),
        name="in_projection",
    )(h, w_in, q_gain, k_gain, cos, sin, perm2)


def _attn_kernel(*refs, has_ctx, stack):
    if has_ctx:
        q_ref, k_ref, v_ref, ck_ref, cv_ref, o_ref = refs
    else:
        q_ref, k_ref, v_ref, o_ref = refs
    scale = HEAD_DIM ** -0.5
    nt = (((1,), (1,)), ((), ()))
    tq = q_ref.shape[0]
    for hk in range(k_ref.shape[1] // HEAD_DIM):
        kcols = slice(hk * HEAD_DIM, (hk + 1) * HEAD_DIM)
        k = k_ref[:, kcols].astype(BF16)
        v = v_ref[:, kcols].astype(BF16)
        if has_ctx:
            ck = ck_ref[:, hk, :].astype(BF16)
            cv = cv_ref[:, hk, :].astype(BF16)
        for g0 in range(hk * Q_PER_KV, (hk + 1) * Q_PER_KV, stack):
            q = jnp.concatenate([q_ref[:, g * HEAD_DIM:(g + 1) * HEAD_DIM] for g in range(g0, g0 + stack)], axis=0)
            q = (q * scale).astype(BF16)
            s = lax.dot_general(q, k, nt, preferred_element_type=F32)
            m = jnp.max(s, axis=-1, keepdims=True)
            if has_ctx:
                sc = lax.dot_general(q, ck, nt, preferred_element_type=F32)
                m = jnp.maximum(m, jnp.max(sc, axis=-1, keepdims=True))
            p = jnp.exp(s - m)
            den = jnp.sum(p, axis=-1, keepdims=True)
            o = _bdot(p.astype(BF16), v)
            if has_ctx:
                pc = jnp.exp(sc - m)
                den = den + jnp.sum(pc, axis=-1, keepdims=True)
                o = o + _bdot(pc.astype(BF16), cv)
            o = (o / den).astype(o_ref.dtype)
            for g in range(stack):
                o_ref[:, (g0 + g) * HEAD_DIM:(g0 + g + 1) * HEAD_DIM] = o[g * tq:(g + 1) * tq]


def _attention(p, n_batch, seq, row0, tq, ctx=None):
    q_blocks = seq // tq
    kvh = N_KV_HEADS
    gw = kvh * Q_PER_KV * HEAD_DIM
    kw = kvh * HEAD_DIM
    in_specs = [
        pl.BlockSpec((tq, gw), lambda b, h, qi: (row0 // tq + b * q_blocks + qi, h)),
        pl.BlockSpec((seq, kw), lambda b, h, qi: (row0 // seq + b, OUT_K // kw + h)),
        pl.BlockSpec((seq, kw), lambda b, h, qi: (row0 // seq + b, OUT_V // kw + h)),
    ]
    args = [p, p, p]
    if ctx is not None:
        assert kvh == N_KV_HEADS
        ctx_spec = pl.BlockSpec((None, None, PAST_LEN, N_KV_HEADS, HEAD_DIM), lambda b, h, qi: (b, 0, 0, 0, 0))
        in_specs += [ctx_spec, ctx_spec]
        args += [ctx[0], ctx[1]]
    return pl.pallas_call(
        functools.partial(_attn_kernel, has_ctx=ctx is not None, stack=Q_PER_KV if ctx is None else 1),
        out_shape=jax.ShapeDtypeStruct((n_batch * seq, ATT_WIDTH), BF16),
        grid=(n_batch, N_KV_HEADS // kvh, q_blocks),
        in_specs=in_specs,
        out_specs=pl.BlockSpec((tq, gw), lambda b, h, qi: (b * q_blocks + qi, h)),
        compiler_params=_params(("arbitrary", "arbitrary", "arbitrary")),
        name="attention_latent" if ctx is not None else "attention_prompt",
    )(*args)


def _hgrn_constants():
    c = HG_CHUNK
    t = np.arange(c)
    cums, sels, pairs, scans, diags = [], [], [], [], []
    for d in range(2):
        pos = t if d == 0 else c - 1 - t
        pu, pt = pos[None, :], pos[:, None]
        ms, ss, ws = [], [], []
        for l in range(HG_LEVELS):
            m = c >> l
            blk = pos // m
            mid = (blk * m + m // 2)[:, None]
            late = ((pos % m) >= m // 2)
            ms.append(np.where(late[:, None], (pu >= mid) & (pu <= pt), (pu > pt) & (pu < mid)))
            ss.append(np.broadcast_to(late[:, None], (c, c)))
            ws.append((blk[:, None] == blk[None, :]) & late[:, None] & ~late[None, :])
        ms.append(pu <= pt)
        ms.append(pu > pt)
        cums.append(np.concatenate(ms, axis=0))
        sels.append(np.stack(ss))
        pairs.append(np.stack(ws))
        scans.append(np.concatenate([pu <= pt, pu <= pt], axis=1))
        diags.append(((pos // (c // 2))[:, None] == (pos // (c // 2))[None, :]) & (pu <= pt))
    return (jnp.asarray(np.stack(cums), BF16), jnp.asarray(np.stack(sels), F32),
            jnp.asarray(np.stack(pairs), F32), jnp.asarray(np.stack(scans), BF16),
            jnp.asarray(np.stack(diags), F32))


def _hgrn_kernel(*refs, n_seqs, n_chunks, has_s0):
    (hq_ref, zf_ref, zb_ref, hi_ref, hg_ref, lb_ref, og_ref, cum_ref, sel_ref, pair_ref, scan_ref,
     diag_ref) = refs[:12]
    refs = refs[12:]
    if has_s0:
        s0f_ref, s0b_ref, o_ref = refs[:3]
        refs = refs[3:]
    else:
        o_ref, sf_ref, sb_ref = refs[:3]
        refs = refs[3:]
    st_ref, acc_ref = refs
    hb = st_ref.shape[0]
    c = HG_CHUNK
    half = c // 2
    nt = (((1,), (1,)), ((), ()))
    tn = (((0,), (0,)), ((), ()))

    def direction(d, z_ref):
        raw = lb_ref[d]
        e = jnp.exp(raw - jnp.max(raw, axis=0, keepdims=True))
        lb = e[0:1] / jnp.sum(e, axis=0, keepdims=True)

        def initial_state(s, hh):
            if has_s0:
                return (s0f_ref, s0b_ref)[d][s, hh].T
            return jnp.zeros((HG_DV, HG_DK), F32)

        def gates(rows):
            f = lb + (1.0 - lb) * _sigmoid(z_ref[rows, :])
            logf = jnp.log(f)
            hi16 = logf.astype(BF16)
            lo16 = (logf - hi16.astype(F32)).astype(BF16)
            return f, hi16, lo16

        def row(p):
            t = p if d == 0 else c - 1 - p
            return slice(t, t + 1)

        early, late = (slice(0, half), slice(half, c)) if d == 0 else (slice(half, c), slice(0, half))

        def in_row_order(x_early, x_late):
            return jnp.concatenate([x_early, x_late] if d == 0 else [x_late, x_early], axis=0)

        def emit(rows, cols, o):
            if d == 0:
                acc_ref[rows, cols] = o
            else:
                tot = acc_ref[rows, cols] + o
                o_ref[rows, cols] = (_rms(tot, og_ref[...]) * hg_ref[rows, cols]).astype(o_ref.dtype)

        def two_level_operands(rows):
            f, hi16, lo16 = gates(rows)
            b = _bdot(scan_ref[d], jnp.concatenate([hi16, lo16], axis=0))
            kk = 1.0 - f
            q = hq_ref[rows, :]
            r_mid = b[row(half - 1)]
            x1 = in_row_order(kk[early] * jnp.exp(r_mid - b[early]), q[late] * jnp.exp(b[late] - r_mid)).astype(BF16)
            dq = in_row_order(b[early] - b[row(half // 2 - 1)], b[late] - b[row(half + half // 2 - 1)])
            b_end = b[row(c - 1)]
            span = jnp.maximum(
                jnp.maximum(b[row(0)] - b[row(half // 2 - 1)], b[row(half // 2 - 1)] - b[row(half - 1)]),
                jnp.maximum(b[row(half)] - b[row(half + half // 2 - 1)],
                            b[row(half + half // 2 - 1)] - b[row(c - 1)]))
            return dict(x1=x1, xq=(q * jnp.exp(dq)).astype(BF16), xk=(kk * jnp.exp(-dq)).astype(BF16),
                        q_in=(q * jnp.exp(b)).astype(BF16), k_out=(kk * jnp.exp(b_end - b)).astype(BF16),
                        a_end=jnp.exp(b_end), iv=hi_ref[rows, :].astype(BF16), span=span)

        def two_level_chunk(rows, ops, states, need_state):
            new_states = []
            for hh in range(hb):
                cols = slice(hh * HG_DK, (hh + 1) * HG_DK)
                g1 = _bdot(ops["x1"][:, cols], ops["x1"][:, cols].astype(F32).T.astype(BF16))
                g2 = _bdot(ops["xq"][:, cols], ops["xk"][:, cols].astype(F32).T.astype(BF16))
                att = jnp.where(pair_ref[d, 0] > 0.5, g1, jnp.where(diag_ref[d] > 0.5, g2, 0.0)).astype(BF16)
                st = states[hh]
                o = _bdot(att, ops["iv"][:, cols])
                if st is not None:
                    o = o + lax.dot_general(ops["q_in"][:, cols], st.astype(BF16), nt, preferred_element_type=F32)
                pending.append((rows, cols, o))
                if not need_state:
                    new_states.append(None)
                    continue
                dst = lax.dot_general(ops["iv"][:, cols], ops["k_out"][:, cols], tn, preferred_element_type=F32)
                new_states.append(dst if st is None else ops["a_end"][:, cols] * st + dst)
            return new_states

        def all_levels(rows):
            f, hi16, lo16 = gates(rows)
            cum = cum_ref[d]
            eall = jnp.exp(_bdot(cum, hi16) + _bdot(cum, lo16))
            for hh in range(hb):
                cols = slice(hh * HG_DK, (hh + 1) * HG_DK)
                kk = 1.0 - f[:, cols]
                q = hq_ref[rows, cols]
                iv = hi_ref[rows, cols]
                iv16 = iv.astype(BF16)
                att = jnp.zeros((c, c), F32)
                for l in range(HG_LEVELS):
                    x = (kk + sel_ref[d, l] * (q - kk)) * eall[l * c:(l + 1) * c, cols]
                    xb = x.astype(BF16)
                    att = att + pair_ref[d, l] * lax.dot_general(xb, xb, nt, preferred_element_type=F32)
                e_in = eall[HG_LEVELS * c:(HG_LEVELS + 1) * c, cols]
                e_out = eall[(HG_LEVELS + 1) * c:, cols]
                st = st_ref[hh]
                o = (_bdot(att.astype(BF16), iv16)
                     + jnp.sum(q * kk, axis=-1, keepdims=True) * iv
                     + lax.dot_general((q * e_in).astype(BF16), st.astype(BF16), nt, preferred_element_type=F32))
                dst = lax.dot_general(iv16, (kk * e_out).astype(BF16), tn, preferred_element_type=F32)
                st_ref[hh] = (e_in[0:1] * e_out[0:1]) * st + dst
                emit(rows, cols, o)

        def chunk_start(s, ci):
            cidx = ci if d == 0 else n_chunks - 1 - ci
            return (s * n_chunks + cidx) * c

        worst = jnp.zeros((1, hb * HG_DK), F32)
        pending = []
        states = [[initial_state(s, hh) if has_s0 else None for hh in range(hb)] for s in range(n_seqs)]
        for ci in range(n_chunks):
            for s in range(n_seqs):
                rows = slice(chunk_start(s, ci), chunk_start(s, ci) + c)
                ops = two_level_operands(rows)
                states[s] = two_level_chunk(rows, ops, states[s], need_state=ci < n_chunks - 1 or not has_s0)
                worst = jnp.maximum(worst, ops["span"])
        for item in pending:
            emit(*item)
        if not has_s0:
            for s in range(n_seqs):
                for hh in range(hb):
                    (sf_ref, sb_ref)[d][s, hh] = states[s][hh].T

        @pl.when(jnp.logical_not(jnp.max(worst) <= HG_SAFE_EXPONENT))
        def _():
            for s in range(n_seqs):
                for hh in range(hb):
                    st_ref[hh] = initial_state(s, hh)

                def chunk(ci, carry):
                    all_levels(pl.ds(pl.multiple_of(chunk_start(s, ci), c), c))
                    return carry

                lax.fori_loop(0, n_chunks, chunk, 0)
                if not has_s0:
                    for hh in range(hb):
                        (sf_ref, sb_ref)[d][s, hh] = st_ref[hh].T

    direction(0, zf_ref)
    direction(1, zb_ref)


def _hgrn(p, lb_raw, o_gain, consts, n_batch, seq, row0, n_seqs, hb, s0=None):
    w = hb * HG_DK
    rows = n_seqs * seq

    def seg(col):
        return pl.BlockSpec((rows, w), lambda b, h: (row0 // rows + b, col // w + h))

    def const(a):
        return pl.BlockSpec(a.shape, lambda b, h: (0,) * a.ndim)

    in_specs = [seg(OUT_HQ), seg(OUT_ZF), seg(OUT_ZF + HG_KW), seg(OUT_HI), seg(OUT_HG),
                pl.BlockSpec((2, DEPTH + 1, w), lambda b, h: (0, 0, h)),
                pl.BlockSpec((1, HG_DV), lambda b, h: (0, 0))] + [const(a) for a in consts]
    args = [p, p, p, p, p, lb_raw, o_gain, *consts]
    has_s0 = s0 is not None
    st_spec = pl.BlockSpec((n_seqs, hb, HG_DK, HG_DV), lambda b, h: (b, h, 0, 0))
    if has_s0:
        in_specs += [st_spec, st_spec]
        args += [s0[0], s0[1]]
    out_shape = [jax.ShapeDtypeStruct((n_batch * seq, HG_VW), BF16)]
    out_specs = [pl.BlockSpec((rows, w), lambda b, h: (b, h))]
    if not has_s0:
        st_shape = jax.ShapeDtypeStruct((n_batch, HG_HEADS, HG_DK, HG_DV), F32)
        out_shape += [st_shape, st_shape]
        out_specs += [st_spec, st_spec]
    return pl.pallas_call(
        functools.partial(_hgrn_kernel, n_seqs=n_seqs, n_chunks=seq // HG_CHUNK, has_s0=has_s0),
        out_shape=tuple(out_shape),
        grid=(n_batch // n_seqs, HG_HEADS // hb),
        in_specs=in_specs,
        out_specs=tuple(out_specs),
        scratch_shapes=[pltpu.VMEM((hb, HG_DV, HG_DK), F32), pltpu.VMEM((rows, w), F32)],
        compiler_params=_params(("arbitrary", "arbitrary")),
        name="hgrn_latent" if has_s0 else "hgrn_prompt",
    )(*args)


def _outproj_kernel(attp_ref, atts_ref, hgp_ref, hgs_ref, wa_ref, wb_ref, xp_ref, xs_ref, g1_ref, gain_ref,
                    sh_ref, sc_ref, x1_ref, h2_ref, full_ref, wcache_ref):
    i = pl.program_id(0)
    n = pl.program_id(1)

    @pl.when(i == 0)
    def _():
        wcache_ref[n, 0] = wa_ref[...].astype(BF16)
        wcache_ref[n, 1] = wb_ref[...].astype(BF16)

    def run(att_ref, hg_ref, x_ref):
        acc = _bdot(att_ref[...], wcache_ref[n, 0]) + _bdot(hg_ref[...], wcache_ref[n, 1])
        x1 = x_ref[...] + g1_ref[...] * acc
        x1_ref[...] = x1
        full_ref[n] = x1

    pl.when(i < N_PROMPT_TILES)(lambda: run(attp_ref, hgp_ref, xp_ref))
    pl.when(i >= N_PROMPT_TILES)(lambda: run(atts_ref, hgs_ref, xs_ref))

    @pl.when(n == D_MODEL // OUT_COL_TILE - 1)
    def _():
        nt = D_MODEL // OUT_COL_TILE
        ms = jnp.sum(full_ref[0] * full_ref[0], axis=-1, keepdims=True)
        for k in range(1, nt):
            ms = ms + jnp.sum(full_ref[k] * full_ref[k], axis=-1, keepdims=True)
        rstd = lax.rsqrt(ms / D_MODEL + EPS)
        gs = gain_ref[...] * (1.0 + sc_ref[...])
        for k in range(nt):
            cols = slice(k * OUT_COL_TILE, (k + 1) * OUT_COL_TILE)
            h2_ref[:, cols] = (full_ref[k] * rstd * gs[:, cols] + sh_ref[:, cols]).astype(BF16)


def _out_projection(att_p, att_s, hg_p, hg_s, w_out, xp, xs, mods, norm_mlp):
    tm, tn = TOK_TILE, OUT_COL_TILE
    nt = D_MODEL // tn

    def prompt_rows(width):
        return pl.BlockSpec((tm, width), lambda i, n: (jnp.minimum(i, N_PROMPT_TILES - 1), 0),
                            pipeline_mode=pl.Buffered(1))

    def sample_rows(width):
        return pl.BlockSpec((tm, width), lambda i, n: (jnp.maximum(i - N_PROMPT_TILES, 0), 0),
                            pipeline_mode=pl.Buffered(1))

    return pl.pallas_call(
        _outproj_kernel,
        out_shape=(jax.ShapeDtypeStruct((N_TOK, D_MODEL), F32), jax.ShapeDtypeStruct((N_TOK, D_MODEL), BF16)),
        grid=(N_TILES, nt),
        in_specs=[prompt_rows(ATT_WIDTH), sample_rows(ATT_WIDTH), prompt_rows(HG_VW), sample_rows(HG_VW),
                  pl.BlockSpec((ATT_WIDTH, tn), lambda i, n: (0, jnp.where(i == 0, n, nt - 1))),
                  pl.BlockSpec((HG_VW, tn), lambda i, n: (1, jnp.where(i == 0, n, nt - 1))),
                  pl.BlockSpec((tm, tn), lambda i, n: (jnp.minimum(i, N_PROMPT_TILES - 1),
                                                       jnp.where(i < N_PROMPT_TILES, n, nt - 1))),
                  pl.BlockSpec((tm, tn), lambda i, n: (jnp.maximum(i - N_PROMPT_TILES, 0),
                                                       jnp.where(i < N_PROMPT_TILES, 0, n))),
                  _mod_spec(0, 2, tm, width=tn, col=lambda i, n: n),
                  pl.BlockSpec((1, D_MODEL), lambda i, n: (0, 0)),
                  _mod_spec(0, 3, tm), _mod_spec(0, 4, tm)],
        out_specs=(pl.BlockSpec((tm, tn), lambda i, n: (i, n)),
                   pl.BlockSpec((tm, D_MODEL), lambda i, n: (i, 0))),
        scratch_shapes=[pltpu.VMEM((nt, tm, tn), F32), pltpu.VMEM((nt, 2, ATT_WIDTH, tn), BF16)],
        compiler_params=_params(("arbitrary", "arbitrary")),
        name="out_projection",
    )(att_p, att_s, hg_p, hg_s, w_out, w_out, xp, xs, mods, norm_mlp, mods, mods)


def _mlp_kernel(h_ref, w1_ref, w2_ref, x_ref, g2_ref, *rest, final):
    if final:
        fin_ref, o_ref, res_ref = rest
    else:
        o_ref, res_ref = rest
    j = pl.program_id(1)
    res_ref[j] = x_ref[...]

    def step(first, last):
        a = jnp.square(jnp.maximum(_bdot(h_ref[...], w1_ref[...].astype(BF16)), 0.0)).astype(BF16)
        for n in range(D_MODEL // MLP_OUT_CHUNK):
            cols = slice(n * MLP_OUT_CHUNK, (n + 1) * MLP_OUT_CHUNK)
            p = _bdot(a, w2_ref[:, cols].astype(BF16))
            if not first:
                p = o_ref[:, cols] + p
            if last:
                per = MLP_OUT_CHUNK // MLP_RES_COLS
                res = jnp.concatenate([res_ref[n * per + k] for k in range(per)], axis=1)
                p = res + g2_ref[:, cols] * p
            o_ref[:, cols] = p
        if last and final:
            o_ref[...] = _rms(o_ref[...], fin_ref[...])

    pl.when(j == 0)(lambda: step(True, False))
    pl.when(jnp.logical_and(j > 0, j < MLP_STEPS - 1))(lambda: step(False, False))
    pl.when(j == MLP_STEPS - 1)(lambda: step(False, True))


def _mlp(h, x, w1, w2, mods, layer, tile0, n_tiles, final_norm=None):
    tm, th = TOK_TILE, FF_TILE
    final = final_norm is not None
    in_specs = [pl.BlockSpec((tm, D_MODEL), lambda i, j: (tile0 + i, 0), pipeline_mode=pl.Buffered(1)),
                pl.BlockSpec((None, D_MODEL, th), lambda i, j: (layer, 0, j)),
                pl.BlockSpec((None, th, D_MODEL), lambda i, j: (layer, j, 0)),
                pl.BlockSpec((tm, MLP_RES_COLS), lambda i, j: (tile0 + i, j)),
                _mod_spec(layer, 5, tm, tile_of=lambda i, j: tile0 + i)]
    args = [h, w1, w2, x, mods]
    if final:
        in_specs.append(pl.BlockSpec((1, D_MODEL), lambda i, j: (0, 0)))
        args.append(final_norm)
    return pl.pallas_call(
        functools.partial(_mlp_kernel, final=final),
        out_shape=jax.ShapeDtypeStruct((n_tiles * tm, D_MODEL), F32),
        grid=(n_tiles, MLP_STEPS),
        in_specs=in_specs,
        out_specs=pl.BlockSpec((tm, D_MODEL), lambda i, j: (i, 0)),
        scratch_shapes=[pltpu.VMEM((MLP_STEPS, tm, MLP_RES_COLS), F32)],
        compiler_params=_params(("arbitrary", "arbitrary")),
        name="mlp_final" if final else "mlp",
    )(*args)


def _pool_kernel(x_ref, wp_ref, ps_ref, gain1_ref, sh1_ref, sc1_ref, g1_ref, gain2_ref, sh2_ref, sc2_ref,
                 x3_ref, h4_ref):
    i = pl.program_id(0)
    tm = TOK_TILE
    rstd = lax.rsqrt(jnp.mean(x_ref[...] * x_ref[...], axis=-1, keepdims=True) + EPS)
    gs1 = gain1_ref[...] * (1.0 + sc1_ref[...])
    mix_gain = g1_ref[...] * ps_ref[...]

    def widen(m):
        return jnp.concatenate([m] * (POOL_GROUP // HEAD_DIM), axis=1)

    def mix_tile(seq):
        n_seq = tm // seq
        pitch = seq + 2 * POOL_HALO
        n_pad = n_seq * pitch
        halo = jnp.zeros((POOL_HALO, POOL_GROUP), F32)
        pos = lax.broadcasted_iota(jnp.int32, (seq, HEAD_DIM), 0)

        def down(a, k):
            return pltpu.roll(a, k % n_pad, 0)

        for g, w in enumerate(POOL_WINDOWS):
            half = w // 2
            cols = slice(g * POOL_GROUP, (g + 1) * POOL_GROUP)
            x = x_ref[:, cols]
            h = x * rstd * gs1[:, cols] + sh1_ref[:, cols]
            padded = jnp.concatenate(
                [piece for s in range(n_seq) for piece in (halo, h[s * seq:(s + 1) * seq], halo)], axis=0)
            back = padded
            m = 1
            while m < half:
                back = back + down(back, m)
                m *= 2
            ahead = back if half == 1 else down(back, -(half - 1))
            total = down(back, 1) + ahead
            total = jnp.concatenate(
                [total[s * pitch + POOL_HALO:s * pitch + POOL_HALO + seq] for s in range(n_seq)], axis=0)
            count = (jnp.minimum(pos + (w - half), seq) - jnp.maximum(pos - half, 0)).astype(F32)
            inv = jnp.concatenate([widen(1.0 / count)] * n_seq, axis=0)
            pooled = (total * inv - h).astype(BF16)
            mix = _bdot(pooled, wp_ref[g].astype(BF16))
            x3_ref[:, cols] = x + mix_gain[:, cols] * mix

    pl.when(i < N_PROMPT_TILES)(lambda: mix_tile(SEQ))
    pl.when(i >= N_PROMPT_TILES)(lambda: mix_tile(DEC_SEQ))
    x3 = x3_ref[...]
    rstd3 = lax.rsqrt(jnp.mean(x3 * x3, axis=-1, keepdims=True) + EPS)
    h4_ref[...] = (x3 * rstd3 * (gain2_ref[...] * (1.0 + sc2_ref[...])) + sh2_ref[...]).astype(BF16)


def _pool_mixer(x, w_pool, pool_scale, norm_mix, norm_mlp, mods):
    tm = TOK_TILE
    vec = pl.BlockSpec((1, D_MODEL), lambda i: (0, 0))
    tile = pl.BlockSpec((tm, D_MODEL), lambda i: (i, 0))
    return pl.pallas_call(
        _pool_kernel,
        out_shape=(jax.ShapeDtypeStruct((N_TOK, D_MODEL), F32), jax.ShapeDtypeStruct((N_TOK, D_MODEL), BF16)),
        grid=(N_TILES,),
        in_specs=[tile, pl.BlockSpec((len(POOL_WINDOWS), POOL_GROUP, POOL_GROUP), lambda i: (0, 0, 0)),
                  vec, vec, _mod_spec(1, 0, tm), _mod_spec(1, 1, tm), _mod_spec(1, 2, tm),
                  vec, _mod_spec(1, 3, tm), _mod_spec(1, 4, tm)],
        out_specs=(tile, tile),
        compiler_params=_params(("arbitrary",)),
        name="pool_mixer",
    )(x, w_pool, pool_scale, norm_mix, mods, mods, mods, norm_mlp, mods, mods)


def kernel(x_prompt, x_sample, cache_k, cache_v, state_hgrn_fwd, state_hgrn_bwd, c, c_ctx, w_ada, b_ada,
           norm_mix, norm_mlp, w_in_ab, w_out_ab, q_norm, k_norm, hg_norm, lb_raw, w_pool, pool_scale,
           w_mlp_in, w_mlp_out, final_norm):
    xp = x_prompt.reshape(N_PROMPT, D_MODEL)
    xs = x_sample.reshape(N_SAMPLE, D_MODEL)
    cv = jnp.concatenate([c_ctx[None, :], c, jnp.zeros((ADA_ROWS - 1 - DEC_BATCH, D_MODEL), F32)], axis=0)
    mods, h0 = _ada_table_and_modulate(cv, w_ada, b_ada, xp, xs, norm_mix)

    proj, new_k, new_v = _in_projection(h0, w_in_ab[0], q_norm[0:1], k_norm[0:1])
    att_p = _attention(proj, BATCH, SEQ, 0, SEQ)
    att_s = _attention(proj, DEC_BATCH, DEC_SEQ, N_PROMPT, 512, ctx=(cache_k, cache_v))
    consts = _hgrn_constants()
    hg_p, s_fwd, s_bwd = _hgrn(proj, lb_raw, hg_norm[0:1], consts, BATCH, SEQ, 0, HG_PROMPT_SEQS, HG_PROMPT_HEADS)
    s0 = (state_hgrn_fwd.reshape(DEC_BATCH, HG_HEADS, HG_DK, HG_DV),
          state_hgrn_bwd.reshape(DEC_BATCH, HG_HEADS, HG_DK, HG_DV))
    (hg_s,) = _hgrn(proj, lb_raw, hg_norm[0:1], consts, DEC_BATCH, DEC_SEQ, N_PROMPT, 1, HG_LATENT_HEADS, s0=s0)
    x1, h2 = _out_projection(att_p, att_s, hg_p, hg_s, w_out_ab[0], xp, xs, mods, norm_mlp[0:1])
    x2 = _mlp(h2, x1, w_mlp_in, w_mlp_out, mods, 0, 0, N_TILES)

    x3, h4 = _pool_mixer(x2, w_pool[0], pool_scale[0:1], norm_mix[1:2], norm_mlp[1:2], mods)
    fin = final_norm[None, :]
    y_prompt = _mlp(h4, x3, w_mlp_in, w_mlp_out, mods, 1, 0, N_PROMPT_TILES, final_norm=fin)
    y_sample = _mlp(h4, x3, w_mlp_in, w_mlp_out, mods, 1, N_PROMPT_TILES, N_TILES - N_PROMPT_TILES,
                    final_norm=fin)

    return (y_prompt.reshape(BATCH, SEQ, D_MODEL), y_sample.reshape(DEC_BATCH, DEC_SEQ, D_MODEL),
            new_k, new_v,
            s_fwd.reshape(BATCH, 1, HG_HEADS, HG_DK, HG_DV), s_bwd.reshape(BATCH, 1, HG_HEADS, HG_DK, HG_DV))
```

```python
import functools

import numpy as np
import jax
import jax.numpy as jnp
from jax import lax
from jax.experimental import pallas as pl
from jax.experimental.pallas import tpu as pltpu

F32 = jnp.float32
BF16 = jnp.bfloat16

D_MODEL = 2048
BATCH = 16
SEQ = 256
DEPTH = 2
DEC_BATCH = 2
DEC_SEQ = 1024
PAST_LEN = 256
GRID_W = 64
HEAD_DIM = 128
N_Q_HEADS = 8
N_KV_HEADS = 2
Q_PER_KV = N_Q_HEADS // N_KV_HEADS
ATT_WIDTH = N_Q_HEADS * HEAD_DIM
KV_WIDTH = N_KV_HEADS * HEAD_DIM
HG_HEADS = 8
HG_DK = 128
HG_DV = 128
HG_KW = HG_HEADS * HG_DK
HG_VW = HG_HEADS * HG_DV
IN_AB = ATT_WIDTH + 2 * KV_WIDTH + 3 * HG_KW + 2 * HG_VW
MIX_WIDTH = ATT_WIDTH + HG_VW
POOL_WINDOWS = (2, 4, 8, 16)
POOL_GROUP = D_MODEL // len(POOL_WINDOWS)
POOL_HALO = 8
D_FF = 4 * D_MODEL
ROPE_THETA = 10000.0
ROPE_HALF = HEAD_DIM // 2
EPS = 1e-6
N_MOD = 6

N_PROMPT = BATCH * SEQ
N_SAMPLE = DEC_BATCH * DEC_SEQ
N_TOK = N_PROMPT + N_SAMPLE
ADA_ROWS = 16
ADA_COL_TILE = 1024
ADA_KEEP = 2 * D_MODEL // ADA_COL_TILE
MOD0_TOK_TILE = 512
TOK_TILE = 1024
N_TILES = N_TOK // TOK_TILE
N_PROMPT_TILES = N_PROMPT // TOK_TILE
PROJ_TOK_TILE = 2048
PROJ_COL_TILE = 512
OUT_COL_TILE = 512
FF_TILE = 512
MLP_STEPS = D_FF // FF_TILE
MLP_RES_COLS = D_MODEL // MLP_STEPS
MLP_OUT_CHUNK = 512
HG_CHUNK = 128
HG_LEVELS = 7
HG_PROMPT_HEADS = 8
HG_PROMPT_SEQS = 1
HG_LATENT_HEADS = 4
HG_SAFE_EXPONENT = 80.0
V7X_VMEM_BYTES = 64 * 2 ** 20
VMEM_LIMIT = V7X_VMEM_BYTES - 8 * 2 ** 20

COL_Q = 0
COL_K = ATT_WIDTH
COL_V = COL_K + KV_WIDTH
COL_HQ = COL_V + KV_WIDTH
COL_ZF = COL_HQ + HG_KW
COL_HI = COL_ZF + 2 * HG_KW
COL_HG = COL_HI + HG_VW
OUT_HQ = ATT_WIDTH
OUT_ZF = OUT_HQ + HG_KW
OUT_HI = OUT_ZF + 2 * HG_KW
OUT_HG = OUT_HI + HG_VW
OUT_K = OUT_HG + HG_VW
OUT_V = OUT_K + KV_WIDTH


def _params(semantics):
    return pltpu.CompilerParams(dimension_semantics=semantics, vmem_limit_bytes=VMEM_LIMIT)


def _sigmoid(x):
    return 1.0 / (1.0 + jnp.exp(-x))


def _silu(x):
    return x * _sigmoid(x)


def _rms(x, gain):
    return x * lax.rsqrt(jnp.mean(x * x, axis=-1, keepdims=True) + EPS) * gain


def _bdot(a, b):
    return jnp.dot(a, b, preferred_element_type=F32)


def _mod_row(tile, tile_rows):
    first = N_PROMPT // tile_rows
    per_seq = DEC_SEQ // tile_rows
    return jnp.where(tile < first, 0, 1 + (tile - first) // per_seq)


def _mod_spec(layer, chunk, tile_rows, width=D_MODEL, col=lambda *g: 0, tile_of=lambda *g: g[0]):
    per = D_MODEL // width
    return pl.BlockSpec((None, None, 1, width),
                        lambda *g: (layer, _mod_row(tile_of(*g), tile_rows), 0, chunk * per + col(*g)))


def _modulate(x, gain, shift, scale):
    return x * lax.rsqrt(jnp.mean(x * x, axis=-1, keepdims=True) + EPS) * (gain * (1.0 + scale)) + shift


def _ada_kernel(cv_ref, w_ref, b_ref, xp_ref, xs_ref, gain_ref, o_ref, h_ref, keep_ref):
    s = pl.program_id(0)
    tile = _bdot(_silu(cv_ref[...]).astype(BF16), w_ref[...].astype(BF16)) + b_ref[...]
    o_ref[:, 0, :] = tile

    @pl.when(s < ADA_KEEP)
    def _():
        keep_ref[s] = tile

    t = s - ADA_KEEP
    n_prompt = N_PROMPT // MOD0_TOK_TILE

    def run(x_ref):
        r = _mod_row(t, MOD0_TOK_TILE)
        per = D_MODEL // ADA_COL_TILE
        shift = jnp.concatenate([keep_ref[k, pl.ds(r, 1), :] for k in range(per)], axis=1)
        scale = jnp.concatenate([keep_ref[per + k, pl.ds(r, 1), :] for k in range(per)], axis=1)
        h_ref[...] = _modulate(x_ref[...], gain_ref[...], shift, scale).astype(BF16)

    pl.when(jnp.logical_and(t >= 0, t < n_prompt))(lambda: run(xp_ref))
    pl.when(jnp.logical_and(t >= n_prompt, t < N_TOK // MOD0_TOK_TILE))(lambda: run(xs_ref))


def _ada_table_and_modulate(cv, w_ada, b_ada, xp, xs, norm_mix):
    tn, tm = ADA_COL_TILE, MOD0_TOK_TILE
    n = N_MOD * D_MODEL
    per_layer = n // tn
    n_prompt = N_PROMPT // tm
    n_tiles = N_TOK // tm
    assert DEPTH * per_layer >= ADA_KEEP + n_tiles

    def tok(s):
        return jnp.clip(s - ADA_KEEP, 0, n_tiles - 1)

    return pl.pallas_call(
        _ada_kernel,
        out_shape=(jax.ShapeDtypeStruct((DEPTH, ADA_ROWS, 1, n), F32), jax.ShapeDtypeStruct((N_TOK, D_MODEL), BF16)),
        grid=(DEPTH * per_layer,),
        in_specs=[pl.BlockSpec((ADA_ROWS, D_MODEL), lambda s: (0, 0)),
                  pl.BlockSpec((None, D_MODEL, tn), lambda s: (s // per_layer, 0, s % per_layer)),
                  pl.BlockSpec((None, 1, tn), lambda s: (s // per_layer, 0, s % per_layer)),
                  pl.BlockSpec((tm, D_MODEL), lambda s: (jnp.minimum(tok(s), n_prompt - 1), 0)),
                  pl.BlockSpec((tm, D_MODEL), lambda s: (jnp.maximum(tok(s) - n_prompt, 0), 0)),
                  pl.BlockSpec((1, D_MODEL), lambda s: (0, 0))],
        out_specs=(pl.BlockSpec((None, ADA_ROWS, 1, tn), lambda s: (s // per_layer, 0, 0, s % per_layer)),
                   pl.BlockSpec((tm, D_MODEL), lambda s: (tok(s), 0))),
        scratch_shapes=[pltpu.VMEM((ADA_KEEP, ADA_ROWS, tn), F32)],
        compiler_params=_params(("arbitrary",)),
        name="ada_table_modulate0",
    )(cv, w_ada, b_ada.reshape(DEPTH, 1, n), xp, xs, norm_mix[0:1])


def _rope(y, cos, sin, perm2):
    hi = y.astype(BF16)
    lo = (y - hi.astype(F32)).astype(BF16)
    rot = _bdot(jnp.concatenate([hi, lo], axis=1), perm2)
    return y * cos + rot * sin


def _inproj_kernel(h_ref, w_ref, qg_ref, kg_ref, cos_ref, sin_ref, perm_ref, p_ref, nk_ref, nv_ref):
    i = pl.program_id(0)
    j = pl.program_id(1)
    latent = i >= N_PROMPT // PROJ_TOK_TILE
    heads = PROJ_COL_TILE // HEAD_DIM
    kv_tile = COL_K // PROJ_COL_TILE
    is_q = j < kv_tile

    def attention_tile(rope):
        gain = jnp.where(is_q, qg_ref[...], kg_ref[...])
        w = w_ref[...].astype(BF16)
        for s in range(PROJ_TOK_TILE // DEC_SEQ):
            rows = slice(s * DEC_SEQ, (s + 1) * DEC_SEQ)
            acc = _bdot(h_ref[rows, :], w)
            for hh in range(heads):
                cols = slice(hh * HEAD_DIM, (hh + 1) * HEAD_DIM)
                x = acc[:, cols]
                y = _rms(x, gain)
                if rope:
                    y = _rope(y, cos_ref[...], sin_ref[...], perm_ref[...])
                if hh >= N_KV_HEADS:
                    y = jnp.where(is_q, y, x)
                p_ref[rows, cols] = y

    pl.when(jnp.logical_and(j <= kv_tile, latent))(lambda: attention_tile(True))
    pl.when(jnp.logical_and(j <= kv_tile, jnp.logical_not(latent)))(lambda: attention_tile(False))

    @pl.when(jnp.logical_and(j == kv_tile, jnp.logical_not(latent)))
    def _():
        for s in range(PROJ_TOK_TILE // SEQ):
            rows = slice(s * SEQ, (s + 1) * SEQ)
            for hh in range(N_KV_HEADS):
                nk_ref[s, :, hh, :] = p_ref[rows, hh * HEAD_DIM:(hh + 1) * HEAD_DIM]
                nv_ref[s, :, hh, :] = p_ref[rows, KV_WIDTH + hh * HEAD_DIM:KV_WIDTH + (hh + 1) * HEAD_DIM]

    def plain_tile(act):
        w = w_ref[...].astype(BF16)
        for s in range(PROJ_TOK_TILE // DEC_SEQ):
            rows = slice(s * DEC_SEQ, (s + 1) * DEC_SEQ)
            p_ref[rows, :] = act(_bdot(h_ref[rows, :], w))

    is_gate = jnp.logical_and(j >= COL_ZF // PROJ_COL_TILE, j < COL_HI // PROJ_COL_TILE)
    is_value = jnp.logical_and(j >= COL_HI // PROJ_COL_TILE, j < COL_HG // PROJ_COL_TILE)
    raw = jnp.logical_or(is_gate, is_value)
    pl.when(raw)(lambda: plain_tile(lambda a: a))
    pl.when(jnp.logical_and(j > kv_tile, jnp.logical_not(raw)))(lambda: plain_tile(_silu))


def _rope_tables():
    t = np.arange(DEC_SEQ)
    row = (t // GRID_W).astype(np.float32)
    col = (t % GRID_W).astype(np.float32)
    inv = (np.float32(ROPE_THETA) ** (-np.arange(0, ROPE_HALF, 2, dtype=np.float32) / np.float32(ROPE_HALF))).astype(np.float32)
    ar = row[:, None] * inv
    ac = col[:, None] * inv
    ang = np.concatenate([ar, ar, ac, ac], axis=-1).astype(np.float32)
    cos = np.cos(ang).astype(np.float32)
    sin = np.sin(ang).astype(np.float32)
    qw = ROPE_HALF // 2
    perm = np.zeros((HEAD_DIM, HEAD_DIM), np.float32)
    for k in range(qw):
        perm[qw + k, k] = -1.0
        perm[k, qw + k] = 1.0
        perm[3 * qw + k, 2 * qw + k] = -1.0
        perm[2 * qw + k, 3 * qw + k] = 1.0
    return jnp.asarray(cos), jnp.asarray(sin), jnp.asarray(np.concatenate([perm, perm], axis=0), BF16)


def _in_projection(h, w_in, q_gain, k_gain):
    tm, tn = PROJ_TOK_TILE, PROJ_COL_TILE
    n_prompt_tiles = N_PROMPT // tm
    cos, sin, perm2 = _rope_tables()
    table = pl.BlockSpec((DEC_SEQ, HEAD_DIM), lambda i, j: (0, 0))
    gain = pl.BlockSpec((1, HEAD_DIM), lambda i, j: (0, 0))
    state = pl.BlockSpec((tm // SEQ, None, SEQ, N_KV_HEADS, HEAD_DIM),
                         lambda i, j: (jnp.minimum(i, n_prompt_tiles - 1), 0, 0, 0, 0),
                         pipeline_mode=pl.Buffered(1))
    state_shape = jax.ShapeDtypeStruct((BATCH, 1, SEQ, N_KV_HEADS, HEAD_DIM), F32)
    kv_tile = COL_K // tn

    def out_tile(j):
        return jnp.where(j < kv_tile, j, jnp.where(j == kv_tile, OUT_K // tn, j - 1))

    return pl.pallas_call(
        _inproj_kernel,
        out_shape=(jax.ShapeDtypeStruct((N_TOK, IN_AB), F32), state_shape, state_shape),
        grid=(N_TOK // tm, IN_AB // tn),
        in_specs=[pl.BlockSpec((tm, D_MODEL), lambda i, j: (i, 0)),
                  pl.BlockSpec((D_MODEL, tn), lambda i, j: (0, j)),
                  gain, gain, table, table, pl.BlockSpec((2 * HEAD_DIM, HEAD_DIM), lambda i, j: (0, 0))],
        out_specs=(pl.BlockSpec((tm, tn), lambda i, j: (i, out_tile(j))), state, state),
        compiler_params=_params(("arbitrary", "arbitrary")),
        name="in_projection",
    )(h, w_in, q_gain, k_gain, cos, sin, perm2)


def _attn_kernel(*refs, has_ctx, stack):
    if has_ctx:
        q_ref, k_ref, v_ref, ck_ref, cv_ref, o_ref = refs
    else:
        q_ref, k_ref, v_ref, o_ref = refs
    scale = HEAD_DIM ** -0.5
    nt = (((1,), (1,)), ((), ()))
    tq = q_ref.shape[0]
    for hk in range(k_ref.shape[1] // HEAD_DIM):
        kcols = slice(hk * HEAD_DIM, (hk + 1) * HEAD_DIM)
        k = k_ref[:, kcols].astype(BF16)
        v = v_ref[:, kcols].astype(BF16)
        if has_ctx:
            ck = ck_ref[:, hk, :].astype(BF16)
            cv = cv_ref[:, hk, :].astype(BF16)
        for g0 in range(hk * Q_PER_KV, (hk + 1) * Q_PER_KV, stack):
            q = jnp.concatenate([q_ref[:, g * HEAD_DIM:(g + 1) * HEAD_DIM] for g in range(g0, g0 + stack)], axis=0)
            q = (q * scale).astype(BF16)
            s = lax.dot_general(q, k, nt, preferred_element_type=F32)
            m = jnp.max(s, axis=-1, keepdims=True)
            if has_ctx:
                sc = lax.dot_general(q, ck, nt, preferred_element_type=F32)
                m = jnp.maximum(m, jnp.max(sc, axis=-1, keepdims=True))
            p = jnp.exp(s - m)
            den = jnp.sum(p, axis=-1, keepdims=True)
            o = _bdot(p.astype(BF16), v)
            if has_ctx:
                pc = jnp.exp(sc - m)
                den = den + jnp.sum(pc, axis=-1, keepdims=True)
                o = o + _bdot(pc.astype(BF16), cv)
            o = (o / den).astype(o_ref.dtype)
            for g in range(stack):
                o_ref[:, (g0 + g) * HEAD_DIM:(g0 + g + 1) * HEAD_DIM] = o[g * tq:(g + 1) * tq]


def _attention(p, n_batch, seq, row0, tq, ctx=None):
    q_blocks = seq // tq
    kvh = N_KV_HEADS
    gw = kvh * Q_PER_KV * HEAD_DIM
    kw = kvh * HEAD_DIM
    in_specs = [
        pl.BlockSpec((tq, gw), lambda b, h, qi: (row0 // tq + b * q_blocks + qi, h)),
        pl.BlockSpec((seq, kw), lambda b, h, qi: (row0 // seq + b, OUT_K // kw + h)),
        pl.BlockSpec((seq, kw), lambda b, h, qi: (row0 // seq + b, OUT_V // kw + h)),
    ]
    args = [p, p, p]
    if ctx is not None:
        assert kvh == N_KV_HEADS
        ctx_spec = pl.BlockSpec((None, None, PAST_LEN, N_KV_HEADS, HEAD_DIM), lambda b, h, qi: (b, 0, 0, 0, 0))
        in_specs += [ctx_spec, ctx_spec]
        args += [ctx[0], ctx[1]]
    return pl.pallas_call(
        functools.partial(_attn_kernel, has_ctx=ctx is not None, stack=Q_PER_KV if ctx is None else 1),
        out_shape=jax.ShapeDtypeStruct((n_batch * seq, ATT_WIDTH), BF16),
        grid=(n_batch, N_KV_HEADS // kvh, q_blocks),
        in_specs=in_specs,
        out_specs=pl.BlockSpec((tq, gw), lambda b, h, qi: (b * q_blocks + qi, h)),
        compiler_params=_params(("arbitrary", "arbitrary", "arbitrary")),
        name="attention_latent" if ctx is not None else "attention_prompt",
    )(*args)


def _hgrn_constants():
    c = HG_CHUNK
    t = np.arange(c)
    cums, sels, pairs, scans, diags = [], [], [], [], []
    for d in range(2):
        pos = t if d == 0 else c - 1 - t
        pu, pt = pos[None, :], pos[:, None]
        ms, ss, ws = [], [], []
        for l in range(HG_LEVELS):
            m = c >> l
            blk = pos // m
            mid = (blk * m + m // 2)[:, None]
            late = ((pos % m) >= m // 2)
            ms.append(np.where(late[:, None], (pu >= mid) & (pu <= pt), (pu > pt) & (pu < mid)))
            ss.append(np.broadcast_to(late[:, None], (c, c)))
            ws.append((blk[:, None] == blk[None, :]) & late[:, None] & ~late[None, :])
        ms.append(pu <= pt)
        ms.append(pu > pt)
        cums.append(np.concatenate(ms, axis=0))
        sels.append(np.stack(ss))
        pairs.append(np.stack(ws))
        scans.append(np.concatenate([pu <= pt, pu <= pt], axis=1))
        diags.append(((pos // (c // 2))[:, None] == (pos // (c // 2))[None, :]) & (pu <= pt))
    return (jnp.asarray(np.stack(cums), BF16), jnp.asarray(np.stack(sels), F32),
            jnp.asarray(np.stack(pairs), F32), jnp.asarray(np.stack(scans), BF16),
            jnp.asarray(np.stack(diags), F32))


def _hgrn_kernel(*refs, n_seqs, n_chunks, has_s0):
    (hq_ref, zf_ref, zb_ref, hi_ref, hg_ref, lb_ref, og_ref, cum_ref, sel_ref, pair_ref, scan_ref,
     diag_ref) = refs[:12]
    refs = refs[12:]
    if has_s0:
        s0f_ref, s0b_ref, o_ref = refs[:3]
        refs = refs[3:]
    else:
        o_ref, sf_ref, sb_ref = refs[:3]
        refs = refs[3:]
    st_ref, acc_ref = refs
    hb = st_ref.shape[0]
    c = HG_CHUNK
    half = c // 2
    nt = (((1,), (1,)), ((), ()))
    tn = (((0,), (0,)), ((), ()))

    def direction(d, z_ref):
        raw = lb_ref[d]
        e = jnp.exp(raw - jnp.max(raw, axis=0, keepdims=True))
        lb = e[0:1] / jnp.sum(e, axis=0, keepdims=True)

        def initial_state(s, hh):
            if has_s0:
                return (s0f_ref, s0b_ref)[d][s, hh].T
            return jnp.zeros((HG_DV, HG_DK), F32)

        def gates(rows):
            f = lb + (1.0 - lb) * _sigmoid(z_ref[rows, :])
            logf = jnp.log(f)
            hi16 = logf.astype(BF16)
            lo16 = (logf - hi16.astype(F32)).astype(BF16)
            return f, hi16, lo16

        def row(p):
            t = p if d == 0 else c - 1 - p
            return slice(t, t + 1)

        early, late = (slice(0, half), slice(half, c)) if d == 0 else (slice(half, c), slice(0, half))

        def in_row_order(x_early, x_late):
            return jnp.concatenate([x_early, x_late] if d == 0 else [x_late, x_early], axis=0)

        def emit(rows, cols, o):
            if d == 0:
                acc_ref[rows, cols] = o
            else:
                tot = acc_ref[rows, cols] + o
                o_ref[rows, cols] = (_rms(tot, og_ref[...]) * hg_ref[rows, cols]).astype(o_ref.dtype)

        def two_level_operands(rows):
            f, hi16, lo16 = gates(rows)
            b = _bdot(scan_ref[d], jnp.concatenate([hi16, lo16], axis=0))
            kk = 1.0 - f
            q = hq_ref[rows, :]
            r_mid = b[row(half - 1)]
            x1 = in_row_order(kk[early] * jnp.exp(r_mid - b[early]), q[late] * jnp.exp(b[late] - r_mid)).astype(BF16)
            dq = in_row_order(b[early] - b[row(half // 2 - 1)], b[late] - b[row(half + half // 2 - 1)])
            b_end = b[row(c - 1)]
            span = jnp.maximum(
                jnp.maximum(b[row(0)] - b[row(half // 2 - 1)], b[row(half // 2 - 1)] - b[row(half - 1)]),
                jnp.maximum(b[row(half)] - b[row(half + half // 2 - 1)],
                            b[row(half + half // 2 - 1)] - b[row(c - 1)]))
            return dict(x1=x1, xq=(q * jnp.exp(dq)).astype(BF16), xk=(kk * jnp.exp(-dq)).astype(BF16),
                        q_in=(q * jnp.exp(b)).astype(BF16), k_out=(kk * jnp.exp(b_end - b)).astype(BF16),
                        a_end=jnp.exp(b_end), iv=hi_ref[rows, :].astype(BF16), span=span)

        def two_level_chunk(rows, ops, states, need_state):
            new_states = []
            for hh in range(hb):
                cols = slice(hh * HG_DK, (hh + 1) * HG_DK)
                g1 = _bdot(ops["x1"][:, cols], ops["x1"][:, cols].astype(F32).T.astype(BF16))
                g2 = _bdot(ops["xq"][:, cols], ops["xk"][:, cols].astype(F32).T.astype(BF16))
                att = jnp.where(pair_ref[d, 0] > 0.5, g1, jnp.where(diag_ref[d] > 0.5, g2, 0.0)).astype(BF16)
                st = states[hh]
                o = _bdot(att, ops["iv"][:, cols])
                if st is not None:
                    o = o + lax.dot_general(ops["q_in"][:, cols], st.astype(BF16), nt, preferred_element_type=F32)
                pending.append((rows, cols, o))
                if not need_state:
                    new_states.append(None)
                    continue
                dst = lax.dot_general(ops["iv"][:, cols], ops["k_out"][:, cols], tn, preferred_element_type=F32)
                new_states.append(dst if st is None else ops["a_end"][:, cols] * st + dst)
            return new_states

        def all_levels(rows):
            f, hi16, lo16 = gates(rows)
            cum = cum_ref[d]
            eall = jnp.exp(_bdot(cum, hi16) + _bdot(cum, lo16))
            for hh in range(hb):
                cols = slice(hh * HG_DK, (hh + 1) * HG_DK)
                kk = 1.0 - f[:, cols]
                q = hq_ref[rows, cols]
                iv = hi_ref[rows, cols]
                iv16 = iv.astype(BF16)
                att = jnp.zeros((c, c), F32)
                for l in range(HG_LEVELS):
                    x = (kk + sel_ref[d, l] * (q - kk)) * eall[l * c:(l + 1) * c, cols]
                    xb = x.astype(BF16)
                    att = att + pair_ref[d, l] * lax.dot_general(xb, xb, nt, preferred_element_type=F32)
                e_in = eall[HG_LEVELS * c:(HG_LEVELS + 1) * c, cols]
                e_out = eall[(HG_LEVELS + 1) * c:, cols]
                st = st_ref[hh]
                o = (_bdot(att.astype(BF16), iv16)
                     + jnp.sum(q * kk, axis=-1, keepdims=True) * iv
                     + lax.dot_general((q * e_in).astype(BF16), st.astype(BF16), nt, preferred_element_type=F32))
                dst = lax.dot_general(iv16, (kk * e_out).astype(BF16), tn, preferred_element_type=F32)
                st_ref[hh] = (e_in[0:1] * e_out[0:1]) * st + dst
                emit(rows, cols, o)

        def chunk_start(s, ci):
            cidx = ci if d == 0 else n_chunks - 1 - ci
            return (s * n_chunks + cidx) * c

        worst = jnp.zeros((1, hb * HG_DK), F32)
        pending = []
        states = [[initial_state(s, hh) if has_s0 else None for hh in range(hb)] for s in range(n_seqs)]
        for ci in range(n_chunks):
            for s in range(n_seqs):
                rows = slice(chunk_start(s, ci), chunk_start(s, ci) + c)
                ops = two_level_operands(rows)
                states[s] = two_level_chunk(rows, ops, states[s], need_state=ci < n_chunks - 1 or not has_s0)
                worst = jnp.maximum(worst, ops["span"])
        for item in pending:
            emit(*item)
        if not has_s0:
            for s in range(n_seqs):
                for hh in range(hb):
                    (sf_ref, sb_ref)[d][s, hh] = states[s][hh].T

        @pl.when(jnp.logical_not(jnp.max(worst) <= HG_SAFE_EXPONENT))
        def _():
            for s in range(n_seqs):
                for hh in range(hb):
                    st_ref[hh] = initial_state(s, hh)

                def chunk(ci, carry):
                    all_levels(pl.ds(pl.multiple_of(chunk_start(s, ci), c), c))
                    return carry

                lax.fori_loop(0, n_chunks, chunk, 0)
                if not has_s0:
                    for hh in range(hb):
                        (sf_ref, sb_ref)[d][s, hh] = st_ref[hh].T

    direction(0, zf_ref)
    direction(1, zb_ref)


def _hgrn(p, lb_raw, o_gain, consts, n_batch, seq, row0, n_seqs, hb, s0=None):
    w = hb * HG_DK
    rows = n_seqs * seq

    def seg(col):
        return pl.BlockSpec((rows, w), lambda b, h: (row0 // rows + b, col // w + h))

    def const(a):
        return pl.BlockSpec(a.shape, lambda b, h: (0,) * a.ndim)

    in_specs = [seg(OUT_HQ), seg(OUT_ZF), seg(OUT_ZF + HG_KW), seg(OUT_HI), seg(OUT_HG),
                pl.BlockSpec((2, DEPTH + 1, w), lambda b, h: (0, 0, h)),
                pl.BlockSpec((1, HG_DV), lambda b, h: (0, 0))] + [const(a) for a in consts]
    args = [p, p, p, p, p, lb_raw, o_gain, *consts]
    has_s0 = s0 is not None
    st_spec = pl.BlockSpec((n_seqs, hb, HG_DK, HG_DV), lambda b, h: (b, h, 0, 0))
    if has_s0:
        in_specs += [st_spec, st_spec]
        args += [s0[0], s0[1]]
    out_shape = [jax.ShapeDtypeStruct((n_batch * seq, HG_VW), BF16)]
    out_specs = [pl.BlockSpec((rows, w), lambda b, h: (b, h))]
    if not has_s0:
        st_shape = jax.ShapeDtypeStruct((n_batch, HG_HEADS, HG_DK, HG_DV), F32)
        out_shape += [st_shape, st_shape]
        out_specs += [st_spec, st_spec]
    return pl.pallas_call(
        functools.partial(_hgrn_kernel, n_seqs=n_seqs, n_chunks=seq // HG_CHUNK, has_s0=has_s0),
        out_shape=tuple(out_shape),
        grid=(n_batch // n_seqs, HG_HEADS // hb),
        in_specs=in_specs,
        out_specs=tuple(out_specs),
        scratch_shapes=[pltpu.VMEM((hb, HG_DV, HG_DK), F32), pltpu.VMEM((rows, w), F32)],
        compiler_params=_params(("arbitrary", "arbitrary")),
        name="hgrn_latent" if has_s0 else "hgrn_prompt",
    )(*args)


def _outproj_kernel(attp_ref, atts_ref, hgp_ref, hgs_ref, wa_ref, wb_ref, xp_ref, xs_ref, g1_ref, gain_ref,
                    sh_ref, sc_ref, x1_ref, h2_ref, full_ref, wcache_ref, ss_ref):
    i = pl.program_id(0)
    n = pl.program_id(1)
    nt = D_MODEL // OUT_COL_TILE

    @pl.when(i == 0)
    def _():
        wcache_ref[n, 0] = wa_ref[...].astype(BF16)
        wcache_ref[n, 1] = wb_ref[...].astype(BF16)

    @pl.when(n == 0)
    def _():
        ss_ref[...] = jnp.zeros(ss_ref.shape, F32)

    def run(att_ref, hg_ref, x_ref, last):
        acc = _bdot(att_ref[...], wcache_ref[n, 0]) + _bdot(hg_ref[...], wcache_ref[n, 1])
        x1 = x_ref[...] + g1_ref[...] * acc
        x1_ref[...] = x1
        sq = x1 * x1
        part = sq[:, 0:HEAD_DIM]
        for k in range(1, OUT_COL_TILE // HEAD_DIM):
            part = part + sq[:, k * HEAD_DIM:(k + 1) * HEAD_DIM]
        if not last:
            full_ref[n] = x1
            ss_ref[...] += part
            return
        rstd = lax.rsqrt(jnp.sum(ss_ref[...] + part, axis=-1, keepdims=True) / D_MODEL + EPS)
        gs = gain_ref[...] * (1.0 + sc_ref[...])
        for k in range(nt):
            cols = slice(k * OUT_COL_TILE, (k + 1) * OUT_COL_TILE)
            xk = x1 if k == nt - 1 else full_ref[k]
            h2_ref[:, cols] = (xk * rstd * gs[:, cols] + sh_ref[:, cols]).astype(BF16)

    prompt = i < N_PROMPT_TILES
    for last in (False, True):
        step = (n == nt - 1) if last else (n < nt - 1)
        pl.when(jnp.logical_and(prompt, step))(functools.partial(run, attp_ref, hgp_ref, xp_ref, last))
        pl.when(jnp.logical_and(jnp.logical_not(prompt), step))(functools.partial(run, atts_ref, hgs_ref, xs_ref, last))


def _out_projection(att_p, att_s, hg_p, hg_s, w_out, xp, xs, mods, norm_mlp):
    tm, tn = TOK_TILE, OUT_COL_TILE
    nt = D_MODEL // tn

    def prompt_rows(width):
        return pl.BlockSpec((tm, width), lambda i, n: (jnp.minimum(i, N_PROMPT_TILES - 1), 0))

    def sample_rows(width):
        return pl.BlockSpec((tm, width), lambda i, n: (jnp.maximum(i - N_PROMPT_TILES, 0), 0),
                            pipeline_mode=pl.Buffered(1))

    return pl.pallas_call(
        _outproj_kernel,
        out_shape=(jax.ShapeDtypeStruct((N_TOK, D_MODEL), F32), jax.ShapeDtypeStruct((N_TOK, D_MODEL), BF16)),
        grid=(N_TILES, nt),
        in_specs=[prompt_rows(ATT_WIDTH), sample_rows(ATT_WIDTH), prompt_rows(HG_VW), sample_rows(HG_VW),
                  pl.BlockSpec((ATT_WIDTH, tn), lambda i, n: (0, jnp.where(i == 0, n, nt - 1))),
                  pl.BlockSpec((HG_VW, tn), lambda i, n: (1, jnp.where(i == 0, n, nt - 1))),
                  pl.BlockSpec((tm, tn), lambda i, n: (jnp.minimum(i, N_PROMPT_TILES - 1),
                                                       jnp.where(i < N_PROMPT_TILES, n, nt - 1))),
                  pl.BlockSpec((tm, tn), lambda i, n: (jnp.maximum(i - N_PROMPT_TILES, 0),
                                                       jnp.where(i < N_PROMPT_TILES, 0, n))),
                  _mod_spec(0, 2, tm, width=tn, col=lambda i, n: n),
                  pl.BlockSpec((1, D_MODEL), lambda i, n: (0, 0)),
                  _mod_spec(0, 3, tm), _mod_spec(0, 4, tm)],
        out_specs=(pl.BlockSpec((tm, tn), lambda i, n: (i, n)),
                   pl.BlockSpec((tm, D_MODEL), lambda i, n: (i, 0))),
        scratch_shapes=[pltpu.VMEM((nt - 1, tm, tn), F32), pltpu.VMEM((nt, 2, ATT_WIDTH, tn), BF16),
                        pltpu.VMEM((tm, HEAD_DIM), F32)],
        compiler_params=_params(("arbitrary", "arbitrary")),
        name="out_projection",
    )(att_p, att_s, hg_p, hg_s, w_out, w_out, xp, xs, mods, norm_mlp, mods, mods)


def _mlp_kernel(h_ref, w1_ref, w2_ref, x_ref, g2_ref, *rest, final):
    if final:
        fin_ref, o_ref, res_ref = rest
    else:
        o_ref, res_ref = rest
    j = pl.program_id(1)
    res_ref[j] = x_ref[...]

    def step(first, last):
        a = jnp.square(jnp.maximum(_bdot(h_ref[...], w1_ref[...].astype(BF16)), 0.0)).astype(BF16)
        for n in range(D_MODEL // MLP_OUT_CHUNK):
            cols = slice(n * MLP_OUT_CHUNK, (n + 1) * MLP_OUT_CHUNK)
            p = _bdot(a, w2_ref[:, cols].astype(BF16))
            if not first:
                p = o_ref[:, cols] + p
            if last:
                per = MLP_OUT_CHUNK // MLP_RES_COLS
                res = jnp.concatenate([res_ref[n * per + k] for k in range(per)], axis=1)
                p = res + g2_ref[:, cols] * p
            o_ref[:, cols] = p
        if last and final:
            o_ref[...] = _rms(o_ref[...], fin_ref[...])

    pl.when(j == 0)(lambda: step(True, False))
    pl.when(jnp.logical_and(j > 0, j < MLP_STEPS - 1))(lambda: step(False, False))
    pl.when(j == MLP_STEPS - 1)(lambda: step(False, True))


def _mlp(h, x, w1, w2, mods, layer, tile0, n_tiles, final_norm=None):
    tm, th = TOK_TILE, FF_TILE
    final = final_norm is not None
    in_specs = [pl.BlockSpec((tm, D_MODEL), lambda i, j: (tile0 + i, 0), pipeline_mode=pl.Buffered(1)),
                pl.BlockSpec((None, D_MODEL, th), lambda i, j: (layer, 0, j)),
                pl.BlockSpec((None, th, D_MODEL), lambda i, j: (layer, j, 0)),
                pl.BlockSpec((tm, MLP_RES_COLS), lambda i, j: (tile0 + i, j)),
                _mod_spec(layer, 5, tm, tile_of=lambda i, j: tile0 + i)]
    args = [h, w1, w2, x, mods]
    if final:
        in_specs.append(pl.BlockSpec((1, D_MODEL), lambda i, j: (0, 0)))
        args.append(final_norm)
    return pl.pallas_call(
        functools.partial(_mlp_kernel, final=final),
        out_shape=jax.ShapeDtypeStruct((n_tiles * tm, D_MODEL), F32),
        grid=(n_tiles, MLP_STEPS),
        in_specs=in_specs,
        out_specs=pl.BlockSpec((tm, D_MODEL), lambda i, j: (i, 0)),
        scratch_shapes=[pltpu.VMEM((MLP_STEPS, tm, MLP_RES_COLS), F32)],
        compiler_params=_params(("arbitrary", "arbitrary")),
        name="mlp_final" if final else "mlp",
    )(*args)


def _pool_kernel(x_ref, wp_ref, ps_ref, gain1_ref, sh1_ref, sc1_ref, g1_ref, gain2_ref, sh2_ref, sc2_ref,
                 x3_ref, h4_ref):
    i = pl.program_id(0)
    tm = TOK_TILE
    rstd = lax.rsqrt(jnp.mean(x_ref[...] * x_ref[...], axis=-1, keepdims=True) + EPS)
    gs1 = gain1_ref[...] * (1.0 + sc1_ref[...])
    mix_gain = g1_ref[...] * ps_ref[...]

    def widen(m):
        return jnp.concatenate([m] * (POOL_GROUP // HEAD_DIM), axis=1)

    def mix_tile(seq):
        n_seq = tm // seq
        pitch = seq + 2 * POOL_HALO
        n_pad = n_seq * pitch
        halo = jnp.zeros((POOL_HALO, POOL_GROUP), F32)
        pos = lax.broadcasted_iota(jnp.int32, (seq, HEAD_DIM), 0)

        def down(a, k):
            return pltpu.roll(a, k % n_pad, 0)

        for g, w in enumerate(POOL_WINDOWS):
            half = w // 2
            cols = slice(g * POOL_GROUP, (g + 1) * POOL_GROUP)
            x = x_ref[:, cols]
            h = x * rstd * gs1[:, cols] + sh1_ref[:, cols]
            padded = jnp.concatenate(
                [piece for s in range(n_seq) for piece in (halo, h[s * seq:(s + 1) * seq], halo)], axis=0)
            back = padded
            m = 1
            while m < half:
                back = back + down(back, m)
                m *= 2
            ahead = back if half == 1 else down(back, -(half - 1))
            total = down(back, 1) + ahead
            total = jnp.concatenate(
                [total[s * pitch + POOL_HALO:s * pitch + POOL_HALO + seq] for s in range(n_seq)], axis=0)
            count = (jnp.minimum(pos + (w - half), seq) - jnp.maximum(pos - half, 0)).astype(F32)
            inv = jnp.concatenate([widen(1.0 / count)] * n_seq, axis=0)
            pooled = (total * inv - h).astype(BF16)
            mix = _bdot(pooled, wp_ref[g].astype(BF16))
            x3_ref[:, cols] = x + mix_gain[:, cols] * mix

    pl.when(i < N_PROMPT_TILES)(lambda: mix_tile(SEQ))
    pl.when(i >= N_PROMPT_TILES)(lambda: mix_tile(DEC_SEQ))
    x3 = x3_ref[...]
    rstd3 = lax.rsqrt(jnp.mean(x3 * x3, axis=-1, keepdims=True) + EPS)
    h4_ref[...] = (x3 * rstd3 * (gain2_ref[...] * (1.0 + sc2_ref[...])) + sh2_ref[...]).astype(BF16)


def _pool_mixer(x, w_pool, pool_scale, norm_mix, norm_mlp, mods):
    tm = TOK_TILE
    vec = pl.BlockSpec((1, D_MODEL), lambda i: (0, 0))
    tile = pl.BlockSpec((tm, D_MODEL), lambda i: (i, 0))
    return pl.pallas_call(
        _pool_kernel,
        out_shape=(jax.ShapeDtypeStruct((N_TOK, D_MODEL), F32), jax.ShapeDtypeStruct((N_TOK, D_MODEL), BF16)),
        grid=(N_TILES,),
        in_specs=[tile, pl.BlockSpec((len(POOL_WINDOWS), POOL_GROUP, POOL_GROUP), lambda i: (0, 0, 0)),
                  vec, vec, _mod_spec(1, 0, tm), _mod_spec(1, 1, tm), _mod_spec(1, 2, tm),
                  vec, _mod_spec(1, 3, tm), _mod_spec(1, 4, tm)],
        out_specs=(tile, tile),
        compiler_params=_params(("arbitrary",)),
        name="pool_mixer",
    )(x, w_pool, pool_scale, norm_mix, mods, mods, mods, norm_mlp, mods, mods)


def kernel(x_prompt, x_sample, cache_k, cache_v, state_hgrn_fwd, state_hgrn_bwd, c, c_ctx, w_ada, b_ada,
           norm_mix, norm_mlp, w_in_ab, w_out_ab, q_norm, k_norm, hg_norm, lb_raw, w_pool, pool_scale,
           w_mlp_in, w_mlp_out, final_norm):
    xp = x_prompt.reshape(N_PROMPT, D_MODEL)
    xs = x_sample.reshape(N_SAMPLE, D_MODEL)
    cv = jnp.concatenate([c_ctx[None, :], c, jnp.zeros((ADA_ROWS - 1 - DEC_BATCH, D_MODEL), F32)], axis=0)
    mods, h0 = _ada_table_and_modulate(cv, w_ada, b_ada, xp, xs, norm_mix)

    proj, new_k, new_v = _in_projection(h0, w_in_ab[0], q_norm[0:1], k_norm[0:1])
    att_p = _attention(proj, BATCH, SEQ, 0, SEQ)
    att_s = _attention(proj, DEC_BATCH, DEC_SEQ, N_PROMPT, 512, ctx=(cache_k, cache_v))
    consts = _hgrn_constants()
    hg_p, s_fwd, s_bwd = _hgrn(proj, lb_raw, hg_norm[0:1], consts, BATCH, SEQ, 0, HG_PROMPT_SEQS, HG_PROMPT_HEADS)
    s0 = (state_hgrn_fwd.reshape(DEC_BATCH, HG_HEADS, HG_DK, HG_DV),
          state_hgrn_bwd.reshape(DEC_BATCH, HG_HEADS, HG_DK, HG_DV))
    (hg_s,) = _hgrn(proj, lb_raw, hg_norm[0:1], consts, DEC_BATCH, DEC_SEQ, N_PROMPT, 1, HG_LATENT_HEADS, s0=s0)
    x1, h2 = _out_projection(att_p, att_s, hg_p, hg_s, w_out_ab[0], xp, xs, mods, norm_mlp[0:1])
    x2 = _mlp(h2, x1, w_mlp_in, w_mlp_out, mods, 0, 0, N_TILES)

    x3, h4 = _pool_mixer(x2, w_pool[0], pool_scale[0:1], norm_mix[1:2], norm_mlp[1:2], mods)
    fin = final_norm[None, :]
    y_prompt = _mlp(h4, x3, w_mlp_in, w_mlp_out, mods, 1, 0, N_PROMPT_TILES, final_norm=fin)
    y_sample = _mlp(h4, x3, w_mlp_in, w_mlp_out, mods, 1, N_PROMPT_TILES, N_TILES - N_PROMPT_TILES,
                    final_norm=fin)

    return (y_prompt.reshape(BATCH, SEQ, D_MODEL), y_sample.reshape(DEC_BATCH, DEC_SEQ, D_MODEL),
            new_k, new_v,
            s_fwd.reshape(BATCH, 1, HG_HEADS, HG_DK, HG_DV), s_bwd.reshape(BATCH, 1, HG_HEADS, HG_DK, HG_DV))
```

```python
import functools

import numpy as np
import jax
import jax.numpy as jnp
from jax import lax
from jax.experimental import pallas as pl
from jax.experimental.pallas import tpu as pltpu

F32 = jnp.float32
BF16 = jnp.bfloat16

D_MODEL = 2048
BATCH = 16
SEQ = 256
DEPTH = 2
DEC_BATCH = 2
DEC_SEQ = 1024
PAST_LEN = 256
GRID_W = 64
HEAD_DIM = 128
N_Q_HEADS = 8
N_KV_HEADS = 2
Q_PER_KV = N_Q_HEADS // N_KV_HEADS
ATT_WIDTH = N_Q_HEADS * HEAD_DIM
KV_WIDTH = N_KV_HEADS * HEAD_DIM
HG_HEADS = 8
HG_DK = 128
HG_DV = 128
HG_KW = HG_HEADS * HG_DK
HG_VW = HG_HEADS * HG_DV
IN_AB = ATT_WIDTH + 2 * KV_WIDTH + 3 * HG_KW + 2 * HG_VW
MIX_WIDTH = ATT_WIDTH + HG_VW
POOL_WINDOWS = (2, 4, 8, 16)
POOL_GROUP = D_MODEL // len(POOL_WINDOWS)
POOL_HALO = 8
D_FF = 4 * D_MODEL
ROPE_THETA = 10000.0
ROPE_HALF = HEAD_DIM // 2
EPS = 1e-6
N_MOD = 6

N_PROMPT = BATCH * SEQ
N_SAMPLE = DEC_BATCH * DEC_SEQ
N_TOK = N_PROMPT + N_SAMPLE
ADA_ROWS = 16
ADA_COL_TILE = 1024
ADA_KEEP = 2 * D_MODEL // ADA_COL_TILE
MOD0_TOK_TILE = 512
TOK_TILE = 1024
N_TILES = N_TOK // TOK_TILE
N_PROMPT_TILES = N_PROMPT // TOK_TILE
PROJ_TOK_TILE = 2048
PROJ_COL_TILE = 512
OUT_COL_TILE = 512
FF_TILE = 512
MLP_STEPS = D_FF // FF_TILE
MLP_RES_COLS = D_MODEL // MLP_STEPS
MLP_OUT_CHUNK = 512
HG_CHUNK = 128
HG_LEVELS = 7
HG_PROMPT_HEADS = 8
HG_PROMPT_SEQS = 1
HG_LATENT_HEADS = 4
HG_SAFE_EXPONENT = 80.0
V7X_VMEM_BYTES = 64 * 2 ** 20
VMEM_LIMIT = V7X_VMEM_BYTES - 8 * 2 ** 20
MLP_VMEM_LIMIT = 60000 * 1024

COL_Q = 0
COL_K = ATT_WIDTH
COL_V = COL_K + KV_WIDTH
COL_HQ = COL_V + KV_WIDTH
COL_ZF = COL_HQ + HG_KW
COL_HI = COL_ZF + 2 * HG_KW
COL_HG = COL_HI + HG_VW
OUT_HQ = ATT_WIDTH
OUT_ZF = OUT_HQ + HG_KW
OUT_HI = OUT_ZF + 2 * HG_KW
OUT_HG = OUT_HI + HG_VW
OUT_K = OUT_HG + HG_VW
OUT_V = OUT_K + KV_WIDTH


def _params(semantics, vmem_limit=VMEM_LIMIT):
    return pltpu.CompilerParams(dimension_semantics=semantics, vmem_limit_bytes=vmem_limit)


def _sigmoid(x):
    return 1.0 / (1.0 + jnp.exp(-x))


def _silu(x):
    return x * _sigmoid(x)


def _rms(x, gain):
    return x * lax.rsqrt(jnp.mean(x * x, axis=-1, keepdims=True) + EPS) * gain


def _bdot(a, b):
    return jnp.dot(a, b, preferred_element_type=F32)


def _mod_row(tile, tile_rows):
    first = N_PROMPT // tile_rows
    per_seq = DEC_SEQ // tile_rows
    return jnp.where(tile < first, 0, 1 + (tile - first) // per_seq)


def _mod_spec(layer, chunk, tile_rows, width=D_MODEL, col=lambda *g: 0, tile_of=lambda *g: g[0]):
    per = D_MODEL // width
    return pl.BlockSpec((None, None, 1, width),
                        lambda *g: (layer, _mod_row(tile_of(*g), tile_rows), 0, chunk * per + col(*g)))


def _modulate(x, gain, shift, scale):
    return x * lax.rsqrt(jnp.mean(x * x, axis=-1, keepdims=True) + EPS) * (gain * (1.0 + scale)) + shift


def _ada_kernel(cv_ref, w_ref, b_ref, xp_ref, xs_ref, gain_ref, o_ref, h_ref, keep_ref):
    s = pl.program_id(0)
    tile = _bdot(_silu(cv_ref[...]).astype(BF16), w_ref[...].astype(BF16)) + b_ref[...]
    o_ref[:, 0, :] = tile

    @pl.when(s < ADA_KEEP)
    def _():
        keep_ref[s] = tile

    t = s - ADA_KEEP
    n_prompt = N_PROMPT // MOD0_TOK_TILE

    def run(x_ref):
        r = _mod_row(t, MOD0_TOK_TILE)
        per = D_MODEL // ADA_COL_TILE
        shift = jnp.concatenate([keep_ref[k, pl.ds(r, 1), :] for k in range(per)], axis=1)
        scale = jnp.concatenate([keep_ref[per + k, pl.ds(r, 1), :] for k in range(per)], axis=1)
        h_ref[...] = _modulate(x_ref[...], gain_ref[...], shift, scale).astype(BF16)

    pl.when(jnp.logical_and(t >= 0, t < n_prompt))(lambda: run(xp_ref))
    pl.when(jnp.logical_and(t >= n_prompt, t < N_TOK // MOD0_TOK_TILE))(lambda: run(xs_ref))


def _ada_table_and_modulate(cv, w_ada, b_ada, xp, xs, norm_mix):
    tn, tm = ADA_COL_TILE, MOD0_TOK_TILE
    n = N_MOD * D_MODEL
    per_layer = n // tn
    n_prompt = N_PROMPT // tm
    n_tiles = N_TOK // tm
    assert DEPTH * per_layer >= ADA_KEEP + n_tiles

    def tok(s):
        return jnp.clip(s - ADA_KEEP, 0, n_tiles - 1)

    return pl.pallas_call(
        _ada_kernel,
        out_shape=(jax.ShapeDtypeStruct((DEPTH, ADA_ROWS, 1, n), F32), jax.ShapeDtypeStruct((N_TOK, D_MODEL), BF16)),
        grid=(DEPTH * per_layer,),
        in_specs=[pl.BlockSpec((ADA_ROWS, D_MODEL), lambda s: (0, 0)),
                  pl.BlockSpec((None, D_MODEL, tn), lambda s: (s // per_layer, 0, s % per_layer)),
                  pl.BlockSpec((None, 1, tn), lambda s: (s // per_layer, 0, s % per_layer)),
                  pl.BlockSpec((tm, D_MODEL), lambda s: (jnp.minimum(tok(s), n_prompt - 1), 0)),
                  pl.BlockSpec((tm, D_MODEL), lambda s: (jnp.maximum(tok(s) - n_prompt, 0), 0)),
                  pl.BlockSpec((1, D_MODEL), lambda s: (0, 0))],
        out_specs=(pl.BlockSpec((None, ADA_ROWS, 1, tn), lambda s: (s // per_layer, 0, 0, s % per_layer)),
                   pl.BlockSpec((tm, D_MODEL), lambda s: (tok(s), 0))),
        scratch_shapes=[pltpu.VMEM((ADA_KEEP, ADA_ROWS, tn), F32)],
        compiler_params=_params(("arbitrary",)),
        name="ada_table_modulate0",
    )(cv, w_ada, b_ada.reshape(DEPTH, 1, n), xp, xs, norm_mix[0:1])


def _rope(y, cos, sin, perm2):
    hi = y.astype(BF16)
    lo = (y - hi.astype(F32)).astype(BF16)
    rot = _bdot(jnp.concatenate([hi, lo], axis=1), perm2)
    return y * cos + rot * sin


def _inproj_kernel(h_ref, w_ref, qg_ref, kg_ref, cos_ref, sin_ref, perm_ref, p_ref, nk_ref, nv_ref):
    i = pl.program_id(0)
    j = pl.program_id(1)
    latent = i >= N_PROMPT // PROJ_TOK_TILE
    heads = PROJ_COL_TILE // HEAD_DIM
    kv_tile = COL_K // PROJ_COL_TILE
    is_q = j < kv_tile

    def attention_tile(rope):
        gain = jnp.where(is_q, qg_ref[...], kg_ref[...])
        w = w_ref[...].astype(BF16)
        for s in range(PROJ_TOK_TILE // DEC_SEQ):
            rows = slice(s * DEC_SEQ, (s + 1) * DEC_SEQ)
            acc = _bdot(h_ref[rows, :], w)
            for hh in range(heads):
                cols = slice(hh * HEAD_DIM, (hh + 1) * HEAD_DIM)
                x = acc[:, cols]
                y = _rms(x, gain)
                if rope:
                    y = _rope(y, cos_ref[...], sin_ref[...], perm_ref[...])
                if hh >= N_KV_HEADS:
                    y = jnp.where(is_q, y, x)
                p_ref[rows, cols] = y

    pl.when(jnp.logical_and(j <= kv_tile, latent))(lambda: attention_tile(True))
    pl.when(jnp.logical_and(j <= kv_tile, jnp.logical_not(latent)))(lambda: attention_tile(False))

    @pl.when(jnp.logical_and(j == kv_tile, jnp.logical_not(latent)))
    def _():
        for s in range(PROJ_TOK_TILE // SEQ):
            rows = slice(s * SEQ, (s + 1) * SEQ)
            for hh in range(N_KV_HEADS):
                nk_ref[s, :, hh, :] = p_ref[rows, hh * HEAD_DIM:(hh + 1) * HEAD_DIM]
                nv_ref[s, :, hh, :] = p_ref[rows, KV_WIDTH + hh * HEAD_DIM:KV_WIDTH + (hh + 1) * HEAD_DIM]

    def plain_tile(act):
        w = w_ref[...].astype(BF16)
        for s in range(PROJ_TOK_TILE // DEC_SEQ):
            rows = slice(s * DEC_SEQ, (s + 1) * DEC_SEQ)
            p_ref[rows, :] = act(_bdot(h_ref[rows, :], w))

    is_gate = jnp.logical_and(j >= COL_ZF // PROJ_COL_TILE, j < COL_HI // PROJ_COL_TILE)
    is_value = jnp.logical_and(j >= COL_HI // PROJ_COL_TILE, j < COL_HG // PROJ_COL_TILE)
    raw = jnp.logical_or(is_gate, is_value)
    pl.when(raw)(lambda: plain_tile(lambda a: a))
    pl.when(jnp.logical_and(j > kv_tile, jnp.logical_not(raw)))(lambda: plain_tile(_silu))


def _rope_tables():
    t = np.arange(DEC_SEQ)
    row = (t // GRID_W).astype(np.float32)
    col = (t % GRID_W).astype(np.float32)
    inv = (np.float32(ROPE_THETA) ** (-np.arange(0, ROPE_HALF, 2, dtype=np.float32) / np.float32(ROPE_HALF))).astype(np.float32)
    ar = row[:, None] * inv
    ac = col[:, None] * inv
    ang = np.concatenate([ar, ar, ac, ac], axis=-1).astype(np.float32)
    cos = np.cos(ang).astype(np.float32)
    sin = np.sin(ang).astype(np.float32)
    qw = ROPE_HALF // 2
    perm = np.zeros((HEAD_DIM, HEAD_DIM), np.float32)
    for k in range(qw):
        perm[qw + k, k] = -1.0
        perm[k, qw + k] = 1.0
        perm[3 * qw + k, 2 * qw + k] = -1.0
        perm[2 * qw + k, 3 * qw + k] = 1.0
    return jnp.asarray(cos), jnp.asarray(sin), jnp.asarray(np.concatenate([perm, perm], axis=0), BF16)


def _in_projection(h, w_in, q_gain, k_gain):
    tm, tn = PROJ_TOK_TILE, PROJ_COL_TILE
    n_prompt_tiles = N_PROMPT // tm
    cos, sin, perm2 = _rope_tables()
    table = pl.BlockSpec((DEC_SEQ, HEAD_DIM), lambda i, j: (0, 0))
    gain = pl.BlockSpec((1, HEAD_DIM), lambda i, j: (0, 0))
    state = pl.BlockSpec((tm // SEQ, None, SEQ, N_KV_HEADS, HEAD_DIM),
                         lambda i, j: (jnp.minimum(i, n_prompt_tiles - 1), 0, 0, 0, 0),
                         pipeline_mode=pl.Buffered(1))
    state_shape = jax.ShapeDtypeStruct((BATCH, 1, SEQ, N_KV_HEADS, HEAD_DIM), F32)
    kv_tile = COL_K // tn

    def out_tile(j):
        return jnp.where(j < kv_tile, j, jnp.where(j == kv_tile, OUT_K // tn, j - 1))

    return pl.pallas_call(
        _inproj_kernel,
        out_shape=(jax.ShapeDtypeStruct((N_TOK, IN_AB), F32), state_shape, state_shape),
        grid=(N_TOK // tm, IN_AB // tn),
        in_specs=[pl.BlockSpec((tm, D_MODEL), lambda i, j: (i, 0)),
                  pl.BlockSpec((D_MODEL, tn), lambda i, j: (0, j)),
                  gain, gain, table, table, pl.BlockSpec((2 * HEAD_DIM, HEAD_DIM), lambda i, j: (0, 0))],
        out_specs=(pl.BlockSpec((tm, tn), lambda i, j: (i, out_tile(j))), state, state),
        compiler_params=_params(("arbitrary", "arbitrary")),
        name="in_projection",
    )(h, w_in, q_gain, k_gain, cos, sin, perm2)


def _attn_kernel(*refs, has_ctx, stack):
    if has_ctx:
        q_ref, k_ref, v_ref, ck_ref, cv_ref, o_ref = refs
    else:
        q_ref, k_ref, v_ref, o_ref = refs
    scale = HEAD_DIM ** -0.5
    nt = (((1,), (1,)), ((), ()))
    tq = q_ref.shape[0]
    for hk in range(k_ref.shape[1] // HEAD_DIM):
        kcols = slice(hk * HEAD_DIM, (hk + 1) * HEAD_DIM)
        k = k_ref[:, kcols].astype(BF16)
        v = v_ref[:, kcols].astype(BF16)
        if has_ctx:
            ck = ck_ref[:, hk, :].astype(BF16)
            cv = cv_ref[:, hk, :].astype(BF16)
        for g0 in range(hk * Q_PER_KV, (hk + 1) * Q_PER_KV, stack):
            q = jnp.concatenate([q_ref[:, g * HEAD_DIM:(g + 1) * HEAD_DIM] for g in range(g0, g0 + stack)], axis=0)
            q = (q * scale).astype(BF16)
            s = lax.dot_general(q, k, nt, preferred_element_type=F32)
            m = jnp.max(s, axis=-1, keepdims=True)
            if has_ctx:
                sc = lax.dot_general(q, ck, nt, preferred_element_type=F32)
                m = jnp.maximum(m, jnp.max(sc, axis=-1, keepdims=True))
            p = jnp.exp(s - m)
            den = jnp.sum(p, axis=-1, keepdims=True)
            o = _bdot(p.astype(BF16), v)
            if has_ctx:
                pc = jnp.exp(sc - m)
                den = den + jnp.sum(pc, axis=-1, keepdims=True)
                o = o + _bdot(pc.astype(BF16), cv)
            o = (o / den).astype(o_ref.dtype)
            for g in range(stack):
                o_ref[:, (g0 + g) * HEAD_DIM:(g0 + g + 1) * HEAD_DIM] = o[g * tq:(g + 1) * tq]


def _attention(p, n_batch, seq, row0, tq, ctx=None):
    q_blocks = seq // tq
    kvh = N_KV_HEADS
    gw = kvh * Q_PER_KV * HEAD_DIM
    kw = kvh * HEAD_DIM
    in_specs = [
        pl.BlockSpec((tq, gw), lambda b, h, qi: (row0 // tq + b * q_blocks + qi, h)),
        pl.BlockSpec((seq, kw), lambda b, h, qi: (row0 // seq + b, OUT_K // kw + h)),
        pl.BlockSpec((seq, kw), lambda b, h, qi: (row0 // seq + b, OUT_V // kw + h)),
    ]
    args = [p, p, p]
    if ctx is not None:
        assert kvh == N_KV_HEADS
        ctx_spec = pl.BlockSpec((None, None, PAST_LEN, N_KV_HEADS, HEAD_DIM), lambda b, h, qi: (b, 0, 0, 0, 0))
        in_specs += [ctx_spec, ctx_spec]
        args += [ctx[0], ctx[1]]
    return pl.pallas_call(
        functools.partial(_attn_kernel, has_ctx=ctx is not None, stack=Q_PER_KV if ctx is None else 1),
        out_shape=jax.ShapeDtypeStruct((n_batch * seq, ATT_WIDTH), BF16),
        grid=(n_batch, N_KV_HEADS // kvh, q_blocks),
        in_specs=in_specs,
        out_specs=pl.BlockSpec((tq, gw), lambda b, h, qi: (b * q_blocks + qi, h)),
        compiler_params=_params(("arbitrary", "arbitrary", "arbitrary")),
        name="attention_latent" if ctx is not None else "attention_prompt",
    )(*args)


def _hgrn_constants():
    c = HG_CHUNK
    t = np.arange(c)
    cums, sels, pairs, scans, diags = [], [], [], [], []
    for d in range(2):
        pos = t if d == 0 else c - 1 - t
        pu, pt = pos[None, :], pos[:, None]
        ms, ss, ws = [], [], []
        for l in range(HG_LEVELS):
            m = c >> l
            blk = pos // m
            mid = (blk * m + m // 2)[:, None]
            late = ((pos % m) >= m // 2)
            ms.append(np.where(late[:, None], (pu >= mid) & (pu <= pt), (pu > pt) & (pu < mid)))
            ss.append(np.broadcast_to(late[:, None], (c, c)))
            ws.append((blk[:, None] == blk[None, :]) & late[:, None] & ~late[None, :])
        ms.append(pu <= pt)
        ms.append(pu > pt)
        cums.append(np.concatenate(ms, axis=0))
        sels.append(np.stack(ss))
        pairs.append(np.stack(ws))
        scans.append(np.concatenate([pu <= pt, pu <= pt], axis=1))
        diags.append(((pos // (c // 2))[:, None] == (pos // (c // 2))[None, :]) & (pu <= pt))
    return (jnp.asarray(np.stack(cums), BF16), jnp.asarray(np.stack(sels), F32),
            jnp.asarray(np.stack(pairs), F32), jnp.asarray(np.stack(scans), BF16),
            jnp.asarray(np.stack(diags), F32))


def _hgrn_kernel(*refs, n_seqs, n_chunks, has_s0):
    (hq_ref, zf_ref, zb_ref, hi_ref, hg_ref, lb_ref, og_ref, cum_ref, sel_ref, pair_ref, scan_ref,
     diag_ref) = refs[:12]
    refs = refs[12:]
    if has_s0:
        s0f_ref, s0b_ref, o_ref = refs[:3]
        refs = refs[3:]
    else:
        o_ref, sf_ref, sb_ref = refs[:3]
        refs = refs[3:]
    st_ref, acc_ref = refs
    hb = st_ref.shape[0]
    c = HG_CHUNK
    half = c // 2
    nt = (((1,), (1,)), ((), ()))
    tn = (((0,), (0,)), ((), ()))

    def direction(d, z_ref):
        raw = lb_ref[d]
        e = jnp.exp(raw - jnp.max(raw, axis=0, keepdims=True))
        lb = e[0:1] / jnp.sum(e, axis=0, keepdims=True)

        def initial_state(s, hh):
            if has_s0:
                return (s0f_ref, s0b_ref)[d][s, hh].T
            return jnp.zeros((HG_DV, HG_DK), F32)

        def gates(rows):
            f = lb + (1.0 - lb) * _sigmoid(z_ref[rows, :])
            logf = jnp.log(f)
            hi16 = logf.astype(BF16)
            lo16 = (logf - hi16.astype(F32)).astype(BF16)
            return f, hi16, lo16

        def row(p):
            t = p if d == 0 else c - 1 - p
            return slice(t, t + 1)

        early, late = (slice(0, half), slice(half, c)) if d == 0 else (slice(half, c), slice(0, half))

        def in_row_order(x_early, x_late):
            return jnp.concatenate([x_early, x_late] if d == 0 else [x_late, x_early], axis=0)

        def emit(rows, cols, o):
            if d == 0:
                acc_ref[rows, cols] = o
            else:
                tot = acc_ref[rows, cols] + o
                o_ref[rows, cols] = (_rms(tot, og_ref[...]) * hg_ref[rows, cols]).astype(o_ref.dtype)

        def two_level_operands(rows):
            f, hi16, lo16 = gates(rows)
            b = _bdot(scan_ref[d], jnp.concatenate([hi16, lo16], axis=0))
            kk = 1.0 - f
            q = hq_ref[rows, :]
            r_mid = b[row(half - 1)]
            x1 = in_row_order(kk[early] * jnp.exp(r_mid - b[early]), q[late] * jnp.exp(b[late] - r_mid)).astype(BF16)
            dq = in_row_order(b[early] - b[row(half // 2 - 1)], b[late] - b[row(half + half // 2 - 1)])
            b_end = b[row(c - 1)]
            span = jnp.maximum(
                jnp.maximum(b[row(0)] - b[row(half // 2 - 1)], b[row(half // 2 - 1)] - b[row(half - 1)]),
                jnp.maximum(b[row(half)] - b[row(half + half // 2 - 1)],
                            b[row(half + half // 2 - 1)] - b[row(c - 1)]))
            return dict(x1=x1, xq=(q * jnp.exp(dq)).astype(BF16), xk=(kk * jnp.exp(-dq)).astype(BF16),
                        q_in=(q * jnp.exp(b)).astype(BF16), k_out=(kk * jnp.exp(b_end - b)).astype(BF16),
                        a_end=jnp.exp(b_end), iv=hi_ref[rows, :].astype(BF16), span=span)

        def two_level_chunk(rows, ops, states, need_state):
            new_states = []
            for hh in range(hb):
                cols = slice(hh * HG_DK, (hh + 1) * HG_DK)
                g1 = _bdot(ops["x1"][:, cols], ops["x1"][:, cols].astype(F32).T.astype(BF16))
                g2 = _bdot(ops["xq"][:, cols], ops["xk"][:, cols].astype(F32).T.astype(BF16))
                att = jnp.where(pair_ref[d, 0] > 0.5, g1, jnp.where(diag_ref[d] > 0.5, g2, 0.0)).astype(BF16)
                st = states[hh]
                o = _bdot(att, ops["iv"][:, cols])
                if st is not None:
                    o = o + lax.dot_general(ops["q_in"][:, cols], st.astype(BF16), nt, preferred_element_type=F32)
                pending.append((rows, cols, o))
                if not need_state:
                    new_states.append(None)
                    continue
                dst = lax.dot_general(ops["iv"][:, cols], ops["k_out"][:, cols], tn, preferred_element_type=F32)
                new_states.append(dst if st is None else ops["a_end"][:, cols] * st + dst)
            return new_states

        def all_levels(rows):
            f, hi16, lo16 = gates(rows)
            cum = cum_ref[d]
            eall = jnp.exp(_bdot(cum, hi16) + _bdot(cum, lo16))
            for hh in range(hb):
                cols = slice(hh * HG_DK, (hh + 1) * HG_DK)
                kk = 1.0 - f[:, cols]
                q = hq_ref[rows, cols]
                iv = hi_ref[rows, cols]
                iv16 = iv.astype(BF16)
                att = jnp.zeros((c, c), F32)
                for l in range(HG_LEVELS):
                    x = (kk + sel_ref[d, l] * (q - kk)) * eall[l * c:(l + 1) * c, cols]
                    xb = x.astype(BF16)
                    att = att + pair_ref[d, l] * lax.dot_general(xb, xb, nt, preferred_element_type=F32)
                e_in = eall[HG_LEVELS * c:(HG_LEVELS + 1) * c, cols]
                e_out = eall[(HG_LEVELS + 1) * c:, cols]
                st = st_ref[hh]
                o = (_bdot(att.astype(BF16), iv16)
                     + jnp.sum(q * kk, axis=-1, keepdims=True) * iv
                     + lax.dot_general((q * e_in).astype(BF16), st.astype(BF16), nt, preferred_element_type=F32))
                dst = lax.dot_general(iv16, (kk * e_out).astype(BF16), tn, preferred_element_type=F32)
                st_ref[hh] = (e_in[0:1] * e_out[0:1]) * st + dst
                emit(rows, cols, o)

        def chunk_start(s, ci):
            cidx = ci if d == 0 else n_chunks - 1 - ci
            return (s * n_chunks + cidx) * c

        worst = jnp.zeros((1, hb * HG_DK), F32)
        pending = []
        states = [[initial_state(s, hh) if has_s0 else None for hh in range(hb)] for s in range(n_seqs)]
        for ci in range(n_chunks):
            for s in range(n_seqs):
                rows = slice(chunk_start(s, ci), chunk_start(s, ci) + c)
                ops = two_level_operands(rows)
                states[s] = two_level_chunk(rows, ops, states[s], need_state=ci < n_chunks - 1 or not has_s0)
                worst = jnp.maximum(worst, ops["span"])
        for item in pending:
            emit(*item)
        if not has_s0:
            for s in range(n_seqs):
                for hh in range(hb):
                    (sf_ref, sb_ref)[d][s, hh] = states[s][hh].T

        @pl.when(jnp.logical_not(jnp.max(worst) <= HG_SAFE_EXPONENT))
        def _():
            for s in range(n_seqs):
                for hh in range(hb):
                    st_ref[hh] = initial_state(s, hh)

                def chunk(ci, carry):
                    all_levels(pl.ds(pl.multiple_of(chunk_start(s, ci), c), c))
                    return carry

                lax.fori_loop(0, n_chunks, chunk, 0)
                if not has_s0:
                    for hh in range(hb):
                        (sf_ref, sb_ref)[d][s, hh] = st_ref[hh].T

    direction(0, zf_ref)
    direction(1, zb_ref)


def _hgrn(p, lb_raw, o_gain, consts, n_batch, seq, row0, n_seqs, hb, s0=None):
    w = hb * HG_DK
    rows = n_seqs * seq

    def seg(col):
        return pl.BlockSpec((rows, w), lambda b, h: (row0 // rows + b, col // w + h))

    def const(a):
        return pl.BlockSpec(a.shape, lambda b, h: (0,) * a.ndim)

    in_specs = [seg(OUT_HQ), seg(OUT_ZF), seg(OUT_ZF + HG_KW), seg(OUT_HI), seg(OUT_HG),
                pl.BlockSpec((2, DEPTH + 1, w), lambda b, h: (0, 0, h)),
                pl.BlockSpec((1, HG_DV), lambda b, h: (0, 0))] + [const(a) for a in consts]
    args = [p, p, p, p, p, lb_raw, o_gain, *consts]
    has_s0 = s0 is not None
    st_spec = pl.BlockSpec((n_seqs, hb, HG_DK, HG_DV), lambda b, h: (b, h, 0, 0))
    if has_s0:
        in_specs += [st_spec, st_spec]
        args += [s0[0], s0[1]]
    out_shape = [jax.ShapeDtypeStruct((n_batch * seq, HG_VW), BF16)]
    out_specs = [pl.BlockSpec((rows, w), lambda b, h: (b, h))]
    if not has_s0:
        st_shape = jax.ShapeDtypeStruct((n_batch, HG_HEADS, HG_DK, HG_DV), F32)
        out_shape += [st_shape, st_shape]
        out_specs += [st_spec, st_spec]
    return pl.pallas_call(
        functools.partial(_hgrn_kernel, n_seqs=n_seqs, n_chunks=seq // HG_CHUNK, has_s0=has_s0),
        out_shape=tuple(out_shape),
        grid=(n_batch // n_seqs, HG_HEADS // hb),
        in_specs=in_specs,
        out_specs=tuple(out_specs),
        scratch_shapes=[pltpu.VMEM((hb, HG_DV, HG_DK), F32), pltpu.VMEM((rows, w), F32)],
        compiler_params=_params(("arbitrary", "arbitrary")),
        name="hgrn_latent" if has_s0 else "hgrn_prompt",
    )(*args)


def _outproj_kernel(attp_ref, atts_ref, hgp_ref, hgs_ref, wa_ref, wb_ref, xp_ref, xs_ref, g1_ref, gain_ref,
                    sh_ref, sc_ref, x1_ref, h2_ref, full_ref, wcache_ref, ss_ref):
    i = pl.program_id(0)
    n = pl.program_id(1)
    nt = D_MODEL // OUT_COL_TILE

    @pl.when(i == 0)
    def _():
        wcache_ref[n, 0] = wa_ref[...].astype(BF16)
        wcache_ref[n, 1] = wb_ref[...].astype(BF16)

    @pl.when(n == 0)
    def _():
        ss_ref[...] = jnp.zeros(ss_ref.shape, F32)

    def run(att_ref, hg_ref, x_ref, last):
        acc = _bdot(att_ref[...], wcache_ref[n, 0]) + _bdot(hg_ref[...], wcache_ref[n, 1])
        x1 = x_ref[...] + g1_ref[...] * acc
        x1_ref[...] = x1
        sq = x1 * x1
        part = sq[:, 0:HEAD_DIM]
        for k in range(1, OUT_COL_TILE // HEAD_DIM):
            part = part + sq[:, k * HEAD_DIM:(k + 1) * HEAD_DIM]
        if not last:
            full_ref[n] = x1
            ss_ref[...] += part
            return
        rstd = lax.rsqrt(jnp.sum(ss_ref[...] + part, axis=-1, keepdims=True) / D_MODEL + EPS)
        gs = gain_ref[...] * (1.0 + sc_ref[...])
        for k in range(nt):
            cols = slice(k * OUT_COL_TILE, (k + 1) * OUT_COL_TILE)
            xk = x1 if k == nt - 1 else full_ref[k]
            h2_ref[:, cols] = (xk * rstd * gs[:, cols] + sh_ref[:, cols]).astype(BF16)

    prompt = i < N_PROMPT_TILES
    for last in (False, True):
        step = (n == nt - 1) if last else (n < nt - 1)
        pl.when(jnp.logical_and(prompt, step))(functools.partial(run, attp_ref, hgp_ref, xp_ref, last))
        pl.when(jnp.logical_and(jnp.logical_not(prompt), step))(functools.partial(run, atts_ref, hgs_ref, xs_ref, last))


def _out_projection(att_p, att_s, hg_p, hg_s, w_out, xp, xs, mods, norm_mlp):
    tm, tn = TOK_TILE, OUT_COL_TILE
    nt = D_MODEL // tn

    def prompt_rows(width):
        return pl.BlockSpec((tm, width), lambda i, n: (jnp.minimum(i, N_PROMPT_TILES - 1), 0))

    def sample_rows(width):
        return pl.BlockSpec((tm, width), lambda i, n: (jnp.maximum(i - N_PROMPT_TILES, 0), 0),
                            pipeline_mode=pl.Buffered(1))

    return pl.pallas_call(
        _outproj_kernel,
        out_shape=(jax.ShapeDtypeStruct((N_TOK, D_MODEL), F32), jax.ShapeDtypeStruct((N_TOK, D_MODEL), BF16)),
        grid=(N_TILES, nt),
        in_specs=[prompt_rows(ATT_WIDTH), sample_rows(ATT_WIDTH), prompt_rows(HG_VW), sample_rows(HG_VW),
                  pl.BlockSpec((ATT_WIDTH, tn), lambda i, n: (0, jnp.where(i == 0, n, nt - 1))),
                  pl.BlockSpec((HG_VW, tn), lambda i, n: (1, jnp.where(i == 0, n, nt - 1))),
                  pl.BlockSpec((tm, tn), lambda i, n: (jnp.minimum(i, N_PROMPT_TILES - 1),
                                                       jnp.where(i < N_PROMPT_TILES, n, nt - 1))),
                  pl.BlockSpec((tm, tn), lambda i, n: (jnp.maximum(i - N_PROMPT_TILES, 0),
                                                       jnp.where(i < N_PROMPT_TILES, 0, n))),
                  _mod_spec(0, 2, tm, width=tn, col=lambda i, n: n),
                  pl.BlockSpec((1, D_MODEL), lambda i, n: (0, 0)),
                  _mod_spec(0, 3, tm), _mod_spec(0, 4, tm)],
        out_specs=(pl.BlockSpec((tm, tn), lambda i, n: (i, n)),
                   pl.BlockSpec((tm, D_MODEL), lambda i, n: (i, 0))),
        scratch_shapes=[pltpu.VMEM((nt - 1, tm, tn), F32), pltpu.VMEM((nt, 2, ATT_WIDTH, tn), BF16),
                        pltpu.VMEM((tm, HEAD_DIM), F32)],
        compiler_params=_params(("arbitrary", "arbitrary")),
        name="out_projection",
    )(att_p, att_s, hg_p, hg_s, w_out, w_out, xp, xs, mods, norm_mlp, mods, mods)


def _mlp_kernel(h_ref, w1_ref, w2_ref, x_ref, g2_ref, *rest, final):
    if final:
        fin_ref, o_ref, res_ref = rest
    else:
        o_ref, res_ref = rest
    j = pl.program_id(1)
    res_ref[j] = x_ref[...]

    def step(first, last):
        a = jnp.square(jnp.maximum(_bdot(h_ref[...], w1_ref[...].astype(BF16)), 0.0)).astype(BF16)
        for n in range(D_MODEL // MLP_OUT_CHUNK):
            cols = slice(n * MLP_OUT_CHUNK, (n + 1) * MLP_OUT_CHUNK)
            p = _bdot(a, w2_ref[:, cols].astype(BF16))
            if not first:
                p = o_ref[:, cols] + p
            if last:
                per = MLP_OUT_CHUNK // MLP_RES_COLS
                res = jnp.concatenate([res_ref[n * per + k] for k in range(per)], axis=1)
                p = res + g2_ref[:, cols] * p
            o_ref[:, cols] = p
        if last and final:
            o_ref[...] = _rms(o_ref[...], fin_ref[...])

    pl.when(j == 0)(lambda: step(True, False))
    pl.when(jnp.logical_and(j > 0, j < MLP_STEPS - 1))(lambda: step(False, False))
    pl.when(j == MLP_STEPS - 1)(lambda: step(False, True))


def _mlp(h, x, w1, w2, mods, layer, tile0, n_tiles, final_norm=None):
    tm, th = TOK_TILE, FF_TILE
    final = final_norm is not None
    in_specs = [pl.BlockSpec((tm, D_MODEL), lambda i, j: (tile0 + i, 0)),
                pl.BlockSpec((None, D_MODEL, th), lambda i, j: (layer, 0, j)),
                pl.BlockSpec((None, th, D_MODEL), lambda i, j: (layer, j, 0)),
                pl.BlockSpec((tm, MLP_RES_COLS), lambda i, j: (tile0 + i, j)),
                _mod_spec(layer, 5, tm, tile_of=lambda i, j: tile0 + i)]
    args = [h, w1, w2, x, mods]
    if final:
        in_specs.append(pl.BlockSpec((1, D_MODEL), lambda i, j: (0, 0)))
        args.append(final_norm)
    return pl.pallas_call(
        functools.partial(_mlp_kernel, final=final),
        out_shape=jax.ShapeDtypeStruct((n_tiles * tm, D_MODEL), F32),
        grid=(n_tiles, MLP_STEPS),
        in_specs=in_specs,
        out_specs=pl.BlockSpec((tm, D_MODEL), lambda i, j: (i, 0)),
        scratch_shapes=[pltpu.VMEM((MLP_STEPS, tm, MLP_RES_COLS), F32)],
        compiler_params=_params(("arbitrary", "arbitrary"), vmem_limit=MLP_VMEM_LIMIT),
        name="mlp_final" if final else "mlp",
    )(*args)


def _pool_kernel(x_ref, wp_ref, ps_ref, gain1_ref, sh1_ref, sc1_ref, g1_ref, gain2_ref, sh2_ref, sc2_ref,
                 x3_ref, h4_ref):
    i = pl.program_id(0)
    tm = TOK_TILE
    rstd = lax.rsqrt(jnp.mean(x_ref[...] * x_ref[...], axis=-1, keepdims=True) + EPS)
    gs1 = gain1_ref[...] * (1.0 + sc1_ref[...])
    mix_gain = g1_ref[...] * ps_ref[...]

    def widen(m):
        return jnp.concatenate([m] * (POOL_GROUP // HEAD_DIM), axis=1)

    def mix_tile(seq):
        n_seq = tm // seq
        pitch = seq + 2 * POOL_HALO
        n_pad = n_seq * pitch
        halo = jnp.zeros((POOL_HALO, POOL_GROUP), F32)
        pos = lax.broadcasted_iota(jnp.int32, (seq, HEAD_DIM), 0)

        def down(a, k):
            return pltpu.roll(a, k % n_pad, 0)

        sumsq = None
        for g, w in enumerate(POOL_WINDOWS):
            half = w // 2
            cols = slice(g * POOL_GROUP, (g + 1) * POOL_GROUP)
            x = x_ref[:, cols]
            h = x * rstd * gs1[:, cols] + sh1_ref[:, cols]
            padded = jnp.concatenate(
                [piece for s in range(n_seq) for piece in (halo, h[s * seq:(s + 1) * seq], halo)], axis=0)
            back = padded
            m = 1
            while m < half:
                back = back + down(back, m)
                m *= 2
            ahead = back if half == 1 else down(back, -(half - 1))
            total = down(back, 1) + ahead
            total = jnp.concatenate(
                [total[s * pitch + POOL_HALO:s * pitch + POOL_HALO + seq] for s in range(n_seq)], axis=0)
            count = (jnp.minimum(pos + (w - half), seq) - jnp.maximum(pos - half, 0)).astype(F32)
            inv = jnp.concatenate([widen(1.0 / count)] * n_seq, axis=0)
            pooled = (total * inv - h).astype(BF16)
            mix = _bdot(pooled, wp_ref[g].astype(BF16))
            x3 = x + mix_gain[:, cols] * mix
            x3_ref[:, cols] = x3
            sq = x3 * x3
            for k in range(POOL_GROUP // HEAD_DIM):
                part = sq[:, k * HEAD_DIM:(k + 1) * HEAD_DIM]
                sumsq = part if sumsq is None else sumsq + part
        rstd3 = lax.rsqrt(jnp.sum(sumsq, axis=-1, keepdims=True) / D_MODEL + EPS)
        h4_ref[...] = (x3_ref[...] * rstd3 * (gain2_ref[...] * (1.0 + sc2_ref[...])) + sh2_ref[...]).astype(BF16)

    pl.when(i < N_PROMPT_TILES)(lambda: mix_tile(SEQ))
    pl.when(i >= N_PROMPT_TILES)(lambda: mix_tile(DEC_SEQ))


def _pool_mixer(x, w_pool, pool_scale, norm_mix, norm_mlp, mods):
    tm = TOK_TILE
    vec = pl.BlockSpec((1, D_MODEL), lambda i: (0, 0))
    tile = pl.BlockSpec((tm, D_MODEL), lambda i: (i, 0))
    return pl.pallas_call(
        _pool_kernel,
        out_shape=(jax.ShapeDtypeStruct((N_TOK, D_MODEL), F32), jax.ShapeDtypeStruct((N_TOK, D_MODEL), BF16)),
        grid=(N_TILES,),
        in_specs=[tile, pl.BlockSpec((len(POOL_WINDOWS), POOL_GROUP, POOL_GROUP), lambda i: (0, 0, 0)),
                  vec, vec, _mod_spec(1, 0, tm), _mod_spec(1, 1, tm), _mod_spec(1, 2, tm),
                  vec, _mod_spec(1, 3, tm), _mod_spec(1, 4, tm)],
        out_specs=(tile, tile),
        compiler_params=_params(("arbitrary",)),
        name="pool_mixer",
    )(x, w_pool, pool_scale, norm_mix, mods, mods, mods, norm_mlp, mods, mods)


def kernel(x_prompt, x_sample, cache_k, cache_v, state_hgrn_fwd, state_hgrn_bwd, c, c_ctx, w_ada, b_ada,
           norm_mix, norm_mlp, w_in_ab, w_out_ab, q_norm, k_norm, hg_norm, lb_raw, w_pool, pool_scale,
           w_mlp_in, w_mlp_out, final_norm):
    xp = x_prompt.reshape(N_PROMPT, D_MODEL)
    xs = x_sample.reshape(N_SAMPLE, D_MODEL)
    cv = jnp.concatenate([c_ctx[None, :], c, jnp.zeros((ADA_ROWS - 1 - DEC_BATCH, D_MODEL), F32)], axis=0)
    mods, h0 = _ada_table_and_modulate(cv, w_ada, b_ada, xp, xs, norm_mix)

    proj, new_k, new_v = _in_projection(h0, w_in_ab[0], q_norm[0:1], k_norm[0:1])
    att_p = _attention(proj, BATCH, SEQ, 0, SEQ)
    att_s = _attention(proj, DEC_BATCH, DEC_SEQ, N_PROMPT, 512, ctx=(cache_k, cache_v))
    consts = _hgrn_constants()
    hg_p, s_fwd, s_bwd = _hgrn(proj, lb_raw, hg_norm[0:1], consts, BATCH, SEQ, 0, HG_PROMPT_SEQS, HG_PROMPT_HEADS)
    s0 = (state_hgrn_fwd.reshape(DEC_BATCH, HG_HEADS, HG_DK, HG_DV),
          state_hgrn_bwd.reshape(DEC_BATCH, HG_HEADS, HG_DK, HG_DV))
    (hg_s,) = _hgrn(proj, lb_raw, hg_norm[0:1], consts, DEC_BATCH, DEC_SEQ, N_PROMPT, 1, HG_LATENT_HEADS, s0=s0)
    x1, h2 = _out_projection(att_p, att_s, hg_p, hg_s, w_out_ab[0], xp, xs, mods, norm_mlp[0:1])
    x2 = _mlp(h2, x1, w_mlp_in, w_mlp_out, mods, 0, 0, N_TILES)

    x3, h4 = _pool_mixer(x2, w_pool[0], pool_scale[0:1], norm_mix[1:2], norm_mlp[1:2], mods)
    fin = final_norm[None, :]
    y_prompt = _mlp(h4, x3, w_mlp_in, w_mlp_out, mods, 1, 0, N_PROMPT_TILES, final_norm=fin)
    y_sample = _mlp(h4, x3, w_mlp_in, w_mlp_out, mods, 1, N_PROMPT_TILES, N_TILES - N_PROMPT_TILES,
                    final_norm=fin)

    return (y_prompt.reshape(BATCH, SEQ, D_MODEL), y_sample.reshape(DEC_BATCH, DEC_SEQ, D_MODEL),
            new_k, new_v,
            s_fwd.reshape(BATCH, 1, HG_HEADS, HG_DK, HG_DV), s_bwd.reshape(BATCH, 1, HG_HEADS, HG_DK, HG_DV))
```

```python
import functools

import numpy as np
import jax
import jax.numpy as jnp
from jax import lax
from jax.experimental import pallas as pl
from jax.experimental.pallas import tpu as pltpu

F32 = jnp.float32
BF16 = jnp.bfloat16

D_MODEL = 2048
BATCH = 16
SEQ = 256
DEPTH = 2
DEC_BATCH = 2
DEC_SEQ = 1024
PAST_LEN = 256
GRID_W = 64
HEAD_DIM = 128
N_Q_HEADS = 8
N_KV_HEADS = 2
Q_PER_KV = N_Q_HEADS // N_KV_HEADS
ATT_WIDTH = N_Q_HEADS * HEAD_DIM
KV_WIDTH = N_KV_HEADS * HEAD_DIM
HG_HEADS = 8
HG_DK = 128
HG_DV = 128
HG_KW = HG_HEADS * HG_DK
HG_VW = HG_HEADS * HG_DV
IN_AB = ATT_WIDTH + 2 * KV_WIDTH + 3 * HG_KW + 2 * HG_VW
POOL_WINDOWS = (2, 4, 8, 16)
POOL_GROUP = D_MODEL // len(POOL_WINDOWS)
POOL_HALO = 8
D_FF = 4 * D_MODEL
ROPE_THETA = 10000.0
ROPE_HALF = HEAD_DIM // 2
EPS = 1e-6
N_MOD = 6

N_PROMPT = BATCH * SEQ
N_SAMPLE = DEC_BATCH * DEC_SEQ
N_TOK = N_PROMPT + N_SAMPLE
ADA_ROWS = 16
ADA_COL_TILE = 1024
ADA_KEEP = 2 * D_MODEL // ADA_COL_TILE
MOD0_TOK_TILE = 512
TOK_TILE = 1024
N_TILES = N_TOK // TOK_TILE
N_PROMPT_TILES = N_PROMPT // TOK_TILE
PROJ_TOK_TILE = 2048
PROJ_COL_TILE = 512
OUT_COL_TILE = 512
FF_TILE = 512
MLP_STEPS = D_FF // FF_TILE
MLP_RES_COLS = D_MODEL // MLP_STEPS
MLP_OUT_CHUNK = 512
HG_CHUNK = 128
HG_LEVELS = 7
HG_PROMPT_HEADS = 8
HG_PROMPT_SEQS = 1
HG_LATENT_HEADS = 4
HG_SAFE_EXPONENT = 80.0
V7X_VMEM_BYTES = 64 * 2 ** 20
VMEM_LIMIT = 60 * 2 ** 20
assert VMEM_LIMIT < V7X_VMEM_BYTES
assert DEPTH == 2
assert ATT_WIDTH == HG_VW

COL_K = ATT_WIDTH
COL_V = COL_K + KV_WIDTH
COL_HQ = COL_V + KV_WIDTH
COL_ZF = COL_HQ + HG_KW
COL_HI = COL_ZF + 2 * HG_KW
COL_HG = COL_HI + HG_VW
OUT_HQ = ATT_WIDTH
OUT_ZF = OUT_HQ + HG_KW
OUT_HI = OUT_ZF + 2 * HG_KW
OUT_HG = OUT_HI + HG_VW
OUT_K = OUT_HG + HG_VW
OUT_V = OUT_K + KV_WIDTH


def _params(semantics):
    return pltpu.CompilerParams(dimension_semantics=semantics, vmem_limit_bytes=VMEM_LIMIT)


def _sigmoid(x):
    return 1.0 / (1.0 + jnp.exp(-x))


def _silu(x):
    return x * _sigmoid(x)


def _rms(x, gain):
    return x * lax.rsqrt(jnp.mean(x * x, axis=-1, keepdims=True) + EPS) * gain


def _bdot(a, b):
    return jnp.dot(a, b, preferred_element_type=F32)


def _mod_row(tile, tile_rows):
    first = N_PROMPT // tile_rows
    per_seq = DEC_SEQ // tile_rows
    return jnp.where(tile < first, 0, 1 + (tile - first) // per_seq)


def _mod_spec(layer, chunk, tile_rows, width=D_MODEL, col=lambda *g: 0, tile_of=lambda *g: g[0]):
    per = D_MODEL // width
    return pl.BlockSpec((None, None, 1, width),
                        lambda *g: (layer, _mod_row(tile_of(*g), tile_rows), 0, chunk * per + col(*g)))


def _modulate(x, gain, shift, scale):
    return x * lax.rsqrt(jnp.mean(x * x, axis=-1, keepdims=True) + EPS) * (gain * (1.0 + scale)) + shift


def _ada_kernel(cv_ref, w_ref, b_ref, xp_ref, xs_ref, gain_ref, o_ref, h_ref, keep_ref):
    s = pl.program_id(0)
    tile = _bdot(_silu(cv_ref[...]).astype(BF16), w_ref[...].astype(BF16)) + b_ref[...]
    o_ref[:, 0, :] = tile

    @pl.when(s < ADA_KEEP)
    def _():
        keep_ref[s] = tile

    t = s - ADA_KEEP
    n_prompt = N_PROMPT // MOD0_TOK_TILE

    def run(x_ref):
        r = _mod_row(t, MOD0_TOK_TILE)
        per = D_MODEL // ADA_COL_TILE
        shift = jnp.concatenate([keep_ref[k, pl.ds(r, 1), :] for k in range(per)], axis=1)
        scale = jnp.concatenate([keep_ref[per + k, pl.ds(r, 1), :] for k in range(per)], axis=1)
        h_ref[...] = _modulate(x_ref[...], gain_ref[...], shift, scale).astype(BF16)

    pl.when(jnp.logical_and(t >= 0, t < n_prompt))(lambda: run(xp_ref))
    pl.when(jnp.logical_and(t >= n_prompt, t < N_TOK // MOD0_TOK_TILE))(lambda: run(xs_ref))


def _ada_table_and_modulate(cv, w_ada, b_ada, xp, xs, norm_mix):
    tn, tm = ADA_COL_TILE, MOD0_TOK_TILE
    n = N_MOD * D_MODEL
    per_layer = n // tn
    n_prompt = N_PROMPT // tm
    n_tiles = N_TOK // tm
    assert DEPTH * per_layer >= ADA_KEEP + n_tiles

    def tok(s):
        return jnp.clip(s - ADA_KEEP, 0, n_tiles - 1)

    return pl.pallas_call(
        _ada_kernel,
        out_shape=(jax.ShapeDtypeStruct((DEPTH, ADA_ROWS, 1, n), F32), jax.ShapeDtypeStruct((N_TOK, D_MODEL), BF16)),
        grid=(DEPTH * per_layer,),
        in_specs=[pl.BlockSpec((ADA_ROWS, D_MODEL), lambda s: (0, 0)),
                  pl.BlockSpec((None, D_MODEL, tn), lambda s: (s // per_layer, 0, s % per_layer)),
                  pl.BlockSpec((None, 1, tn), lambda s: (s // per_layer, 0, s % per_layer)),
                  pl.BlockSpec((tm, D_MODEL), lambda s: (jnp.minimum(tok(s), n_prompt - 1), 0)),
                  pl.BlockSpec((tm, D_MODEL), lambda s: (jnp.maximum(tok(s) - n_prompt, 0), 0)),
                  pl.BlockSpec((1, D_MODEL), lambda s: (0, 0))],
        out_specs=(pl.BlockSpec((None, ADA_ROWS, 1, tn), lambda s: (s // per_layer, 0, 0, s % per_layer)),
                   pl.BlockSpec((tm, D_MODEL), lambda s: (tok(s), 0))),
        scratch_shapes=[pltpu.VMEM((ADA_KEEP, ADA_ROWS, tn), F32)],
        compiler_params=_params(("arbitrary",)),
        name="ada_table_modulate0",
    )(cv, w_ada, b_ada.reshape(DEPTH, 1, n), xp, xs, norm_mix[0:1])


def _rope(y, cos, sin, perm2):
    hi = y.astype(BF16)
    lo = (y - hi.astype(F32)).astype(BF16)
    rot = _bdot(jnp.concatenate([hi, lo], axis=1), perm2)
    return y * cos + rot * sin


def _inproj_kernel(h_ref, w_ref, qg_ref, kg_ref, cos_ref, sin_ref, perm_ref, p_ref, nk_ref, nv_ref):
    i = pl.program_id(0)
    j = pl.program_id(1)
    latent = i >= N_PROMPT // PROJ_TOK_TILE
    heads = PROJ_COL_TILE // HEAD_DIM
    kv_tile = COL_K // PROJ_COL_TILE
    is_q = j < kv_tile

    def attention_tile(rope):
        gain = jnp.where(is_q, qg_ref[...], kg_ref[...])
        w = w_ref[...].astype(BF16)
        for s in range(PROJ_TOK_TILE // DEC_SEQ):
            rows = slice(s * DEC_SEQ, (s + 1) * DEC_SEQ)
            acc = _bdot(h_ref[rows, :], w)
            for hh in range(heads):
                cols = slice(hh * HEAD_DIM, (hh + 1) * HEAD_DIM)
                x = acc[:, cols]
                y = _rms(x, gain)
                if rope:
                    y = _rope(y, cos_ref[...], sin_ref[...], perm_ref[...])
                if hh >= N_KV_HEADS:
                    y = jnp.where(is_q, y, x)
                p_ref[rows, cols] = y

    pl.when(jnp.logical_and(j <= kv_tile, latent))(lambda: attention_tile(True))
    pl.when(jnp.logical_and(j <= kv_tile, jnp.logical_not(latent)))(lambda: attention_tile(False))

    @pl.when(jnp.logical_and(j == kv_tile, jnp.logical_not(latent)))
    def _():
        for s in range(PROJ_TOK_TILE // SEQ):
            rows = slice(s * SEQ, (s + 1) * SEQ)
            for hh in range(N_KV_HEADS):
                nk_ref[s, :, hh, :] = p_ref[rows, hh * HEAD_DIM:(hh + 1) * HEAD_DIM]
                nv_ref[s, :, hh, :] = p_ref[rows, KV_WIDTH + hh * HEAD_DIM:KV_WIDTH + (hh + 1) * HEAD_DIM]

    def plain_tile(act):
        w = w_ref[...].astype(BF16)
        for s in range(PROJ_TOK_TILE // DEC_SEQ):
            rows = slice(s * DEC_SEQ, (s + 1) * DEC_SEQ)
            p_ref[rows, :] = act(_bdot(h_ref[rows, :], w))

    is_gate = jnp.logical_and(j >= COL_ZF // PROJ_COL_TILE, j < COL_HI // PROJ_COL_TILE)
    is_value = jnp.logical_and(j >= COL_HI // PROJ_COL_TILE, j < COL_HG // PROJ_COL_TILE)
    raw = jnp.logical_or(is_gate, is_value)
    pl.when(raw)(lambda: plain_tile(lambda a: a))
    pl.when(jnp.logical_and(j > kv_tile, jnp.logical_not(raw)))(lambda: plain_tile(_silu))


def _rope_tables():
    t = np.arange(DEC_SEQ)
    row = (t // GRID_W).astype(np.float32)
    col = (t % GRID_W).astype(np.float32)
    inv = (np.float32(ROPE_THETA) ** (-np.arange(0, ROPE_HALF, 2, dtype=np.float32) / np.float32(ROPE_HALF))).astype(np.float32)
    ar = row[:, None] * inv
    ac = col[:, None] * inv
    ang = np.concatenate([ar, ar, ac, ac], axis=-1).astype(np.float32)
    cos = np.cos(ang).astype(np.float32)
    sin = np.sin(ang).astype(np.float32)
    qw = ROPE_HALF // 2
    perm = np.zeros((HEAD_DIM, HEAD_DIM), np.float32)
    for k in range(qw):
        perm[qw + k, k] = -1.0
        perm[k, qw + k] = 1.0
        perm[3 * qw + k, 2 * qw + k] = -1.0
        perm[2 * qw + k, 3 * qw + k] = 1.0
    return jnp.asarray(cos), jnp.asarray(sin), jnp.asarray(np.concatenate([perm, perm], axis=0), BF16)


def _in_projection(h, w_in, q_gain, k_gain):
    tm, tn = PROJ_TOK_TILE, PROJ_COL_TILE
    n_prompt_tiles = N_PROMPT // tm
    cos, sin, perm2 = _rope_tables()
    table = pl.BlockSpec((DEC_SEQ, HEAD_DIM), lambda i, j: (0, 0))
    gain = pl.BlockSpec((1, HEAD_DIM), lambda i, j: (0, 0))
    state = pl.BlockSpec((tm // SEQ, None, SEQ, N_KV_HEADS, HEAD_DIM),
                         lambda i, j: (jnp.minimum(i, n_prompt_tiles - 1), 0, 0, 0, 0),
                         pipeline_mode=pl.Buffered(1))
    state_shape = jax.ShapeDtypeStruct((BATCH, 1, SEQ, N_KV_HEADS, HEAD_DIM), F32)
    kv_tile = COL_K // tn

    def out_tile(j):
        return jnp.where(j < kv_tile, j, jnp.where(j == kv_tile, OUT_K // tn, j - 1))

    return pl.pallas_call(
        _inproj_kernel,
        out_shape=(jax.ShapeDtypeStruct((N_TOK, IN_AB), F32), state_shape, state_shape),
        grid=(N_TOK // tm, IN_AB // tn),
        in_specs=[pl.BlockSpec((tm, D_MODEL), lambda i, j: (i, 0)),
                  pl.BlockSpec((D_MODEL, tn), lambda i, j: (0, j)),
                  gain, gain, table, table, pl.BlockSpec((2 * HEAD_DIM, HEAD_DIM), lambda i, j: (0, 0))],
        out_specs=(pl.BlockSpec((tm, tn), lambda i, j: (i, out_tile(j))), state, state),
        compiler_params=_params(("arbitrary", "arbitrary")),
        name="in_projection",
    )(h, w_in, q_gain, k_gain, cos, sin, perm2)


def _attn_kernel(*refs, has_ctx, stack):
    if has_ctx:
        q_ref, k_ref, v_ref, ck_ref, cv_ref, o_ref = refs
    else:
        q_ref, k_ref, v_ref, o_ref = refs
    scale = HEAD_DIM ** -0.5
    nt = (((1,), (1,)), ((), ()))
    tq = q_ref.shape[0]
    for hk in range(k_ref.shape[1] // HEAD_DIM):
        kcols = slice(hk * HEAD_DIM, (hk + 1) * HEAD_DIM)
        k = k_ref[:, kcols].astype(BF16)
        v = v_ref[:, kcols].astype(BF16)
        if has_ctx:
            ck = ck_ref[:, hk, :].astype(BF16)
            cv = cv_ref[:, hk, :].astype(BF16)
        for g0 in range(hk * Q_PER_KV, (hk + 1) * Q_PER_KV, stack):
            q = jnp.concatenate([q_ref[:, g * HEAD_DIM:(g + 1) * HEAD_DIM] for g in range(g0, g0 + stack)], axis=0)
            q = (q * scale).astype(BF16)
            s = lax.dot_general(q, k, nt, preferred_element_type=F32)
            m = jnp.max(s, axis=-1, keepdims=True)
            if has_ctx:
                sc = lax.dot_general(q, ck, nt, preferred_element_type=F32)
                m = jnp.maximum(m, jnp.max(sc, axis=-1, keepdims=True))
            p = jnp.exp(s - m)
            den = jnp.sum(p, axis=-1, keepdims=True)
            o = _bdot(p.astype(BF16), v)
            if has_ctx:
                pc = jnp.exp(sc - m)
                den = den + jnp.sum(pc, axis=-1, keepdims=True)
                o = o + _bdot(pc.astype(BF16), cv)
            o = (o / den).astype(o_ref.dtype)
            for g in range(stack):
                o_ref[:, (g0 + g) * HEAD_DIM:(g0 + g + 1) * HEAD_DIM] = o[g * tq:(g + 1) * tq]


def _attention(p, n_batch, seq, row0, tq, ctx=None):
    q_blocks = seq // tq
    kvh = N_KV_HEADS
    gw = kvh * Q_PER_KV * HEAD_DIM
    kw = kvh * HEAD_DIM
    in_specs = [
        pl.BlockSpec((tq, gw), lambda b, h, qi: (row0 // tq + b * q_blocks + qi, h)),
        pl.BlockSpec((seq, kw), lambda b, h, qi: (row0 // seq + b, OUT_K // kw + h)),
        pl.BlockSpec((seq, kw), lambda b, h, qi: (row0 // seq + b, OUT_V // kw + h)),
    ]
    args = [p, p, p]
    if ctx is not None:
        assert kvh == N_KV_HEADS
        ctx_spec = pl.BlockSpec((None, None, PAST_LEN, N_KV_HEADS, HEAD_DIM), lambda b, h, qi: (b, 0, 0, 0, 0))
        in_specs += [ctx_spec, ctx_spec]
        args += [ctx[0], ctx[1]]
    return pl.pallas_call(
        functools.partial(_attn_kernel, has_ctx=ctx is not None, stack=Q_PER_KV if ctx is None else 1),
        out_shape=jax.ShapeDtypeStruct((n_batch * seq, ATT_WIDTH), BF16),
        grid=(n_batch, N_KV_HEADS // kvh, q_blocks),
        in_specs=in_specs,
        out_specs=pl.BlockSpec((tq, gw), lambda b, h, qi: (b * q_blocks + qi, h)),
        compiler_params=_params(("arbitrary", "arbitrary", "arbitrary")),
        name="attention_latent" if ctx is not None else "attention_prompt",
    )(*args)


def _hgrn_constants():
    c = HG_CHUNK
    t = np.arange(c)
    cums, sels, pairs, scans, diags = [], [], [], [], []
    for d in range(2):
        pos = t if d == 0 else c - 1 - t
        pu, pt = pos[None, :], pos[:, None]
        ms, ss, ws = [], [], []
        for l in range(HG_LEVELS):
            m = c >> l
            blk = pos // m
            mid = (blk * m + m // 2)[:, None]
            late = ((pos % m) >= m // 2)
            ms.append(np.where(late[:, None], (pu >= mid) & (pu <= pt), (pu > pt) & (pu < mid)))
            ss.append(np.broadcast_to(late[:, None], (c, c)))
            ws.append((blk[:, None] == blk[None, :]) & late[:, None] & ~late[None, :])
        ms.append(pu <= pt)
        ms.append(pu > pt)
        cums.append(np.concatenate(ms, axis=0))
        sels.append(np.stack(ss))
        pairs.append(np.stack(ws))
        scans.append(np.concatenate([pu <= pt, pu <= pt], axis=1))
        diags.append(((pos // (c // 2))[:, None] == (pos // (c // 2))[None, :]) & (pu <= pt))
    return (jnp.asarray(np.stack(cums), BF16), jnp.asarray(np.stack(sels), F32),
            jnp.asarray(np.stack(pairs), F32), jnp.asarray(np.stack(scans), BF16),
            jnp.asarray(np.stack(diags), F32))


def _hgrn_kernel(*refs, n_seqs, n_chunks, has_s0):
    (hq_ref, zf_ref, zb_ref, hi_ref, hg_ref, lb_ref, og_ref, cum_ref, sel_ref, pair_ref, scan_ref,
     diag_ref) = refs[:12]
    refs = refs[12:]
    if has_s0:
        s0f_ref, s0b_ref, o_ref = refs[:3]
        refs = refs[3:]
    else:
        o_ref, sf_ref, sb_ref = refs[:3]
        refs = refs[3:]
    st_ref, acc_ref = refs
    hb = st_ref.shape[0]
    c = HG_CHUNK
    half = c // 2
    nt = (((1,), (1,)), ((), ()))
    tn = (((0,), (0,)), ((), ()))

    def direction(d, z_ref):
        raw = lb_ref[d]
        e = jnp.exp(raw - jnp.max(raw, axis=0, keepdims=True))
        lb = e[0:1] / jnp.sum(e, axis=0, keepdims=True)

        def initial_state(s, hh):
            if has_s0:
                return (s0f_ref, s0b_ref)[d][s, hh].T
            return jnp.zeros((HG_DV, HG_DK), F32)

        def gates(rows):
            f = lb + (1.0 - lb) * _sigmoid(z_ref[rows, :])
            logf = jnp.log(f)
            hi16 = logf.astype(BF16)
            lo16 = (logf - hi16.astype(F32)).astype(BF16)
            return f, hi16, lo16

        def row(p):
            t = p if d == 0 else c - 1 - p
            return slice(t, t + 1)

        early, late = (slice(0, half), slice(half, c)) if d == 0 else (slice(half, c), slice(0, half))

        def in_row_order(x_early, x_late):
            return jnp.concatenate([x_early, x_late] if d == 0 else [x_late, x_early], axis=0)

        def emit(rows, cols, o):
            if d == 0:
                acc_ref[rows, cols] = o
            else:
                tot = acc_ref[rows, cols] + o
                o_ref[rows, cols] = (_rms(tot, og_ref[...]) * hg_ref[rows, cols]).astype(o_ref.dtype)

        def two_level_operands(rows):
            f, hi16, lo16 = gates(rows)
            b = _bdot(scan_ref[d], jnp.concatenate([hi16, lo16], axis=0))
            kk = 1.0 - f
            q = hq_ref[rows, :]
            r_mid = b[row(half - 1)]
            x1 = in_row_order(kk[early] * jnp.exp(r_mid - b[early]), q[late] * jnp.exp(b[late] - r_mid)).astype(BF16)
            dq = in_row_order(b[early] - b[row(half // 2 - 1)], b[late] - b[row(half + half // 2 - 1)])
            b_end = b[row(c - 1)]
            span = jnp.maximum(
                jnp.maximum(b[row(0)] - b[row(half // 2 - 1)], b[row(half // 2 - 1)] - b[row(half - 1)]),
                jnp.maximum(b[row(half)] - b[row(half + half // 2 - 1)],
                            b[row(half + half // 2 - 1)] - b[row(c - 1)]))
            return dict(x1=x1, xq=(q * jnp.exp(dq)).astype(BF16), xk=(kk * jnp.exp(-dq)).astype(BF16),
                        q_in=(q * jnp.exp(b)).astype(BF16), k_out=(kk * jnp.exp(b_end - b)).astype(BF16),
                        a_end=jnp.exp(b_end), iv=hi_ref[rows, :].astype(BF16), span=span)

        def two_level_chunk(rows, ops, states, need_state):
            new_states = []
            for hh in range(hb):
                cols = slice(hh * HG_DK, (hh + 1) * HG_DK)
                g1 = _bdot(ops["x1"][:, cols], ops["x1"][:, cols].astype(F32).T.astype(BF16))
                g2 = _bdot(ops["xq"][:, cols], ops["xk"][:, cols].astype(F32).T.astype(BF16))
                att = jnp.where(pair_ref[d, 0] > 0.5, g1, jnp.where(diag_ref[d] > 0.5, g2, 0.0)).astype(BF16)
                st = states[hh]
                o = _bdot(att, ops["iv"][:, cols])
                if st is not None:
                    o = o + lax.dot_general(ops["q_in"][:, cols], st.astype(BF16), nt, preferred_element_type=F32)
                pending.append((rows, cols, o))
                if not need_state:
                    new_states.append(None)
                    continue
                dst = lax.dot_general(ops["iv"][:, cols], ops["k_out"][:, cols], tn, preferred_element_type=F32)
                new_states.append(dst if st is None else ops["a_end"][:, cols] * st + dst)
            return new_states

        def all_levels(rows):
            f, hi16, lo16 = gates(rows)
            cum = cum_ref[d]
            eall = jnp.exp(_bdot(cum, hi16) + _bdot(cum, lo16))
            for hh in range(hb):
                cols = slice(hh * HG_DK, (hh + 1) * HG_DK)
                kk = 1.0 - f[:, cols]
                q = hq_ref[rows, cols]
                iv = hi_ref[rows, cols]
                iv16 = iv.astype(BF16)
                att = jnp.zeros((c, c), F32)
                for l in range(HG_LEVELS):
                    x = (kk + sel_ref[d, l] * (q - kk)) * eall[l * c:(l + 1) * c, cols]
                    xb = x.astype(BF16)
                    att = att + pair_ref[d, l] * lax.dot_general(xb, xb, nt, preferred_element_type=F32)
                e_in = eall[HG_LEVELS * c:(HG_LEVELS + 1) * c, cols]
                e_out = eall[(HG_LEVELS + 1) * c:, cols]
                st = st_ref[hh]
                o = (_bdot(att.astype(BF16), iv16)
                     + jnp.sum(q * kk, axis=-1, keepdims=True) * iv
                     + lax.dot_general((q * e_in).astype(BF16), st.astype(BF16), nt, preferred_element_type=F32))
                dst = lax.dot_general(iv16, (kk * e_out).astype(BF16), tn, preferred_element_type=F32)
                st_ref[hh] = (e_in[0:1] * e_out[0:1]) * st + dst
                emit(rows, cols, o)

        def chunk_start(s, ci):
            cidx = ci if d == 0 else n_chunks - 1 - ci
            return (s * n_chunks + cidx) * c

        worst = jnp.zeros((1, hb * HG_DK), F32)
        pending = []
        states = [[initial_state(s, hh) if has_s0 else None for hh in range(hb)] for s in range(n_seqs)]
        for ci in range(n_chunks):
            for s in range(n_seqs):
                rows = slice(chunk_start(s, ci), chunk_start(s, ci) + c)
                ops = two_level_operands(rows)
                states[s] = two_level_chunk(rows, ops, states[s], need_state=ci < n_chunks - 1 or not has_s0)
                worst = jnp.maximum(worst, ops["span"])
        for item in pending:
            emit(*item)
        if not has_s0:
            for s in range(n_seqs):
                for hh in range(hb):
                    (sf_ref, sb_ref)[d][s, hh] = states[s][hh].T

        @pl.when(jnp.logical_not(jnp.max(worst) <= HG_SAFE_EXPONENT))
        def _():
            for s in range(n_seqs):
                for hh in range(hb):
                    st_ref[hh] = initial_state(s, hh)

                def chunk(ci, carry):
                    all_levels(pl.ds(pl.multiple_of(chunk_start(s, ci), c), c))
                    return carry

                lax.fori_loop(0, n_chunks, chunk, 0)
                if not has_s0:
                    for hh in range(hb):
                        (sf_ref, sb_ref)[d][s, hh] = st_ref[hh].T

    direction(0, zf_ref)
    direction(1, zb_ref)


def _hgrn(p, lb_raw, o_gain, consts, n_batch, seq, row0, n_seqs, hb, s0=None):
    w = hb * HG_DK
    rows = n_seqs * seq

    def seg(col):
        return pl.BlockSpec((rows, w), lambda b, h: (row0 // rows + b, col // w + h))

    def const(a):
        return pl.BlockSpec(a.shape, lambda b, h: (0,) * a.ndim)

    in_specs = [seg(OUT_HQ), seg(OUT_ZF), seg(OUT_ZF + HG_KW), seg(OUT_HI), seg(OUT_HG),
                pl.BlockSpec((2, DEPTH + 1, w), lambda b, h: (0, 0, h)),
                pl.BlockSpec((1, HG_DV), lambda b, h: (0, 0))] + [const(a) for a in consts]
    args = [p, p, p, p, p, lb_raw, o_gain, *consts]
    has_s0 = s0 is not None
    st_spec = pl.BlockSpec((n_seqs, hb, HG_DK, HG_DV), lambda b, h: (b, h, 0, 0))
    if has_s0:
        in_specs += [st_spec, st_spec]
        args += [s0[0], s0[1]]
    out_shape = [jax.ShapeDtypeStruct((n_batch * seq, HG_VW), BF16)]
    out_specs = [pl.BlockSpec((rows, w), lambda b, h: (b, h))]
    if not has_s0:
        st_shape = jax.ShapeDtypeStruct((n_batch, HG_HEADS, HG_DK, HG_DV), F32)
        out_shape += [st_shape, st_shape]
        out_specs += [st_spec, st_spec]
    return pl.pallas_call(
        functools.partial(_hgrn_kernel, n_seqs=n_seqs, n_chunks=seq // HG_CHUNK, has_s0=has_s0),
        out_shape=tuple(out_shape),
        grid=(n_batch // n_seqs, HG_HEADS // hb),
        in_specs=in_specs,
        out_specs=tuple(out_specs),
        scratch_shapes=[pltpu.VMEM((hb, HG_DV, HG_DK), F32), pltpu.VMEM((rows, w), F32)],
        compiler_params=_params(("arbitrary", "arbitrary")),
        name="hgrn_latent" if has_s0 else "hgrn_prompt",
    )(*args)


def _outproj_kernel(attp_ref, atts_ref, hgp_ref, hgs_ref, wa_ref, wb_ref, xp_ref, xs_ref, g1_ref, gain_ref,
                    sh_ref, sc_ref, x1_ref, h2_ref, full_ref, wcache_ref, ss_ref):
    i = pl.program_id(0)
    n = pl.program_id(1)
    nt = D_MODEL // OUT_COL_TILE

    @pl.when(i == 0)
    def _():
        wcache_ref[n, 0] = wa_ref[...].astype(BF16)
        wcache_ref[n, 1] = wb_ref[...].astype(BF16)

    @pl.when(n == 0)
    def _():
        ss_ref[...] = jnp.zeros(ss_ref.shape, F32)

    def run(att_ref, hg_ref, x_ref, last):
        acc = _bdot(att_ref[...], wcache_ref[n, 0]) + _bdot(hg_ref[...], wcache_ref[n, 1])
        x1 = x_ref[...] + g1_ref[...] * acc
        x1_ref[...] = x1
        sq = x1 * x1
        part = sq[:, 0:HEAD_DIM]
        for k in range(1, OUT_COL_TILE // HEAD_DIM):
            part = part + sq[:, k * HEAD_DIM:(k + 1) * HEAD_DIM]
        if not last:
            full_ref[n] = x1
            ss_ref[...] += part
            return
        rstd = lax.rsqrt(jnp.sum(ss_ref[...] + part, axis=-1, keepdims=True) / D_MODEL + EPS)
        gs = gain_ref[...] * (1.0 + sc_ref[...])
        for k in range(nt):
            cols = slice(k * OUT_COL_TILE, (k + 1) * OUT_COL_TILE)
            xk = x1 if k == nt - 1 else full_ref[k]
            h2_ref[:, cols] = (xk * rstd * gs[:, cols] + sh_ref[:, cols]).astype(BF16)

    prompt = i < N_PROMPT_TILES
    for last in (False, True):
        step = (n == nt - 1) if last else (n < nt - 1)
        pl.when(jnp.logical_and(prompt, step))(functools.partial(run, attp_ref, hgp_ref, xp_ref, last))
        pl.when(jnp.logical_and(jnp.logical_not(prompt), step))(functools.partial(run, atts_ref, hgs_ref, xs_ref, last))


def _out_projection(att_p, att_s, hg_p, hg_s, w_out, xp, xs, mods, norm_mlp):
    tm, tn = TOK_TILE, OUT_COL_TILE
    nt = D_MODEL // tn

    def prompt_rows(width):
        return pl.BlockSpec((tm, width), lambda i, n: (jnp.minimum(i, N_PROMPT_TILES - 1), 0))

    def sample_rows(width):
        return pl.BlockSpec((tm, width), lambda i, n: (jnp.maximum(i - N_PROMPT_TILES, 0), 0))

    return pl.pallas_call(
        _outproj_kernel,
        out_shape=(jax.ShapeDtypeStruct((N_TOK, D_MODEL), F32), jax.ShapeDtypeStruct((N_TOK, D_MODEL), BF16)),
        grid=(N_TILES, nt),
        in_specs=[prompt_rows(ATT_WIDTH), sample_rows(ATT_WIDTH), prompt_rows(HG_VW), sample_rows(HG_VW),
                  pl.BlockSpec((ATT_WIDTH, tn), lambda i, n: (0, jnp.where(i == 0, n, nt - 1))),
                  pl.BlockSpec((HG_VW, tn), lambda i, n: (1, jnp.where(i == 0, n, nt - 1))),
                  pl.BlockSpec((tm, tn), lambda i, n: (jnp.minimum(i, N_PROMPT_TILES - 1),
                                                       jnp.where(i < N_PROMPT_TILES, n, nt - 1))),
                  pl.BlockSpec((tm, tn), lambda i, n: (jnp.maximum(i - N_PROMPT_TILES, 0),
                                                       jnp.where(i < N_PROMPT_TILES, 0, n))),
                  _mod_spec(0, 2, tm, width=tn, col=lambda i, n: n),
                  pl.BlockSpec((1, D_MODEL), lambda i, n: (0, 0)),
                  _mod_spec(0, 3, tm), _mod_spec(0, 4, tm)],
        out_specs=(pl.BlockSpec((tm, tn), lambda i, n: (i, n)),
                   pl.BlockSpec((tm, D_MODEL), lambda i, n: (i, 0))),
        scratch_shapes=[pltpu.VMEM((nt - 1, tm, tn), F32), pltpu.VMEM((nt, 2, ATT_WIDTH, tn), BF16),
                        pltpu.VMEM((tm, HEAD_DIM), F32)],
        compiler_params=_params(("arbitrary", "arbitrary")),
        name="out_projection",
    )(att_p, att_s, hg_p, hg_s, w_out, w_out, xp, xs, mods, norm_mlp, mods, mods)


def _mlp_kernel(h_ref, w1_ref, w2_ref, x_ref, g2_ref, *rest, final):
    if final:
        fin_ref, o_ref, res_ref = rest
    else:
        o_ref, res_ref = rest
    j = pl.program_id(1)
    res_ref[j] = x_ref[...]

    def step(first, last):
        a = jnp.square(jnp.maximum(_bdot(h_ref[...], w1_ref[...].astype(BF16)), 0.0)).astype(BF16)
        for n in range(D_MODEL // MLP_OUT_CHUNK):
            cols = slice(n * MLP_OUT_CHUNK, (n + 1) * MLP_OUT_CHUNK)
            p = _bdot(a, w2_ref[:, cols].astype(BF16))
            if not first:
                p = o_ref[:, cols] + p
            if last:
                per = MLP_OUT_CHUNK // MLP_RES_COLS
                res = jnp.concatenate([res_ref[n * per + k] for k in range(per)], axis=1)
                p = res + g2_ref[:, cols] * p
            o_ref[:, cols] = p
        if last and final:
            o_ref[...] = _rms(o_ref[...], fin_ref[...])

    pl.when(j == 0)(lambda: step(True, False))
    pl.when(jnp.logical_and(j > 0, j < MLP_STEPS - 1))(lambda: step(False, False))
    pl.when(j == MLP_STEPS - 1)(lambda: step(False, True))


def _mlp(h, x, w1, w2, mods, layer, tile0, n_tiles, final_norm=None):
    tm, th = TOK_TILE, FF_TILE
    final = final_norm is not None
    in_specs = [pl.BlockSpec((tm, D_MODEL), lambda i, j: (tile0 + i, 0)),
                pl.BlockSpec((None, D_MODEL, th), lambda i, j: (layer, 0, j)),
                pl.BlockSpec((None, th, D_MODEL), lambda i, j: (layer, j, 0)),
                pl.BlockSpec((tm, MLP_RES_COLS), lambda i, j: (tile0 + i, j)),
                _mod_spec(layer, 5, tm, tile_of=lambda i, j: tile0 + i)]
    args = [h, w1, w2, x, mods]
    if final:
        in_specs.append(pl.BlockSpec((1, D_MODEL), lambda i, j: (0, 0)))
        args.append(final_norm)
    return pl.pallas_call(
        functools.partial(_mlp_kernel, final=final),
        out_shape=jax.ShapeDtypeStruct((n_tiles * tm, D_MODEL), F32),
        grid=(n_tiles, MLP_STEPS),
        in_specs=in_specs,
        out_specs=pl.BlockSpec((tm, D_MODEL), lambda i, j: (i, 0)),
        scratch_shapes=[pltpu.VMEM((MLP_STEPS, tm, MLP_RES_COLS), F32)],
        compiler_params=_params(("arbitrary", "arbitrary")),
        name="mlp_final" if final else "mlp",
    )(*args)


def _pool_kernel(x_ref, wp_ref, ps_ref, gain1_ref, sh1_ref, sc1_ref, g1_ref, gain2_ref, sh2_ref, sc2_ref,
                 x3_ref, h4_ref):
    i = pl.program_id(0)
    tm = TOK_TILE
    rstd = lax.rsqrt(jnp.mean(x_ref[...] * x_ref[...], axis=-1, keepdims=True) + EPS)
    gs1 = gain1_ref[...] * (1.0 + sc1_ref[...])
    mix_gain = g1_ref[...] * ps_ref[...]

    def widen(m):
        return jnp.concatenate([m] * (POOL_GROUP // HEAD_DIM), axis=1)

    def mix_tile(seq):
        n_seq = tm // seq
        pitch = seq + 2 * POOL_HALO
        n_pad = n_seq * pitch
        halo = jnp.zeros((POOL_HALO, POOL_GROUP), F32)
        pos = lax.broadcasted_iota(jnp.int32, (seq, HEAD_DIM), 0)

        def down(a, k):
            return pltpu.roll(a, k % n_pad, 0)

        sumsq = None
        for g, w in enumerate(POOL_WINDOWS):
            half = w // 2
            cols = slice(g * POOL_GROUP, (g + 1) * POOL_GROUP)
            x = x_ref[:, cols]
            h = x * rstd * gs1[:, cols] + sh1_ref[:, cols]
            padded = jnp.concatenate(
                [piece for s in range(n_seq) for piece in (halo, h[s * seq:(s + 1) * seq], halo)], axis=0)
            back = padded
            m = 1
            while m < half:
                back = back + down(back, m)
                m *= 2
            ahead = back if half == 1 else down(back, -(half - 1))
            total = down(back, 1) + ahead
            total = jnp.concatenate(
                [total[s * pitch + POOL_HALO:s * pitch + POOL_HALO + seq] for s in range(n_seq)], axis=0)
            count = (jnp.minimum(pos + (w - half), seq) - jnp.maximum(pos - half, 0)).astype(F32)
            inv = jnp.concatenate([widen(1.0 / count)] * n_seq, axis=0)
            pooled = (total * inv - h).astype(BF16)
            mix = _bdot(pooled, wp_ref[g].astype(BF16))
            x3 = x + mix_gain[:, cols] * mix
            x3_ref[:, cols] = x3
            sq = x3 * x3
            for k in range(POOL_GROUP // HEAD_DIM):
                part = sq[:, k * HEAD_DIM:(k + 1) * HEAD_DIM]
                sumsq = part if sumsq is None else sumsq + part
        rstd3 = lax.rsqrt(jnp.sum(sumsq, axis=-1, keepdims=True) / D_MODEL + EPS)
        h4_ref[...] = (x3_ref[...] * rstd3 * (gain2_ref[...] * (1.0 + sc2_ref[...])) + sh2_ref[...]).astype(BF16)

    pl.when(i < N_PROMPT_TILES)(lambda: mix_tile(SEQ))
    pl.when(i >= N_PROMPT_TILES)(lambda: mix_tile(DEC_SEQ))


def _pool_mixer(x, w_pool, pool_scale, norm_mix, norm_mlp, mods):
    tm = TOK_TILE
    vec = pl.BlockSpec((1, D_MODEL), lambda i: (0, 0))
    tile = pl.BlockSpec((tm, D_MODEL), lambda i: (i, 0))
    return pl.pallas_call(
        _pool_kernel,
        out_shape=(jax.ShapeDtypeStruct((N_TOK, D_MODEL), F32), jax.ShapeDtypeStruct((N_TOK, D_MODEL), BF16)),
        grid=(N_TILES,),
        in_specs=[tile, pl.BlockSpec((len(POOL_WINDOWS), POOL_GROUP, POOL_GROUP), lambda i: (0, 0, 0)),
                  vec, vec, _mod_spec(1, 0, tm), _mod_spec(1, 1, tm), _mod_spec(1, 2, tm),
                  vec, _mod_spec(1, 3, tm), _mod_spec(1, 4, tm)],
        out_specs=(tile, tile),
        compiler_params=_params(("arbitrary",)),
        name="pool_mixer",
    )(x, w_pool, pool_scale, norm_mix, mods, mods, mods, norm_mlp, mods, mods)


def kernel(x_prompt, x_sample, cache_k, cache_v, state_hgrn_fwd, state_hgrn_bwd, c, c_ctx, w_ada, b_ada,
           norm_mix, norm_mlp, w_in_ab, w_out_ab, q_norm, k_norm, hg_norm, lb_raw, w_pool, pool_scale,
           w_mlp_in, w_mlp_out, final_norm):
    xp = x_prompt.reshape(N_PROMPT, D_MODEL)
    xs = x_sample.reshape(N_SAMPLE, D_MODEL)
    cv = jnp.concatenate([c_ctx[None, :], c, jnp.zeros((ADA_ROWS - 1 - DEC_BATCH, D_MODEL), F32)], axis=0)
    mods, h0 = _ada_table_and_modulate(cv, w_ada, b_ada, xp, xs, norm_mix)

    proj, new_k, new_v = _in_projection(h0, w_in_ab[0], q_norm[0:1], k_norm[0:1])
    att_p = _attention(proj, BATCH, SEQ, 0, SEQ)
    att_s = _attention(proj, DEC_BATCH, DEC_SEQ, N_PROMPT, 512, ctx=(cache_k, cache_v))
    consts = _hgrn_constants()
    hg_p, s_fwd, s_bwd = _hgrn(proj, lb_raw, hg_norm[0:1], consts, BATCH, SEQ, 0, HG_PROMPT_SEQS, HG_PROMPT_HEADS)
    s0 = (state_hgrn_fwd.reshape(DEC_BATCH, HG_HEADS, HG_DK, HG_DV),
          state_hgrn_bwd.reshape(DEC_BATCH, HG_HEADS, HG_DK, HG_DV))
    (hg_s,) = _hgrn(proj, lb_raw, hg_norm[0:1], consts, DEC_BATCH, DEC_SEQ, N_PROMPT, 1, HG_LATENT_HEADS, s0=s0)
    x1, h2 = _out_projection(att_p, att_s, hg_p, hg_s, w_out_ab[0], xp, xs, mods, norm_mlp[0:1])
    x2 = _mlp(h2, x1, w_mlp_in, w_mlp_out, mods, 0, 0, N_TILES)

    x3, h4 = _pool_mixer(x2, w_pool[0], pool_scale[0:1], norm_mix[1:2], norm_mlp[1:2], mods)
    fin = final_norm[None, :]
    y_prompt = _mlp(h4, x3, w_mlp_in, w_mlp_out, mods, 1, 0, N_PROMPT_TILES, final_norm=fin)
    y_sample = _mlp(h4, x3, w_mlp_in, w_mlp_out, mods, 1, N_PROMPT_TILES, N_TILES - N_PROMPT_TILES,
                    final_norm=fin)

    return (y_prompt.reshape(BATCH, SEQ, D_MODEL), y_sample.reshape(DEC_BATCH, DEC_SEQ, D_MODEL),
            new_k, new_v,
            s_fwd.reshape(BATCH, 1, HG_HEADS, HG_DK, HG_DV), s_bwd.reshape(BATCH, 1, HG_HEADS, HG_DK, HG_DV))
```

```python
import functools

import numpy as np
import jax
import jax.numpy as jnp
from jax import lax
from jax.experimental import pallas as pl
from jax.experimental.pallas import tpu as pltpu

F32 = jnp.float32
BF16 = jnp.bfloat16

D_MODEL = 2048
BATCH = 16
SEQ = 256
DEPTH = 2
DEC_BATCH = 2
DEC_SEQ = 1024
PAST_LEN = 256
GRID_W = 64
HEAD_DIM = 128
N_Q_HEADS = 8
N_KV_HEADS = 2
Q_PER_KV = N_Q_HEADS // N_KV_HEADS
ATT_WIDTH = N_Q_HEADS * HEAD_DIM
KV_WIDTH = N_KV_HEADS * HEAD_DIM
HG_HEADS = 8
HG_DK = 128
HG_DV = 128
HG_KW = HG_HEADS * HG_DK
HG_VW = HG_HEADS * HG_DV
IN_AB = ATT_WIDTH + 2 * KV_WIDTH + 3 * HG_KW + 2 * HG_VW
POOL_WINDOWS = (2, 4, 8, 16)
POOL_GROUP = D_MODEL // len(POOL_WINDOWS)
POOL_HALO = 8
D_FF = 4 * D_MODEL
ROPE_THETA = 10000.0
ROPE_HALF = HEAD_DIM // 2
EPS = 1e-6
N_MOD = 6

N_PROMPT = BATCH * SEQ
N_SAMPLE = DEC_BATCH * DEC_SEQ
N_TOK = N_PROMPT + N_SAMPLE
ADA_ROWS = 16
ADA_COL_TILE = 1024
ADA_KEEP = 2 * D_MODEL // ADA_COL_TILE
MOD0_TOK_TILE = 512
TOK_TILE = 1024
N_TILES = N_TOK // TOK_TILE
N_PROMPT_TILES = N_PROMPT // TOK_TILE
PROJ_TOK_TILE = 2048
PROJ_COL_TILE = 512
OUT_COL_TILE = 512
FF_TILE = 512
MLP_STEPS = D_FF // FF_TILE
MLP_RES_COLS = D_MODEL // MLP_STEPS
MLP_OUT_CHUNK = 512
HG_CHUNK = 128
HG_LEVELS = 7
HG_PROMPT_HEADS = 8
HG_PROMPT_SEQS = 1
HG_LATENT_HEADS = 4
HG_SAFE_EXPONENT = 80.0
V7X_VMEM_BYTES = 64 * 2 ** 20
VMEM_LIMIT = 60 * 2 ** 20
assert VMEM_LIMIT < V7X_VMEM_BYTES
assert DEPTH == 2
assert ATT_WIDTH == HG_VW

COL_K = ATT_WIDTH
COL_V = COL_K + KV_WIDTH
COL_HQ = COL_V + KV_WIDTH
COL_ZF = COL_HQ + HG_KW
COL_HI = COL_ZF + 2 * HG_KW
COL_HG = COL_HI + HG_VW
OUT_HQ = ATT_WIDTH
OUT_ZF = OUT_HQ + HG_KW
OUT_HI = OUT_ZF + 2 * HG_KW
OUT_HG = OUT_HI + HG_VW
OUT_K = OUT_HG + HG_VW
OUT_V = OUT_K + KV_WIDTH


def _params(semantics):
    return pltpu.CompilerParams(dimension_semantics=semantics, vmem_limit_bytes=VMEM_LIMIT)


def _sigmoid(x):
    return 1.0 / (1.0 + jnp.exp(-x))


def _silu(x):
    return x * _sigmoid(x)


def _rms(x, gain):
    return x * lax.rsqrt(jnp.mean(x * x, axis=-1, keepdims=True) + EPS) * gain


def _bdot(a, b):
    return jnp.dot(a, b, preferred_element_type=F32)


def _mod_row(tile, tile_rows):
    first = N_PROMPT // tile_rows
    per_seq = DEC_SEQ // tile_rows
    return jnp.where(tile < first, 0, 1 + (tile - first) // per_seq)


def _mod_spec(layer, chunk, tile_rows, width=D_MODEL, col=lambda *g: 0, tile_of=lambda *g: g[0]):
    per = D_MODEL // width
    return pl.BlockSpec((None, None, 1, width),
                        lambda *g: (layer, _mod_row(tile_of(*g), tile_rows), 0, chunk * per + col(*g)))


def _modulate(x, gain, shift, scale):
    return x * lax.rsqrt(jnp.mean(x * x, axis=-1, keepdims=True) + EPS) * (gain * (1.0 + scale)) + shift


def _ada_kernel(cv_ref, w_ref, b_ref, xp_ref, xs_ref, gain_ref, o_ref, h_ref, keep_ref):
    s = pl.program_id(0)
    tile = _bdot(_silu(cv_ref[...]).astype(BF16), w_ref[...].astype(BF16)) + b_ref[...]
    o_ref[:, 0, :] = tile

    @pl.when(s < ADA_KEEP)
    def _():
        keep_ref[s] = tile

    t = s - ADA_KEEP
    n_prompt = N_PROMPT // MOD0_TOK_TILE

    def run(x_ref):
        r = _mod_row(t, MOD0_TOK_TILE)
        per = D_MODEL // ADA_COL_TILE
        shift = jnp.concatenate([keep_ref[k, pl.ds(r, 1), :] for k in range(per)], axis=1)
        scale = jnp.concatenate([keep_ref[per + k, pl.ds(r, 1), :] for k in range(per)], axis=1)
        h_ref[...] = _modulate(x_ref[...], gain_ref[...], shift, scale).astype(BF16)

    pl.when(jnp.logical_and(t >= 0, t < n_prompt))(lambda: run(xp_ref))
    pl.when(jnp.logical_and(t >= n_prompt, t < N_TOK // MOD0_TOK_TILE))(lambda: run(xs_ref))


def _ada_table_and_modulate(cv, w_ada, b_ada, xp, xs, norm_mix):
    tn, tm = ADA_COL_TILE, MOD0_TOK_TILE
    n = N_MOD * D_MODEL
    per_layer = n // tn
    n_prompt = N_PROMPT // tm
    n_tiles = N_TOK // tm
    assert DEPTH * per_layer >= ADA_KEEP + n_tiles

    def tok(s):
        return jnp.clip(s - ADA_KEEP, 0, n_tiles - 1)

    return pl.pallas_call(
        _ada_kernel,
        out_shape=(jax.ShapeDtypeStruct((DEPTH, ADA_ROWS, 1, n), F32), jax.ShapeDtypeStruct((N_TOK, D_MODEL), BF16)),
        grid=(DEPTH * per_layer,),
        in_specs=[pl.BlockSpec((ADA_ROWS, D_MODEL), lambda s: (0, 0)),
                  pl.BlockSpec((None, D_MODEL, tn), lambda s: (s // per_layer, 0, s % per_layer)),
                  pl.BlockSpec((None, 1, tn), lambda s: (s // per_layer, 0, s % per_layer)),
                  pl.BlockSpec((tm, D_MODEL), lambda s: (jnp.minimum(tok(s), n_prompt - 1), 0)),
                  pl.BlockSpec((tm, D_MODEL), lambda s: (jnp.maximum(tok(s) - n_prompt, 0), 0)),
                  pl.BlockSpec((1, D_MODEL), lambda s: (0, 0))],
        out_specs=(pl.BlockSpec((None, ADA_ROWS, 1, tn), lambda s: (s // per_layer, 0, 0, s % per_layer)),
                   pl.BlockSpec((tm, D_MODEL), lambda s: (tok(s), 0))),
        scratch_shapes=[pltpu.VMEM((ADA_KEEP, ADA_ROWS, tn), F32)],
        compiler_params=_params(("arbitrary",)),
        name="ada_table_modulate0",
    )(cv, w_ada, b_ada.reshape(DEPTH, 1, n), xp, xs, norm_mix[0:1])


def _rope(y, cos, sin, perm2):
    hi = y.astype(BF16)
    lo = (y - hi.astype(F32)).astype(BF16)
    rot = _bdot(jnp.concatenate([hi, lo], axis=1), perm2)
    return y * cos + rot * sin


def _inproj_kernel(h_ref, w_ref, qg_ref, kg_ref, cos_ref, sin_ref, perm_ref, p_ref, nk_ref, nv_ref):
    i = pl.program_id(0)
    j = pl.program_id(1)
    latent = i >= N_PROMPT // PROJ_TOK_TILE
    heads = PROJ_COL_TILE // HEAD_DIM
    kv_tile = COL_K // PROJ_COL_TILE
    is_q = j < kv_tile

    def attention_tile(rope):
        gain = jnp.where(is_q, qg_ref[...], kg_ref[...])
        w = w_ref[...].astype(BF16)
        for s in range(PROJ_TOK_TILE // DEC_SEQ):
            rows = slice(s * DEC_SEQ, (s + 1) * DEC_SEQ)
            acc = _bdot(h_ref[rows, :], w)
            for hh in range(heads):
                cols = slice(hh * HEAD_DIM, (hh + 1) * HEAD_DIM)
                x = acc[:, cols]
                y = _rms(x, gain)
                if rope:
                    y = _rope(y, cos_ref[...], sin_ref[...], perm_ref[...])
                if hh >= N_KV_HEADS:
                    y = jnp.where(is_q, y, x)
                p_ref[rows, cols] = y

    pl.when(jnp.logical_and(j <= kv_tile, latent))(lambda: attention_tile(True))
    pl.when(jnp.logical_and(j <= kv_tile, jnp.logical_not(latent)))(lambda: attention_tile(False))

    @pl.when(jnp.logical_and(j == kv_tile, jnp.logical_not(latent)))
    def _():
        for s in range(PROJ_TOK_TILE // SEQ):
            rows = slice(s * SEQ, (s + 1) * SEQ)
            for hh in range(N_KV_HEADS):
                nk_ref[s, :, hh, :] = p_ref[rows, hh * HEAD_DIM:(hh + 1) * HEAD_DIM]
                nv_ref[s, :, hh, :] = p_ref[rows, KV_WIDTH + hh * HEAD_DIM:KV_WIDTH + (hh + 1) * HEAD_DIM]

    def plain_tile(act):
        w = w_ref[...].astype(BF16)
        for s in range(PROJ_TOK_TILE // DEC_SEQ):
            rows = slice(s * DEC_SEQ, (s + 1) * DEC_SEQ)
            p_ref[rows, :] = act(_bdot(h_ref[rows, :], w))

    is_gate = jnp.logical_and(j >= COL_ZF // PROJ_COL_TILE, j < COL_HI // PROJ_COL_TILE)
    is_value = jnp.logical_and(j >= COL_HI // PROJ_COL_TILE, j < COL_HG // PROJ_COL_TILE)
    raw = jnp.logical_or(is_gate, is_value)
    pl.when(raw)(lambda: plain_tile(lambda a: a))
    pl.when(jnp.logical_and(j > kv_tile, jnp.logical_not(raw)))(lambda: plain_tile(_silu))


def _rope_tables():
    t = np.arange(DEC_SEQ)
    row = (t // GRID_W).astype(np.float32)
    col = (t % GRID_W).astype(np.float32)
    inv = (np.float32(ROPE_THETA) ** (-np.arange(0, ROPE_HALF, 2, dtype=np.float32) / np.float32(ROPE_HALF))).astype(np.float32)
    ar = row[:, None] * inv
    ac = col[:, None] * inv
    ang = np.concatenate([ar, ar, ac, ac], axis=-1).astype(np.float32)
    cos = np.cos(ang).astype(np.float32)
    sin = np.sin(ang).astype(np.float32)
    qw = ROPE_HALF // 2
    perm = np.zeros((HEAD_DIM, HEAD_DIM), np.float32)
    for k in range(qw):
        perm[qw + k, k] = -1.0
        perm[k, qw + k] = 1.0
        perm[3 * qw + k, 2 * qw + k] = -1.0
        perm[2 * qw + k, 3 * qw + k] = 1.0
    return jnp.asarray(cos), jnp.asarray(sin), jnp.asarray(np.concatenate([perm, perm], axis=0), BF16)


def _in_projection(h, w_in, q_gain, k_gain):
    tm, tn = PROJ_TOK_TILE, PROJ_COL_TILE
    n_prompt_tiles = N_PROMPT // tm
    cos, sin, perm2 = _rope_tables()
    table = pl.BlockSpec((DEC_SEQ, HEAD_DIM), lambda i, j: (0, 0))
    gain = pl.BlockSpec((1, HEAD_DIM), lambda i, j: (0, 0))
    state = pl.BlockSpec((tm // SEQ, None, SEQ, N_KV_HEADS, HEAD_DIM),
                         lambda i, j: (jnp.minimum(i, n_prompt_tiles - 1), 0, 0, 0, 0),
                         pipeline_mode=pl.Buffered(1))
    state_shape = jax.ShapeDtypeStruct((BATCH, 1, SEQ, N_KV_HEADS, HEAD_DIM), F32)
    kv_tile = COL_K // tn

    def out_tile(j):
        return jnp.where(j < kv_tile, j, jnp.where(j == kv_tile, OUT_K // tn, j - 1))

    return pl.pallas_call(
        _inproj_kernel,
        out_shape=(jax.ShapeDtypeStruct((N_TOK, IN_AB), F32), state_shape, state_shape),
        grid=(N_TOK // tm, IN_AB // tn),
        in_specs=[pl.BlockSpec((tm, D_MODEL), lambda i, j: (i, 0)),
                  pl.BlockSpec((D_MODEL, tn), lambda i, j: (0, j)),
                  gain, gain, table, table, pl.BlockSpec((2 * HEAD_DIM, HEAD_DIM), lambda i, j: (0, 0))],
        out_specs=(pl.BlockSpec((tm, tn), lambda i, j: (i, out_tile(j))), state, state),
        compiler_params=_params(("arbitrary", "arbitrary")),
        name="in_projection",
    )(h, w_in, q_gain, k_gain, cos, sin, perm2)


def _attn_kernel(*refs, has_ctx, stack):
    if has_ctx:
        q_ref, k_ref, v_ref, ck_ref, cv_ref, o_ref = refs
    else:
        q_ref, k_ref, v_ref, o_ref = refs
    scale = HEAD_DIM ** -0.5
    nt = (((1,), (1,)), ((), ()))
    tq = q_ref.shape[0]
    for hk in range(k_ref.shape[1] // HEAD_DIM):
        kcols = slice(hk * HEAD_DIM, (hk + 1) * HEAD_DIM)
        k = k_ref[:, kcols].astype(BF16)
        v = v_ref[:, kcols].astype(BF16)
        if has_ctx:
            ck = ck_ref[:, hk, :].astype(BF16)
            cv = cv_ref[:, hk, :].astype(BF16)
        for g0 in range(hk * Q_PER_KV, (hk + 1) * Q_PER_KV, stack):
            q = jnp.concatenate([q_ref[:, g * HEAD_DIM:(g + 1) * HEAD_DIM] for g in range(g0, g0 + stack)], axis=0)
            q = (q * scale).astype(BF16)
            s = lax.dot_general(q, k, nt, preferred_element_type=F32)
            m = jnp.max(s, axis=-1, keepdims=True)
            if has_ctx:
                sc = lax.dot_general(q, ck, nt, preferred_element_type=F32)
                m = jnp.maximum(m, jnp.max(sc, axis=-1, keepdims=True))
            p = jnp.exp(s - m)
            den = jnp.sum(p, axis=-1, keepdims=True)
            o = _bdot(p.astype(BF16), v)
            if has_ctx:
                pc = jnp.exp(sc - m)
                den = den + jnp.sum(pc, axis=-1, keepdims=True)
                o = o + _bdot(pc.astype(BF16), cv)
            o = (o / den).astype(o_ref.dtype)
            for g in range(stack):
                o_ref[:, (g0 + g) * HEAD_DIM:(g0 + g + 1) * HEAD_DIM] = o[g * tq:(g + 1) * tq]


def _attention(p, n_batch, seq, row0, tq, ctx=None):
    q_blocks = seq // tq
    kvh = N_KV_HEADS
    gw = kvh * Q_PER_KV * HEAD_DIM
    kw = kvh * HEAD_DIM
    in_specs = [
        pl.BlockSpec((tq, gw), lambda b, h, qi: (row0 // tq + b * q_blocks + qi, h)),
        pl.BlockSpec((seq, kw), lambda b, h, qi: (row0 // seq + b, OUT_K // kw + h)),
        pl.BlockSpec((seq, kw), lambda b, h, qi: (row0 // seq + b, OUT_V // kw + h)),
    ]
    args = [p, p, p]
    if ctx is not None:
        assert kvh == N_KV_HEADS
        ctx_spec = pl.BlockSpec((None, None, PAST_LEN, N_KV_HEADS, HEAD_DIM), lambda b, h, qi: (b, 0, 0, 0, 0))
        in_specs += [ctx_spec, ctx_spec]
        args += [ctx[0], ctx[1]]
    return pl.pallas_call(
        functools.partial(_attn_kernel, has_ctx=ctx is not None, stack=Q_PER_KV if ctx is None else 1),
        out_shape=jax.ShapeDtypeStruct((n_batch * seq, ATT_WIDTH), BF16),
        grid=(n_batch, N_KV_HEADS // kvh, q_blocks),
        in_specs=in_specs,
        out_specs=pl.BlockSpec((tq, gw), lambda b, h, qi: (b * q_blocks + qi, h)),
        compiler_params=_params(("arbitrary", "arbitrary", "arbitrary")),
        name="attention_latent" if ctx is not None else "attention_prompt",
    )(*args)


def _hgrn_constants():
    c = HG_CHUNK
    t = np.arange(c)
    cums, sels, pairs, scans, diags = [], [], [], [], []
    for d in range(2):
        pos = t if d == 0 else c - 1 - t
        pu, pt = pos[None, :], pos[:, None]
        ms, ss, ws = [], [], []
        for l in range(HG_LEVELS):
            m = c >> l
            blk = pos // m
            mid = (blk * m + m // 2)[:, None]
            late = ((pos % m) >= m // 2)
            ms.append(np.where(late[:, None], (pu >= mid) & (pu <= pt), (pu > pt) & (pu < mid)))
            ss.append(np.broadcast_to(late[:, None], (c, c)))
            ws.append((blk[:, None] == blk[None, :]) & late[:, None] & ~late[None, :])
        ms.append(pu <= pt)
        ms.append(pu > pt)
        cums.append(np.concatenate(ms, axis=0))
        sels.append(np.stack(ss))
        pairs.append(np.stack(ws))
        scans.append(np.concatenate([pu <= pt, pu <= pt], axis=1))
        diags.append(((pos // (c // 2))[:, None] == (pos // (c // 2))[None, :]) & (pu <= pt))
    return (jnp.asarray(np.stack(cums), BF16), jnp.asarray(np.stack(sels), F32),
            jnp.asarray(np.stack(pairs), F32), jnp.asarray(np.stack(scans), BF16),
            jnp.asarray(np.stack(diags), F32))


def _hgrn_kernel(*refs, n_seqs, n_chunks, has_s0):
    (hq_ref, zf_ref, zb_ref, hi_ref, hg_ref, lb_ref, og_ref, cum_ref, sel_ref, pair_ref, scan_ref,
     diag_ref) = refs[:12]
    refs = refs[12:]
    if has_s0:
        s0f_ref, s0b_ref, o_ref = refs[:3]
        refs = refs[3:]
    else:
        o_ref, sf_ref, sb_ref = refs[:3]
        refs = refs[3:]
    st_ref, acc_ref = refs
    hb = st_ref.shape[0]
    c = HG_CHUNK
    half = c // 2
    nt = (((1,), (1,)), ((), ()))
    tn = (((0,), (0,)), ((), ()))

    def direction(d, z_ref):
        raw = lb_ref[d]
        e = jnp.exp(raw - jnp.max(raw, axis=0, keepdims=True))
        lb = e[0:1] / jnp.sum(e, axis=0, keepdims=True)

        def initial_state(s, hh):
            if has_s0:
                return (s0f_ref, s0b_ref)[d][s, hh].T
            return jnp.zeros((HG_DV, HG_DK), F32)

        def gates(rows):
            f = lb + (1.0 - lb) * _sigmoid(z_ref[rows, :])
            logf = jnp.log(f)
            hi16 = logf.astype(BF16)
            lo16 = (logf - hi16.astype(F32)).astype(BF16)
            return f, hi16, lo16

        def row(p):
            t = p if d == 0 else c - 1 - p
            return slice(t, t + 1)

        early, late = (slice(0, half), slice(half, c)) if d == 0 else (slice(half, c), slice(0, half))

        def in_row_order(x_early, x_late):
            return jnp.concatenate([x_early, x_late] if d == 0 else [x_late, x_early], axis=0)

        def emit(rows, cols, o):
            if d == 0:
                acc_ref[rows, cols] = o
            else:
                tot = acc_ref[rows, cols] + o
                o_ref[rows, cols] = (_rms(tot, og_ref[...]) * hg_ref[rows, cols]).astype(o_ref.dtype)

        def two_level_operands(rows):
            f, hi16, lo16 = gates(rows)
            b = _bdot(scan_ref[d], jnp.concatenate([hi16, lo16], axis=0))
            kk = 1.0 - f
            q = hq_ref[rows, :]
            r_mid = b[row(half - 1)]
            x1 = in_row_order(kk[early] * jnp.exp(r_mid - b[early]), q[late] * jnp.exp(b[late] - r_mid)).astype(BF16)
            dq = in_row_order(b[early] - b[row(half // 2 - 1)], b[late] - b[row(half + half // 2 - 1)])
            b_end = b[row(c - 1)]
            span = jnp.maximum(
                jnp.maximum(b[row(0)] - b[row(half // 2 - 1)], b[row(half // 2 - 1)] - b[row(half - 1)]),
                jnp.maximum(b[row(half)] - b[row(half + half // 2 - 1)],
                            b[row(half + half // 2 - 1)] - b[row(c - 1)]))
            return dict(x1=x1, xq=(q * jnp.exp(dq)).astype(BF16), xk=(kk * jnp.exp(-dq)).astype(BF16),
                        q_in=(q * jnp.exp(b)).astype(BF16), k_out=(kk * jnp.exp(b_end - b)).astype(BF16),
                        a_end=jnp.exp(b_end), iv=hi_ref[rows, :].astype(BF16), span=span)

        def two_level_chunk(rows, ops, states):
            new_states = []
            for hh in range(hb):
                cols = slice(hh * HG_DK, (hh + 1) * HG_DK)
                g1 = _bdot(ops["x1"][:, cols], ops["x1"][:, cols].astype(F32).T.astype(BF16))
                g2 = _bdot(ops["xq"][:, cols], ops["xk"][:, cols].astype(F32).T.astype(BF16))
                att = jnp.where(pair_ref[d, 0] > 0.5, g1, jnp.where(diag_ref[d] > 0.5, g2, 0.0)).astype(BF16)
                st = states[hh]
                o = (_bdot(att, ops["iv"][:, cols])
                     + lax.dot_general(ops["q_in"][:, cols], st.astype(BF16), nt, preferred_element_type=F32))
                pending.append((rows, cols, o))
                dst = lax.dot_general(ops["iv"][:, cols], ops["k_out"][:, cols], tn, preferred_element_type=F32)
                new_states.append(ops["a_end"][:, cols] * st + dst)
            return new_states

        def all_levels(rows):
            f, hi16, lo16 = gates(rows)
            cum = cum_ref[d]
            eall = jnp.exp(_bdot(cum, hi16) + _bdot(cum, lo16))
            for hh in range(hb):
                cols = slice(hh * HG_DK, (hh + 1) * HG_DK)
                kk = 1.0 - f[:, cols]
                q = hq_ref[rows, cols]
                iv = hi_ref[rows, cols]
                iv16 = iv.astype(BF16)
                att = jnp.zeros((c, c), F32)
                for l in range(HG_LEVELS):
                    x = (kk + sel_ref[d, l] * (q - kk)) * eall[l * c:(l + 1) * c, cols]
                    xb = x.astype(BF16)
                    att = att + pair_ref[d, l] * lax.dot_general(xb, xb, nt, preferred_element_type=F32)
                e_in = eall[HG_LEVELS * c:(HG_LEVELS + 1) * c, cols]
                e_out = eall[(HG_LEVELS + 1) * c:, cols]
                st = st_ref[hh]
                o = (_bdot(att.astype(BF16), iv16)
                     + jnp.sum(q * kk, axis=-1, keepdims=True) * iv
                     + lax.dot_general((q * e_in).astype(BF16), st.astype(BF16), nt, preferred_element_type=F32))
                dst = lax.dot_general(iv16, (kk * e_out).astype(BF16), tn, preferred_element_type=F32)
                st_ref[hh] = (e_in[0:1] * e_out[0:1]) * st + dst
                emit(rows, cols, o)

        def chunk_start(s, ci):
            cidx = ci if d == 0 else n_chunks - 1 - ci
            return (s * n_chunks + cidx) * c

        worst = jnp.zeros((1, hb * HG_DK), F32)
        pending = []
        states = [[initial_state(s, hh) for hh in range(hb)] for s in range(n_seqs)]
        for ci in range(n_chunks):
            for s in range(n_seqs):
                rows = slice(chunk_start(s, ci), chunk_start(s, ci) + c)
                ops = two_level_operands(rows)
                states[s] = two_level_chunk(rows, ops, states[s])
                worst = jnp.maximum(worst, ops["span"])
        for item in pending:
            emit(*item)
        if not has_s0:
            for s in range(n_seqs):
                for hh in range(hb):
                    (sf_ref, sb_ref)[d][s, hh] = states[s][hh].T

        @pl.when(jnp.logical_not(jnp.max(worst) <= HG_SAFE_EXPONENT))
        def _():
            for s in range(n_seqs):
                for hh in range(hb):
                    st_ref[hh] = initial_state(s, hh)

                def chunk(ci, carry):
                    all_levels(pl.ds(pl.multiple_of(chunk_start(s, ci), c), c))
                    return carry

                lax.fori_loop(0, n_chunks, chunk, 0)
                if not has_s0:
                    for hh in range(hb):
                        (sf_ref, sb_ref)[d][s, hh] = st_ref[hh].T

    direction(0, zf_ref)
    direction(1, zb_ref)


def _hgrn(p, lb_raw, o_gain, consts, n_batch, seq, row0, n_seqs, hb, s0=None):
    w = hb * HG_DK
    rows = n_seqs * seq

    def seg(col):
        return pl.BlockSpec((rows, w), lambda b, h: (row0 // rows + b, col // w + h))

    def const(a):
        return pl.BlockSpec(a.shape, lambda b, h: (0,) * a.ndim)

    in_specs = [seg(OUT_HQ), seg(OUT_ZF), seg(OUT_ZF + HG_KW), seg(OUT_HI), seg(OUT_HG),
                pl.BlockSpec((2, DEPTH + 1, w), lambda b, h: (0, 0, h)),
                pl.BlockSpec((1, HG_DV), lambda b, h: (0, 0))] + [const(a) for a in consts]
    args = [p, p, p, p, p, lb_raw, o_gain, *consts]
    has_s0 = s0 is not None
    st_spec = pl.BlockSpec((n_seqs, hb, HG_DK, HG_DV), lambda b, h: (b, h, 0, 0))
    if has_s0:
        in_specs += [st_spec, st_spec]
        args += [s0[0], s0[1]]
    out_shape = [jax.ShapeDtypeStruct((n_batch * seq, HG_VW), BF16)]
    out_specs = [pl.BlockSpec((rows, w), lambda b, h: (b, h))]
    if not has_s0:
        st_shape = jax.ShapeDtypeStruct((n_batch, HG_HEADS, HG_DK, HG_DV), F32)
        out_shape += [st_shape, st_shape]
        out_specs += [st_spec, st_spec]
    return pl.pallas_call(
        functools.partial(_hgrn_kernel, n_seqs=n_seqs, n_chunks=seq // HG_CHUNK, has_s0=has_s0),
        out_shape=tuple(out_shape),
        grid=(n_batch // n_seqs, HG_HEADS // hb),
        in_specs=in_specs,
        out_specs=tuple(out_specs),
        scratch_shapes=[pltpu.VMEM((hb, HG_DV, HG_DK), F32), pltpu.VMEM((rows, w), F32)],
        compiler_params=_params(("arbitrary", "arbitrary")),
        name="hgrn_latent" if has_s0 else "hgrn_prompt",
    )(*args)


def _outproj_kernel(attp_ref, atts_ref, hgp_ref, hgs_ref, wa_ref, wb_ref, xp_ref, xs_ref, g1_ref, gain_ref,
                    sh_ref, sc_ref, x1_ref, h2_ref, full_ref, wcache_ref, ss_ref):
    i = pl.program_id(0)
    n = pl.program_id(1)
    nt = D_MODEL // OUT_COL_TILE

    @pl.when(i == 0)
    def _():
        wcache_ref[n, 0] = wa_ref[...].astype(BF16)
        wcache_ref[n, 1] = wb_ref[...].astype(BF16)

    @pl.when(n == 0)
    def _():
        ss_ref[...] = jnp.zeros(ss_ref.shape, F32)

    def run(att_ref, hg_ref, x_ref, last):
        acc = _bdot(att_ref[...], wcache_ref[n, 0]) + _bdot(hg_ref[...], wcache_ref[n, 1])
        x1 = x_ref[...] + g1_ref[...] * acc
        x1_ref[...] = x1
        sq = x1 * x1
        part = sq[:, 0:HEAD_DIM]
        for k in range(1, OUT_COL_TILE // HEAD_DIM):
            part = part + sq[:, k * HEAD_DIM:(k + 1) * HEAD_DIM]
        if not last:
            full_ref[n] = x1
            ss_ref[...] += part
            return
        rstd = lax.rsqrt(jnp.sum(ss_ref[...] + part, axis=-1, keepdims=True) / D_MODEL + EPS)
        gs = gain_ref[...] * (1.0 + sc_ref[...])
        for k in range(nt):
            cols = slice(k * OUT_COL_TILE, (k + 1) * OUT_COL_TILE)
            xk = x1 if k == nt - 1 else full_ref[k]
            h2_ref[:, cols] = (xk * rstd * gs[:, cols] + sh_ref[:, cols]).astype(BF16)

    prompt = i < N_PROMPT_TILES
    for last in (False, True):
        step = (n == nt - 1) if last else (n < nt - 1)
        pl.when(jnp.logical_and(prompt, step))(functools.partial(run, attp_ref, hgp_ref, xp_ref, last))
        pl.when(jnp.logical_and(jnp.logical_not(prompt), step))(functools.partial(run, atts_ref, hgs_ref, xs_ref, last))


def _out_projection(att_p, att_s, hg_p, hg_s, w_out, xp, xs, mods, norm_mlp):
    tm, tn = TOK_TILE, OUT_COL_TILE
    nt = D_MODEL // tn

    def prompt_rows(width):
        return pl.BlockSpec((tm, width), lambda i, n: (jnp.minimum(i, N_PROMPT_TILES - 1), 0))

    def sample_rows(width):
        return pl.BlockSpec((tm, width), lambda i, n: (jnp.maximum(i - N_PROMPT_TILES, 0), 0))

    return pl.pallas_call(
        _outproj_kernel,
        out_shape=(jax.ShapeDtypeStruct((N_TOK, D_MODEL), F32), jax.ShapeDtypeStruct((N_TOK, D_MODEL), BF16)),
        grid=(N_TILES, nt),
        in_specs=[prompt_rows(ATT_WIDTH), sample_rows(ATT_WIDTH), prompt_rows(HG_VW), sample_rows(HG_VW),
                  pl.BlockSpec((ATT_WIDTH, tn), lambda i, n: (0, jnp.where(i == 0, n, nt - 1))),
                  pl.BlockSpec((HG_VW, tn), lambda i, n: (1, jnp.where(i == 0, n, nt - 1))),
                  pl.BlockSpec((tm, tn), lambda i, n: (jnp.minimum(i, N_PROMPT_TILES - 1),
                                                       jnp.where(i < N_PROMPT_TILES, n, nt - 1))),
                  pl.BlockSpec((tm, tn), lambda i, n: (jnp.maximum(i - N_PROMPT_TILES, 0),
                                                       jnp.where(i < N_PROMPT_TILES, 0, n))),
                  _mod_spec(0, 2, tm, width=tn, col=lambda i, n: n),
                  pl.BlockSpec((1, D_MODEL), lambda i, n: (0, 0)),
                  _mod_spec(0, 3, tm), _mod_spec(0, 4, tm)],
        out_specs=(pl.BlockSpec((tm, tn), lambda i, n: (i, n)),
                   pl.BlockSpec((tm, D_MODEL), lambda i, n: (i, 0))),
        scratch_shapes=[pltpu.VMEM((nt - 1, tm, tn), F32), pltpu.VMEM((nt, 2, ATT_WIDTH, tn), BF16),
                        pltpu.VMEM((tm, HEAD_DIM), F32)],
        compiler_params=_params(("arbitrary", "arbitrary")),
        name="out_projection",
    )(att_p, att_s, hg_p, hg_s, w_out, w_out, xp, xs, mods, norm_mlp, mods, mods)


def _mlp_kernel(h_ref, w1_ref, w2_ref, x_ref, g2_ref, *rest, final):
    if final:
        fin_ref, o_ref, res_ref = rest
    else:
        o_ref, res_ref = rest
    j = pl.program_id(1)
    res_ref[j] = x_ref[...]

    def step(first, last):
        a = jnp.square(jnp.maximum(_bdot(h_ref[...], w1_ref[...].astype(BF16)), 0.0)).astype(BF16)
        for n in range(D_MODEL // MLP_OUT_CHUNK):
            cols = slice(n * MLP_OUT_CHUNK, (n + 1) * MLP_OUT_CHUNK)
            p = _bdot(a, w2_ref[:, cols].astype(BF16))
            if not first:
                p = o_ref[:, cols] + p
            if last:
                per = MLP_OUT_CHUNK // MLP_RES_COLS
                res = jnp.concatenate([res_ref[n * per + k] for k in range(per)], axis=1)
                p = res + g2_ref[:, cols] * p
            o_ref[:, cols] = p
        if last and final:
            o_ref[...] = _rms(o_ref[...], fin_ref[...])

    pl.when(j == 0)(lambda: step(True, False))
    pl.when(jnp.logical_and(j > 0, j < MLP_STEPS - 1))(lambda: step(False, False))
    pl.when(j == MLP_STEPS - 1)(lambda: step(False, True))


def _mlp(h, x, w1, w2, mods, layer, tile0, n_tiles, final_norm=None):
    tm, th = TOK_TILE, FF_TILE
    final = final_norm is not None
    in_specs = [pl.BlockSpec((tm, D_MODEL), lambda i, j: (tile0 + i, 0)),
                pl.BlockSpec((None, D_MODEL, th), lambda i, j: (layer, 0, j)),
                pl.BlockSpec((None, th, D_MODEL), lambda i, j: (layer, j, 0)),
                pl.BlockSpec((tm, MLP_RES_COLS), lambda i, j: (tile0 + i, j)),
                _mod_spec(layer, 5, tm, tile_of=lambda i, j: tile0 + i)]
    args = [h, w1, w2, x, mods]
    if final:
        in_specs.append(pl.BlockSpec((1, D_MODEL), lambda i, j: (0, 0)))
        args.append(final_norm)
    return pl.pallas_call(
        functools.partial(_mlp_kernel, final=final),
        out_shape=jax.ShapeDtypeStruct((n_tiles * tm, D_MODEL), F32),
        grid=(n_tiles, MLP_STEPS),
        in_specs=in_specs,
        out_specs=pl.BlockSpec((tm, D_MODEL), lambda i, j: (i, 0)),
        scratch_shapes=[pltpu.VMEM((MLP_STEPS, tm, MLP_RES_COLS), F32)],
        compiler_params=_params(("arbitrary", "arbitrary")),
        name="mlp_final" if final else "mlp",
    )(*args)


def _pool_kernel(x_ref, wp_ref, ps_ref, gain1_ref, sh1_ref, sc1_ref, g1_ref, gain2_ref, sh2_ref, sc2_ref,
                 x3_ref, h4_ref):
    i = pl.program_id(0)
    tm = TOK_TILE
    rstd = lax.rsqrt(jnp.mean(x_ref[...] * x_ref[...], axis=-1, keepdims=True) + EPS)
    gs1 = gain1_ref[...] * (1.0 + sc1_ref[...])
    mix_gain = g1_ref[...] * ps_ref[...]

    def widen(m):
        return jnp.concatenate([m] * (POOL_GROUP // HEAD_DIM), axis=1)

    def mix_tile(seq):
        n_seq = tm // seq
        pitch = seq + 2 * POOL_HALO
        n_pad = n_seq * pitch
        halo = jnp.zeros((POOL_HALO, POOL_GROUP), F32)
        pos = lax.broadcasted_iota(jnp.int32, (seq, HEAD_DIM), 0)

        def down(a, k):
            return pltpu.roll(a, k % n_pad, 0)

        sumsq = None
        for g, w in enumerate(POOL_WINDOWS):
            half = w // 2
            cols = slice(g * POOL_GROUP, (g + 1) * POOL_GROUP)
            x = x_ref[:, cols]
            h = x * rstd * gs1[:, cols] + sh1_ref[:, cols]
            padded = jnp.concatenate(
                [piece for s in range(n_seq) for piece in (halo, h[s * seq:(s + 1) * seq], halo)], axis=0)
            back = padded
            m = 1
            while m < half:
                back = back + down(back, m)
                m *= 2
            ahead = back if half == 1 else down(back, -(half - 1))
            total = down(back, 1) + ahead
            total = jnp.concatenate(
                [total[s * pitch + POOL_HALO:s * pitch + POOL_HALO + seq] for s in range(n_seq)], axis=0)
            count = (jnp.minimum(pos + (w - half), seq) - jnp.maximum(pos - half, 0)).astype(F32)
            inv = jnp.concatenate([widen(1.0 / count)] * n_seq, axis=0)
            pooled = (total * inv - h).astype(BF16)
            mix = _bdot(pooled, wp_ref[g].astype(BF16))
            x3 = x + mix_gain[:, cols] * mix
            x3_ref[:, cols] = x3
            sq = x3 * x3
            for k in range(POOL_GROUP // HEAD_DIM):
                part = sq[:, k * HEAD_DIM:(k + 1) * HEAD_DIM]
                sumsq = part if sumsq is None else sumsq + part
        rstd3 = lax.rsqrt(jnp.sum(sumsq, axis=-1, keepdims=True) / D_MODEL + EPS)
        h4_ref[...] = (x3_ref[...] * rstd3 * (gain2_ref[...] * (1.0 + sc2_ref[...])) + sh2_ref[...]).astype(BF16)

    pl.when(i < N_PROMPT_TILES)(lambda: mix_tile(SEQ))
    pl.when(i >= N_PROMPT_TILES)(lambda: mix_tile(DEC_SEQ))


def _pool_mixer(x, w_pool, pool_scale, norm_mix, norm_mlp, mods):
    tm = TOK_TILE
    vec = pl.BlockSpec((1, D_MODEL), lambda i: (0, 0))
    tile = pl.BlockSpec((tm, D_MODEL), lambda i: (i, 0))
    return pl.pallas_call(
        _pool_kernel,
        out_shape=(jax.ShapeDtypeStruct((N_TOK, D_MODEL), F32), jax.ShapeDtypeStruct((N_TOK, D_MODEL), BF16)),
        grid=(N_TILES,),
        in_specs=[tile, pl.BlockSpec((len(POOL_WINDOWS), POOL_GROUP, POOL_GROUP), lambda i: (0, 0, 0)),
                  vec, vec, _mod_spec(1, 0, tm), _mod_spec(1, 1, tm), _mod_spec(1, 2, tm),
                  vec, _mod_spec(1, 3, tm), _mod_spec(1, 4, tm)],
        out_specs=(tile, tile),
        compiler_params=_params(("arbitrary",)),
        name="pool_mixer",
    )(x, w_pool, pool_scale, norm_mix, mods, mods, mods, norm_mlp, mods, mods)


def kernel(x_prompt, x_sample, cache_k, cache_v, state_hgrn_fwd, state_hgrn_bwd, c, c_ctx, w_ada, b_ada,
           norm_mix, norm_mlp, w_in_ab, w_out_ab, q_norm, k_norm, hg_norm, lb_raw, w_pool, pool_scale,
           w_mlp_in, w_mlp_out, final_norm):
    xp = x_prompt.reshape(N_PROMPT, D_MODEL)
    xs = x_sample.reshape(N_SAMPLE, D_MODEL)
    cv = jnp.concatenate([c_ctx[None, :], c, jnp.zeros((ADA_ROWS - 1 - DEC_BATCH, D_MODEL), F32)], axis=0)
    mods, h0 = _ada_table_and_modulate(cv, w_ada, b_ada, xp, xs, norm_mix)

    proj, new_k, new_v = _in_projection(h0, w_in_ab[0], q_norm[0:1], k_norm[0:1])
    att_p = _attention(proj, BATCH, SEQ, 0, SEQ)
    att_s = _attention(proj, DEC_BATCH, DEC_SEQ, N_PROMPT, 512, ctx=(cache_k, cache_v))
    consts = _hgrn_constants()
    hg_p, s_fwd, s_bwd = _hgrn(proj, lb_raw, hg_norm[0:1], consts, BATCH, SEQ, 0, HG_PROMPT_SEQS, HG_PROMPT_HEADS)
    s0 = (state_hgrn_fwd.reshape(DEC_BATCH, HG_HEADS, HG_DK, HG_DV),
          state_hgrn_bwd.reshape(DEC_BATCH, HG_HEADS, HG_DK, HG_DV))
    (hg_s,) = _hgrn(proj, lb_raw, hg_norm[0:1], consts, DEC_BATCH, DEC_SEQ, N_PROMPT, 1, HG_LATENT_HEADS, s0=s0)
    x1, h2 = _out_projection(att_p, att_s, hg_p, hg_s, w_out_ab[0], xp, xs, mods, norm_mlp[0:1])
    x2 = _mlp(h2, x1, w_mlp_in, w_mlp_out, mods, 0, 0, N_TILES)

    x3, h4 = _pool_mixer(x2, w_pool[0], pool_scale[0:1], norm_mix[1:2], norm_mlp[1:2], mods)
    fin = final_norm[None, :]
    y_prompt = _mlp(h4, x3, w_mlp_in, w_mlp_out, mods, 1, 0, N_PROMPT_TILES, final_norm=fin)
    y_sample = _mlp(h4, x3, w_mlp_in, w_mlp_out, mods, 1, N_PROMPT_TILES, N_TILES - N_PROMPT_TILES,
                    final_norm=fin)

    return (y_prompt.reshape(BATCH, SEQ, D_MODEL), y_sample.reshape(DEC_BATCH, DEC_SEQ, D_MODEL),
            new_k, new_v,
            s_fwd.reshape(BATCH, 1, HG_HEADS, HG_DK, HG_DV), s_bwd.reshape(BATCH, 1, HG_HEADS, HG_DK, HG_DV))
```

```python
import functools

import numpy as np
import jax
import jax.numpy as jnp
from jax import lax
from jax.experimental import pallas as pl
from jax.experimental.pallas import tpu as pltpu

F32 = jnp.float32
BF16 = jnp.bfloat16

D_MODEL = 2048
BATCH = 16
SEQ = 256
DEPTH = 2
DEC_BATCH = 2
DEC_SEQ = 1024
PAST_LEN = 256
GRID_W = 64
HEAD_DIM = 128
N_Q_HEADS = 8
N_KV_HEADS = 2
Q_PER_KV = N_Q_HEADS // N_KV_HEADS
ATT_WIDTH = N_Q_HEADS * HEAD_DIM
KV_WIDTH = N_KV_HEADS * HEAD_DIM
HG_HEADS = 8
HG_DK = 128
HG_DV = 128
HG_KW = HG_HEADS * HG_DK
HG_VW = HG_HEADS * HG_DV
IN_AB = ATT_WIDTH + 2 * KV_WIDTH + 3 * HG_KW + 2 * HG_VW
POOL_WINDOWS = (2, 4, 8, 16)
POOL_GROUP = D_MODEL // len(POOL_WINDOWS)
POOL_HALO = 8
D_FF = 4 * D_MODEL
ROPE_THETA = 10000.0
ROPE_HALF = HEAD_DIM // 2
EPS = 1e-6
N_MOD = 6

N_PROMPT = BATCH * SEQ
N_SAMPLE = DEC_BATCH * DEC_SEQ
N_TOK = N_PROMPT + N_SAMPLE
ADA_ROWS = 16
ADA_COL_TILE = 1024
ADA_KEEP = 2 * D_MODEL // ADA_COL_TILE
MOD0_TOK_TILE = 512
TOK_TILE = 1024
N_TILES = N_TOK // TOK_TILE
N_PROMPT_TILES = N_PROMPT // TOK_TILE
PROJ_TOK_TILE = 2048
PROJ_COL_TILE = 512
OUT_COL_TILE = 512
FF_TILE = 512
MLP_STEPS = D_FF // FF_TILE
MLP_RES_COLS = D_MODEL // MLP_STEPS
MLP_OUT_CHUNK = 512
HG_CHUNK = 128
HG_LEVELS = 7
HG_PROMPT_HEADS = 8
HG_PROMPT_SEQS = 1
HG_LATENT_HEADS = 4
HG_SAFE_EXPONENT = 80.0
V7X_VMEM_BYTES = 64 * 2 ** 20
VMEM_LIMIT = 60 * 2 ** 20
assert VMEM_LIMIT < V7X_VMEM_BYTES
assert DEPTH == 2
assert ATT_WIDTH == HG_VW

COL_K = ATT_WIDTH
COL_V = COL_K + KV_WIDTH
COL_HQ = COL_V + KV_WIDTH
COL_ZF = COL_HQ + HG_KW
COL_HI = COL_ZF + 2 * HG_KW
COL_HG = COL_HI + HG_VW
OUT_HQ = ATT_WIDTH
OUT_ZF = OUT_HQ + HG_KW
OUT_HI = OUT_ZF + 2 * HG_KW
OUT_HG = OUT_HI + HG_VW
OUT_K = OUT_HG + HG_VW
OUT_V = OUT_K + KV_WIDTH


def _params(semantics):
    return pltpu.CompilerParams(dimension_semantics=semantics, vmem_limit_bytes=VMEM_LIMIT)


def _sigmoid(x):
    return 1.0 / (1.0 + jnp.exp(-x))


def _silu(x):
    return x * _sigmoid(x)


def _rms(x, gain):
    return x * lax.rsqrt(jnp.mean(x * x, axis=-1, keepdims=True) + EPS) * gain


def _bdot(a, b):
    return jnp.dot(a, b, preferred_element_type=F32)


def _mod_row(tile, tile_rows):
    first = N_PROMPT // tile_rows
    per_seq = DEC_SEQ // tile_rows
    return jnp.where(tile < first, 0, 1 + (tile - first) // per_seq)


def _mod_spec(layer, chunk, tile_rows, width=D_MODEL, col=lambda *g: 0, tile_of=lambda *g: g[0]):
    per = D_MODEL // width
    return pl.BlockSpec((None, None, 1, width),
                        lambda *g: (layer, _mod_row(tile_of(*g), tile_rows), 0, chunk * per + col(*g)))


def _modulate(x, gain, shift, scale):
    return x * lax.rsqrt(jnp.mean(x * x, axis=-1, keepdims=True) + EPS) * (gain * (1.0 + scale)) + shift


def _ada_kernel(cv_ref, w_ref, b_ref, xp_ref, xs_ref, gain_ref, o_ref, h_ref, keep_ref):
    s = pl.program_id(0)
    tile = _bdot(_silu(cv_ref[...]).astype(BF16), w_ref[...].astype(BF16)) + b_ref[...]
    o_ref[:, 0, :] = tile

    @pl.when(s < ADA_KEEP)
    def _():
        keep_ref[s] = tile

    t = s - ADA_KEEP
    n_prompt = N_PROMPT // MOD0_TOK_TILE

    def run(x_ref):
        r = _mod_row(t, MOD0_TOK_TILE)
        per = D_MODEL // ADA_COL_TILE
        shift = jnp.concatenate([keep_ref[k, pl.ds(r, 1), :] for k in range(per)], axis=1)
        scale = jnp.concatenate([keep_ref[per + k, pl.ds(r, 1), :] for k in range(per)], axis=1)
        h_ref[...] = _modulate(x_ref[...], gain_ref[...], shift, scale).astype(BF16)

    pl.when(jnp.logical_and(t >= 0, t < n_prompt))(lambda: run(xp_ref))
    pl.when(jnp.logical_and(t >= n_prompt, t < N_TOK // MOD0_TOK_TILE))(lambda: run(xs_ref))


def _ada_table_and_modulate(cv, w_ada, b_ada, xp, xs, norm_mix):
    tn, tm = ADA_COL_TILE, MOD0_TOK_TILE
    n = N_MOD * D_MODEL
    per_layer = n // tn
    n_prompt = N_PROMPT // tm
    n_tiles = N_TOK // tm
    assert DEPTH * per_layer >= ADA_KEEP + n_tiles

    def tok(s):
        return jnp.clip(s - ADA_KEEP, 0, n_tiles - 1)

    return pl.pallas_call(
        _ada_kernel,
        out_shape=(jax.ShapeDtypeStruct((DEPTH, ADA_ROWS, 1, n), F32), jax.ShapeDtypeStruct((N_TOK, D_MODEL), BF16)),
        grid=(DEPTH * per_layer,),
        in_specs=[pl.BlockSpec((ADA_ROWS, D_MODEL), lambda s: (0, 0)),
                  pl.BlockSpec((None, D_MODEL, tn), lambda s: (s // per_layer, 0, s % per_layer)),
                  pl.BlockSpec((None, 1, tn), lambda s: (s // per_layer, 0, s % per_layer)),
                  pl.BlockSpec((tm, D_MODEL), lambda s: (jnp.minimum(tok(s), n_prompt - 1), 0)),
                  pl.BlockSpec((tm, D_MODEL), lambda s: (jnp.maximum(tok(s) - n_prompt, 0), 0)),
                  pl.BlockSpec((1, D_MODEL), lambda s: (0, 0))],
        out_specs=(pl.BlockSpec((None, ADA_ROWS, 1, tn), lambda s: (s // per_layer, 0, 0, s % per_layer)),
                   pl.BlockSpec((tm, D_MODEL), lambda s: (tok(s), 0))),
        scratch_shapes=[pltpu.VMEM((ADA_KEEP, ADA_ROWS, tn), F32)],
        compiler_params=_params(("arbitrary",)),
        name="ada_table_modulate0",
    )(cv, w_ada, b_ada.reshape(DEPTH, 1, n), xp, xs, norm_mix[0:1])


def _rope(y, cos, sin, perm2):
    hi = y.astype(BF16)
    lo = (y - hi.astype(F32)).astype(BF16)
    rot = _bdot(jnp.concatenate([hi, lo], axis=1), perm2)
    return y * cos + rot * sin


def _inproj_kernel(h_ref, w_ref, qg_ref, kg_ref, cos_ref, sin_ref, perm_ref, p_ref, nk_ref, nv_ref):
    i = pl.program_id(0)
    j = pl.program_id(1)
    latent = i >= N_PROMPT // PROJ_TOK_TILE
    heads = PROJ_COL_TILE // HEAD_DIM
    kv_tile = COL_K // PROJ_COL_TILE
    is_q = j < kv_tile

    def attention_tile(rope):
        gain = jnp.where(is_q, qg_ref[...], kg_ref[...])
        w = w_ref[...].astype(BF16)
        for s in range(PROJ_TOK_TILE // DEC_SEQ):
            rows = slice(s * DEC_SEQ, (s + 1) * DEC_SEQ)
            acc = _bdot(h_ref[rows, :], w)
            for hh in range(heads):
                cols = slice(hh * HEAD_DIM, (hh + 1) * HEAD_DIM)
                x = acc[:, cols]
                y = _rms(x, gain)
                if rope:
                    y = _rope(y, cos_ref[...], sin_ref[...], perm_ref[...])
                if hh >= N_KV_HEADS:
                    y = jnp.where(is_q, y, x)
                p_ref[rows, cols] = y

    pl.when(jnp.logical_and(j <= kv_tile, latent))(lambda: attention_tile(True))
    pl.when(jnp.logical_and(j <= kv_tile, jnp.logical_not(latent)))(lambda: attention_tile(False))

    @pl.when(jnp.logical_and(j == kv_tile, jnp.logical_not(latent)))
    def _():
        for s in range(PROJ_TOK_TILE // SEQ):
            rows = slice(s * SEQ, (s + 1) * SEQ)
            for hh in range(N_KV_HEADS):
                nk_ref[s, :, hh, :] = p_ref[rows, hh * HEAD_DIM:(hh + 1) * HEAD_DIM]
                nv_ref[s, :, hh, :] = p_ref[rows, KV_WIDTH + hh * HEAD_DIM:KV_WIDTH + (hh + 1) * HEAD_DIM]

    def plain_tile(act):
        w = w_ref[...].astype(BF16)
        for s in range(PROJ_TOK_TILE // DEC_SEQ):
            rows = slice(s * DEC_SEQ, (s + 1) * DEC_SEQ)
            p_ref[rows, :] = act(_bdot(h_ref[rows, :], w))

    is_gate = jnp.logical_and(j >= COL_ZF // PROJ_COL_TILE, j < COL_HI // PROJ_COL_TILE)
    is_value = jnp.logical_and(j >= COL_HI // PROJ_COL_TILE, j < COL_HG // PROJ_COL_TILE)
    raw = jnp.logical_or(is_gate, is_value)
    pl.when(raw)(lambda: plain_tile(lambda a: a))
    pl.when(jnp.logical_and(j > kv_tile, jnp.logical_not(raw)))(lambda: plain_tile(_silu))


def _rope_tables():
    t = np.arange(DEC_SEQ)
    row = (t // GRID_W).astype(np.float32)
    col = (t % GRID_W).astype(np.float32)
    inv = (np.float32(ROPE_THETA) ** (-np.arange(0, ROPE_HALF, 2, dtype=np.float32) / np.float32(ROPE_HALF))).astype(np.float32)
    ar = row[:, None] * inv
    ac = col[:, None] * inv
    ang = np.concatenate([ar, ar, ac, ac], axis=-1).astype(np.float32)
    cos = np.cos(ang).astype(np.float32)
    sin = np.sin(ang).astype(np.float32)
    qw = ROPE_HALF // 2
    perm = np.zeros((HEAD_DIM, HEAD_DIM), np.float32)
    for k in range(qw):
        perm[qw + k, k] = -1.0
        perm[k, qw + k] = 1.0
        perm[3 * qw + k, 2 * qw + k] = -1.0
        perm[2 * qw + k, 3 * qw + k] = 1.0
    return jnp.asarray(cos), jnp.asarray(sin), jnp.asarray(np.concatenate([perm, perm], axis=0), BF16)


def _in_projection(h, w_in, q_gain, k_gain):
    tm, tn = PROJ_TOK_TILE, PROJ_COL_TILE
    n_prompt_tiles = N_PROMPT // tm
    cos, sin, perm2 = _rope_tables()
    table = pl.BlockSpec((DEC_SEQ, HEAD_DIM), lambda i, j: (0, 0))
    gain = pl.BlockSpec((1, HEAD_DIM), lambda i, j: (0, 0))
    state = pl.BlockSpec((tm // SEQ, None, SEQ, N_KV_HEADS, HEAD_DIM),
                         lambda i, j: (jnp.minimum(i, n_prompt_tiles - 1), 0, 0, 0, 0),
                         pipeline_mode=pl.Buffered(1))
    state_shape = jax.ShapeDtypeStruct((BATCH, 1, SEQ, N_KV_HEADS, HEAD_DIM), F32)
    kv_tile = COL_K // tn

    def out_tile(j):
        return jnp.where(j < kv_tile, j, jnp.where(j == kv_tile, OUT_K // tn, j - 1))

    return pl.pallas_call(
        _inproj_kernel,
        out_shape=(jax.ShapeDtypeStruct((N_TOK, IN_AB), F32), state_shape, state_shape),
        grid=(N_TOK // tm, IN_AB // tn),
        in_specs=[pl.BlockSpec((tm, D_MODEL), lambda i, j: (i, 0)),
                  pl.BlockSpec((D_MODEL, tn), lambda i, j: (0, j)),
                  gain, gain, table, table, pl.BlockSpec((2 * HEAD_DIM, HEAD_DIM), lambda i, j: (0, 0))],
        out_specs=(pl.BlockSpec((tm, tn), lambda i, j: (i, out_tile(j))), state, state),
        compiler_params=_params(("arbitrary", "arbitrary")),
        name="in_projection",
    )(h, w_in, q_gain, k_gain, cos, sin, perm2)


def _attn_kernel(*refs, has_ctx, stack):
    if has_ctx:
        q_ref, k_ref, v_ref, ck_ref, cv_ref, o_ref = refs
    else:
        q_ref, k_ref, v_ref, o_ref = refs
    scale = HEAD_DIM ** -0.5
    nt = (((1,), (1,)), ((), ()))
    tq = q_ref.shape[0]
    for hk in range(k_ref.shape[1] // HEAD_DIM):
        kcols = slice(hk * HEAD_DIM, (hk + 1) * HEAD_DIM)
        k = k_ref[:, kcols].astype(BF16)
        v = v_ref[:, kcols].astype(BF16)
        if has_ctx:
            ck = ck_ref[:, hk, :].astype(BF16)
            cv = cv_ref[:, hk, :].astype(BF16)
        for g0 in range(hk * Q_PER_KV, (hk + 1) * Q_PER_KV, stack):
            q = jnp.concatenate([q_ref[:, g * HEAD_DIM:(g + 1) * HEAD_DIM] for g in range(g0, g0 + stack)], axis=0)
            q = (q * scale).astype(BF16)
            s = lax.dot_general(q, k, nt, preferred_element_type=F32)
            m = jnp.max(s, axis=-1, keepdims=True)
            if has_ctx:
                sc = lax.dot_general(q, ck, nt, preferred_element_type=F32)
                m = jnp.maximum(m, jnp.max(sc, axis=-1, keepdims=True))
            p = jnp.exp(s - m)
            den = jnp.sum(p, axis=-1, keepdims=True)
            o = _bdot(p.astype(BF16), v)
            if has_ctx:
                pc = jnp.exp(sc - m)
                den = den + jnp.sum(pc, axis=-1, keepdims=True)
                o = o + _bdot(pc.astype(BF16), cv)
            o = (o / den).astype(o_ref.dtype)
            for g in range(stack):
                o_ref[:, (g0 + g) * HEAD_DIM:(g0 + g + 1) * HEAD_DIM] = o[g * tq:(g + 1) * tq]


def _attention(p, n_batch, seq, row0, tq, ctx=None):
    q_blocks = seq // tq
    kvh = N_KV_HEADS
    gw = kvh * Q_PER_KV * HEAD_DIM
    kw = kvh * HEAD_DIM
    in_specs = [
        pl.BlockSpec((tq, gw), lambda b, h, qi: (row0 // tq + b * q_blocks + qi, h)),
        pl.BlockSpec((seq, kw), lambda b, h, qi: (row0 // seq + b, OUT_K // kw + h)),
        pl.BlockSpec((seq, kw), lambda b, h, qi: (row0 // seq + b, OUT_V // kw + h)),
    ]
    args = [p, p, p]
    if ctx is not None:
        assert kvh == N_KV_HEADS
        ctx_spec = pl.BlockSpec((None, None, PAST_LEN, N_KV_HEADS, HEAD_DIM), lambda b, h, qi: (b, 0, 0, 0, 0))
        in_specs += [ctx_spec, ctx_spec]
        args += [ctx[0], ctx[1]]
    return pl.pallas_call(
        functools.partial(_attn_kernel, has_ctx=ctx is not None, stack=Q_PER_KV if ctx is None else 1),
        out_shape=jax.ShapeDtypeStruct((n_batch * seq, ATT_WIDTH), BF16),
        grid=(n_batch, N_KV_HEADS // kvh, q_blocks),
        in_specs=in_specs,
        out_specs=pl.BlockSpec((tq, gw), lambda b, h, qi: (b * q_blocks + qi, h)),
        compiler_params=_params(("arbitrary", "arbitrary", "arbitrary")),
        name="attention_latent" if ctx is not None else "attention_prompt",
    )(*args)


def _hgrn_constants():
    c = HG_CHUNK
    t = np.arange(c)
    cums, sels, pairs, scans, diags = [], [], [], [], []
    for d in range(2):
        pos = t if d == 0 else c - 1 - t
        pu, pt = pos[None, :], pos[:, None]
        ms, ss, ws = [], [], []
        for l in range(HG_LEVELS):
            m = c >> l
            blk = pos // m
            mid = (blk * m + m // 2)[:, None]
            late = ((pos % m) >= m // 2)
            ms.append(np.where(late[:, None], (pu >= mid) & (pu <= pt), (pu > pt) & (pu < mid)))
            ss.append(np.broadcast_to(late[:, None], (c, c)))
            ws.append((blk[:, None] == blk[None, :]) & late[:, None] & ~late[None, :])
        ms.append(pu <= pt)
        ms.append(pu > pt)
        cums.append(np.concatenate(ms, axis=0))
        sels.append(np.stack(ss))
        pairs.append(np.stack(ws))
        scans.append(np.concatenate([pu <= pt, pu <= pt], axis=1))
        diags.append(((pos // (c // 2))[:, None] == (pos // (c // 2))[None, :]) & (pu <= pt))
    return (jnp.asarray(np.stack(cums), BF16), jnp.asarray(np.stack(sels), F32),
            jnp.asarray(np.stack(pairs), F32), jnp.asarray(np.stack(scans), BF16),
            jnp.asarray(np.stack(diags), F32))


def _hgrn_kernel(*refs, n_seqs, n_chunks, has_s0, split_dirs):
    (hq_ref, zf_ref, zb_ref, hi_ref, hg_ref, lb_ref, og_ref, cum_ref, sel_ref, pair_ref, scan_ref,
     diag_ref) = refs[:12]
    refs = refs[12:]
    if has_s0:
        s0f_ref, s0b_ref, o_ref = refs[:3]
        refs = refs[3:]
    else:
        o_ref, sf_ref, sb_ref = refs[:3]
        refs = refs[3:]
    st_ref, acc_ref = refs
    hb = st_ref.shape[0]
    c = HG_CHUNK
    half = c // 2
    nt = (((1,), (1,)), ((), ()))
    tn = (((0,), (0,)), ((), ()))

    def direction(d, z_ref):
        raw = lb_ref[d]
        e = jnp.exp(raw - jnp.max(raw, axis=0, keepdims=True))
        lb = e[0:1] / jnp.sum(e, axis=0, keepdims=True)

        def initial_state(s, hh):
            if has_s0:
                return (s0f_ref, s0b_ref)[d][s, hh].T
            return jnp.zeros((HG_DV, HG_DK), F32)

        def gates(rows):
            f = lb + (1.0 - lb) * _sigmoid(z_ref[rows, :])
            logf = jnp.log(f)
            hi16 = logf.astype(BF16)
            lo16 = (logf - hi16.astype(F32)).astype(BF16)
            return f, hi16, lo16

        def row(p):
            t = p if d == 0 else c - 1 - p
            return slice(t, t + 1)

        early, late = (slice(0, half), slice(half, c)) if d == 0 else (slice(half, c), slice(0, half))

        def in_row_order(x_early, x_late):
            return jnp.concatenate([x_early, x_late] if d == 0 else [x_late, x_early], axis=0)

        def emit(rows, cols, o):
            if d == 0:
                acc_ref[rows, cols] = o
            else:
                tot = acc_ref[rows, cols] + o
                o_ref[rows, cols] = (_rms(tot, og_ref[...]) * hg_ref[rows, cols]).astype(o_ref.dtype)

        def two_level_operands(rows):
            f, hi16, lo16 = gates(rows)
            b = _bdot(scan_ref[d], jnp.concatenate([hi16, lo16], axis=0))
            kk = 1.0 - f
            q = hq_ref[rows, :]
            r_mid = b[row(half - 1)]
            x1 = in_row_order(kk[early] * jnp.exp(r_mid - b[early]), q[late] * jnp.exp(b[late] - r_mid)).astype(BF16)
            dq = in_row_order(b[early] - b[row(half // 2 - 1)], b[late] - b[row(half + half // 2 - 1)])
            b_end = b[row(c - 1)]
            span = jnp.maximum(
                jnp.maximum(b[row(0)] - b[row(half // 2 - 1)], b[row(half // 2 - 1)] - b[row(half - 1)]),
                jnp.maximum(b[row(half)] - b[row(half + half // 2 - 1)],
                            b[row(half + half // 2 - 1)] - b[row(c - 1)]))
            return dict(x1=x1, xq=(q * jnp.exp(dq)).astype(BF16), xk=(kk * jnp.exp(-dq)).astype(BF16),
                        q_in=(q * jnp.exp(b)).astype(BF16), k_out=(kk * jnp.exp(b_end - b)).astype(BF16),
                        a_end=jnp.exp(b_end), iv=hi_ref[rows, :].astype(BF16), span=span)

        def two_level_chunk(rows, ops, states):
            new_states = []
            for hh in range(hb):
                cols = slice(hh * HG_DK, (hh + 1) * HG_DK)
                g1 = _bdot(ops["x1"][:, cols], ops["x1"][:, cols].astype(F32).T.astype(BF16))
                g2 = _bdot(ops["xq"][:, cols], ops["xk"][:, cols].astype(F32).T.astype(BF16))
                att = jnp.where(pair_ref[d, 0] > 0.5, g1, jnp.where(diag_ref[d] > 0.5, g2, 0.0)).astype(BF16)
                st = states[hh]
                o = (_bdot(att, ops["iv"][:, cols])
                     + lax.dot_general(ops["q_in"][:, cols], st.astype(BF16), nt, preferred_element_type=F32))
                pending.append((rows, cols, o))
                dst = lax.dot_general(ops["iv"][:, cols], ops["k_out"][:, cols], tn, preferred_element_type=F32)
                new_states.append(ops["a_end"][:, cols] * st + dst)
            return new_states

        def all_levels(rows):
            f, hi16, lo16 = gates(rows)
            cum = cum_ref[d]
            eall = jnp.exp(_bdot(cum, hi16) + _bdot(cum, lo16))
            for hh in range(hb):
                cols = slice(hh * HG_DK, (hh + 1) * HG_DK)
                kk = 1.0 - f[:, cols]
                q = hq_ref[rows, cols]
                iv = hi_ref[rows, cols]
                iv16 = iv.astype(BF16)
                att = jnp.zeros((c, c), F32)
                for l in range(HG_LEVELS):
                    x = (kk + sel_ref[d, l] * (q - kk)) * eall[l * c:(l + 1) * c, cols]
                    xb = x.astype(BF16)
                    att = att + pair_ref[d, l] * lax.dot_general(xb, xb, nt, preferred_element_type=F32)
                e_in = eall[HG_LEVELS * c:(HG_LEVELS + 1) * c, cols]
                e_out = eall[(HG_LEVELS + 1) * c:, cols]
                st = st_ref[hh]
                o = (_bdot(att.astype(BF16), iv16)
                     + jnp.sum(q * kk, axis=-1, keepdims=True) * iv
                     + lax.dot_general((q * e_in).astype(BF16), st.astype(BF16), nt, preferred_element_type=F32))
                dst = lax.dot_general(iv16, (kk * e_out).astype(BF16), tn, preferred_element_type=F32)
                st_ref[hh] = (e_in[0:1] * e_out[0:1]) * st + dst
                emit(rows, cols, o)

        def chunk_start(s, ci):
            cidx = ci if d == 0 else n_chunks - 1 - ci
            return (s * n_chunks + cidx) * c

        worst = jnp.zeros((1, hb * HG_DK), F32)
        pending = []
        states = [[initial_state(s, hh) for hh in range(hb)] for s in range(n_seqs)]
        for ci in range(n_chunks):
            for s in range(n_seqs):
                rows = slice(chunk_start(s, ci), chunk_start(s, ci) + c)
                ops = two_level_operands(rows)
                states[s] = two_level_chunk(rows, ops, states[s])
                worst = jnp.maximum(worst, ops["span"])
        for item in pending:
            emit(*item)
        if not has_s0:
            for s in range(n_seqs):
                for hh in range(hb):
                    (sf_ref, sb_ref)[d][s, hh] = states[s][hh].T

        @pl.when(jnp.logical_not(jnp.max(worst) <= HG_SAFE_EXPONENT))
        def _():
            for s in range(n_seqs):
                for hh in range(hb):
                    st_ref[hh] = initial_state(s, hh)

                def chunk(ci, carry):
                    all_levels(pl.ds(pl.multiple_of(chunk_start(s, ci), c), c))
                    return carry

                lax.fori_loop(0, n_chunks, chunk, 0)
                if not has_s0:
                    for hh in range(hb):
                        (sf_ref, sb_ref)[d][s, hh] = st_ref[hh].T

    if split_dirs:
        pl.when(pl.program_id(2) == 0)(lambda: direction(0, zf_ref))
        pl.when(pl.program_id(2) == 1)(lambda: direction(1, zb_ref))
    else:
        direction(0, zf_ref)
        direction(1, zb_ref)


def _hgrn(p, lb_raw, o_gain, consts, n_batch, seq, row0, n_seqs, hb, s0=None, split_dirs=False):
    w = hb * HG_DK
    rows = n_seqs * seq

    def seg(col):
        return pl.BlockSpec((rows, w), lambda b, h, *_: (row0 // rows + b, col // w + h))

    def const(a):
        return pl.BlockSpec(a.shape, lambda b, h, *_: (0,) * a.ndim)

    in_specs = [seg(OUT_HQ), seg(OUT_ZF), seg(OUT_ZF + HG_KW), seg(OUT_HI), seg(OUT_HG),
                pl.BlockSpec((2, DEPTH + 1, w), lambda b, h, *_: (0, 0, h)),
                pl.BlockSpec((1, HG_DV), lambda b, h, *_: (0, 0))] + [const(a) for a in consts]
    args = [p, p, p, p, p, lb_raw, o_gain, *consts]
    has_s0 = s0 is not None
    st_spec = pl.BlockSpec((n_seqs, hb, HG_DK, HG_DV), lambda b, h, *_: (b, h, 0, 0))
    if has_s0:
        in_specs += [st_spec, st_spec]
        args += [s0[0], s0[1]]
    out_shape = [jax.ShapeDtypeStruct((n_batch * seq, HG_VW), BF16)]
    out_specs = [pl.BlockSpec((rows, w), lambda b, h, *_: (b, h))]
    if not has_s0:
        st_shape = jax.ShapeDtypeStruct((n_batch, HG_HEADS, HG_DK, HG_DV), F32)
        out_shape += [st_shape, st_shape]
        out_specs += [st_spec, st_spec]
    return pl.pallas_call(
        functools.partial(_hgrn_kernel, n_seqs=n_seqs, n_chunks=seq // HG_CHUNK, has_s0=has_s0,
                          split_dirs=split_dirs),
        out_shape=tuple(out_shape),
        grid=(n_batch // n_seqs, HG_HEADS // hb) + ((2,) if split_dirs else ()),
        in_specs=in_specs,
        out_specs=tuple(out_specs),
        scratch_shapes=[pltpu.VMEM((hb, HG_DV, HG_DK), F32), pltpu.VMEM((rows, w), F32)],
        compiler_params=_params(("arbitrary",) * (3 if split_dirs else 2)),
        name="hgrn_latent" if has_s0 else "hgrn_prompt",
    )(*args)


def _outproj_kernel(attp_ref, atts_ref, hgp_ref, hgs_ref, wa_ref, wb_ref, xp_ref, xs_ref, g1_ref, gain_ref,
                    sh_ref, sc_ref, x1_ref, h2_ref, full_ref, wcache_ref, ss_ref):
    i = pl.program_id(0)
    n = pl.program_id(1)
    nt = D_MODEL // OUT_COL_TILE

    @pl.when(i == 0)
    def _():
        wcache_ref[n, 0] = wa_ref[...].astype(BF16)
        wcache_ref[n, 1] = wb_ref[...].astype(BF16)

    @pl.when(n == 0)
    def _():
        ss_ref[...] = jnp.zeros(ss_ref.shape, F32)

    def run(att_ref, hg_ref, x_ref, last):
        acc = _bdot(att_ref[...], wcache_ref[n, 0]) + _bdot(hg_ref[...], wcache_ref[n, 1])
        x1 = x_ref[...] + g1_ref[...] * acc
        x1_ref[...] = x1
        sq = x1 * x1
        part = sq[:, 0:HEAD_DIM]
        for k in range(1, OUT_COL_TILE // HEAD_DIM):
            part = part + sq[:, k * HEAD_DIM:(k + 1) * HEAD_DIM]
        if not last:
            full_ref[n] = x1
            ss_ref[...] += part
            return
        rstd = lax.rsqrt(jnp.sum(ss_ref[...] + part, axis=-1, keepdims=True) / D_MODEL + EPS)
        gs = gain_ref[...] * (1.0 + sc_ref[...])
        for k in range(nt):
            cols = slice(k * OUT_COL_TILE, (k + 1) * OUT_COL_TILE)
            xk = x1 if k == nt - 1 else full_ref[k]
            h2_ref[:, cols] = (xk * rstd * gs[:, cols] + sh_ref[:, cols]).astype(BF16)

    prompt = i < N_PROMPT_TILES
    for last in (False, True):
        step = (n == nt - 1) if last else (n < nt - 1)
        pl.when(jnp.logical_and(prompt, step))(functools.partial(run, attp_ref, hgp_ref, xp_ref, last))
        pl.when(jnp.logical_and(jnp.logical_not(prompt), step))(functools.partial(run, atts_ref, hgs_ref, xs_ref, last))


def _out_projection(att_p, att_s, hg_p, hg_s, w_out, xp, xs, mods, norm_mlp):
    tm, tn = TOK_TILE, OUT_COL_TILE
    nt = D_MODEL // tn

    def prompt_rows(width):
        return pl.BlockSpec((tm, width), lambda i, n: (jnp.minimum(i, N_PROMPT_TILES - 1), 0))

    def sample_rows(width):
        return pl.BlockSpec((tm, width), lambda i, n: (jnp.maximum(i - N_PROMPT_TILES, 0), 0))

    return pl.pallas_call(
        _outproj_kernel,
        out_shape=(jax.ShapeDtypeStruct((N_TOK, D_MODEL), F32), jax.ShapeDtypeStruct((N_TOK, D_MODEL), BF16)),
        grid=(N_TILES, nt),
        in_specs=[prompt_rows(ATT_WIDTH), sample_rows(ATT_WIDTH), prompt_rows(HG_VW), sample_rows(HG_VW),
                  pl.BlockSpec((ATT_WIDTH, tn), lambda i, n: (0, jnp.where(i == 0, n, nt - 1))),
                  pl.BlockSpec((HG_VW, tn), lambda i, n: (1, jnp.where(i == 0, n, nt - 1))),
                  pl.BlockSpec((tm, tn), lambda i, n: (jnp.minimum(i, N_PROMPT_TILES - 1),
                                                       jnp.where(i < N_PROMPT_TILES, n, nt - 1))),
                  pl.BlockSpec((tm, tn), lambda i, n: (jnp.maximum(i - N_PROMPT_TILES, 0),
                                                       jnp.where(i < N_PROMPT_TILES, 0, n))),
                  _mod_spec(0, 2, tm, width=tn, col=lambda i, n: n),
                  pl.BlockSpec((1, D_MODEL), lambda i, n: (0, 0)),
                  _mod_spec(0, 3, tm), _mod_spec(0, 4, tm)],
        out_specs=(pl.BlockSpec((tm, tn), lambda i, n: (i, n)),
                   pl.BlockSpec((tm, D_MODEL), lambda i, n: (i, 0))),
        scratch_shapes=[pltpu.VMEM((nt - 1, tm, tn), F32), pltpu.VMEM((nt, 2, ATT_WIDTH, tn), BF16),
                        pltpu.VMEM((tm, HEAD_DIM), F32)],
        compiler_params=_params(("arbitrary", "arbitrary")),
        name="out_projection",
    )(att_p, att_s, hg_p, hg_s, w_out, w_out, xp, xs, mods, norm_mlp, mods, mods)


def _mlp_kernel(h_ref, w1_ref, w2_ref, x_ref, g2_ref, *rest, final):
    if final:
        fin_ref, o_ref, res_ref = rest
    else:
        o_ref, res_ref = rest
    j = pl.program_id(1)
    res_ref[j] = x_ref[...]

    def step(first, last):
        a = jnp.square(jnp.maximum(_bdot(h_ref[...], w1_ref[...].astype(BF16)), 0.0)).astype(BF16)
        for n in range(D_MODEL // MLP_OUT_CHUNK):
            cols = slice(n * MLP_OUT_CHUNK, (n + 1) * MLP_OUT_CHUNK)
            p = _bdot(a, w2_ref[:, cols].astype(BF16))
            if not first:
                p = o_ref[:, cols] + p
            if last:
                per = MLP_OUT_CHUNK // MLP_RES_COLS
                res = jnp.concatenate([res_ref[n * per + k] for k in range(per)], axis=1)
                p = res + g2_ref[:, cols] * p
            o_ref[:, cols] = p
        if last and final:
            o_ref[...] = _rms(o_ref[...], fin_ref[...])

    pl.when(j == 0)(lambda: step(True, False))
    pl.when(jnp.logical_and(j > 0, j < MLP_STEPS - 1))(lambda: step(False, False))
    pl.when(j == MLP_STEPS - 1)(lambda: step(False, True))


def _mlp(h, x, w1, w2, mods, layer, tile0, n_tiles, final_norm=None):
    tm, th = TOK_TILE, FF_TILE
    final = final_norm is not None
    in_specs = [pl.BlockSpec((tm, D_MODEL), lambda i, j: (tile0 + i, 0)),
                pl.BlockSpec((None, D_MODEL, th), lambda i, j: (layer, 0, j)),
                pl.BlockSpec((None, th, D_MODEL), lambda i, j: (layer, j, 0)),
                pl.BlockSpec((tm, MLP_RES_COLS), lambda i, j: (tile0 + i, j)),
                _mod_spec(layer, 5, tm, tile_of=lambda i, j: tile0 + i)]
    args = [h, w1, w2, x, mods]
    if final:
        in_specs.append(pl.BlockSpec((1, D_MODEL), lambda i, j: (0, 0)))
        args.append(final_norm)
    return pl.pallas_call(
        functools.partial(_mlp_kernel, final=final),
        out_shape=jax.ShapeDtypeStruct((n_tiles * tm, D_MODEL), F32),
        grid=(n_tiles, MLP_STEPS),
        in_specs=in_specs,
        out_specs=pl.BlockSpec((tm, D_MODEL), lambda i, j: (i, 0)),
        scratch_shapes=[pltpu.VMEM((MLP_STEPS, tm, MLP_RES_COLS), F32)],
        compiler_params=_params(("arbitrary", "arbitrary")),
        name="mlp_final" if final else "mlp",
    )(*args)


def _pool_kernel(x_ref, wp_ref, ps_ref, gain1_ref, sh1_ref, sc1_ref, g1_ref, gain2_ref, sh2_ref, sc2_ref,
                 x3_ref, h4_ref):
    i = pl.program_id(0)
    tm = TOK_TILE
    rstd = lax.rsqrt(jnp.mean(x_ref[...] * x_ref[...], axis=-1, keepdims=True) + EPS)
    gs1 = gain1_ref[...] * (1.0 + sc1_ref[...])
    mix_gain = g1_ref[...] * ps_ref[...]

    def widen(m):
        return jnp.concatenate([m] * (POOL_GROUP // HEAD_DIM), axis=1)

    def mix_tile(seq):
        n_seq = tm // seq
        pitch = seq + 2 * POOL_HALO
        n_pad = n_seq * pitch
        halo = jnp.zeros((POOL_HALO, POOL_GROUP), F32)
        pos = lax.broadcasted_iota(jnp.int32, (seq, HEAD_DIM), 0)

        def down(a, k):
            return pltpu.roll(a, k % n_pad, 0)

        sumsq = None
        for g, w in enumerate(POOL_WINDOWS):
            half = w // 2
            cols = slice(g * POOL_GROUP, (g + 1) * POOL_GROUP)
            x = x_ref[:, cols]
            h = x * rstd * gs1[:, cols] + sh1_ref[:, cols]
            padded = jnp.concatenate(
                [piece for s in range(n_seq) for piece in (halo, h[s * seq:(s + 1) * seq], halo)], axis=0)
            back = padded
            m = 1
            while m < half:
                back = back + down(back, m)
                m *= 2
            ahead = back if half == 1 else down(back, -(half - 1))
            total = down(back, 1) + ahead
            total = jnp.concatenate(
                [total[s * pitch + POOL_HALO:s * pitch + POOL_HALO + seq] for s in range(n_seq)], axis=0)
            count = (jnp.minimum(pos + (w - half), seq) - jnp.maximum(pos - half, 0)).astype(F32)
            inv = jnp.concatenate([widen(1.0 / count)] * n_seq, axis=0)
            pooled = (total * inv - h).astype(BF16)
            mix = _bdot(pooled, wp_ref[g].astype(BF16))
            x3 = x + mix_gain[:, cols] * mix
            x3_ref[:, cols] = x3
            sq = x3 * x3
            for k in range(POOL_GROUP // HEAD_DIM):
                part = sq[:, k * HEAD_DIM:(k + 1) * HEAD_DIM]
                sumsq = part if sumsq is None else sumsq + part
        rstd3 = lax.rsqrt(jnp.sum(sumsq, axis=-1, keepdims=True) / D_MODEL + EPS)
        h4_ref[...] = (x3_ref[...] * rstd3 * (gain2_ref[...] * (1.0 + sc2_ref[...])) + sh2_ref[...]).astype(BF16)

    pl.when(i < N_PROMPT_TILES)(lambda: mix_tile(SEQ))
    pl.when(i >= N_PROMPT_TILES)(lambda: mix_tile(DEC_SEQ))


def _pool_mixer(x, w_pool, pool_scale, norm_mix, norm_mlp, mods):
    tm = TOK_TILE
    vec = pl.BlockSpec((1, D_MODEL), lambda i: (0, 0))
    tile = pl.BlockSpec((tm, D_MODEL), lambda i: (i, 0))
    return pl.pallas_call(
        _pool_kernel,
        out_shape=(jax.ShapeDtypeStruct((N_TOK, D_MODEL), F32), jax.ShapeDtypeStruct((N_TOK, D_MODEL), BF16)),
        grid=(N_TILES,),
        in_specs=[tile, pl.BlockSpec((len(POOL_WINDOWS), POOL_GROUP, POOL_GROUP), lambda i: (0, 0, 0)),
                  vec, vec, _mod_spec(1, 0, tm), _mod_spec(1, 1, tm), _mod_spec(1, 2, tm),
                  vec, _mod_spec(1, 3, tm), _mod_spec(1, 4, tm)],
        out_specs=(tile, tile),
        compiler_params=_params(("arbitrary",)),
        name="pool_mixer",
    )(x, w_pool, pool_scale, norm_mix, mods, mods, mods, norm_mlp, mods, mods)


def kernel(x_prompt, x_sample, cache_k, cache_v, state_hgrn_fwd, state_hgrn_bwd, c, c_ctx, w_ada, b_ada,
           norm_mix, norm_mlp, w_in_ab, w_out_ab, q_norm, k_norm, hg_norm, lb_raw, w_pool, pool_scale,
           w_mlp_in, w_mlp_out, final_norm):
    xp = x_prompt.reshape(N_PROMPT, D_MODEL)
    xs = x_sample.reshape(N_SAMPLE, D_MODEL)
    cv = jnp.concatenate([c_ctx[None, :], c, jnp.zeros((ADA_ROWS - 1 - DEC_BATCH, D_MODEL), F32)], axis=0)
    mods, h0 = _ada_table_and_modulate(cv, w_ada, b_ada, xp, xs, norm_mix)

    proj, new_k, new_v = _in_projection(h0, w_in_ab[0], q_norm[0:1], k_norm[0:1])
    att_p = _attention(proj, BATCH, SEQ, 0, SEQ)
    att_s = _attention(proj, DEC_BATCH, DEC_SEQ, N_PROMPT, 512, ctx=(cache_k, cache_v))
    consts = _hgrn_constants()
    hg_p, s_fwd, s_bwd = _hgrn(proj, lb_raw, hg_norm[0:1], consts, BATCH, SEQ, 0, HG_PROMPT_SEQS, HG_PROMPT_HEADS)
    s0 = (state_hgrn_fwd.reshape(DEC_BATCH, HG_HEADS, HG_DK, HG_DV),
          state_hgrn_bwd.reshape(DEC_BATCH, HG_HEADS, HG_DK, HG_DV))
    (hg_s,) = _hgrn(proj, lb_raw, hg_norm[0:1], consts, DEC_BATCH, DEC_SEQ, N_PROMPT, 1, HG_LATENT_HEADS, s0=s0,
                    split_dirs=True)
    x1, h2 = _out_projection(att_p, att_s, hg_p, hg_s, w_out_ab[0], xp, xs, mods, norm_mlp[0:1])
    x2 = _mlp(h2, x1, w_mlp_in, w_mlp_out, mods, 0, 0, N_TILES)

    x3, h4 = _pool_mixer(x2, w_pool[0], pool_scale[0:1], norm_mix[1:2], norm_mlp[1:2], mods)
    fin = final_norm[None, :]
    y_prompt = _mlp(h4, x3, w_mlp_in, w_mlp_out, mods, 1, 0, N_PROMPT_TILES, final_norm=fin)
    y_sample = _mlp(h4, x3, w_mlp_in, w_mlp_out, mods, 1, N_PROMPT_TILES, N_TILES - N_PROMPT_TILES,
                    final_norm=fin)

    return (y_prompt.reshape(BATCH, SEQ, D_MODEL), y_sample.reshape(DEC_BATCH, DEC_SEQ, D_MODEL),
            new_k, new_v,
            s_fwd.reshape(BATCH, 1, HG_HEADS, HG_DK, HG_DV), s_bwd.reshape(BATCH, 1, HG_HEADS, HG_DK, HG_DV))
```

```python
import functools

import numpy as np
import jax
import jax.numpy as jnp
from jax import lax
from jax.experimental import pallas as pl
from jax.experimental.pallas import tpu as pltpu

F32 = jnp.float32
BF16 = jnp.bfloat16

D_MODEL = 2048
BATCH = 16
SEQ = 256
DEPTH = 2
DEC_BATCH = 2
DEC_SEQ = 1024
PAST_LEN = 256
GRID_W = 64
HEAD_DIM = 128
N_Q_HEADS = 8
N_KV_HEADS = 2
Q_PER_KV = N_Q_HEADS // N_KV_HEADS
ATT_WIDTH = N_Q_HEADS * HEAD_DIM
KV_WIDTH = N_KV_HEADS * HEAD_DIM
HG_HEADS = 8
HG_DK = 128
HG_DV = 128
HG_KW = HG_HEADS * HG_DK
HG_VW = HG_HEADS * HG_DV
IN_AB = ATT_WIDTH + 2 * KV_WIDTH + 3 * HG_KW + 2 * HG_VW
POOL_WINDOWS = (2, 4, 8, 16)
POOL_GROUP = D_MODEL // len(POOL_WINDOWS)
POOL_HALO = 8
D_FF = 4 * D_MODEL
ROPE_THETA = 10000.0
ROPE_HALF = HEAD_DIM // 2
EPS = 1e-6
N_MOD = 6

N_PROMPT = BATCH * SEQ
N_SAMPLE = DEC_BATCH * DEC_SEQ
N_TOK = N_PROMPT + N_SAMPLE
ADA_ROWS = 16
ADA_COL_TILE = 1024
ADA_KEEP = 2 * D_MODEL // ADA_COL_TILE
MOD0_TOK_TILE = 512
TOK_TILE = 1024
N_TILES = N_TOK // TOK_TILE
N_PROMPT_TILES = N_PROMPT // TOK_TILE
PROJ_TOK_TILE = 2048
PROJ_COL_TILE = 512
OUT_COL_TILE = 512
FF_TILE = 512
MLP_STEPS = D_FF // FF_TILE
MLP_RES_COLS = D_MODEL // MLP_STEPS
MLP_OUT_CHUNK = 512
HG_CHUNK = 128
HG_LEVELS = 7
HG_PROMPT_HEADS = 8
HG_PROMPT_SEQS = 1
HG_LATENT_HEADS = 4
HG_SAFE_EXPONENT = 80.0
V7X_VMEM_BYTES = 64 * 2 ** 20
VMEM_LIMIT = 60 * 2 ** 20
assert VMEM_LIMIT < V7X_VMEM_BYTES
assert DEPTH == 2
assert ATT_WIDTH == HG_VW

COL_K = ATT_WIDTH
COL_V = COL_K + KV_WIDTH
COL_HQ = COL_V + KV_WIDTH
COL_ZF = COL_HQ + HG_KW
COL_HI = COL_ZF + 2 * HG_KW
COL_HG = COL_HI + HG_VW
OUT_HQ = ATT_WIDTH
OUT_ZF = OUT_HQ + HG_KW
OUT_HI = OUT_ZF + 2 * HG_KW
OUT_HG = OUT_HI + HG_VW
OUT_K = OUT_HG + HG_VW
OUT_V = OUT_K + KV_WIDTH


def _params(semantics):
    return pltpu.CompilerParams(dimension_semantics=semantics, vmem_limit_bytes=VMEM_LIMIT)


def _sigmoid(x):
    return 1.0 / (1.0 + jnp.exp(-x))


def _silu(x):
    return x * _sigmoid(x)


def _rms(x, gain):
    return x * lax.rsqrt(jnp.mean(x * x, axis=-1, keepdims=True) + EPS) * gain


def _bdot(a, b):
    return jnp.dot(a, b, preferred_element_type=F32)


def _mod_row(tile, tile_rows):
    first = N_PROMPT // tile_rows
    per_seq = DEC_SEQ // tile_rows
    return jnp.where(tile < first, 0, 1 + (tile - first) // per_seq)


def _mod_spec(layer, chunk, tile_rows, width=D_MODEL, col=lambda *g: 0, tile_of=lambda *g: g[0]):
    per = D_MODEL // width
    return pl.BlockSpec((None, None, 1, width),
                        lambda *g: (layer, _mod_row(tile_of(*g), tile_rows), 0, chunk * per + col(*g)))


def _modulate(x, gain, shift, scale):
    return x * lax.rsqrt(jnp.mean(x * x, axis=-1, keepdims=True) + EPS) * (gain * (1.0 + scale)) + shift


def _ada_kernel(cv_ref, w_ref, b_ref, xp_ref, xs_ref, gain_ref, o_ref, h_ref, keep_ref):
    s = pl.program_id(0)
    tile = _bdot(_silu(cv_ref[...]).astype(BF16), w_ref[...].astype(BF16)) + b_ref[...]
    o_ref[:, 0, :] = tile

    @pl.when(s < ADA_KEEP)
    def _():
        keep_ref[s] = tile

    t = s - ADA_KEEP
    n_prompt = N_PROMPT // MOD0_TOK_TILE

    def run(x_ref):
        r = _mod_row(t, MOD0_TOK_TILE)
        per = D_MODEL // ADA_COL_TILE
        shift = jnp.concatenate([keep_ref[k, pl.ds(r, 1), :] for k in range(per)], axis=1)
        scale = jnp.concatenate([keep_ref[per + k, pl.ds(r, 1), :] for k in range(per)], axis=1)
        h_ref[...] = _modulate(x_ref[...], gain_ref[...], shift, scale).astype(BF16)

    pl.when(jnp.logical_and(t >= 0, t < n_prompt))(lambda: run(xp_ref))
    pl.when(jnp.logical_and(t >= n_prompt, t < N_TOK // MOD0_TOK_TILE))(lambda: run(xs_ref))


def _ada_table_and_modulate(cv, w_ada, b_ada, xp, xs, norm_mix):
    tn, tm = ADA_COL_TILE, MOD0_TOK_TILE
    n = N_MOD * D_MODEL
    per_layer = n // tn
    n_prompt = N_PROMPT // tm
    n_tiles = N_TOK // tm
    assert DEPTH * per_layer >= ADA_KEEP + n_tiles

    def tok(s):
        return jnp.clip(s - ADA_KEEP, 0, n_tiles - 1)

    return pl.pallas_call(
        _ada_kernel,
        out_shape=(jax.ShapeDtypeStruct((DEPTH, ADA_ROWS, 1, n), F32), jax.ShapeDtypeStruct((N_TOK, D_MODEL), BF16)),
        grid=(DEPTH * per_layer,),
        in_specs=[pl.BlockSpec((ADA_ROWS, D_MODEL), lambda s: (0, 0)),
                  pl.BlockSpec((None, D_MODEL, tn), lambda s: (s // per_layer, 0, s % per_layer)),
                  pl.BlockSpec((None, 1, tn), lambda s: (s // per_layer, 0, s % per_layer)),
                  pl.BlockSpec((tm, D_MODEL), lambda s: (jnp.minimum(tok(s), n_prompt - 1), 0)),
                  pl.BlockSpec((tm, D_MODEL), lambda s: (jnp.maximum(tok(s) - n_prompt, 0), 0)),
                  pl.BlockSpec((1, D_MODEL), lambda s: (0, 0))],
        out_specs=(pl.BlockSpec((None, ADA_ROWS, 1, tn), lambda s: (s // per_layer, 0, 0, s % per_layer)),
                   pl.BlockSpec((tm, D_MODEL), lambda s: (tok(s), 0))),
        scratch_shapes=[pltpu.VMEM((ADA_KEEP, ADA_ROWS, tn), F32)],
        compiler_params=_params(("arbitrary",)),
        name="ada_table_modulate0",
    )(cv, w_ada, b_ada.reshape(DEPTH, 1, n), xp, xs, norm_mix[0:1])


def _rope(y, cos, sin, perm2):
    hi = y.astype(BF16)
    lo = (y - hi.astype(F32)).astype(BF16)
    rot = _bdot(jnp.concatenate([hi, lo], axis=1), perm2)
    return y * cos + rot * sin


def _inproj_kernel(h_ref, w_ref, qg_ref, kg_ref, cos_ref, sin_ref, perm_ref, p_ref, nk_ref, nv_ref):
    i = pl.program_id(0)
    j = pl.program_id(1)
    latent = i >= N_PROMPT // PROJ_TOK_TILE
    heads = PROJ_COL_TILE // HEAD_DIM
    kv_tile = COL_K // PROJ_COL_TILE
    is_q = j < kv_tile

    def attention_tile(rope):
        gain = jnp.where(is_q, qg_ref[...], kg_ref[...])
        w = w_ref[...].astype(BF16)
        for s in range(PROJ_TOK_TILE // DEC_SEQ):
            rows = slice(s * DEC_SEQ, (s + 1) * DEC_SEQ)
            acc = _bdot(h_ref[rows, :], w)
            for hh in range(heads):
                cols = slice(hh * HEAD_DIM, (hh + 1) * HEAD_DIM)
                x = acc[:, cols]
                y = _rms(x, gain)
                if rope:
                    y = _rope(y, cos_ref[...], sin_ref[...], perm_ref[...])
                if hh >= N_KV_HEADS:
                    y = jnp.where(is_q, y, x)
                p_ref[rows, cols] = y

    pl.when(jnp.logical_and(j <= kv_tile, latent))(lambda: attention_tile(True))
    pl.when(jnp.logical_and(j <= kv_tile, jnp.logical_not(latent)))(lambda: attention_tile(False))

    @pl.when(jnp.logical_and(j == kv_tile, jnp.logical_not(latent)))
    def _():
        for s in range(PROJ_TOK_TILE // SEQ):
            rows = slice(s * SEQ, (s + 1) * SEQ)
            for hh in range(N_KV_HEADS):
                nk_ref[s, :, hh, :] = p_ref[rows, hh * HEAD_DIM:(hh + 1) * HEAD_DIM]
                nv_ref[s, :, hh, :] = p_ref[rows, KV_WIDTH + hh * HEAD_DIM:KV_WIDTH + (hh + 1) * HEAD_DIM]

    def plain_tile(act):
        w = w_ref[...].astype(BF16)
        for s in range(PROJ_TOK_TILE // DEC_SEQ):
            rows = slice(s * DEC_SEQ, (s + 1) * DEC_SEQ)
            p_ref[rows, :] = act(_bdot(h_ref[rows, :], w))

    is_gate = jnp.logical_and(j >= COL_ZF // PROJ_COL_TILE, j < COL_HI // PROJ_COL_TILE)
    is_value = jnp.logical_and(j >= COL_HI // PROJ_COL_TILE, j < COL_HG // PROJ_COL_TILE)
    raw = jnp.logical_or(is_gate, is_value)
    pl.when(raw)(lambda: plain_tile(lambda a: a))
    pl.when(jnp.logical_and(j > kv_tile, jnp.logical_not(raw)))(lambda: plain_tile(_silu))


def _rope_tables():
    t = np.arange(DEC_SEQ)
    row = (t // GRID_W).astype(np.float32)
    col = (t % GRID_W).astype(np.float32)
    inv = (np.float32(ROPE_THETA) ** (-np.arange(0, ROPE_HALF, 2, dtype=np.float32) / np.float32(ROPE_HALF))).astype(np.float32)
    ar = row[:, None] * inv
    ac = col[:, None] * inv
    ang = np.concatenate([ar, ar, ac, ac], axis=-1).astype(np.float32)
    cos = np.cos(ang).astype(np.float32)
    sin = np.sin(ang).astype(np.float32)
    qw = ROPE_HALF // 2
    perm = np.zeros((HEAD_DIM, HEAD_DIM), np.float32)
    for k in range(qw):
        perm[qw + k, k] = -1.0
        perm[k, qw + k] = 1.0
        perm[3 * qw + k, 2 * qw + k] = -1.0
        perm[2 * qw + k, 3 * qw + k] = 1.0
    return jnp.asarray(cos), jnp.asarray(sin), jnp.asarray(np.concatenate([perm, perm], axis=0), BF16)


def _in_projection(h, w_in, q_gain, k_gain):
    tm, tn = PROJ_TOK_TILE, PROJ_COL_TILE
    n_prompt_tiles = N_PROMPT // tm
    cos, sin, perm2 = _rope_tables()
    table = pl.BlockSpec((DEC_SEQ, HEAD_DIM), lambda i, j: (0, 0))
    gain = pl.BlockSpec((1, HEAD_DIM), lambda i, j: (0, 0))
    state = pl.BlockSpec((tm // SEQ, None, SEQ, N_KV_HEADS, HEAD_DIM),
                         lambda i, j: (jnp.minimum(i, n_prompt_tiles - 1), 0, 0, 0, 0),
                         pipeline_mode=pl.Buffered(1))
    state_shape = jax.ShapeDtypeStruct((BATCH, 1, SEQ, N_KV_HEADS, HEAD_DIM), F32)
    kv_tile = COL_K // tn

    def out_tile(j):
        return jnp.where(j < kv_tile, j, jnp.where(j == kv_tile, OUT_K // tn, j - 1))

    return pl.pallas_call(
        _inproj_kernel,
        out_shape=(jax.ShapeDtypeStruct((N_TOK, IN_AB), F32), state_shape, state_shape),
        grid=(N_TOK // tm, IN_AB // tn),
        in_specs=[pl.BlockSpec((tm, D_MODEL), lambda i, j: (i, 0)),
                  pl.BlockSpec((D_MODEL, tn), lambda i, j: (0, j)),
                  gain, gain, table, table, pl.BlockSpec((2 * HEAD_DIM, HEAD_DIM), lambda i, j: (0, 0))],
        out_specs=(pl.BlockSpec((tm, tn), lambda i, j: (i, out_tile(j))), state, state),
        compiler_params=_params(("arbitrary", "arbitrary")),
        name="in_projection",
    )(h, w_in, q_gain, k_gain, cos, sin, perm2)


def _attn_kernel(*refs, has_ctx, stack):
    if has_ctx:
        q_ref, k_ref, v_ref, ck_ref, cv_ref, o_ref = refs
    else:
        q_ref, k_ref, v_ref, o_ref = refs
    scale = HEAD_DIM ** -0.5
    nt = (((1,), (1,)), ((), ()))
    tq = q_ref.shape[0]
    for hk in range(k_ref.shape[1] // HEAD_DIM):
        kcols = slice(hk * HEAD_DIM, (hk + 1) * HEAD_DIM)
        k = k_ref[:, kcols].astype(BF16)
        v = v_ref[:, kcols].astype(BF16)
        if has_ctx:
            ck = ck_ref[:, hk, :].astype(BF16)
            cv = cv_ref[:, hk, :].astype(BF16)
        for g0 in range(hk * Q_PER_KV, (hk + 1) * Q_PER_KV, stack):
            q = jnp.concatenate([q_ref[:, g * HEAD_DIM:(g + 1) * HEAD_DIM] for g in range(g0, g0 + stack)], axis=0)
            q = (q * scale).astype(BF16)
            s = lax.dot_general(q, k, nt, preferred_element_type=F32)
            m = jnp.max(s, axis=-1, keepdims=True)
            if has_ctx:
                sc = lax.dot_general(q, ck, nt, preferred_element_type=F32)
                m = jnp.maximum(m, jnp.max(sc, axis=-1, keepdims=True))
            p = jnp.exp(s - m)
            den = jnp.sum(p, axis=-1, keepdims=True)
            o = _bdot(p.astype(BF16), v)
            if has_ctx:
                pc = jnp.exp(sc - m)
                den = den + jnp.sum(pc, axis=-1, keepdims=True)
                o = o + _bdot(pc.astype(BF16), cv)
            o = (o / den).astype(o_ref.dtype)
            for g in range(stack):
                o_ref[:, (g0 + g) * HEAD_DIM:(g0 + g + 1) * HEAD_DIM] = o[g * tq:(g + 1) * tq]


def _attention(p, n_batch, seq, row0, tq, ctx=None):
    q_blocks = seq // tq
    kvh = N_KV_HEADS
    gw = kvh * Q_PER_KV * HEAD_DIM
    kw = kvh * HEAD_DIM
    in_specs = [
        pl.BlockSpec((tq, gw), lambda b, h, qi: (row0 // tq + b * q_blocks + qi, h)),
        pl.BlockSpec((seq, kw), lambda b, h, qi: (row0 // seq + b, OUT_K // kw + h)),
        pl.BlockSpec((seq, kw), lambda b, h, qi: (row0 // seq + b, OUT_V // kw + h)),
    ]
    args = [p, p, p]
    if ctx is not None:
        assert kvh == N_KV_HEADS
        ctx_spec = pl.BlockSpec((None, None, PAST_LEN, N_KV_HEADS, HEAD_DIM), lambda b, h, qi: (b, 0, 0, 0, 0))
        in_specs += [ctx_spec, ctx_spec]
        args += [ctx[0], ctx[1]]
    return pl.pallas_call(
        functools.partial(_attn_kernel, has_ctx=ctx is not None, stack=Q_PER_KV if ctx is None else 1),
        out_shape=jax.ShapeDtypeStruct((n_batch * seq, ATT_WIDTH), BF16),
        grid=(n_batch, N_KV_HEADS // kvh, q_blocks),
        in_specs=in_specs,
        out_specs=pl.BlockSpec((tq, gw), lambda b, h, qi: (b * q_blocks + qi, h)),
        compiler_params=_params(("arbitrary", "arbitrary", "arbitrary")),
        name="attention_latent" if ctx is not None else "attention_prompt",
    )(*args)


def _hgrn_constants():
    c = HG_CHUNK
    t = np.arange(c)
    cums, sels, pairs, scans, diags = [], [], [], [], []
    for d in range(2):
        pos = t if d == 0 else c - 1 - t
        pu, pt = pos[None, :], pos[:, None]
        ms, ss, ws = [], [], []
        for l in range(HG_LEVELS):
            m = c >> l
            blk = pos // m
            mid = (blk * m + m // 2)[:, None]
            late = ((pos % m) >= m // 2)
            ms.append(np.where(late[:, None], (pu >= mid) & (pu <= pt), (pu > pt) & (pu < mid)))
            ss.append(np.broadcast_to(late[:, None], (c, c)))
            ws.append((blk[:, None] == blk[None, :]) & late[:, None] & ~late[None, :])
        ms.append(pu <= pt)
        ms.append(pu > pt)
        cums.append(np.concatenate(ms, axis=0))
        sels.append(np.stack(ss))
        pairs.append(np.stack(ws))
        scans.append(np.concatenate([pu <= pt, pu <= pt], axis=1))
        diags.append(((pos // (c // 2))[:, None] == (pos // (c // 2))[None, :]) & (pu <= pt))
    return (jnp.asarray(np.stack(cums), BF16), jnp.asarray(np.stack(sels), F32),
            jnp.asarray(np.stack(pairs), F32), jnp.asarray(np.stack(scans), BF16),
            jnp.asarray(np.stack(diags), F32))


def _hgrn_kernel(*refs, n_seqs, n_chunks, has_s0):
    (hq_ref, zf_ref, zb_ref, hi_ref, hg_ref, lb_ref, og_ref, cum_ref, sel_ref, pair_ref, scan_ref,
     diag_ref) = refs[:12]
    refs = refs[12:]
    if has_s0:
        s0f_ref, s0b_ref, o_ref = refs[:3]
        refs = refs[3:]
    else:
        o_ref, sf_ref, sb_ref = refs[:3]
        refs = refs[3:]
    st_ref, acc_ref = refs
    hb = st_ref.shape[0]
    c = HG_CHUNK
    half = c // 2
    nt = (((1,), (1,)), ((), ()))
    tn = (((0,), (0,)), ((), ()))

    def direction(d, z_ref):
        raw = lb_ref[d]
        e = jnp.exp(raw - jnp.max(raw, axis=0, keepdims=True))
        lb = e[0:1] / jnp.sum(e, axis=0, keepdims=True)

        def initial_state(s, hh):
            if has_s0:
                return (s0f_ref, s0b_ref)[d][s, hh].T
            return jnp.zeros((HG_DV, HG_DK), F32)

        def gates(rows):
            f = lb + (1.0 - lb) * _sigmoid(z_ref[rows, :])
            logf = jnp.log(f)
            hi16 = logf.astype(BF16)
            lo16 = (logf - hi16.astype(F32)).astype(BF16)
            return f, hi16, lo16

        def row(p):
            t = p if d == 0 else c - 1 - p
            return slice(t, t + 1)

        early, late = (slice(0, half), slice(half, c)) if d == 0 else (slice(half, c), slice(0, half))

        def in_row_order(x_early, x_late):
            return jnp.concatenate([x_early, x_late] if d == 0 else [x_late, x_early], axis=0)

        def emit(rows, cols, o):
            if d == 0:
                acc_ref[rows, cols] = o
            else:
                tot = acc_ref[rows, cols] + o
                o_ref[rows, cols] = (_rms(tot, og_ref[...]) * hg_ref[rows, cols]).astype(o_ref.dtype)

        def two_level_operands(rows):
            f, hi16, lo16 = gates(rows)
            b = _bdot(scan_ref[d], jnp.concatenate([hi16, lo16], axis=0))
            span = jnp.maximum(
                jnp.maximum(b[row(0)] - b[row(half // 2 - 1)], b[row(half // 2 - 1)] - b[row(half - 1)]),
                jnp.maximum(b[row(half)] - b[row(half + half // 2 - 1)],
                            b[row(half + half // 2 - 1)] - b[row(c - 1)]))
            return dict(f=f, b=b, span=span)

        def two_level_chunk(rows, ops, states):
            new_states = []
            for hh in range(hb):
                cols = slice(hh * HG_DK, (hh + 1) * HG_DK)
                b = ops["b"][:, cols]
                kk = 1.0 - ops["f"][:, cols]
                q = hq_ref[rows, cols]
                iv = hi_ref[rows, cols].astype(BF16)
                r_mid = b[row(half - 1)]
                x1 = in_row_order(kk[early] * jnp.exp(r_mid - b[early]),
                                  q[late] * jnp.exp(b[late] - r_mid)).astype(BF16)
                dq = in_row_order(b[early] - b[row(half // 2 - 1)], b[late] - b[row(half + half // 2 - 1)])
                b_end = b[row(c - 1)]
                xq = (q * jnp.exp(dq)).astype(BF16)
                xk = kk * jnp.exp(-dq)
                g1 = _bdot(x1, x1.astype(F32).T.astype(BF16))
                g2 = _bdot(xq, xk.T.astype(BF16))
                att = jnp.where(pair_ref[d, 0] > 0.5, g1, jnp.where(diag_ref[d] > 0.5, g2, 0.0)).astype(BF16)
                st = states[hh]
                o = (_bdot(att, iv)
                     + lax.dot_general((q * jnp.exp(b)).astype(BF16), st.astype(BF16), nt, preferred_element_type=F32))
                pending.append((rows, cols, o))
                dst = lax.dot_general(iv, (kk * jnp.exp(b_end - b)).astype(BF16), tn, preferred_element_type=F32)
                new_states.append(jnp.exp(b_end) * st + dst)
            return new_states

        def all_levels(rows):
            f, hi16, lo16 = gates(rows)
            cum = cum_ref[d]
            eall = jnp.exp(_bdot(cum, hi16) + _bdot(cum, lo16))
            for hh in range(hb):
                cols = slice(hh * HG_DK, (hh + 1) * HG_DK)
                kk = 1.0 - f[:, cols]
                q = hq_ref[rows, cols]
                iv = hi_ref[rows, cols]
                iv16 = iv.astype(BF16)
                att = jnp.zeros((c, c), F32)
                for l in range(HG_LEVELS):
                    x = (kk + sel_ref[d, l] * (q - kk)) * eall[l * c:(l + 1) * c, cols]
                    xb = x.astype(BF16)
                    att = att + pair_ref[d, l] * lax.dot_general(xb, xb, nt, preferred_element_type=F32)
                e_in = eall[HG_LEVELS * c:(HG_LEVELS + 1) * c, cols]
                e_out = eall[(HG_LEVELS + 1) * c:, cols]
                st = st_ref[hh]
                o = (_bdot(att.astype(BF16), iv16)
                     + jnp.sum(q * kk, axis=-1, keepdims=True) * iv
                     + lax.dot_general((q * e_in).astype(BF16), st.astype(BF16), nt, preferred_element_type=F32))
                dst = lax.dot_general(iv16, (kk * e_out).astype(BF16), tn, preferred_element_type=F32)
                st_ref[hh] = (e_in[0:1] * e_out[0:1]) * st + dst
                emit(rows, cols, o)

        def chunk_start(s, ci):
            cidx = ci if d == 0 else n_chunks - 1 - ci
            return (s * n_chunks + cidx) * c

        worst = jnp.zeros((1, hb * HG_DK), F32)
        pending = []
        states = [[initial_state(s, hh) for hh in range(hb)] for s in range(n_seqs)]
        for ci in range(n_chunks):
            for s in range(n_seqs):
                rows = slice(chunk_start(s, ci), chunk_start(s, ci) + c)
                ops = two_level_operands(rows)
                states[s] = two_level_chunk(rows, ops, states[s])
                worst = jnp.maximum(worst, ops["span"])
        for item in pending:
            emit(*item)
        if not has_s0:
            for s in range(n_seqs):
                for hh in range(hb):
                    (sf_ref, sb_ref)[d][s, hh] = states[s][hh].T

        @pl.when(jnp.logical_not(jnp.max(worst) <= HG_SAFE_EXPONENT))
        def _():
            for s in range(n_seqs):
                for hh in range(hb):
                    st_ref[hh] = initial_state(s, hh)

                def chunk(ci, carry):
                    all_levels(pl.ds(pl.multiple_of(chunk_start(s, ci), c), c))
                    return carry

                lax.fori_loop(0, n_chunks, chunk, 0)
                if not has_s0:
                    for hh in range(hb):
                        (sf_ref, sb_ref)[d][s, hh] = st_ref[hh].T

    direction(0, zf_ref)
    direction(1, zb_ref)


def _hgrn(p, lb_raw, o_gain, consts, n_batch, seq, row0, n_seqs, hb, s0=None):
    w = hb * HG_DK
    rows = n_seqs * seq

    def seg(col):
        return pl.BlockSpec((rows, w), lambda b, h: (row0 // rows + b, col // w + h))

    def const(a):
        return pl.BlockSpec(a.shape, lambda b, h: (0,) * a.ndim)

    in_specs = [seg(OUT_HQ), seg(OUT_ZF), seg(OUT_ZF + HG_KW), seg(OUT_HI), seg(OUT_HG),
                pl.BlockSpec((2, DEPTH + 1, w), lambda b, h: (0, 0, h)),
                pl.BlockSpec((1, HG_DV), lambda b, h: (0, 0))] + [const(a) for a in consts]
    args = [p, p, p, p, p, lb_raw, o_gain, *consts]
    has_s0 = s0 is not None
    st_spec = pl.BlockSpec((n_seqs, hb, HG_DK, HG_DV), lambda b, h: (b, h, 0, 0))
    if has_s0:
        in_specs += [st_spec, st_spec]
        args += [s0[0], s0[1]]
    out_shape = [jax.ShapeDtypeStruct((n_batch * seq, HG_VW), BF16)]
    out_specs = [pl.BlockSpec((rows, w), lambda b, h: (b, h))]
    if not has_s0:
        st_shape = jax.ShapeDtypeStruct((n_batch, HG_HEADS, HG_DK, HG_DV), F32)
        out_shape += [st_shape, st_shape]
        out_specs += [st_spec, st_spec]
    return pl.pallas_call(
        functools.partial(_hgrn_kernel, n_seqs=n_seqs, n_chunks=seq // HG_CHUNK, has_s0=has_s0),
        out_shape=tuple(out_shape),
        grid=(n_batch // n_seqs, HG_HEADS // hb),
        in_specs=in_specs,
        out_specs=tuple(out_specs),
        scratch_shapes=[pltpu.VMEM((hb, HG_DV, HG_DK), F32), pltpu.VMEM((rows, w), F32)],
        compiler_params=_params(("arbitrary", "arbitrary")),
        name="hgrn_latent" if has_s0 else "hgrn_prompt",
    )(*args)


def _outproj_kernel(attp_ref, atts_ref, hgp_ref, hgs_ref, wa_ref, wb_ref, xp_ref, xs_ref, g1_ref, gain_ref,
                    sh_ref, sc_ref, x1_ref, h2_ref, full_ref, wcache_ref, ss_ref):
    i = pl.program_id(0)
    n = pl.program_id(1)
    nt = D_MODEL // OUT_COL_TILE

    @pl.when(i == 0)
    def _():
        wcache_ref[n, 0] = wa_ref[...].astype(BF16)
        wcache_ref[n, 1] = wb_ref[...].astype(BF16)

    @pl.when(n == 0)
    def _():
        ss_ref[...] = jnp.zeros(ss_ref.shape, F32)

    def run(att_ref, hg_ref, x_ref, last):
        acc = _bdot(att_ref[...], wcache_ref[n, 0]) + _bdot(hg_ref[...], wcache_ref[n, 1])
        x1 = x_ref[...] + g1_ref[...] * acc
        x1_ref[...] = x1
        sq = x1 * x1
        part = sq[:, 0:HEAD_DIM]
        for k in range(1, OUT_COL_TILE // HEAD_DIM):
            part = part + sq[:, k * HEAD_DIM:(k + 1) * HEAD_DIM]
        if not last:
            full_ref[n] = x1
            ss_ref[...] += part
            return
        rstd = lax.rsqrt(jnp.sum(ss_ref[...] + part, axis=-1, keepdims=True) / D_MODEL + EPS)
        gs = gain_ref[...] * (1.0 + sc_ref[...])
        for k in range(nt):
            cols = slice(k * OUT_COL_TILE, (k + 1) * OUT_COL_TILE)
            xk = x1 if k == nt - 1 else full_ref[k]
            h2_ref[:, cols] = (xk * rstd * gs[:, cols] + sh_ref[:, cols]).astype(BF16)

    prompt = i < N_PROMPT_TILES
    for last in (False, True):
        step = (n == nt - 1) if last else (n < nt - 1)
        pl.when(jnp.logical_and(prompt, step))(functools.partial(run, attp_ref, hgp_ref, xp_ref, last))
        pl.when(jnp.logical_and(jnp.logical_not(prompt), step))(functools.partial(run, atts_ref, hgs_ref, xs_ref, last))


def _out_projection(att_p, att_s, hg_p, hg_s, w_out, xp, xs, mods, norm_mlp):
    tm, tn = TOK_TILE, OUT_COL_TILE
    nt = D_MODEL // tn

    def prompt_rows(width):
        return pl.BlockSpec((tm, width), lambda i, n: (jnp.minimum(i, N_PROMPT_TILES - 1), 0))

    def sample_rows(width):
        return pl.BlockSpec((tm, width), lambda i, n: (jnp.maximum(i - N_PROMPT_TILES, 0), 0))

    return pl.pallas_call(
        _outproj_kernel,
        out_shape=(jax.ShapeDtypeStruct((N_TOK, D_MODEL), F32), jax.ShapeDtypeStruct((N_TOK, D_MODEL), BF16)),
        grid=(N_TILES, nt),
        in_specs=[prompt_rows(ATT_WIDTH), sample_rows(ATT_WIDTH), prompt_rows(HG_VW), sample_rows(HG_VW),
                  pl.BlockSpec((ATT_WIDTH, tn), lambda i, n: (0, jnp.where(i == 0, n, nt - 1))),
                  pl.BlockSpec((HG_VW, tn), lambda i, n: (1, jnp.where(i == 0, n, nt - 1))),
                  pl.BlockSpec((tm, tn), lambda i, n: (jnp.minimum(i, N_PROMPT_TILES - 1),
                                                       jnp.where(i < N_PROMPT_TILES, n, nt - 1))),
                  pl.BlockSpec((tm, tn), lambda i, n: (jnp.maximum(i - N_PROMPT_TILES, 0),
                                                       jnp.where(i < N_PROMPT_TILES, 0, n))),
                  _mod_spec(0, 2, tm, width=tn, col=lambda i, n: n),
                  pl.BlockSpec((1, D_MODEL), lambda i, n: (0, 0)),
                  _mod_spec(0, 3, tm), _mod_spec(0, 4, tm)],
        out_specs=(pl.BlockSpec((tm, tn), lambda i, n: (i, n)),
                   pl.BlockSpec((tm, D_MODEL), lambda i, n: (i, 0))),
        scratch_shapes=[pltpu.VMEM((nt - 1, tm, tn), F32), pltpu.VMEM((nt, 2, ATT_WIDTH, tn), BF16),
                        pltpu.VMEM((tm, HEAD_DIM), F32)],
        compiler_params=_params(("arbitrary", "arbitrary")),
        name="out_projection",
    )(att_p, att_s, hg_p, hg_s, w_out, w_out, xp, xs, mods, norm_mlp, mods, mods)


def _mlp_kernel(h_ref, w1_ref, w2_ref, x_ref, g2_ref, *rest, final):
    if final:
        fin_ref, o_ref, res_ref = rest
    else:
        o_ref, res_ref = rest
    j = pl.program_id(1)
    res_ref[j] = x_ref[...]

    def step(first, last):
        a = jnp.square(jnp.maximum(_bdot(h_ref[...], w1_ref[...].astype(BF16)), 0.0)).astype(BF16)
        for n in range(D_MODEL // MLP_OUT_CHUNK):
            cols = slice(n * MLP_OUT_CHUNK, (n + 1) * MLP_OUT_CHUNK)
            p = _bdot(a, w2_ref[:, cols].astype(BF16))
            if not first:
                p = o_ref[:, cols] + p
            if last:
                per = MLP_OUT_CHUNK // MLP_RES_COLS
                res = jnp.concatenate([res_ref[n * per + k] for k in range(per)], axis=1)
                p = res + g2_ref[:, cols] * p
            o_ref[:, cols] = p
        if last and final:
            o_ref[...] = _rms(o_ref[...], fin_ref[...])

    pl.when(j == 0)(lambda: step(True, False))
    pl.when(jnp.logical_and(j > 0, j < MLP_STEPS - 1))(lambda: step(False, False))
    pl.when(j == MLP_STEPS - 1)(lambda: step(False, True))


def _mlp(h, x, w1, w2, mods, layer, tile0, n_tiles, final_norm=None):
    tm, th = TOK_TILE, FF_TILE
    final = final_norm is not None
    in_specs = [pl.BlockSpec((tm, D_MODEL), lambda i, j: (tile0 + i, 0)),
                pl.BlockSpec((None, D_MODEL, th), lambda i, j: (layer, 0, j)),
                pl.BlockSpec((None, th, D_MODEL), lambda i, j: (layer, j, 0)),
                pl.BlockSpec((tm, MLP_RES_COLS), lambda i, j: (tile0 + i, j)),
                _mod_spec(layer, 5, tm, tile_of=lambda i, j: tile0 + i)]
    args = [h, w1, w2, x, mods]
    if final:
        in_specs.append(pl.BlockSpec((1, D_MODEL), lambda i, j: (0, 0)))
        args.append(final_norm)
    return pl.pallas_call(
        functools.partial(_mlp_kernel, final=final),
        out_shape=jax.ShapeDtypeStruct((n_tiles * tm, D_MODEL), F32),
        grid=(n_tiles, MLP_STEPS),
        in_specs=in_specs,
        out_specs=pl.BlockSpec((tm, D_MODEL), lambda i, j: (i, 0)),
        scratch_shapes=[pltpu.VMEM((MLP_STEPS, tm, MLP_RES_COLS), F32)],
        compiler_params=_params(("arbitrary", "arbitrary")),
        name="mlp_final" if final else "mlp",
    )(*args)


def _pool_kernel(x_ref, wp_ref, ps_ref, gain1_ref, sh1_ref, sc1_ref, g1_ref, gain2_ref, sh2_ref, sc2_ref,
                 x3_ref, h4_ref):
    i = pl.program_id(0)
    tm = TOK_TILE
    rstd = lax.rsqrt(jnp.mean(x_ref[...] * x_ref[...], axis=-1, keepdims=True) + EPS)
    gs1 = gain1_ref[...] * (1.0 + sc1_ref[...])
    mix_gain = g1_ref[...] * ps_ref[...]

    def widen(m):
        return jnp.concatenate([m] * (POOL_GROUP // HEAD_DIM), axis=1)

    def mix_tile(seq):
        n_seq = tm // seq
        pitch = seq + 2 * POOL_HALO
        n_pad = n_seq * pitch
        halo = jnp.zeros((POOL_HALO, POOL_GROUP), F32)
        pos = lax.broadcasted_iota(jnp.int32, (seq, HEAD_DIM), 0)

        def down(a, k):
            return pltpu.roll(a, k % n_pad, 0)

        sumsq = None
        for g, w in enumerate(POOL_WINDOWS):
            half = w // 2
            cols = slice(g * POOL_GROUP, (g + 1) * POOL_GROUP)
            x = x_ref[:, cols]
            h = x * rstd * gs1[:, cols] + sh1_ref[:, cols]
            padded = jnp.concatenate(
                [piece for s in range(n_seq) for piece in (halo, h[s * seq:(s + 1) * seq], halo)], axis=0)
            back = padded
            m = 1
            while m < half:
                back = back + down(back, m)
                m *= 2
            ahead = back if half == 1 else down(back, -(half - 1))
            total = down(back, 1) + ahead
            total = jnp.concatenate(
                [total[s * pitch + POOL_HALO:s * pitch + POOL_HALO + seq] for s in range(n_seq)], axis=0)
            count = (jnp.minimum(pos + (w - half), seq) - jnp.maximum(pos - half, 0)).astype(F32)
            inv = jnp.concatenate([widen(1.0 / count)] * n_seq, axis=0)
            pooled = (total * inv - h).astype(BF16)
            mix = _bdot(pooled, wp_ref[g].astype(BF16))
            x3 = x + mix_gain[:, cols] * mix
            x3_ref[:, cols] = x3
            sq = x3 * x3
            for k in range(POOL_GROUP // HEAD_DIM):
                part = sq[:, k * HEAD_DIM:(k + 1) * HEAD_DIM]
                sumsq = part if sumsq is None else sumsq + part
        rstd3 = lax.rsqrt(jnp.sum(sumsq, axis=-1, keepdims=True) / D_MODEL + EPS)
        h4_ref[...] = (x3_ref[...] * rstd3 * (gain2_ref[...] * (1.0 + sc2_ref[...])) + sh2_ref[...]).astype(BF16)

    pl.when(i < N_PROMPT_TILES)(lambda: mix_tile(SEQ))
    pl.when(i >= N_PROMPT_TILES)(lambda: mix_tile(DEC_SEQ))


def _pool_mixer(x, w_pool, pool_scale, norm_mix, norm_mlp, mods):
    tm = TOK_TILE
    vec = pl.BlockSpec((1, D_MODEL), lambda i: (0, 0))
    tile = pl.BlockSpec((tm, D_MODEL), lambda i: (i, 0))
    return pl.pallas_call(
        _pool_kernel,
        out_shape=(jax.ShapeDtypeStruct((N_TOK, D_MODEL), F32), jax.ShapeDtypeStruct((N_TOK, D_MODEL), BF16)),
        grid=(N_TILES,),
        in_specs=[tile, pl.BlockSpec((len(POOL_WINDOWS), POOL_GROUP, POOL_GROUP), lambda i: (0, 0, 0)),
                  vec, vec, _mod_spec(1, 0, tm), _mod_spec(1, 1, tm), _mod_spec(1, 2, tm),
                  vec, _mod_spec(1, 3, tm), _mod_spec(1, 4, tm)],
        out_specs=(tile, tile),
        compiler_params=_params(("arbitrary",)),
        name="pool_mixer",
    )(x, w_pool, pool_scale, norm_mix, mods, mods, mods, norm_mlp, mods, mods)


def kernel(x_prompt, x_sample, cache_k, cache_v, state_hgrn_fwd, state_hgrn_bwd, c, c_ctx, w_ada, b_ada,
           norm_mix, norm_mlp, w_in_ab, w_out_ab, q_norm, k_norm, hg_norm, lb_raw, w_pool, pool_scale,
           w_mlp_in, w_mlp_out, final_norm):
    xp = x_prompt.reshape(N_PROMPT, D_MODEL)
    xs = x_sample.reshape(N_SAMPLE, D_MODEL)
    cv = jnp.concatenate([c_ctx[None, :], c, jnp.zeros((ADA_ROWS - 1 - DEC_BATCH, D_MODEL), F32)], axis=0)
    mods, h0 = _ada_table_and_modulate(cv, w_ada, b_ada, xp, xs, norm_mix)

    proj, new_k, new_v = _in_projection(h0, w_in_ab[0], q_norm[0:1], k_norm[0:1])
    att_p = _attention(proj, BATCH, SEQ, 0, SEQ)
    att_s = _attention(proj, DEC_BATCH, DEC_SEQ, N_PROMPT, 512, ctx=(cache_k, cache_v))
    consts = _hgrn_constants()
    hg_p, s_fwd, s_bwd = _hgrn(proj, lb_raw, hg_norm[0:1], consts, BATCH, SEQ, 0, HG_PROMPT_SEQS, HG_PROMPT_HEADS)
    s0 = (state_hgrn_fwd.reshape(DEC_BATCH, HG_HEADS, HG_DK, HG_DV),
          state_hgrn_bwd.reshape(DEC_BATCH, HG_HEADS, HG_DK, HG_DV))
    (hg_s,) = _hgrn(proj, lb_raw, hg_norm[0:1], consts, DEC_BATCH, DEC_SEQ, N_PROMPT, 1, HG_LATENT_HEADS, s0=s0)
    x1, h2 = _out_projection(att_p, att_s, hg_p, hg_s, w_out_ab[0], xp, xs, mods, norm_mlp[0:1])
    x2 = _mlp(h2, x1, w_mlp_in, w_mlp_out, mods, 0, 0, N_TILES)

    x3, h4 = _pool_mixer(x2, w_pool[0], pool_scale[0:1], norm_mix[1:2], norm_mlp[1:2], mods)
    fin = final_norm[None, :]
    y_prompt = _mlp(h4, x3, w_mlp_in, w_mlp_out, mods, 1, 0, N_PROMPT_TILES, final_norm=fin)
    y_sample = _mlp(h4, x3, w_mlp_in, w_mlp_out, mods, 1, N_PROMPT_TILES, N_TILES - N_PROMPT_TILES,
                    final_norm=fin)

    return (y_prompt.reshape(BATCH, SEQ, D_MODEL), y_sample.reshape(DEC_BATCH, DEC_SEQ, D_MODEL),
            new_k, new_v,
            s_fwd.reshape(BATCH, 1, HG_HEADS, HG_DK, HG_DV), s_bwd.reshape(BATCH, 1, HG_HEADS, HG_DK, HG_DV))
```
